```python
import jax, jax.numpy as jnp
from jax import lax
import numpy as np

D_MODEL = 1024
BATCH = 8
SEQ = 2048
DEPTH = 1

ATTN_Q_HEADS = 8
ATTN_KV_HEADS = 2
ATTN_HEAD_DIM = 64
ATTN_GROUP = ATTN_Q_HEADS // ATTN_KV_HEADS
ATTN_WIDTH = ATTN_Q_HEADS * ATTN_HEAD_DIM
ATTN_KV_WIDTH = ATTN_KV_HEADS * ATTN_HEAD_DIM
WINDOW = 128
HGRN_HEADS = 4
HGRN_KEY_DIM = 128
HGRN_VAL_DIM = 128
HGRN_KEY_WIDTH = HGRN_HEADS * HGRN_KEY_DIM
HGRN_WIDTH = HGRN_HEADS * HGRN_VAL_DIM
HGRN_CHUNK = 64
MIX_WIDTH = ATTN_WIDTH + HGRN_WIDTH
IN_WIDTH = ATTN_WIDTH + 2 * ATTN_KV_WIDTH + 2 * HGRN_KEY_WIDTH + 2 * HGRN_WIDTH
N_EXPERTS = 32
TOP_K = 4
EXPERT_FF = D_MODEL
SWIGLU_LIMIT = 7.0
SWIGLU_ALPHA = 1.702
EXPERT_BLOCK = 128
NORM_EPS = 1e-5

kernel_name = "hymba_swa_sink_hgrn2_moe_block"


def rmsnorm(x, g):
    x32 = x.astype(jnp.float32)
    y = x32 * lax.rsqrt(jnp.mean(x32 * x32, axis=-1, keepdims=True) + NORM_EPS)
    return (y * g.astype(jnp.float32)).astype(x.dtype)


def alibi_slopes(n):
    return (2.0 ** (-8.0 * np.arange(1, n + 1) / n)).astype(np.float32)


def sliding_window_attention(q, k, v, sinks):
    B, S = q.shape[:2]
    nb = S // WINDOW
    dt = q.dtype
    qb = q.reshape(B, nb, WINDOW, ATTN_KV_HEADS, ATTN_GROUP, ATTN_HEAD_DIM)

    def band(t):
        cur = t.reshape(B, nb, WINDOW, ATTN_KV_HEADS, ATTN_HEAD_DIM)
        prev = jnp.concatenate([jnp.zeros_like(cur[:, :1]), cur[:, :-1]], axis=1)
        return jnp.concatenate([prev, cur], axis=2)

    kb, vb = band(k), band(v)
    scores = jnp.einsum('bnqhgd,bnkhd->bhgnqk', qb, kb,
                        preferred_element_type=jnp.float32) * (ATTN_HEAD_DIM ** -0.5)
    qi = jnp.arange(WINDOW)[:, None]
    ki = jnp.arange(2 * WINDOW)[None, :]
    dist = WINDOW + qi - ki
    blk = jnp.arange(nb)[:, None, None]
    valid = (dist >= 0) & (dist < WINDOW) & (blk * WINDOW - WINDOW + ki >= 0)
    slopes = jnp.asarray(alibi_slopes(ATTN_Q_HEADS)).reshape(ATTN_KV_HEADS, ATTN_GROUP)
    scores = scores - slopes[:, :, None, None, None] * dist.astype(jnp.float32)
    scores = jnp.where(valid, scores, -jnp.inf)
    sink = sinks.astype(jnp.float32).reshape(ATTN_KV_HEADS, ATTN_GROUP)[:, :, None, None, None]
    m = jnp.maximum(scores.max(axis=-1, keepdims=True), sink)
    p = jnp.exp(scores - m)
    probs = p / (p.sum(axis=-1, keepdims=True) + jnp.exp(sink - m))
    out = jnp.einsum('bhgnqk,bnkhd->bnqhgd', probs.astype(dt), vb)
    return out.reshape(B, S, ATTN_WIDTH)


def hgrn2_recurrence(q, f_logit, i, lb):
    B, S = q.shape[:2]
    f32 = jnp.float32
    nc = S // HGRN_CHUNK
    q = jax.nn.silu(q.astype(f32))
    lb = lb.astype(f32)
    f = lb + (1.0 - lb) * jax.nn.sigmoid(f_logit.astype(f32))
    logf = jnp.log(f)
    k = 1.0 - f

    def chunks(t, d):
        return t.reshape(B, nc, HGRN_CHUNK, HGRN_HEADS, d).transpose(1, 0, 3, 2, 4)

    qc = chunks(q, HGRN_KEY_DIM)
    kc = chunks(k, HGRN_KEY_DIM)
    vc = chunks(i.astype(f32), HGRN_VAL_DIM)
    bc = jnp.cumsum(chunks(logf, HGRN_KEY_DIM), axis=3)
    causal = jnp.tril(jnp.ones((HGRN_CHUNK, HGRN_CHUNK), bool))

    def step(state, inp):
        q_, k_, v_, b_ = inp
        inter = jnp.einsum('bhtk,bhkv->bhtv', q_ * jnp.exp(b_), state)
        rel = jnp.where(causal[:, :, None], b_[:, :, :, None, :] - b_[:, :, None, :, :], -jnp.inf)
        att = jnp.einsum('bhtk,bhtsk,bhsk->bhts', q_, jnp.exp(rel), k_)
        intra = jnp.einsum('bhts,bhsv->bhtv', att, v_)
        b_last = b_[:, :, -1:, :]
        new_state = (jnp.exp(b_last[:, :, 0, :, None]) * state
                     + jnp.einsum('bhsk,bhsv->bhkv', k_ * jnp.exp(b_last - b_), v_))
        return new_state, inter + intra

    s0 = jnp.zeros((B, HGRN_HEADS, HGRN_KEY_DIM, HGRN_VAL_DIM), f32)
    _, o = lax.scan(step, s0, (qc, kc, vc, bc))
    return o.transpose(1, 0, 3, 2, 4).reshape(B, S, HGRN_HEADS, HGRN_VAL_DIM)


def moe_ffn(h, router_w, router_b, w1, b1, w2, b2):
    T, D = h.shape
    f32 = jnp.float32
    logits = jnp.dot(h.astype(f32), router_w.astype(f32)) + router_b.astype(f32)
    top_v, top_e = lax.top_k(logits, TOP_K)
    gates = jax.nn.softmax(top_v, axis=-1)
    TK = T * TOP_K
    flat_e = top_e.reshape(TK)
    flat_tok = jnp.arange(TK, dtype=jnp.int32) // TOP_K
    flat_g = gates.reshape(TK)
    order = jnp.argsort(flat_e)
    se = flat_e[order]
    counts = jnp.bincount(flat_e, length=N_EXPERTS)
    padded = (counts + EXPERT_BLOCK - 1) // EXPERT_BLOCK * EXPERT_BLOCK
    start = jnp.cumsum(counts) - counts
    ends = jnp.cumsum(padded)
    pstart = ends - padded
    dest = pstart[se] + jnp.arange(TK, dtype=jnp.int32) - start[se]
    n_rows = -(-TK // EXPERT_BLOCK) * EXPERT_BLOCK + N_EXPERTS * EXPERT_BLOCK
    n_blocks = n_rows // EXPERT_BLOCK
    row_tok = jnp.zeros((n_rows,), jnp.int32).at[dest].set(flat_tok[order])
    row_gate = jnp.zeros((n_rows,), f32).at[dest].set(flat_g[order])
    block_e = jnp.clip(jnp.searchsorted(ends, jnp.arange(n_blocks) * EXPERT_BLOCK, side='right'),
                       0, N_EXPERTS - 1)
    xb = h[row_tok].reshape(n_blocks, EXPERT_BLOCK, D)

    def expert_block(args):
        xblk, e = args
        hid = jnp.dot(xblk, w1[e]) + b1[e]
        glu = jnp.minimum(hid[:, 0::2], SWIGLU_LIMIT)
        lin = jnp.clip(hid[:, 1::2], -SWIGLU_LIMIT, SWIGLU_LIMIT)
        act = glu * jax.nn.sigmoid(SWIGLU_ALPHA * glu) * (lin + 1.0)
        return jnp.dot(act, w2[e]) + b2[e]

    yb = lax.map(expert_block, (xb, block_e))
    y = jnp.zeros((T, D), f32).at[row_tok].add(
        yb.reshape(n_rows, D).astype(f32) * row_gate[:, None])
    return y.astype(h.dtype)


def setup_inputs(seed: int = 0) -> dict:
    key = jax.random.key(seed)
    ks = jax.random.split(key, 18)

    def nrm(k, shape, scale):
        return jax.random.normal(k, shape, jnp.float32) * scale

    return {
        "x": nrm(ks[0], (BATCH, SEQ, D_MODEL), 1.0),
        "norm1_g": 1.0 + nrm(ks[1], (DEPTH, D_MODEL), 0.02),
        "w_in": nrm(ks[2], (DEPTH, D_MODEL, IN_WIDTH), D_MODEL ** -0.5),
        "b_in": nrm(ks[3], (DEPTH, IN_WIDTH), 0.02),
        "attn_sinks": nrm(ks[4], (DEPTH, ATTN_Q_HEADS), 0.5),
        "attn_out_g": 1.0 + nrm(ks[5], (DEPTH, ATTN_WIDTH), 0.02),
        "hgrn_lb_logits": nrm(ks[6], (DEPTH + 1, HGRN_KEY_WIDTH), 1.0),
        "hgrn_out_g": 1.0 + nrm(ks[7], (DEPTH, HGRN_WIDTH), 0.02),
        "w_out": nrm(ks[8], (DEPTH, MIX_WIDTH, D_MODEL), MIX_WIDTH ** -0.5),
        "b_out": nrm(ks[9], (DEPTH, D_MODEL), 0.02),
        "norm2_g": 1.0 + nrm(ks[10], (DEPTH, D_MODEL), 0.02),
        "router_w": nrm(ks[11], (DEPTH, D_MODEL, N_EXPERTS), D_MODEL ** -0.5),
        "router_b": nrm(ks[12], (DEPTH, N_EXPERTS), 0.01),
        "w1": nrm(ks[13], (DEPTH, N_EXPERTS, D_MODEL, 2 * EXPERT_FF), D_MODEL ** -0.5),
        "b1": nrm(ks[14], (DEPTH, N_EXPERTS, 2 * EXPERT_FF), 0.02),
        "w2": nrm(ks[15], (DEPTH, N_EXPERTS, EXPERT_FF, D_MODEL), EXPERT_FF ** -0.5),
        "b2": nrm(ks[16], (DEPTH, N_EXPERTS, D_MODEL), 0.02),
        "final_g": 1.0 + nrm(ks[17], (D_MODEL,), 0.02),
    }


def reference(x, norm1_g, w_in, b_in, attn_sinks, attn_out_g, hgrn_lb_logits, hgrn_out_g,
              w_out, b_out, norm2_g, router_w, router_b, w1, b1, w2, b2, final_g):
    B, S, D = x.shape
    lower_bounds = jnp.cumsum(jax.nn.softmax(hgrn_lb_logits.astype(jnp.float32), axis=0), axis=0)
    split_at = list(np.cumsum([ATTN_WIDTH, ATTN_KV_WIDTH, ATTN_KV_WIDTH,
                               HGRN_KEY_WIDTH, HGRN_KEY_WIDTH, HGRN_WIDTH]))
    for l in range(DEPTH):
        h = rmsnorm(x, norm1_g[l])
        proj = jnp.dot(h, w_in[l]) + b_in[l]
        aq, ak, av, hq, hf, hi, hg = jnp.split(proj, split_at, axis=-1)
        attn = sliding_window_attention(aq, ak, av, attn_sinks[l])
        attn = rmsnorm(attn, attn_out_g[l])
        o = hgrn2_recurrence(hq, hf, hi, lower_bounds[l])
        o = o * lax.rsqrt(jnp.mean(o * o, axis=-1, keepdims=True) + NORM_EPS)
        o = o * hgrn_out_g[l].astype(jnp.float32).reshape(HGRN_HEADS, HGRN_VAL_DIM)
        o = o.reshape(B, S, HGRN_WIDTH) * jax.nn.silu(hg.astype(jnp.float32))
        mixed = jnp.concatenate([attn, o.astype(x.dtype)], axis=-1)
        x = x + jnp.dot(mixed, w_out[l]) + b_out[l]
        h2 = rmsnorm(x, norm2_g[l]).reshape(B * S, D)
        x = x + moe_ffn(h2, router_w[l], router_b[l], w1[l], b1[l], w2[l], b2[l]).reshape(B, S, D)
    return rmsnorm(x, final_g)
```

```python
import functools

import numpy as np
import jax
import jax.numpy as jnp
from jax import lax
from jax.experimental import pallas as pl
from jax.experimental.pallas import tpu as pltpu

F32 = jnp.float32
BF16 = jnp.bfloat16

D_MODEL = 1024
ATTN_Q_HEADS = 8
ATTN_KV_HEADS = 2
ATTN_HEAD_DIM = 64
ATTN_GROUP = ATTN_Q_HEADS // ATTN_KV_HEADS
ATTN_WIDTH = ATTN_Q_HEADS * ATTN_HEAD_DIM
ATTN_KV_WIDTH = ATTN_KV_HEADS * ATTN_HEAD_DIM
WINDOW = 128
HGRN_HEADS = 4
HGRN_DIM = 128
HGRN_WIDTH = HGRN_HEADS * HGRN_DIM
HGRN_CHUNK = 64
HGRN_SUB = 16
IN_WIDTH = ATTN_WIDTH + 2 * ATTN_KV_WIDTH + 4 * HGRN_WIDTH
N_EXPERTS = 32
TOP_K = 4
EXPERT_FF = D_MODEL
SWIGLU_LIMIT = 7.0
SWIGLU_ALPHA = 1.702
NORM_EPS = 1e-5

ROW_TILE = 512
EXPERT_BLOCK = 256
DISPATCH_TILE = 256
VMEM_LIMIT = 56 * 1024 * 1024

_ALIBI = [float(2.0 ** (-8.0 * (h + 1) / ATTN_Q_HEADS)) for h in range(ATTN_Q_HEADS)]


def _rms(x, g):
    return x * lax.rsqrt(jnp.mean(x * x, axis=-1, keepdims=True) + NORM_EPS) * g


def _params(*sem):
    return pltpu.CompilerParams(dimension_semantics=sem, vmem_limit_bytes=VMEM_LIMIT)


_IN_SPLITS = (ATTN_WIDTH, ATTN_KV_WIDTH, ATTN_KV_WIDTH, HGRN_WIDTH, HGRN_WIDTH, HGRN_WIDTH, HGRN_WIDTH)


def _inproj_kernel(x_ref, g_ref, w_ref, b_ref, *out_refs):
    h = _rms(x_ref[...], g_ref[...]).astype(BF16)
    lo = 0
    for ref, width in zip(out_refs, _IN_SPLITS):
        ref[...] = jnp.dot(h, w_ref[:, lo:lo + width], preferred_element_type=F32) + b_ref[:, lo:lo + width]
        lo += width


def _inproj(x2, g, w_bf, b):
    t = x2.shape[0]
    row = lambda i: (i, 0)
    fixed = lambda i: (0, 0)
    return pl.pallas_call(
        _inproj_kernel,
        grid=(t // ROW_TILE,),
        in_specs=[pl.BlockSpec((ROW_TILE, D_MODEL), row),
                  pl.BlockSpec((1, D_MODEL), fixed),
                  pl.BlockSpec((D_MODEL, IN_WIDTH), fixed),
                  pl.BlockSpec((1, IN_WIDTH), fixed)],
        out_specs=[pl.BlockSpec((ROW_TILE, w), row) for w in _IN_SPLITS],
        out_shape=[jax.ShapeDtypeStruct((t, w), F32) for w in _IN_SPLITS],
        compiler_params=_params("parallel"),
        name="inproj",
    )(x2, g, w_bf, b)


def _attn_kernel(sink_ref, q_ref, kp_ref, kc_ref, vp_ref, vc_ref, g_ref, o_ref, acc_ref):
    n = pl.program_id(1)
    k2 = jnp.concatenate([kp_ref[...], kc_ref[...]], axis=0).astype(BF16)
    v2 = jnp.concatenate([vp_ref[...], vc_ref[...]], axis=0).astype(BF16)
    qi = lax.broadcasted_iota(jnp.int32, (WINDOW, 2 * WINDOW), 0)
    ki = lax.broadcasted_iota(jnp.int32, (WINDOW, 2 * WINDOW), 1)
    dist = WINDOW + qi - ki
    valid = (dist >= 0) & (dist < WINDOW) & ((ki >= WINDOW) | (n > 0))
    dist_f = dist.astype(F32)
    scale = ATTN_HEAD_DIM ** -0.5
    for head in range(ATTN_Q_HEADS):
        kv = head // ATTN_GROUP
        kh = k2[:, kv * ATTN_HEAD_DIM:(kv + 1) * ATTN_HEAD_DIM]
        vh = v2[:, kv * ATTN_HEAD_DIM:(kv + 1) * ATTN_HEAD_DIM]
        qh = q_ref[:, head * ATTN_HEAD_DIM:(head + 1) * ATTN_HEAD_DIM].astype(BF16)
        s = lax.dot_general(qh, kh, (((1,), (1,)), ((), ())), preferred_element_type=F32) * scale
        s = jnp.where(valid, s - _ALIBI[head] * dist_f, -jnp.inf)
        sink = sink_ref[head]
        m = jnp.maximum(jnp.max(s, axis=-1, keepdims=True), sink)
        p = jnp.exp(s - m)
        den = jnp.sum(p, axis=-1, keepdims=True) + jnp.exp(sink - m)
        probs = (p / den).astype(BF16)
        acc_ref[:, head * ATTN_HEAD_DIM:(head + 1) * ATTN_HEAD_DIM] = jnp.dot(
            probs, vh, preferred_element_type=F32)
    o_ref[...] = _rms(acc_ref[...], g_ref[...])


def _attention(aq, ak, av, sinks, g, batch, seq):
    nb = seq // WINDOW
    aq = aq.reshape(batch, seq, ATTN_WIDTH)
    ak = ak.reshape(batch, seq, ATTN_KV_WIDTH)
    av = av.reshape(batch, seq, ATTN_KV_WIDTH)
    cur = lambda b, n, s: (b, n, 0)
    prev = lambda b, n, s: (b, jnp.maximum(n - 1, 0), 0)
    out = pl.pallas_call(
        _attn_kernel,
        grid_spec=pltpu.PrefetchScalarGridSpec(
            num_scalar_prefetch=1,
            grid=(batch, nb),
            in_specs=[pl.BlockSpec((None, WINDOW, ATTN_WIDTH), cur),
                      pl.BlockSpec((None, WINDOW, ATTN_KV_WIDTH), prev),
                      pl.BlockSpec((None, WINDOW, ATTN_KV_WIDTH), cur),
                      pl.BlockSpec((None, WINDOW, ATTN_KV_WIDTH), prev),
                      pl.BlockSpec((None, WINDOW, ATTN_KV_WIDTH), cur),
                      pl.BlockSpec((1, ATTN_WIDTH), lambda b, n, s: (0, 0))],
            out_specs=pl.BlockSpec((None, WINDOW, ATTN_WIDTH), cur),
            scratch_shapes=[pltpu.VMEM((WINDOW, ATTN_WIDTH), F32)]),
        out_shape=jax.ShapeDtypeStruct((batch, seq, ATTN_WIDTH), F32),
        compiler_params=_params("parallel", "parallel"),
        name="attn",
    )(sinks, aq, ak, ak, av, av, g)
    return out.reshape(batch * seq, ATTN_WIDTH)


def _hgrn_kernel(q_ref, f_ref, i_ref, gate_ref, lb_ref, og_ref, o_ref, st_ref):
    c = pl.program_id(1)

    @pl.when(c == 0)
    def _():
        st_ref[...] = jnp.zeros_like(st_ref)

    C, S = HGRN_CHUNK, HGRN_SUB
    r_i = lax.broadcasted_iota(jnp.int32, (C, C), 0)
    c_i = lax.broadcasted_iota(jnp.int32, (C, C), 1)
    tri = (r_i >= c_i).astype(F32)
    rowblk = lax.broadcasted_iota(jnp.int32, (C, HGRN_DIM), 0) // S
    sub_r = lax.broadcasted_iota(jnp.int32, (S, HGRN_DIM), 0)
    for h in range(HGRN_HEADS):
        sl = slice(h * HGRN_DIM, (h + 1) * HGRN_DIM)
        qx = q_ref[:, sl]
        q = qx * jax.nn.sigmoid(qx)
        lb = lb_ref[:, sl]
        f = lb + (1.0 - lb) * jax.nn.sigmoid(f_ref[:, sl])
        logf = jnp.log(f)
        k = 1.0 - f
        v = i_ref[:, sl]
        b = jnp.dot(tri, logf, preferred_element_type=F32, precision=lax.Precision.HIGHEST)
        st = st_ref[h]
        o = lax.dot_general((q * jnp.exp(b)).astype(BF16), st.astype(BF16),
                            (((1,), (1,)), ((), ())), preferred_element_type=F32)
        qcat, kcat = [], []
        for i in range(1, C // S):
            bref = b[i * S:i * S + 1, :]
            qcat.append(jnp.where(rowblk == i, q * jnp.exp(jnp.minimum(b - bref, 0.0)), 0.0))
            kcat.append(jnp.where(rowblk < i, k * jnp.exp(jnp.minimum(bref - b, 0.0)), 0.0))
        qcat = jnp.concatenate(qcat, axis=1).astype(BF16)
        kcat = jnp.concatenate(kcat, axis=1).astype(BF16)
        att = lax.dot_general(qcat, kcat, (((1,), (1,)), ((), ())), preferred_element_type=F32)
        o = o + jnp.dot(att.astype(BF16), v.astype(BF16), preferred_element_type=F32)
        diag = []
        for i in range(C // S):
            b_blk = b[i * S:(i + 1) * S, :]
            q_blk = q[i * S:(i + 1) * S, :]
            acc = jnp.zeros((S, HGRN_DIM), F32)
            for s in range(S):
                r = i * S + s
                e = jnp.exp(jnp.minimum(b_blk - b[r:r + 1, :], 0.0))
                a = jnp.where(sub_r >= s, q_blk * e * k[r:r + 1, :], 0.0)
                acc = acc + jnp.sum(a, axis=-1, keepdims=True) * v[r:r + 1, :]
            diag.append(acc)
        o = o + jnp.concatenate(diag, axis=0)
        bl = b[C - 1:C, :]
        kd = (k * jnp.exp(bl - b)).astype(BF16)
        st_ref[h] = st * jnp.exp(bl) + lax.dot_general(
            v.astype(BF16), kd, (((0,), (0,)), ((), ())), preferred_element_type=F32)
        gx = gate_ref[:, sl]
        o = o * lax.rsqrt(jnp.mean(o * o, axis=-1, keepdims=True) + NORM_EPS) * og_ref[:, sl]
        o_ref[:, sl] = o * (gx * jax.nn.sigmoid(gx))


def _hgrn(hq, hf, hi, hg, lb, og, batch, seq):
    nc = seq // HGRN_CHUNK
    shp = (batch, seq, HGRN_WIDTH)
    blk = pl.BlockSpec((None, HGRN_CHUNK, HGRN_WIDTH), lambda b, c: (b, c, 0))
    vec = pl.BlockSpec((1, HGRN_WIDTH), lambda b, c: (0, 0))
    out = pl.pallas_call(
        _hgrn_kernel,
        grid=(batch, nc),
        in_specs=[blk, blk, blk, blk, vec, vec],
        out_specs=blk,
        out_shape=jax.ShapeDtypeStruct(shp, F32),
        scratch_shapes=[pltpu.VMEM((HGRN_HEADS, HGRN_DIM, HGRN_DIM), F32)],
        compiler_params=_params("parallel", "arbitrary"),
        name="hgrn",
    )(hq.reshape(shp), hf.reshape(shp), hi.reshape(shp), hg.reshape(shp), lb, og)
    return out.reshape(batch * seq, HGRN_WIDTH)


def _outproj_kernel(attn_ref, hg_ref, x_ref, wo_ref, bo_ref, g2_ref, rw_ref, rb_ref,
                    x1_ref, h2_ref, tope_ref, rank_ref, gate_ref, cnt_ref, carry_ref):
    step = pl.program_id(0)

    @pl.when(step == 0)
    def _():
        carry_ref[...] = jnp.zeros_like(carry_ref)

    tm = x_ref.shape[0]
    y = jnp.dot(attn_ref[...].astype(BF16), wo_ref[:ATTN_WIDTH, :], preferred_element_type=F32)
    y = y + jnp.dot(hg_ref[...].astype(BF16), wo_ref[ATTN_WIDTH:, :], preferred_element_type=F32)
    x1 = x_ref[...] + y + bo_ref[...]
    x1_ref[...] = x1
    h2 = _rms(x1, g2_ref[...])
    h2_ref[...] = h2
    logits = jnp.dot(h2, rw_ref[...], preferred_element_type=F32,
                     precision=lax.Precision.HIGHEST) + rb_ref[...]
    lane = lax.broadcasted_iota(jnp.int32, (tm, N_EXPERTS), 1)
    work = logits
    picks, vals, hots = [], [], []
    for _ in range(TOP_K):
        m = jnp.max(work, axis=-1, keepdims=True)
        idx = jnp.min(jnp.where(work == m, lane, N_EXPERTS), axis=-1, keepdims=True)
        hot = lane == idx
        picks.append(idx)
        vals.append(m)
        hots.append(hot)
        work = jnp.where(hot, -jnp.inf, work)
    ex = [jnp.exp(v - vals[0]) for v in vals]
    den = ex[0] + ex[1] + ex[2] + ex[3]
    sel = jnp.zeros((tm, N_EXPERTS), F32)
    for hot in hots:
        sel = sel + hot.astype(F32)
    r_i = lax.broadcasted_iota(jnp.int32, (tm, tm), 0)
    c_i = lax.broadcasted_iota(jnp.int32, (tm, tm), 1)
    strict = (r_i > c_i).astype(BF16)
    ranks = jnp.dot(strict, sel.astype(BF16), preferred_element_type=F32) + carry_ref[...]
    carry_ref[...] = carry_ref[...] + jnp.sum(sel, axis=0, keepdims=True)
    cnt_ref[...] = carry_ref[...]
    l4 = lax.broadcasted_iota(jnp.int32, (tm, TOP_K), 1)
    tope = jnp.zeros((tm, TOP_K), jnp.int32)
    rank4 = jnp.zeros((tm, TOP_K), F32)
    gate4 = jnp.zeros((tm, TOP_K), F32)
    for j in range(TOP_K):
        rj = jnp.sum(jnp.where(hots[j], ranks, 0.0), axis=-1, keepdims=True)
        tope = jnp.where(l4 == j, picks[j], tope)
        rank4 = jnp.where(l4 == j, rj, rank4)
        gate4 = jnp.where(l4 == j, ex[j] / den, gate4)
    tope_ref[...] = tope
    rank_ref[...] = rank4.astype(jnp.int32)
    gate_ref[...] = gate4


def _outproj(attn, hgo, x2, wo_bf, bo, g2, rw, rb):
    t = x2.shape[0]
    row = lambda i: (i, 0)
    fixed = lambda i: (0, 0)
    return pl.pallas_call(
        _outproj_kernel,
        grid=(t // ROW_TILE,),
        in_specs=[pl.BlockSpec((ROW_TILE, ATTN_WIDTH), row),
                  pl.BlockSpec((ROW_TILE, HGRN_WIDTH), row),
                  pl.BlockSpec((ROW_TILE, D_MODEL), row),
                  pl.BlockSpec((ATTN_WIDTH + HGRN_WIDTH, D_MODEL), fixed),
                  pl.BlockSpec((1, D_MODEL), fixed),
                  pl.BlockSpec((1, D_MODEL), fixed),
                  pl.BlockSpec((D_MODEL, N_EXPERTS), fixed),
                  pl.BlockSpec((1, N_EXPERTS), fixed)],
        out_specs=[pl.BlockSpec((ROW_TILE, D_MODEL), row),
                   pl.BlockSpec((ROW_TILE, D_MODEL), row),
                   pl.BlockSpec((ROW_TILE, TOP_K), row),
                   pl.BlockSpec((ROW_TILE, TOP_K), row),
                   pl.BlockSpec((ROW_TILE, TOP_K), row),
                   pl.BlockSpec((1, N_EXPERTS), fixed)],
        out_shape=[jax.ShapeDtypeStruct((t, D_MODEL), F32),
                   jax.ShapeDtypeStruct((t, D_MODEL), F32),
                   jax.ShapeDtypeStruct((t, TOP_K), jnp.int32),
                   jax.ShapeDtypeStruct((t, TOP_K), jnp.int32),
                   jax.ShapeDtypeStruct((t, TOP_K), F32),
                   jax.ShapeDtypeStruct((1, N_EXPERTS), F32)],
        scratch_shapes=[pltpu.VMEM((1, N_EXPERTS), F32)],
        compiler_params=_params("arbitrary"),
        name="outproj",
    )(attn, hgo, x2, wo_bf, bo, g2, rw, rb)


def _row_copy(src_ref, s, dst_ref, d, sem):
    return pltpu.make_async_copy(src_ref.at[pl.ds(s, 1), :], dst_ref.at[pl.ds(d, 1), :], sem)


def _dispatch_kernel(dest_ref, h_ref, xs_in_ref, xs_ref, sem):
    del xs_in_ref
    base = pl.program_id(0) * (DISPATCH_TILE * TOP_K)

    def issue(t, carry):
        for j in range(TOP_K):
            _row_copy(h_ref, t, xs_ref, dest_ref[base + t * TOP_K + j], sem).start()
        return carry

    lax.fori_loop(0, DISPATCH_TILE, issue, 0)

    def drain(t, carry):
        for j in range(TOP_K):
            _row_copy(h_ref, 0, xs_ref, 0, sem).wait()
        return carry

    lax.fori_loop(0, DISPATCH_TILE, drain, 0)


def _dispatch(dest_flat, h2, n_rows):
    t = h2.shape[0]
    xs0 = jnp.zeros((n_rows, D_MODEL), F32)
    return pl.pallas_call(
        _dispatch_kernel,
        grid_spec=pltpu.PrefetchScalarGridSpec(
            num_scalar_prefetch=1,
            grid=(t // DISPATCH_TILE,),
            in_specs=[pl.BlockSpec((DISPATCH_TILE, D_MODEL), lambda i, d: (i, 0)),
                      pl.BlockSpec(memory_space=pl.ANY)],
            out_specs=pl.BlockSpec(memory_space=pl.ANY),
            scratch_shapes=[pltpu.SemaphoreType.DMA(())]),
        out_shape=jax.ShapeDtypeStruct((n_rows, D_MODEL), F32),
        input_output_aliases={2: 0},
        compiler_params=_params("arbitrary"),
        name="dispatch",
    )(dest_flat, h2, xs0)


def _expert_kernel(be_ref, nb_ref, xs_ref, w1g_ref, w1l_ref, b1g_ref, b1l_ref, w2_ref, b2_ref, y_ref):
    blk = pl.program_id(0)

    @pl.when(blk < nb_ref[0])
    def _():
        x = xs_ref[...].astype(BF16)
        glu = jnp.dot(x, w1g_ref[...], preferred_element_type=F32) + b1g_ref[...]
        lin = jnp.dot(x, w1l_ref[...], preferred_element_type=F32) + b1l_ref[...]
        glu = jnp.minimum(glu, SWIGLU_LIMIT)
        lin = jnp.clip(lin, -SWIGLU_LIMIT, SWIGLU_LIMIT)
        act = glu * jax.nn.sigmoid(SWIGLU_ALPHA * glu) * (lin + 1.0)
        y_ref[...] = jnp.dot(act.astype(BF16), w2_ref[...], preferred_element_type=F32) + b2_ref[...]

    @pl.when(blk >= nb_ref[0])
    def _():
        y_ref[...] = jnp.zeros_like(y_ref)


def _experts(block_e, n_used, xs, w1g, w1l, b1g, b1l, w2_bf, b2):
    n_rows = xs.shape[0]
    nblk = n_rows // EXPERT_BLOCK
    rows = lambda b, be, nb: (b, 0)
    wsel = lambda b, be, nb: (be[b], 0, 0)
    return pl.pallas_call(
        _expert_kernel,
        grid_spec=pltpu.PrefetchScalarGridSpec(
            num_scalar_prefetch=2,
            grid=(nblk,),
            in_specs=[pl.BlockSpec((EXPERT_BLOCK, D_MODEL), rows),
                      pl.BlockSpec((None, D_MODEL, EXPERT_FF), wsel),
                      pl.BlockSpec((None, D_MODEL, EXPERT_FF), wsel),
                      pl.BlockSpec((None, 1, EXPERT_FF), wsel),
                      pl.BlockSpec((None, 1, EXPERT_FF), wsel),
                      pl.BlockSpec((None, EXPERT_FF, D_MODEL), wsel),
                      pl.BlockSpec((None, 1, D_MODEL), wsel)],
            out_specs=pl.BlockSpec((EXPERT_BLOCK, D_MODEL), rows)),
        out_shape=jax.ShapeDtypeStruct((n_rows, D_MODEL), F32),
        compiler_params=_params("arbitrary"),
        name="experts",
    )(block_e, n_used, xs, w1g, w1l, b1g, b1l, w2_bf, b2)


def _combine_kernel(final_norm, dest_ref, yb_ref, x1_ref, gate_ref, g_ref, o_ref, buf_ref, sem):
    base = pl.program_id(0) * (DISPATCH_TILE * TOP_K)

    def issue(t, carry):
        for j in range(TOP_K):
            pltpu.make_async_copy(yb_ref.at[pl.ds(dest_ref[base + t * TOP_K + j], 1), :],
                                  buf_ref.at[j, pl.ds(t, 1), :], sem).start()
        return carry

    lax.fori_loop(0, DISPATCH_TILE, issue, 0)

    def drain(t, carry):
        for j in range(TOP_K):
            pltpu.make_async_copy(yb_ref.at[pl.ds(0, 1), :], buf_ref.at[0, pl.ds(0, 1), :], sem).wait()
        return carry

    lax.fori_loop(0, DISPATCH_TILE, drain, 0)
    gates = gate_ref[...]
    y = x1_ref[...]
    for j in range(TOP_K):
        y = y + buf_ref[j] * gates[:, j:j + 1]
    o_ref[...] = _rms(y, g_ref[...]) if final_norm else y


def _combine(dest_flat, yb, x1, gates, g, final_norm):
    t = x1.shape[0]
    return pl.pallas_call(
        functools.partial(_combine_kernel, final_norm),
        grid_spec=pltpu.PrefetchScalarGridSpec(
            num_scalar_prefetch=1,
            grid=(t // DISPATCH_TILE,),
            in_specs=[pl.BlockSpec(memory_space=pl.ANY),
                      pl.BlockSpec((DISPATCH_TILE, D_MODEL), lambda i, d: (i, 0)),
                      pl.BlockSpec((DISPATCH_TILE, TOP_K), lambda i, d: (i, 0)),
                      pl.BlockSpec((1, D_MODEL), lambda i, d: (0, 0))],
            out_specs=pl.BlockSpec((DISPATCH_TILE, D_MODEL), lambda i, d: (i, 0)),
            scratch_shapes=[pltpu.VMEM((TOP_K, DISPATCH_TILE, D_MODEL), F32),
                            pltpu.SemaphoreType.DMA(())]),
        out_shape=jax.ShapeDtypeStruct((t, D_MODEL), F32),
        compiler_params=_params("arbitrary"),
        name="combine",
    )(dest_flat, yb, x1, gates, g)


def kernel(x, norm1_g, w_in, b_in, attn_sinks, attn_out_g, hgrn_lb_logits, hgrn_out_g, w_out, b_out,
           norm2_g, router_w, router_b, w1, b1, w2, b2, final_g):
    batch, seq, d = x.shape
    t = batch * seq
    depth = w_in.shape[0]
    lower_bounds = jnp.cumsum(jax.nn.softmax(hgrn_lb_logits.astype(F32), axis=0), axis=0)
    n_rows = (t * TOP_K) // EXPERT_BLOCK * EXPERT_BLOCK + N_EXPERTS * EXPERT_BLOCK
    nblk = n_rows // EXPERT_BLOCK
    x2 = x.reshape(t, d)
    for l in range(depth):
        aq, ak, av, hq, hf, hi, hg = _inproj(x2, norm1_g[l][None], w_in[l].astype(BF16), b_in[l][None])
        attn = _attention(aq, ak, av, attn_sinks[l], attn_out_g[l][None], batch, seq)
        hgo = _hgrn(hq, hf, hi, hg, lower_bounds[l][None], hgrn_out_g[l][None], batch, seq)
        x1, h2, top_e, rank4, gates, counts = _outproj(
            attn, hgo, x2, w_out[l].astype(BF16), b_out[l][None], norm2_g[l][None],
            router_w[l], router_b[l][None])
        cnt = counts[0].astype(jnp.int32)
        padded = (cnt + EXPERT_BLOCK - 1) // EXPERT_BLOCK * EXPERT_BLOCK
        ends = jnp.cumsum(padded)
        pstart = ends - padded
        hot = top_e[:, :, None] == jnp.arange(N_EXPERTS, dtype=jnp.int32)
        dest = rank4 + jnp.sum(jnp.where(hot, pstart, 0), axis=-1)
        dest_flat = dest.reshape(t * TOP_K)
        blk_start = jnp.arange(nblk, dtype=jnp.int32) * EXPERT_BLOCK
        block_e = jnp.minimum(jnp.sum(blk_start[:, None] >= ends[None, :], axis=-1), N_EXPERTS - 1)
        n_used = (ends[-1] // EXPERT_BLOCK).reshape(1)
        xs = _dispatch(dest_flat, h2, n_rows)
        yb = _experts(block_e.astype(jnp.int32), n_used.astype(jnp.int32), xs,
                      w1[l][:, :, 0::2].astype(BF16), w1[l][:, :, 1::2].astype(BF16),
                      b1[l][:, None, 0::2], b1[l][:, None, 1::2],
                      w2[l].astype(BF16), b2[l][:, None, :])
        x2 = _combine(dest_flat, yb, x1, gates, final_g[None], l == depth - 1)
    return x2.reshape(batch, seq, d)
```

```python
import functools

import numpy as np
import jax
import jax.numpy as jnp
from jax import lax
from jax.experimental import pallas as pl
from jax.experimental.pallas import tpu as pltpu

F32 = jnp.float32
BF16 = jnp.bfloat16

D_MODEL = 1024
ATTN_Q_HEADS = 8
ATTN_KV_HEADS = 2
ATTN_HEAD_DIM = 64
ATTN_GROUP = ATTN_Q_HEADS // ATTN_KV_HEADS
ATTN_WIDTH = ATTN_Q_HEADS * ATTN_HEAD_DIM
ATTN_KV_WIDTH = ATTN_KV_HEADS * ATTN_HEAD_DIM
WINDOW = 128
HGRN_HEADS = 4
HGRN_DIM = 128
HGRN_WIDTH = HGRN_HEADS * HGRN_DIM
HGRN_CHUNK = 64
HGRN_SUB = 16
IN_WIDTH = ATTN_WIDTH + 2 * ATTN_KV_WIDTH + 4 * HGRN_WIDTH
N_EXPERTS = 32
TOP_K = 4
EXPERT_FF = D_MODEL
SWIGLU_LIMIT = 7.0
SWIGLU_ALPHA = 1.702
NORM_EPS = 1e-5

LANES = 128
ROW_TILE = 512
EXPERT_BLOCK = 256
DISPATCH_TILE = 256
VMEM_LIMIT = 56 * 1024 * 1024

_ALIBI = [float(2.0 ** (-8.0 * (h + 1) / ATTN_Q_HEADS)) for h in range(ATTN_Q_HEADS)]


def _rms(x, g):
    return x * lax.rsqrt(jnp.mean(x * x, axis=-1, keepdims=True) + NORM_EPS) * g


def _params(*sem):
    return pltpu.CompilerParams(dimension_semantics=sem, vmem_limit_bytes=VMEM_LIMIT)


_IN_SPLITS = (ATTN_WIDTH, ATTN_KV_WIDTH, ATTN_KV_WIDTH, HGRN_WIDTH, HGRN_WIDTH, HGRN_WIDTH, HGRN_WIDTH)


def _inproj_kernel(x_ref, g_ref, w_ref, b_ref, *out_refs):
    h = _rms(x_ref[...], g_ref[...]).astype(BF16)
    lo = 0
    for ref, width in zip(out_refs, _IN_SPLITS):
        ref[...] = jnp.dot(h, w_ref[:, lo:lo + width], preferred_element_type=F32) + b_ref[:, lo:lo + width]
        lo += width


def _inproj(x2, g, w_bf, b):
    t = x2.shape[0]
    row = lambda i: (i, 0)
    fixed = lambda i: (0, 0)
    return pl.pallas_call(
        _inproj_kernel,
        grid=(t // ROW_TILE,),
        in_specs=[pl.BlockSpec((ROW_TILE, D_MODEL), row),
                  pl.BlockSpec((1, D_MODEL), fixed),
                  pl.BlockSpec((D_MODEL, IN_WIDTH), fixed),
                  pl.BlockSpec((1, IN_WIDTH), fixed)],
        out_specs=[pl.BlockSpec((ROW_TILE, w), row) for w in _IN_SPLITS],
        out_shape=[jax.ShapeDtypeStruct((t, w), F32) for w in _IN_SPLITS],
        compiler_params=_params("parallel"),
        name="inproj",
    )(x2, g, w_bf, b)


def _attn_kernel(sink_ref, q_ref, kp_ref, kc_ref, vp_ref, vc_ref, g_ref, o_ref, acc_ref):
    n = pl.program_id(1)
    k2 = jnp.concatenate([kp_ref[...], kc_ref[...]], axis=0).astype(BF16)
    v2 = jnp.concatenate([vp_ref[...], vc_ref[...]], axis=0).astype(BF16)
    qi = lax.broadcasted_iota(jnp.int32, (WINDOW, 2 * WINDOW), 0)
    ki = lax.broadcasted_iota(jnp.int32, (WINDOW, 2 * WINDOW), 1)
    dist = WINDOW + qi - ki
    valid = (dist >= 0) & (dist < WINDOW) & ((ki >= WINDOW) | (n > 0))
    dist_f = dist.astype(F32)
    scale = ATTN_HEAD_DIM ** -0.5
    for head in range(ATTN_Q_HEADS):
        kv = head // ATTN_GROUP
        kh = k2[:, kv * ATTN_HEAD_DIM:(kv + 1) * ATTN_HEAD_DIM]
        vh = v2[:, kv * ATTN_HEAD_DIM:(kv + 1) * ATTN_HEAD_DIM]
        qh = q_ref[:, head * ATTN_HEAD_DIM:(head + 1) * ATTN_HEAD_DIM].astype(BF16)
        s = lax.dot_general(qh, kh, (((1,), (1,)), ((), ())), preferred_element_type=F32) * scale
        s = jnp.where(valid, s - _ALIBI[head] * dist_f, -jnp.inf)
        sink = sink_ref[head]
        m = jnp.maximum(jnp.max(s, axis=-1, keepdims=True), sink)
        p = jnp.exp(s - m)
        den = jnp.sum(p, axis=-1, keepdims=True) + jnp.exp(sink - m)
        probs = (p / den).astype(BF16)
        acc_ref[:, head * ATTN_HEAD_DIM:(head + 1) * ATTN_HEAD_DIM] = jnp.dot(
            probs, vh, preferred_element_type=F32)
    o_ref[...] = _rms(acc_ref[...], g_ref[...])


def _attention(aq, ak, av, sinks, g, batch, seq):
    nb = seq // WINDOW
    aq = aq.reshape(batch, seq, ATTN_WIDTH)
    ak = ak.reshape(batch, seq, ATTN_KV_WIDTH)
    av = av.reshape(batch, seq, ATTN_KV_WIDTH)
    cur = lambda b, n, s: (b, n, 0)
    prev = lambda b, n, s: (b, jnp.maximum(n - 1, 0), 0)
    out = pl.pallas_call(
        _attn_kernel,
        grid_spec=pltpu.PrefetchScalarGridSpec(
            num_scalar_prefetch=1,
            grid=(batch, nb),
            in_specs=[pl.BlockSpec((None, WINDOW, ATTN_WIDTH), cur),
                      pl.BlockSpec((None, WINDOW, ATTN_KV_WIDTH), prev),
                      pl.BlockSpec((None, WINDOW, ATTN_KV_WIDTH), cur),
                      pl.BlockSpec((None, WINDOW, ATTN_KV_WIDTH), prev),
                      pl.BlockSpec((None, WINDOW, ATTN_KV_WIDTH), cur),
                      pl.BlockSpec((1, ATTN_WIDTH), lambda b, n, s: (0, 0))],
            out_specs=pl.BlockSpec((None, WINDOW, ATTN_WIDTH), cur),
            scratch_shapes=[pltpu.VMEM((WINDOW, ATTN_WIDTH), F32)]),
        out_shape=jax.ShapeDtypeStruct((batch, seq, ATTN_WIDTH), F32),
        compiler_params=_params("parallel", "parallel"),
        name="attn",
    )(sinks, aq, ak, ak, av, av, g)
    return out.reshape(batch * seq, ATTN_WIDTH)


def _hgrn_kernel(q_ref, f_ref, i_ref, gate_ref, lb_ref, og_ref, o_ref, st_ref):
    c = pl.program_id(1)

    @pl.when(c == 0)
    def _():
        st_ref[...] = jnp.zeros_like(st_ref)

    C, S = HGRN_CHUNK, HGRN_SUB
    r_i = lax.broadcasted_iota(jnp.int32, (C, C), 0)
    c_i = lax.broadcasted_iota(jnp.int32, (C, C), 1)
    tri = (r_i >= c_i).astype(F32)
    rowblk = lax.broadcasted_iota(jnp.int32, (C, HGRN_DIM), 0) // S
    sub_r = lax.broadcasted_iota(jnp.int32, (S, HGRN_DIM), 0)
    for h in range(HGRN_HEADS):
        sl = slice(h * HGRN_DIM, (h + 1) * HGRN_DIM)
        qx = q_ref[:, sl]
        q = qx * jax.nn.sigmoid(qx)
        lb = lb_ref[:, sl]
        f = lb + (1.0 - lb) * jax.nn.sigmoid(f_ref[:, sl])
        logf = jnp.log(f)
        k = 1.0 - f
        v = i_ref[:, sl]
        b = jnp.dot(tri, logf, preferred_element_type=F32, precision=lax.Precision.HIGHEST)
        st = st_ref[h]
        o = lax.dot_general((q * jnp.exp(b)).astype(BF16), st.astype(BF16),
                            (((1,), (1,)), ((), ())), preferred_element_type=F32)
        qcat, kcat = [], []
        for i in range(1, C // S):
            bref = b[i * S:i * S + 1, :]
            qcat.append(jnp.where(rowblk == i, q * jnp.exp(jnp.minimum(b - bref, 0.0)), 0.0))
            kcat.append(jnp.where(rowblk < i, k * jnp.exp(jnp.minimum(bref - b, 0.0)), 0.0))
        qcat = jnp.concatenate(qcat, axis=1).astype(BF16)
        kcat = jnp.concatenate(kcat, axis=1).astype(BF16)
        att = lax.dot_general(qcat, kcat, (((1,), (1,)), ((), ())), preferred_element_type=F32)
        o = o + jnp.dot(att.astype(BF16), v.astype(BF16), preferred_element_type=F32)
        diag = []
        for i in range(C // S):
            b_blk = b[i * S:(i + 1) * S, :]
            q_blk = q[i * S:(i + 1) * S, :]
            acc = jnp.zeros((S, HGRN_DIM), F32)
            for s in range(S):
                r = i * S + s
                e = jnp.exp(jnp.minimum(b_blk - b[r:r + 1, :], 0.0))
                a = jnp.where(sub_r >= s, q_blk * e * k[r:r + 1, :], 0.0)
                acc = acc + jnp.sum(a, axis=-1, keepdims=True) * v[r:r + 1, :]
            diag.append(acc)
        o = o + jnp.concatenate(diag, axis=0)
        bl = b[C - 1:C, :]
        kd = (k * jnp.exp(bl - b)).astype(BF16)
        st_ref[h] = st * jnp.exp(bl) + lax.dot_general(
            v.astype(BF16), kd, (((0,), (0,)), ((), ())), preferred_element_type=F32)
        gx = gate_ref[:, sl]
        o = o * lax.rsqrt(jnp.mean(o * o, axis=-1, keepdims=True) + NORM_EPS) * og_ref[:, sl]
        o_ref[:, sl] = o * (gx * jax.nn.sigmoid(gx))


def _hgrn(hq, hf, hi, hg, lb, og, batch, seq):
    nc = seq // HGRN_CHUNK
    shp = (batch, seq, HGRN_WIDTH)
    blk = pl.BlockSpec((None, HGRN_CHUNK, HGRN_WIDTH), lambda b, c: (b, c, 0))
    vec = pl.BlockSpec((1, HGRN_WIDTH), lambda b, c: (0, 0))
    out = pl.pallas_call(
        _hgrn_kernel,
        grid=(batch, nc),
        in_specs=[blk, blk, blk, blk, vec, vec],
        out_specs=blk,
        out_shape=jax.ShapeDtypeStruct(shp, F32),
        scratch_shapes=[pltpu.VMEM((HGRN_HEADS, HGRN_DIM, HGRN_DIM), F32)],
        compiler_params=_params("parallel", "arbitrary"),
        name="hgrn",
    )(hq.reshape(shp), hf.reshape(shp), hi.reshape(shp), hg.reshape(shp), lb, og)
    return out.reshape(batch * seq, HGRN_WIDTH)


def _outproj_kernel(attn_ref, hg_ref, x_ref, wo_ref, bo_ref, g2_ref, rw_ref, rb_ref,
                    x1_ref, h2_ref, tope_ref, rank_ref, gate_ref, cnt_ref, carry_ref):
    step = pl.program_id(0)

    @pl.when(step == 0)
    def _():
        carry_ref[...] = jnp.zeros_like(carry_ref)

    tm = x_ref.shape[0]
    y = jnp.dot(attn_ref[...].astype(BF16), wo_ref[:ATTN_WIDTH, :], preferred_element_type=F32)
    y = y + jnp.dot(hg_ref[...].astype(BF16), wo_ref[ATTN_WIDTH:, :], preferred_element_type=F32)
    x1 = x_ref[...] + y + bo_ref[...]
    x1_ref[...] = x1
    h2 = _rms(x1, g2_ref[...])
    h2_ref[...] = h2
    h_hi = h2.astype(BF16)
    h_lo = (h2 - h_hi.astype(F32)).astype(BF16)
    p_hi = jnp.dot(h_hi, rw_ref[...], preferred_element_type=F32)
    p_lo = jnp.dot(h_lo, rw_ref[...], preferred_element_type=F32)
    logits = (p_hi[:, :N_EXPERTS] + p_hi[:, LANES:LANES + N_EXPERTS]) + p_lo[:, :N_EXPERTS] + rb_ref[...]
    lane = lax.broadcasted_iota(jnp.int32, (tm, N_EXPERTS), 1)
    work = logits
    picks, vals, hots = [], [], []
    for _ in range(TOP_K):
        m = jnp.max(work, axis=-1, keepdims=True)
        idx = jnp.min(jnp.where(work == m, lane, N_EXPERTS), axis=-1, keepdims=True)
        hot = lane == idx
        picks.append(idx)
        vals.append(m)
        hots.append(hot)
        work = jnp.where(hot, -jnp.inf, work)
    ex = [jnp.exp(v - vals[0]) for v in vals]
    den = ex[0] + ex[1] + ex[2] + ex[3]
    sel = jnp.zeros((tm, N_EXPERTS), F32)
    for hot in hots:
        sel = sel + hot.astype(F32)
    r_i = lax.broadcasted_iota(jnp.int32, (tm, tm), 0)
    c_i = lax.broadcasted_iota(jnp.int32, (tm, tm), 1)
    strict = (r_i > c_i).astype(BF16)
    ranks = jnp.dot(strict, sel.astype(BF16), preferred_element_type=F32) + carry_ref[...]
    carry_ref[...] = carry_ref[...] + jnp.sum(sel, axis=0, keepdims=True)
    cnt_ref[...] = carry_ref[...]
    l4 = lax.broadcasted_iota(jnp.int32, (tm, TOP_K), 1)
    tope = jnp.zeros((tm, TOP_K), jnp.int32)
    rank4 = jnp.zeros((tm, TOP_K), F32)
    gate4 = jnp.zeros((tm, TOP_K), F32)
    for j in range(TOP_K):
        rj = jnp.sum(jnp.where(hots[j], ranks, 0.0), axis=-1, keepdims=True)
        tope = jnp.where(l4 == j, picks[j], tope)
        rank4 = jnp.where(l4 == j, rj, rank4)
        gate4 = jnp.where(l4 == j, ex[j] / den, gate4)
    tope_ref[...] = tope
    rank_ref[...] = rank4.astype(jnp.int32)
    gate_ref[...] = gate4


def _outproj(attn, hgo, x2, wo_bf, bo, g2, rw, rb):
    t = x2.shape[0]
    row = lambda i: (i, 0)
    fixed = lambda i: (0, 0)
    return pl.pallas_call(
        _outproj_kernel,
        grid=(t // ROW_TILE,),
        in_specs=[pl.BlockSpec((ROW_TILE, ATTN_WIDTH), row),
                  pl.BlockSpec((ROW_TILE, HGRN_WIDTH), row),
                  pl.BlockSpec((ROW_TILE, D_MODEL), row),
                  pl.BlockSpec((ATTN_WIDTH + HGRN_WIDTH, D_MODEL), fixed),
                  pl.BlockSpec((1, D_MODEL), fixed),
                  pl.BlockSpec((1, D_MODEL), fixed),
                  pl.BlockSpec((D_MODEL, 2 * LANES), fixed),
                  pl.BlockSpec((1, N_EXPERTS), fixed)],
        out_specs=[pl.BlockSpec((ROW_TILE, D_MODEL), row),
                   pl.BlockSpec((ROW_TILE, D_MODEL), row),
                   pl.BlockSpec((ROW_TILE, TOP_K), row),
                   pl.BlockSpec((ROW_TILE, TOP_K), row),
                   pl.BlockSpec((ROW_TILE, TOP_K), row),
                   pl.BlockSpec((1, N_EXPERTS), fixed)],
        out_shape=[jax.ShapeDtypeStruct((t, D_MODEL), F32),
                   jax.ShapeDtypeStruct((t, D_MODEL), F32),
                   jax.ShapeDtypeStruct((t, TOP_K), jnp.int32),
                   jax.ShapeDtypeStruct((t, TOP_K), jnp.int32),
                   jax.ShapeDtypeStruct((t, TOP_K), F32),
                   jax.ShapeDtypeStruct((1, N_EXPERTS), F32)],
        scratch_shapes=[pltpu.VMEM((1, N_EXPERTS), F32)],
        compiler_params=_params("arbitrary"),
        name="outproj",
    )(attn, hgo, x2, wo_bf, bo, g2, rw, rb)


def _row_copy(src_ref, s, dst_ref, d, sem):
    return pltpu.make_async_copy(src_ref.at[pl.ds(s, 1), :], dst_ref.at[pl.ds(d, 1), :], sem)


def _dispatch_kernel(dest_ref, h_ref, xs_in_ref, xs_ref, sem):
    del xs_in_ref
    base = pl.program_id(0) * (DISPATCH_TILE * TOP_K)

    def issue(t, carry):
        for j in range(TOP_K):
            _row_copy(h_ref, t, xs_ref, dest_ref[base + t * TOP_K + j], sem).start()
        return carry

    lax.fori_loop(0, DISPATCH_TILE, issue, 0)

    def drain(t, carry):
        for j in range(TOP_K):
            _row_copy(h_ref, 0, xs_ref, 0, sem).wait()
        return carry

    lax.fori_loop(0, DISPATCH_TILE, drain, 0)


def _dispatch(dest_flat, h2, n_rows):
    t = h2.shape[0]
    xs0 = jnp.zeros((n_rows, D_MODEL), F32)
    return pl.pallas_call(
        _dispatch_kernel,
        grid_spec=pltpu.PrefetchScalarGridSpec(
            num_scalar_prefetch=1,
            grid=(t // DISPATCH_TILE,),
            in_specs=[pl.BlockSpec((DISPATCH_TILE, D_MODEL), lambda i, d: (i, 0)),
                      pl.BlockSpec(memory_space=pl.ANY)],
            out_specs=pl.BlockSpec(memory_space=pl.ANY),
            scratch_shapes=[pltpu.SemaphoreType.DMA(())]),
        out_shape=jax.ShapeDtypeStruct((n_rows, D_MODEL), F32),
        input_output_aliases={2: 0},
        compiler_params=_params("arbitrary"),
        name="dispatch",
    )(dest_flat, h2, xs0)


CAST_ROWS = 128


def _expert_kernel(be_ref, nb_ref, xs_ref, w1_ref, b1_ref, w2_ref, b2_ref, y_ref, w1b_ref, w2i_ref, w2b_ref):
    blk = pl.program_id(0)
    half = LANES // 2

    @pl.when((blk == 0) | (be_ref[blk] != be_ref[jnp.maximum(blk - 1, 0)]))
    def _():
        def cast_rows(r, carry):
            rows = pl.ds(pl.multiple_of(r * CAST_ROWS, CAST_ROWS), CAST_ROWS)
            w1b_ref[rows, :] = w1_ref[rows, :].astype(BF16)
            return carry

        lax.fori_loop(0, D_MODEL // CAST_ROWS, cast_rows, 0)
        for c in range(D_MODEL // LANES):
            cols = slice(c * LANES, (c + 1) * LANES)
            for m in range(EXPERT_FF // LANES):
                lo = m * LANES
                w2i_ref[c, pl.ds(lo, half, stride=2), :] = w2_ref[lo:lo + half, cols]
                w2i_ref[c, pl.ds(lo + 1, half, stride=2), :] = w2_ref[lo + half:lo + LANES, cols]
            w2b_ref[:, cols] = w2i_ref[c].astype(BF16)

    @pl.when(blk < nb_ref[0])
    def _():
        x = xs_ref[...].astype(BF16)
        hid = jnp.dot(x, w1b_ref[...], preferred_element_type=F32) + b1_ref[...]
        even = (lax.broadcasted_iota(jnp.int32, (x.shape[0], LANES), 1) & 1) == 0
        glu, lin = [], []
        for m in range(EXPERT_FF // LANES):
            ha = hid[:, 2 * m * LANES:(2 * m + 1) * LANES]
            hb = hid[:, (2 * m + 1) * LANES:(2 * m + 2) * LANES]
            glu.append(jnp.where(even, ha, pltpu.roll(hb, 1, axis=1)))
            lin.append(jnp.where(even, pltpu.roll(ha, LANES - 1, axis=1), hb))
        glu = jnp.minimum(jnp.concatenate(glu, axis=1), SWIGLU_LIMIT)
        lin = jnp.clip(jnp.concatenate(lin, axis=1), -SWIGLU_LIMIT, SWIGLU_LIMIT)
        act = glu * jax.nn.sigmoid(SWIGLU_ALPHA * glu) * (lin + 1.0)
        y_ref[...] = jnp.dot(act.astype(BF16), w2b_ref[...], preferred_element_type=F32) + b2_ref[...]

    @pl.when(blk >= nb_ref[0])
    def _():
        y_ref[...] = jnp.zeros_like(y_ref)


def _experts(block_e, n_used, xs, w1, b1, w2, b2):
    n_rows = xs.shape[0]
    nblk = n_rows // EXPERT_BLOCK
    rows = lambda b, be, nb: (b, 0)
    wsel = lambda b, be, nb: (be[b], 0, 0)
    return pl.pallas_call(
        _expert_kernel,
        grid_spec=pltpu.PrefetchScalarGridSpec(
            num_scalar_prefetch=2,
            grid=(nblk,),
            in_specs=[pl.BlockSpec((EXPERT_BLOCK, D_MODEL), rows),
                      pl.BlockSpec((None, D_MODEL, 2 * EXPERT_FF), wsel),
                      pl.BlockSpec((None, 1, 2 * EXPERT_FF), wsel),
                      pl.BlockSpec((None, EXPERT_FF, D_MODEL), wsel),
                      pl.BlockSpec((None, 1, D_MODEL), wsel)],
            out_specs=pl.BlockSpec((EXPERT_BLOCK, D_MODEL), rows),
            scratch_shapes=[pltpu.VMEM((D_MODEL, 2 * EXPERT_FF), BF16),
                            pltpu.VMEM((D_MODEL // LANES, EXPERT_FF, LANES), F32),
                            pltpu.VMEM((EXPERT_FF, D_MODEL), BF16)]),
        out_shape=jax.ShapeDtypeStruct((n_rows, D_MODEL), F32),
        compiler_params=_params("arbitrary"),
        name="experts",
    )(block_e, n_used, xs, w1, b1, w2, b2)


def _combine_kernel(final_norm, dest_ref, yb_ref, x1_ref, gate_ref, g_ref, o_ref, buf_ref, sem):
    base = pl.program_id(0) * (DISPATCH_TILE * TOP_K)

    def issue(t, carry):
        for j in range(TOP_K):
            pltpu.make_async_copy(yb_ref.at[pl.ds(dest_ref[base + t * TOP_K + j], 1), :],
                                  buf_ref.at[j, pl.ds(t, 1), :], sem).start()
        return carry

    lax.fori_loop(0, DISPATCH_TILE, issue, 0)

    def drain(t, carry):
        for j in range(TOP_K):
            pltpu.make_async_copy(yb_ref.at[pl.ds(0, 1), :], buf_ref.at[0, pl.ds(0, 1), :], sem).wait()
        return carry

    lax.fori_loop(0, DISPATCH_TILE, drain, 0)
    gates = gate_ref[...]
    y = x1_ref[...]
    for j in range(TOP_K):
        y = y + buf_ref[j] * gates[:, j:j + 1]
    o_ref[...] = _rms(y, g_ref[...]) if final_norm else y


def _combine(dest_flat, yb, x1, gates, g, final_norm):
    t = x1.shape[0]
    return pl.pallas_call(
        functools.partial(_combine_kernel, final_norm),
        grid_spec=pltpu.PrefetchScalarGridSpec(
            num_scalar_prefetch=1,
            grid=(t // DISPATCH_TILE,),
            in_specs=[pl.BlockSpec(memory_space=pl.ANY),
                      pl.BlockSpec((DISPATCH_TILE, D_MODEL), lambda i, d: (i, 0)),
                      pl.BlockSpec((DISPATCH_TILE, TOP_K), lambda i, d: (i, 0)),
                      pl.BlockSpec((1, D_MODEL), lambda i, d: (0, 0))],
            out_specs=pl.BlockSpec((DISPATCH_TILE, D_MODEL), lambda i, d: (i, 0)),
            scratch_shapes=[pltpu.VMEM((TOP_K, DISPATCH_TILE, D_MODEL), F32),
                            pltpu.SemaphoreType.DMA(())]),
        out_shape=jax.ShapeDtypeStruct((t, D_MODEL), F32),
        compiler_params=_params("arbitrary"),
        name="combine",
    )(dest_flat, yb, x1, gates, g)


def kernel(x, norm1_g, w_in, b_in, attn_sinks, attn_out_g, hgrn_lb_logits, hgrn_out_g, w_out, b_out,
           norm2_g, router_w, router_b, w1, b1, w2, b2, final_g):
    batch, seq, d = x.shape
    t = batch * seq
    depth = w_in.shape[0]
    lower_bounds = jnp.cumsum(jax.nn.softmax(hgrn_lb_logits.astype(F32), axis=0), axis=0)
    n_rows = (t * TOP_K) // EXPERT_BLOCK * EXPERT_BLOCK + N_EXPERTS * EXPERT_BLOCK
    nblk = n_rows // EXPERT_BLOCK
    x2 = x.reshape(t, d)
    for l in range(depth):
        aq, ak, av, hq, hf, hi, hg = _inproj(x2, norm1_g[l][None], w_in[l].astype(BF16), b_in[l][None])
        attn = _attention(aq, ak, av, attn_sinks[l], attn_out_g[l][None], batch, seq)
        hgo = _hgrn(hq, hf, hi, hg, lower_bounds[l][None], hgrn_out_g[l][None], batch, seq)
        rw_hi = router_w[l].astype(BF16)
        rw_lo = (router_w[l] - rw_hi.astype(F32)).astype(BF16)
        rw_cat = jnp.zeros((d, 2 * LANES), BF16)
        rw_cat = rw_cat.at[:, :N_EXPERTS].set(rw_hi).at[:, LANES:LANES + N_EXPERTS].set(rw_lo)
        x1, h2, top_e, rank4, gates, counts = _outproj(
            attn, hgo, x2, w_out[l].astype(BF16), b_out[l][None], norm2_g[l][None],
            rw_cat, router_b[l][None])
        cnt = counts[0].astype(jnp.int32)
        padded = (cnt + EXPERT_BLOCK - 1) // EXPERT_BLOCK * EXPERT_BLOCK
        ends = jnp.cumsum(padded)
        pstart = ends - padded
        hot = top_e[:, :, None] == jnp.arange(N_EXPERTS, dtype=jnp.int32)
        dest = rank4 + jnp.sum(jnp.where(hot, pstart, 0), axis=-1)
        dest_flat = dest.reshape(t * TOP_K)
        blk_start = jnp.arange(nblk, dtype=jnp.int32) * EXPERT_BLOCK
        block_e = jnp.minimum(jnp.sum(blk_start[:, None] >= ends[None, :], axis=-1), N_EXPERTS - 1)
        n_used = (ends[-1] // EXPERT_BLOCK).reshape(1)
        xs = _dispatch(dest_flat, h2, n_rows)
        yb = _experts(block_e.astype(jnp.int32), n_used.astype(jnp.int32), xs,
                      w1[l], b1[l][:, None, :], w2[l], b2[l][:, None, :])
        x2 = _combine(dest_flat, yb, x1, gates, final_g[None], l == depth - 1)
    return x2.reshape(batch, seq, d)
```

```python
import functools

import numpy as np
import jax
import jax.numpy as jnp
from jax import lax
from jax.experimental import pallas as pl
from jax.experimental.pallas import tpu as pltpu

F32 = jnp.float32
BF16 = jnp.bfloat16

D_MODEL = 1024
ATTN_Q_HEADS = 8
ATTN_KV_HEADS = 2
ATTN_HEAD_DIM = 64
ATTN_GROUP = ATTN_Q_HEADS // ATTN_KV_HEADS
ATTN_WIDTH = ATTN_Q_HEADS * ATTN_HEAD_DIM
ATTN_KV_WIDTH = ATTN_KV_HEADS * ATTN_HEAD_DIM
WINDOW = 128
HGRN_HEADS = 4
HGRN_DIM = 128
HGRN_WIDTH = HGRN_HEADS * HGRN_DIM
HGRN_CHUNK = 64
HGRN_SUB = 16
IN_WIDTH = ATTN_WIDTH + 2 * ATTN_KV_WIDTH + 4 * HGRN_WIDTH
N_EXPERTS = 32
TOP_K = 4
EXPERT_FF = D_MODEL
SWIGLU_LIMIT = 7.0
SWIGLU_ALPHA = 1.702
NORM_EPS = 1e-5

LANES = 128
ROW_TILE = 512
EXPERT_BLOCK = 256
SORT_TILE = 256
SEG_ROWS = 8
SORT_SLOTS = 1280
VMEM_LIMIT = 56 * 1024 * 1024

_ALIBI = [float(2.0 ** (-8.0 * (h + 1) / ATTN_Q_HEADS)) for h in range(ATTN_Q_HEADS)]


def _rms(x, g):
    return x * lax.rsqrt(jnp.mean(x * x, axis=-1, keepdims=True) + NORM_EPS) * g


def _params(*sem):
    return pltpu.CompilerParams(dimension_semantics=sem, vmem_limit_bytes=VMEM_LIMIT)


_IN_SPLITS = (ATTN_WIDTH, ATTN_KV_WIDTH, ATTN_KV_WIDTH, HGRN_WIDTH, HGRN_WIDTH, HGRN_WIDTH, HGRN_WIDTH)


def _inproj_kernel(x_ref, g_ref, w_ref, b_ref, *out_refs):
    h = _rms(x_ref[...], g_ref[...]).astype(BF16)
    lo = 0
    for ref, width in zip(out_refs, _IN_SPLITS):
        ref[...] = jnp.dot(h, w_ref[:, lo:lo + width], preferred_element_type=F32) + b_ref[:, lo:lo + width]
        lo += width


def _inproj(x2, g, w_bf, b):
    t = x2.shape[0]
    row = lambda i: (i, 0)
    fixed = lambda i: (0, 0)
    return pl.pallas_call(
        _inproj_kernel,
        grid=(t // ROW_TILE,),
        in_specs=[pl.BlockSpec((ROW_TILE, D_MODEL), row),
                  pl.BlockSpec((1, D_MODEL), fixed),
                  pl.BlockSpec((D_MODEL, IN_WIDTH), fixed),
                  pl.BlockSpec((1, IN_WIDTH), fixed)],
        out_specs=[pl.BlockSpec((ROW_TILE, w), row) for w in _IN_SPLITS],
        out_shape=[jax.ShapeDtypeStruct((t, w), F32) for w in _IN_SPLITS],
        compiler_params=_params("parallel"),
        name="inproj",
    )(x2, g, w_bf, b)


def _attn_kernel(sink_ref, q_ref, kp_ref, kc_ref, vp_ref, vc_ref, g_ref, o_ref, acc_ref):
    n = pl.program_id(1)
    k2 = jnp.concatenate([kp_ref[...], kc_ref[...]], axis=0).astype(BF16)
    v2 = jnp.concatenate([vp_ref[...], vc_ref[...]], axis=0).astype(BF16)
    qi = lax.broadcasted_iota(jnp.int32, (WINDOW, 2 * WINDOW), 0)
    ki = lax.broadcasted_iota(jnp.int32, (WINDOW, 2 * WINDOW), 1)
    dist = WINDOW + qi - ki
    valid = (dist >= 0) & (dist < WINDOW) & ((ki >= WINDOW) | (n > 0))
    dist_f = dist.astype(F32)
    scale = ATTN_HEAD_DIM ** -0.5
    for head in range(ATTN_Q_HEADS):
        kv = head // ATTN_GROUP
        kh = k2[:, kv * ATTN_HEAD_DIM:(kv + 1) * ATTN_HEAD_DIM]
        vh = v2[:, kv * ATTN_HEAD_DIM:(kv + 1) * ATTN_HEAD_DIM]
        qh = q_ref[:, head * ATTN_HEAD_DIM:(head + 1) * ATTN_HEAD_DIM].astype(BF16)
        s = lax.dot_general(qh, kh, (((1,), (1,)), ((), ())), preferred_element_type=F32) * scale
        s = jnp.where(valid, s - _ALIBI[head] * dist_f, -jnp.inf)
        sink = sink_ref[head]
        m = jnp.maximum(jnp.max(s, axis=-1, keepdims=True), sink)
        p = jnp.exp(s - m)
        den = jnp.sum(p, axis=-1, keepdims=True) + jnp.exp(sink - m)
        probs = (p / den).astype(BF16)
        acc_ref[:, head * ATTN_HEAD_DIM:(head + 1) * ATTN_HEAD_DIM] = jnp.dot(
            probs, vh, preferred_element_type=F32)
    o_ref[...] = _rms(acc_ref[...], g_ref[...])


def _attention(aq, ak, av, sinks, g, batch, seq):
    nb = seq // WINDOW
    aq = aq.reshape(batch, seq, ATTN_WIDTH)
    ak = ak.reshape(batch, seq, ATTN_KV_WIDTH)
    av = av.reshape(batch, seq, ATTN_KV_WIDTH)
    cur = lambda b, n, s: (b, n, 0)
    prev = lambda b, n, s: (b, jnp.maximum(n - 1, 0), 0)
    out = pl.pallas_call(
        _attn_kernel,
        grid_spec=pltpu.PrefetchScalarGridSpec(
            num_scalar_prefetch=1,
            grid=(batch, nb),
            in_specs=[pl.BlockSpec((None, WINDOW, ATTN_WIDTH), cur),
                      pl.BlockSpec((None, WINDOW, ATTN_KV_WIDTH), prev),
                      pl.BlockSpec((None, WINDOW, ATTN_KV_WIDTH), cur),
                      pl.BlockSpec((None, WINDOW, ATTN_KV_WIDTH), prev),
                      pl.BlockSpec((None, WINDOW, ATTN_KV_WIDTH), cur),
                      pl.BlockSpec((1, ATTN_WIDTH), lambda b, n, s: (0, 0))],
            out_specs=pl.BlockSpec((None, WINDOW, ATTN_WIDTH), cur),
            scratch_shapes=[pltpu.VMEM((WINDOW, ATTN_WIDTH), F32)]),
        out_shape=jax.ShapeDtypeStruct((batch, seq, ATTN_WIDTH), F32),
        compiler_params=_params("parallel", "parallel"),
        name="attn",
    )(sinks, aq, ak, ak, av, av, g)
    return out.reshape(batch * seq, ATTN_WIDTH)


def _hgrn_kernel(q_ref, f_ref, i_ref, gate_ref, lb_ref, og_ref, o_ref, st_ref):
    c = pl.program_id(1)

    @pl.when(c == 0)
    def _():
        st_ref[...] = jnp.zeros_like(st_ref)

    C, S = HGRN_CHUNK, HGRN_SUB
    r_i = lax.broadcasted_iota(jnp.int32, (C, C), 0)
    c_i = lax.broadcasted_iota(jnp.int32, (C, C), 1)
    tri = (r_i >= c_i).astype(F32)
    rowblk = lax.broadcasted_iota(jnp.int32, (C, HGRN_DIM), 0) // S
    sub_r = lax.broadcasted_iota(jnp.int32, (S, HGRN_DIM), 0)
    for h in range(HGRN_HEADS):
        sl = slice(h * HGRN_DIM, (h + 1) * HGRN_DIM)
        qx = q_ref[:, sl]
        q = qx * jax.nn.sigmoid(qx)
        lb = lb_ref[:, sl]
        f = lb + (1.0 - lb) * jax.nn.sigmoid(f_ref[:, sl])
        logf = jnp.log(f)
        k = 1.0 - f
        v = i_ref[:, sl]
        b = jnp.dot(tri, logf, preferred_element_type=F32, precision=lax.Precision.HIGHEST)
        st = st_ref[h]
        o = lax.dot_general((q * jnp.exp(b)).astype(BF16), st.astype(BF16),
                            (((1,), (1,)), ((), ())), preferred_element_type=F32)
        qcat, kcat = [], []
        for i in range(1, C // S):
            bref = b[i * S:i * S + 1, :]
            qcat.append(jnp.where(rowblk == i, q * jnp.exp(jnp.minimum(b - bref, 0.0)), 0.0))
            kcat.append(jnp.where(rowblk < i, k * jnp.exp(jnp.minimum(bref - b, 0.0)), 0.0))
        qcat = jnp.concatenate(qcat, axis=1).astype(BF16)
        kcat = jnp.concatenate(kcat, axis=1).astype(BF16)
        att = lax.dot_general(qcat, kcat, (((1,), (1,)), ((), ())), preferred_element_type=F32)
        o = o + jnp.dot(att.astype(BF16), v.astype(BF16), preferred_element_type=F32)
        diag = []
        for i in range(C // S):
            b_blk = b[i * S:(i + 1) * S, :]
            q_blk = q[i * S:(i + 1) * S, :]
            acc = jnp.zeros((S, HGRN_DIM), F32)
            for s in range(S):
                r = i * S + s
                e = jnp.exp(jnp.minimum(b_blk - b[r:r + 1, :], 0.0))
                a = jnp.where(sub_r >= s, q_blk * e * k[r:r + 1, :], 0.0)
                acc = acc + jnp.sum(a, axis=-1, keepdims=True) * v[r:r + 1, :]
            diag.append(acc)
        o = o + jnp.concatenate(diag, axis=0)
        bl = b[C - 1:C, :]
        kd = (k * jnp.exp(bl - b)).astype(BF16)
        st_ref[h] = st * jnp.exp(bl) + lax.dot_general(
            v.astype(BF16), kd, (((0,), (0,)), ((), ())), preferred_element_type=F32)
        gx = gate_ref[:, sl]
        o = o * lax.rsqrt(jnp.mean(o * o, axis=-1, keepdims=True) + NORM_EPS) * og_ref[:, sl]
        o_ref[:, sl] = o * (gx * jax.nn.sigmoid(gx))


def _hgrn(hq, hf, hi, hg, lb, og, batch, seq):
    nc = seq // HGRN_CHUNK
    shp = (batch, seq, HGRN_WIDTH)
    blk = pl.BlockSpec((None, HGRN_CHUNK, HGRN_WIDTH), lambda b, c: (b, c, 0))
    vec = pl.BlockSpec((1, HGRN_WIDTH), lambda b, c: (0, 0))
    out = pl.pallas_call(
        _hgrn_kernel,
        grid=(batch, nc),
        in_specs=[blk, blk, blk, blk, vec, vec],
        out_specs=blk,
        out_shape=jax.ShapeDtypeStruct(shp, F32),
        scratch_shapes=[pltpu.VMEM((HGRN_HEADS, HGRN_DIM, HGRN_DIM), F32)],
        compiler_params=_params("parallel", "arbitrary"),
        name="hgrn",
    )(hq.reshape(shp), hf.reshape(shp), hi.reshape(shp), hg.reshape(shp), lb, og)
    return out.reshape(batch * seq, HGRN_WIDTH)


def _outproj_kernel(attn_ref, hg_ref, x_ref, wo_ref, bo_ref, g2_ref, rw_ref, rb_ref,
                    x1_ref, h2_ref, lpos_ref, gate_ref, seg_ref):
    tm = x_ref.shape[0]
    y = jnp.dot(attn_ref[...].astype(BF16), wo_ref[:ATTN_WIDTH, :], preferred_element_type=F32)
    y = y + jnp.dot(hg_ref[...].astype(BF16), wo_ref[ATTN_WIDTH:, :], preferred_element_type=F32)
    x1 = x_ref[...] + y + bo_ref[...]
    x1_ref[...] = x1
    h2 = _rms(x1, g2_ref[...])
    h2_ref[...] = h2.astype(BF16)
    h_hi = h2.astype(BF16)
    h_lo = (h2 - h_hi.astype(F32)).astype(BF16)
    p_hi = jnp.dot(h_hi, rw_ref[...], preferred_element_type=F32)
    p_lo = jnp.dot(h_lo, rw_ref[...], preferred_element_type=F32)
    logits = (p_hi[:, :N_EXPERTS] + p_hi[:, LANES:LANES + N_EXPERTS]) + p_lo[:, :N_EXPERTS] + rb_ref[...]
    lane = lax.broadcasted_iota(jnp.int32, (tm, N_EXPERTS), 1)
    work = logits
    vals, hots = [], []
    for _ in range(TOP_K):
        m = jnp.max(work, axis=-1, keepdims=True)
        idx = jnp.min(jnp.where(work == m, lane, N_EXPERTS), axis=-1, keepdims=True)
        hot = lane == idx
        vals.append(m)
        hots.append(hot)
        work = jnp.where(hot, -jnp.inf, work)
    ex = [jnp.exp(v - vals[0]) for v in vals]
    den = ex[0] + ex[1] + ex[2] + ex[3]
    sel = jnp.zeros((tm, N_EXPERTS), F32)
    for hot in hots:
        sel = sel + hot.astype(F32)
    r_i = lax.broadcasted_iota(jnp.int32, (tm, tm), 0)
    c_i = lax.broadcasted_iota(jnp.int32, (tm, tm), 1)
    strict = jnp.where(r_i > c_i, 1.0, 0.0).astype(BF16)
    ranks = jnp.dot(strict, sel.astype(BF16), preferred_element_type=F32)
    seg = jnp.floor((jnp.sum(sel, axis=0, keepdims=True) + (SEG_ROWS - 1.0)) * (1.0 / SEG_ROWS)) * SEG_ROWS
    e_r = lax.broadcasted_iota(jnp.int32, (N_EXPERTS, N_EXPERTS), 0)
    e_c = lax.broadcasted_iota(jnp.int32, (N_EXPERTS, N_EXPERTS), 1)
    before = jnp.where(e_r < e_c, 1.0, 0.0).astype(BF16)
    seg_off = jnp.dot(jnp.broadcast_to(seg, (SEG_ROWS, N_EXPERTS)).astype(BF16), before,
                      preferred_element_type=F32)[0:1, :]
    slots = ranks + seg_off
    l4 = lax.broadcasted_iota(jnp.int32, (tm, TOP_K), 1)
    lpos4 = jnp.zeros((tm, TOP_K), F32)
    gate4 = jnp.zeros((tm, TOP_K), F32)
    for j in range(TOP_K):
        sj = jnp.sum(jnp.where(hots[j], slots, 0.0), axis=-1, keepdims=True)
        lpos4 = jnp.where(l4 == j, sj, lpos4)
        gate4 = jnp.where(l4 == j, ex[j] / den, gate4)
    lpos_ref[...] = lpos4.astype(jnp.int32)
    gate_ref[...] = gate4
    seg_ref[...] = seg


def _outproj(attn, hgo, x2, wo_bf, bo, g2, rw, rb):
    t = x2.shape[0]
    row = lambda i: (i, 0)
    fixed = lambda i: (0, 0)
    return pl.pallas_call(
        _outproj_kernel,
        grid=(t // SORT_TILE,),
        in_specs=[pl.BlockSpec((SORT_TILE, ATTN_WIDTH), row),
                  pl.BlockSpec((SORT_TILE, HGRN_WIDTH), row),
                  pl.BlockSpec((SORT_TILE, D_MODEL), row),
                  pl.BlockSpec((ATTN_WIDTH + HGRN_WIDTH, D_MODEL), fixed),
                  pl.BlockSpec((1, D_MODEL), fixed),
                  pl.BlockSpec((1, D_MODEL), fixed),
                  pl.BlockSpec((D_MODEL, 2 * LANES), fixed),
                  pl.BlockSpec((1, N_EXPERTS), fixed)],
        out_specs=[pl.BlockSpec((SORT_TILE, D_MODEL), row),
                   pl.BlockSpec((SORT_TILE, D_MODEL), row),
                   pl.BlockSpec((SORT_TILE, TOP_K), row),
                   pl.BlockSpec((SORT_TILE, TOP_K), row),
                   pl.BlockSpec((None, 1, N_EXPERTS), lambda i: (i, 0, 0))],
        out_shape=[jax.ShapeDtypeStruct((t, D_MODEL), F32),
                   jax.ShapeDtypeStruct((t, D_MODEL), BF16),
                   jax.ShapeDtypeStruct((t, TOP_K), jnp.int32),
                   jax.ShapeDtypeStruct((t, TOP_K), F32),
                   jax.ShapeDtypeStruct((t // SORT_TILE, 1, N_EXPERTS), F32)],
        compiler_params=_params("parallel"),
        name="outproj",
    )(attn, hgo, x2, wo_bf, bo, g2, rw, rb)


def _rows8(start):
    return pl.ds(pl.multiple_of(start, SEG_ROWS), SEG_ROWS)


def _for_each_piece(cnt, fn):
    def body(j, carry):
        fn(j * SEG_ROWS)
        return carry

    lax.fori_loop(0, cnt, body, 0)


def _dispatch_kernel(loc_ref, glob_ref, cnt_ref, tot_ref, zdst_ref, zcnt_ref,
                     h_ref, lpos_t_ref, xs_ref, lbuf_ref, zbuf_ref, sem, zsem):
    i = pl.program_id(0)
    last = pl.num_programs(0) - 1
    buf = i % 2

    def piece(b, local, glob):
        return pltpu.make_async_copy(lbuf_ref.at[b, _rows8(local), :], xs_ref.at[_rows8(glob), :], sem.at[b])

    def drain(step):
        _for_each_piece(tot_ref[step], lambda off: piece(step % 2, 0, 0).wait())

    def zero_piece(glob):
        return pltpu.make_async_copy(zbuf_ref, xs_ref.at[_rows8(glob), :], zsem)

    @pl.when(i == 0)
    def _():
        zbuf_ref[...] = jnp.zeros_like(zbuf_ref)
        for e in range(N_EXPERTS + 1):
            _for_each_piece(zcnt_ref[e], lambda off, e=e: zero_piece(zdst_ref[e] + off).start())
        for e in range(N_EXPERTS + 1):
            _for_each_piece(zcnt_ref[e], lambda off: zero_piece(0).wait())

    @pl.when(i >= 2)
    def _():
        drain(i - 2)

    slot = lax.broadcasted_iota(jnp.int32, (SORT_SLOTS, SORT_TILE), 0)
    onehot = jnp.zeros((SORT_SLOTS, SORT_TILE), F32)
    for k in range(TOP_K):
        onehot = onehot + jnp.where(slot == lpos_t_ref[k:k + 1, :], 1.0, 0.0)
    lbuf_ref[buf] = jnp.dot(onehot.astype(BF16), h_ref[...], preferred_element_type=F32)
    for e in range(N_EXPERTS):
        idx = i * N_EXPERTS + e
        _for_each_piece(cnt_ref[idx],
                        lambda off, idx=idx: piece(buf, loc_ref[idx] + off, glob_ref[idx] + off).start())

    @pl.when(i == last)
    def _():
        @pl.when(i >= 1)
        def _():
            drain(i - 1)

        drain(i)


def _dispatch(tables, zero_tables, h2, lpos_t, n_rows):
    t = h2.shape[0]
    tile = lambda i, *_: (i, 0)
    return pl.pallas_call(
        _dispatch_kernel,
        grid_spec=pltpu.PrefetchScalarGridSpec(
            num_scalar_prefetch=6,
            grid=(t // SORT_TILE,),
            in_specs=[pl.BlockSpec((SORT_TILE, D_MODEL), tile),
                      pl.BlockSpec((TOP_K, SORT_TILE), lambda i, *_: (0, i))],
            out_specs=pl.BlockSpec(memory_space=pl.ANY),
            scratch_shapes=[pltpu.VMEM((2, SORT_SLOTS, D_MODEL), F32),
                            pltpu.VMEM((SEG_ROWS, D_MODEL), F32),
                            pltpu.SemaphoreType.DMA((2,)),
                            pltpu.SemaphoreType.DMA(())]),
        out_shape=jax.ShapeDtypeStruct((n_rows, D_MODEL), F32),
        compiler_params=_params("arbitrary"),
        name="dispatch",
    )(*tables, *zero_tables, h2, lpos_t)


CAST_ROWS = 128


def _expert_kernel(be_ref, nb_ref, xs_ref, w1_ref, b1_ref, w2_ref, b2_ref, y_ref, w1b_ref, w2i_ref, w2b_ref):
    blk = pl.program_id(0)
    half = LANES // 2

    @pl.when((blk == 0) | (be_ref[blk] != be_ref[jnp.maximum(blk - 1, 0)]))
    def _():
        def cast_rows(r, carry):
            rows = pl.ds(pl.multiple_of(r * CAST_ROWS, CAST_ROWS), CAST_ROWS)
            w1b_ref[rows, :] = w1_ref[rows, :].astype(BF16)
            return carry

        lax.fori_loop(0, D_MODEL // CAST_ROWS, cast_rows, 0)
        for c in range(D_MODEL // LANES):
            cols = slice(c * LANES, (c + 1) * LANES)
            for m in range(EXPERT_FF // LANES):
                lo = m * LANES
                w2i_ref[c, pl.ds(lo, half, stride=2), :] = w2_ref[lo:lo + half, cols]
                w2i_ref[c, pl.ds(lo + 1, half, stride=2), :] = w2_ref[lo + half:lo + LANES, cols]
            w2b_ref[:, cols] = w2i_ref[c].astype(BF16)

    @pl.when(blk < nb_ref[0])
    def _():
        x = xs_ref[...].astype(BF16)
        hid = jnp.dot(x, w1b_ref[...], preferred_element_type=F32) + b1_ref[...]
        even = (lax.broadcasted_iota(jnp.int32, (x.shape[0], LANES), 1) & 1) == 0
        glu, lin = [], []
        for m in range(EXPERT_FF // LANES):
            ha = hid[:, 2 * m * LANES:(2 * m + 1) * LANES]
            hb = hid[:, (2 * m + 1) * LANES:(2 * m + 2) * LANES]
            glu.append(jnp.where(even, ha, pltpu.roll(hb, 1, axis=1)))
            lin.append(jnp.where(even, pltpu.roll(ha, LANES - 1, axis=1), hb))
        glu = jnp.minimum(jnp.concatenate(glu, axis=1), SWIGLU_LIMIT)
        lin = jnp.clip(jnp.concatenate(lin, axis=1), -SWIGLU_LIMIT, SWIGLU_LIMIT)
        act = glu * jax.nn.sigmoid(SWIGLU_ALPHA * glu) * (lin + 1.0)
        y_ref[...] = jnp.dot(act.astype(BF16), w2b_ref[...], preferred_element_type=F32) + b2_ref[...]

    @pl.when(blk >= nb_ref[0])
    def _():
        y_ref[...] = jnp.zeros_like(y_ref)


def _experts(block_e, n_used, xs, w1, b1, w2, b2):
    n_rows = xs.shape[0]
    nblk = n_rows // EXPERT_BLOCK
    rows = lambda b, be, nb: (b, 0)
    used_rows = lambda b, be, nb: (jnp.minimum(b, nb[0] - 1), 0)
    wsel = lambda b, be, nb: (be[b], 0, 0)
    return pl.pallas_call(
        _expert_kernel,
        grid_spec=pltpu.PrefetchScalarGridSpec(
            num_scalar_prefetch=2,
            grid=(nblk,),
            in_specs=[pl.BlockSpec((EXPERT_BLOCK, D_MODEL), used_rows),
                      pl.BlockSpec((None, D_MODEL, 2 * EXPERT_FF), wsel),
                      pl.BlockSpec((None, 1, 2 * EXPERT_FF), wsel),
                      pl.BlockSpec((None, EXPERT_FF, D_MODEL), wsel),
                      pl.BlockSpec((None, 1, D_MODEL), wsel)],
            out_specs=pl.BlockSpec((EXPERT_BLOCK, D_MODEL), rows),
            scratch_shapes=[pltpu.VMEM((D_MODEL, 2 * EXPERT_FF), BF16),
                            pltpu.VMEM((D_MODEL // LANES, EXPERT_FF, LANES), F32),
                            pltpu.VMEM((EXPERT_FF, D_MODEL), BF16)]),
        out_shape=jax.ShapeDtypeStruct((n_rows, D_MODEL), F32),
        compiler_params=_params("arbitrary"),
        name="experts",
    )(block_e, n_used, xs, w1, b1, w2, b2)


def _combine_kernel(final_norm, loc_ref, glob_ref, cnt_ref, tot_ref,
                    yb_ref, x1_ref, lpos_ref, gate_ref, g_ref, o_ref, gbuf_ref, sem):
    i = pl.program_id(0)
    last = pl.num_programs(0) - 1
    buf = i % 2

    def piece(b, local, glob):
        return pltpu.make_async_copy(yb_ref.at[_rows8(glob), :], gbuf_ref.at[b, _rows8(local), :], sem.at[b])

    def fetch(step):
        for e in range(N_EXPERTS):
            idx = step * N_EXPERTS + e
            _for_each_piece(cnt_ref[idx], lambda off, idx=idx: piece(
                step % 2, loc_ref[idx] + off, glob_ref[idx] + off).start())

    @pl.when(i == 0)
    def _():
        gbuf_ref[...] = jnp.zeros_like(gbuf_ref)
        fetch(0)

    @pl.when(i < last)
    def _():
        fetch(i + 1)

    _for_each_piece(tot_ref[i], lambda off: piece(buf, 0, 0).wait())
    slot = lax.broadcasted_iota(jnp.int32, (SORT_TILE, SORT_SLOTS), 1)
    lpos = lpos_ref[...]
    gates = gate_ref[...]
    weights = jnp.zeros((SORT_TILE, SORT_SLOTS), F32)
    for k in range(TOP_K):
        weights = weights + jnp.where(slot == lpos[:, k:k + 1], gates[:, k:k + 1], 0.0)
    y = x1_ref[...] + jnp.dot(weights.astype(BF16), gbuf_ref[buf].astype(BF16), preferred_element_type=F32)
    o_ref[...] = _rms(y, g_ref[...]) if final_norm else y


def _combine(tables, yb, x1, lpos, gates, g, final_norm):
    t = x1.shape[0]
    tile = lambda i, *_: (i, 0)
    return pl.pallas_call(
        functools.partial(_combine_kernel, final_norm),
        grid_spec=pltpu.PrefetchScalarGridSpec(
            num_scalar_prefetch=4,
            grid=(t // SORT_TILE,),
            in_specs=[pl.BlockSpec(memory_space=pl.ANY),
                      pl.BlockSpec((SORT_TILE, D_MODEL), tile),
                      pl.BlockSpec((SORT_TILE, TOP_K), tile),
                      pl.BlockSpec((SORT_TILE, TOP_K), tile),
                      pl.BlockSpec((1, D_MODEL), lambda i, *_: (0, 0))],
            out_specs=pl.BlockSpec((SORT_TILE, D_MODEL), tile),
            scratch_shapes=[pltpu.VMEM((2, SORT_SLOTS, D_MODEL), F32),
                            pltpu.SemaphoreType.DMA((2,))]),
        out_shape=jax.ShapeDtypeStruct((t, D_MODEL), F32),
        compiler_params=_params("arbitrary"),
        name="combine",
    )(*tables, yb, x1, lpos, gates, g)


def kernel(x, norm1_g, w_in, b_in, attn_sinks, attn_out_g, hgrn_lb_logits, hgrn_out_g, w_out, b_out,
           norm2_g, router_w, router_b, w1, b1, w2, b2, final_g):
    batch, seq, d = x.shape
    t = batch * seq
    depth = w_in.shape[0]
    lower_bounds = jnp.cumsum(jax.nn.softmax(hgrn_lb_logits.astype(F32), axis=0), axis=0)
    assert t % SORT_TILE == 0 and SORT_SLOTS >= SORT_TILE * TOP_K + N_EXPERTS * (SEG_ROWS - 1)
    ntiles = t // SORT_TILE
    nblk = (t * TOP_K + ntiles * N_EXPERTS * (SEG_ROWS - 1)) // EXPERT_BLOCK + N_EXPERTS
    n_rows = nblk * EXPERT_BLOCK
    x2 = x.reshape(t, d)
    for l in range(depth):
        aq, ak, av, hq, hf, hi, hg = _inproj(x2, norm1_g[l][None], w_in[l].astype(BF16), b_in[l][None])
        attn = _attention(aq, ak, av, attn_sinks[l], attn_out_g[l][None], batch, seq)
        hgo = _hgrn(hq, hf, hi, hg, lower_bounds[l][None], hgrn_out_g[l][None], batch, seq)
        rw_hi = router_w[l].astype(BF16)
        rw_lo = (router_w[l] - rw_hi.astype(F32)).astype(BF16)
        rw_cat = jnp.zeros((d, 2 * LANES), BF16)
        rw_cat = rw_cat.at[:, :N_EXPERTS].set(rw_hi).at[:, LANES:LANES + N_EXPERTS].set(rw_lo)
        x1, h2, lpos, gates, seg = _outproj(
            attn, hgo, x2, w_out[l].astype(BF16), b_out[l][None], norm2_g[l][None],
            rw_cat, router_b[l][None])
        seg = seg.reshape(ntiles, N_EXPERTS).astype(jnp.int32)
        rows_e = jnp.sum(seg, axis=0)
        padded = (rows_e + EXPERT_BLOCK - 1) // EXPERT_BLOCK * EXPERT_BLOCK
        ends = jnp.cumsum(padded)
        pstart = ends - padded
        seg_glob = pstart[None, :] + jnp.cumsum(seg, axis=0) - seg
        seg_loc = jnp.cumsum(seg, axis=1) - seg
        tables = (seg_loc.reshape(-1), seg_glob.reshape(-1), (seg // SEG_ROWS).reshape(-1),
                  jnp.sum(seg, axis=1) // SEG_ROWS)
        zero_tables = (jnp.append(pstart + rows_e, ends[-1]),
                       jnp.append(padded - rows_e, n_rows - ends[-1]) // SEG_ROWS)
        blk_start = jnp.arange(nblk, dtype=jnp.int32) * EXPERT_BLOCK
        block_e = jnp.minimum(jnp.sum(blk_start[:, None] >= ends[None, :], axis=-1), N_EXPERTS - 1)
        n_used = (ends[-1] // EXPERT_BLOCK).reshape(1)
        xs = _dispatch(tables, zero_tables, h2, lpos.T, n_rows)
        yb = _experts(block_e.astype(jnp.int32), n_used.astype(jnp.int32), xs,
                      w1[l], b1[l][:, None, :], w2[l], b2[l][:, None, :])
        x2 = _combine(tables, yb, x1, lpos, gates, final_g[None], l == depth - 1)
    return x2.reshape(batch, seq, d)
```

```python
import functools

import numpy as np
import jax
import jax.numpy as jnp
from jax import lax
from jax.experimental import pallas as pl
from jax.experimental.pallas import tpu as pltpu

F32 = jnp.float32
BF16 = jnp.bfloat16

D_MODEL = 1024
ATTN_Q_HEADS = 8
ATTN_KV_HEADS = 2
ATTN_HEAD_DIM = 64
ATTN_GROUP = ATTN_Q_HEADS // ATTN_KV_HEADS
ATTN_WIDTH = ATTN_Q_HEADS * ATTN_HEAD_DIM
ATTN_KV_WIDTH = ATTN_KV_HEADS * ATTN_HEAD_DIM
WINDOW = 128
HGRN_HEADS = 4
HGRN_DIM = 128
HGRN_WIDTH = HGRN_HEADS * HGRN_DIM
HGRN_CHUNK = 64
HGRN_SUB = 16
IN_WIDTH = ATTN_WIDTH + 2 * ATTN_KV_WIDTH + 4 * HGRN_WIDTH
N_EXPERTS = 32
TOP_K = 4
EXPERT_FF = D_MODEL
SWIGLU_LIMIT = 7.0
SWIGLU_ALPHA = 1.702
NORM_EPS = 1e-5

LANES = 128
ROW_TILE = 512
EXPERT_BLOCK = 256
SORT_TILE = 256
SEG_ROWS = 8
BIG_PIECE = 32
SORT_SLOTS = 1280
VMEM_LIMIT = 56 * 1024 * 1024

_ALIBI = [float(2.0 ** (-8.0 * (h + 1) / ATTN_Q_HEADS)) for h in range(ATTN_Q_HEADS)]


def _rms(x, g):
    return x * lax.rsqrt(jnp.mean(x * x, axis=-1, keepdims=True) + NORM_EPS) * g


def _params(*sem):
    return pltpu.CompilerParams(dimension_semantics=sem, vmem_limit_bytes=VMEM_LIMIT)


_IN_SPLITS = (ATTN_WIDTH, ATTN_KV_WIDTH, ATTN_KV_WIDTH, HGRN_WIDTH, HGRN_WIDTH, HGRN_WIDTH, HGRN_WIDTH)


def _inproj_kernel(x_ref, g_ref, w_ref, b_ref, *out_refs):
    h = _rms(x_ref[...], g_ref[...]).astype(BF16)
    lo = 0
    for ref, width in zip(out_refs, _IN_SPLITS):
        ref[...] = jnp.dot(h, w_ref[:, lo:lo + width], preferred_element_type=F32) + b_ref[:, lo:lo + width]
        lo += width


def _inproj(x2, g, w_bf, b):
    t = x2.shape[0]
    row = lambda i: (i, 0)
    fixed = lambda i: (0, 0)
    return pl.pallas_call(
        _inproj_kernel,
        grid=(t // ROW_TILE,),
        in_specs=[pl.BlockSpec((ROW_TILE, D_MODEL), row),
                  pl.BlockSpec((1, D_MODEL), fixed),
                  pl.BlockSpec((D_MODEL, IN_WIDTH), fixed),
                  pl.BlockSpec((1, IN_WIDTH), fixed)],
        out_specs=[pl.BlockSpec((ROW_TILE, w), row) for w in _IN_SPLITS],
        out_shape=[jax.ShapeDtypeStruct((t, w), F32) for w in _IN_SPLITS],
        compiler_params=_params("parallel"),
        name="inproj",
    )(x2, g, w_bf, b)


def _attn_kernel(sink_ref, q_ref, kp_ref, kc_ref, vp_ref, vc_ref, g_ref, o_ref, acc_ref):
    n = pl.program_id(1)
    k2 = jnp.concatenate([kp_ref[...], kc_ref[...]], axis=0).astype(BF16)
    v2 = jnp.concatenate([vp_ref[...], vc_ref[...]], axis=0).astype(BF16)
    qi = lax.broadcasted_iota(jnp.int32, (WINDOW, 2 * WINDOW), 0)
    ki = lax.broadcasted_iota(jnp.int32, (WINDOW, 2 * WINDOW), 1)
    dist = WINDOW + qi - ki
    valid = (dist >= 0) & (dist < WINDOW) & ((ki >= WINDOW) | (n > 0))
    dist_f = dist.astype(F32)
    scale = ATTN_HEAD_DIM ** -0.5
    for head in range(ATTN_Q_HEADS):
        kv = head // ATTN_GROUP
        kh = k2[:, kv * ATTN_HEAD_DIM:(kv + 1) * ATTN_HEAD_DIM]
        vh = v2[:, kv * ATTN_HEAD_DIM:(kv + 1) * ATTN_HEAD_DIM]
        qh = q_ref[:, head * ATTN_HEAD_DIM:(head + 1) * ATTN_HEAD_DIM].astype(BF16)
        s = lax.dot_general(qh, kh, (((1,), (1,)), ((), ())), preferred_element_type=F32) * scale
        s = jnp.where(valid, s - _ALIBI[head] * dist_f, -jnp.inf)
        sink = sink_ref[head]
        m = jnp.maximum(jnp.max(s, axis=-1, keepdims=True), sink)
        p = jnp.exp(s - m)
        den = jnp.sum(p, axis=-1, keepdims=True) + jnp.exp(sink - m)
        probs = (p / den).astype(BF16)
        acc_ref[:, head * ATTN_HEAD_DIM:(head + 1) * ATTN_HEAD_DIM] = jnp.dot(
            probs, vh, preferred_element_type=F32)
    o_ref[...] = _rms(acc_ref[...], g_ref[...])


def _attention(aq, ak, av, sinks, g, batch, seq):
    nb = seq // WINDOW
    aq = aq.reshape(batch, seq, ATTN_WIDTH)
    ak = ak.reshape(batch, seq, ATTN_KV_WIDTH)
    av = av.reshape(batch, seq, ATTN_KV_WIDTH)
    cur = lambda b, n, s: (b, n, 0)
    prev = lambda b, n, s: (b, jnp.maximum(n - 1, 0), 0)
    out = pl.pallas_call(
        _attn_kernel,
        grid_spec=pltpu.PrefetchScalarGridSpec(
            num_scalar_prefetch=1,
            grid=(batch, nb),
            in_specs=[pl.BlockSpec((None, WINDOW, ATTN_WIDTH), cur),
                      pl.BlockSpec((None, WINDOW, ATTN_KV_WIDTH), prev),
                      pl.BlockSpec((None, WINDOW, ATTN_KV_WIDTH), cur),
                      pl.BlockSpec((None, WINDOW, ATTN_KV_WIDTH), prev),
                      pl.BlockSpec((None, WINDOW, ATTN_KV_WIDTH), cur),
                      pl.BlockSpec((1, ATTN_WIDTH), lambda b, n, s: (0, 0))],
            out_specs=pl.BlockSpec((None, WINDOW, ATTN_WIDTH), cur),
            scratch_shapes=[pltpu.VMEM((WINDOW, ATTN_WIDTH), F32)]),
        out_shape=jax.ShapeDtypeStruct((batch, seq, ATTN_WIDTH), F32),
        compiler_params=_params("parallel", "parallel"),
        name="attn",
    )(sinks, aq, ak, ak, av, av, g)
    return out.reshape(batch * seq, ATTN_WIDTH)


def _hgrn_kernel(q_ref, f_ref, i_ref, gate_ref, lb_ref, og_ref, o_ref, st_ref):
    c = pl.program_id(1)

    @pl.when(c == 0)
    def _():
        st_ref[...] = jnp.zeros_like(st_ref)

    C, S = HGRN_CHUNK, HGRN_SUB
    r_i = lax.broadcasted_iota(jnp.int32, (C, C), 0)
    c_i = lax.broadcasted_iota(jnp.int32, (C, C), 1)
    tri = (r_i >= c_i).astype(F32)
    rowblk = lax.broadcasted_iota(jnp.int32, (C, HGRN_DIM), 0) // S
    sub_r = lax.broadcasted_iota(jnp.int32, (S, HGRN_DIM), 0)
    for h in range(HGRN_HEADS):
        sl = slice(h * HGRN_DIM, (h + 1) * HGRN_DIM)
        qx = q_ref[:, sl]
        q = qx * jax.nn.sigmoid(qx)
        lb = lb_ref[:, sl]
        f = lb + (1.0 - lb) * jax.nn.sigmoid(f_ref[:, sl])
        logf = jnp.log(f)
        k = 1.0 - f
        v = i_ref[:, sl]
        b = jnp.dot(tri, logf, preferred_element_type=F32, precision=lax.Precision.HIGHEST)
        st = st_ref[h]
        o = lax.dot_general((q * jnp.exp(b)).astype(BF16), st.astype(BF16),
                            (((1,), (1,)), ((), ())), preferred_element_type=F32)
        qcat, kcat = [], []
        for i in range(1, C // S):
            bref = b[i * S:i * S + 1, :]
            qcat.append(jnp.where(rowblk == i, q * jnp.exp(jnp.minimum(b - bref, 0.0)), 0.0))
            kcat.append(jnp.where(rowblk < i, k * jnp.exp(jnp.minimum(bref - b, 0.0)), 0.0))
        qcat = jnp.concatenate(qcat, axis=1).astype(BF16)
        kcat = jnp.concatenate(kcat, axis=1).astype(BF16)
        att = lax.dot_general(qcat, kcat, (((1,), (1,)), ((), ())), preferred_element_type=F32)
        o = o + jnp.dot(att.astype(BF16), v.astype(BF16), preferred_element_type=F32)
        diag = []
        for i in range(C // S):
            b_blk = b[i * S:(i + 1) * S, :]
            q_blk = q[i * S:(i + 1) * S, :]
            acc = jnp.zeros((S, HGRN_DIM), F32)
            for s in range(S):
                r = i * S + s
                e = jnp.exp(jnp.minimum(b_blk - b[r:r + 1, :], 0.0))
                a = jnp.where(sub_r >= s, q_blk * e * k[r:r + 1, :], 0.0)
                acc = acc + jnp.sum(a, axis=-1, keepdims=True) * v[r:r + 1, :]
            diag.append(acc)
        o = o + jnp.concatenate(diag, axis=0)
        bl = b[C - 1:C, :]
        kd = (k * jnp.exp(bl - b)).astype(BF16)
        st_ref[h] = st * jnp.exp(bl) + lax.dot_general(
            v.astype(BF16), kd, (((0,), (0,)), ((), ())), preferred_element_type=F32)
        gx = gate_ref[:, sl]
        o = o * lax.rsqrt(jnp.mean(o * o, axis=-1, keepdims=True) + NORM_EPS) * og_ref[:, sl]
        o_ref[:, sl] = o * (gx * jax.nn.sigmoid(gx))


def _hgrn(hq, hf, hi, hg, lb, og, batch, seq):
    nc = seq // HGRN_CHUNK
    shp = (batch, seq, HGRN_WIDTH)
    blk = pl.BlockSpec((None, HGRN_CHUNK, HGRN_WIDTH), lambda b, c: (b, c, 0))
    vec = pl.BlockSpec((1, HGRN_WIDTH), lambda b, c: (0, 0))
    out = pl.pallas_call(
        _hgrn_kernel,
        grid=(batch, nc),
        in_specs=[blk, blk, blk, blk, vec, vec],
        out_specs=blk,
        out_shape=jax.ShapeDtypeStruct(shp, F32),
        scratch_shapes=[pltpu.VMEM((HGRN_HEADS, HGRN_DIM, HGRN_DIM), F32)],
        compiler_params=_params("parallel", "arbitrary"),
        name="hgrn",
    )(hq.reshape(shp), hf.reshape(shp), hi.reshape(shp), hg.reshape(shp), lb, og)
    return out.reshape(batch * seq, HGRN_WIDTH)


def _outproj_kernel(attn_ref, hg_ref, x_ref, wo_ref, bo_ref, g2_ref, rw_ref, rb_ref,
                    x1_ref, h2_ref, lpos_ref, gate_ref, seg_ref):
    tm = x_ref.shape[0]
    y = jnp.dot(attn_ref[...].astype(BF16), wo_ref[:ATTN_WIDTH, :], preferred_element_type=F32)
    y = y + jnp.dot(hg_ref[...].astype(BF16), wo_ref[ATTN_WIDTH:, :], preferred_element_type=F32)
    x1 = x_ref[...] + y + bo_ref[...]
    x1_ref[...] = x1
    h2 = _rms(x1, g2_ref[...])
    h2_ref[...] = h2.astype(BF16)
    h_hi = h2.astype(BF16)
    h_lo = (h2 - h_hi.astype(F32)).astype(BF16)
    p_hi = jnp.dot(h_hi, rw_ref[...], preferred_element_type=F32)
    p_lo = jnp.dot(h_lo, rw_ref[...], preferred_element_type=F32)
    logits = (p_hi[:, :N_EXPERTS] + p_hi[:, LANES:LANES + N_EXPERTS]) + p_lo[:, :N_EXPERTS] + rb_ref[...]
    lane = lax.broadcasted_iota(jnp.int32, (tm, N_EXPERTS), 1)
    work = logits
    vals, hots = [], []
    for _ in range(TOP_K):
        m = jnp.max(work, axis=-1, keepdims=True)
        idx = jnp.min(jnp.where(work == m, lane, N_EXPERTS), axis=-1, keepdims=True)
        hot = lane == idx
        vals.append(m)
        hots.append(hot)
        work = jnp.where(hot, -jnp.inf, work)
    ex = [jnp.exp(v - vals[0]) for v in vals]
    den = ex[0] + ex[1] + ex[2] + ex[3]
    sel = jnp.zeros((tm, N_EXPERTS), F32)
    for hot in hots:
        sel = sel + hot.astype(F32)
    r_i = lax.broadcasted_iota(jnp.int32, (tm, tm), 0)
    c_i = lax.broadcasted_iota(jnp.int32, (tm, tm), 1)
    strict = jnp.where(r_i > c_i, 1.0, 0.0).astype(BF16)
    ranks = jnp.dot(strict, sel.astype(BF16), preferred_element_type=F32)
    seg = jnp.floor((jnp.sum(sel, axis=0, keepdims=True) + (SEG_ROWS - 1.0)) * (1.0 / SEG_ROWS)) * SEG_ROWS
    e_r = lax.broadcasted_iota(jnp.int32, (N_EXPERTS, N_EXPERTS), 0)
    e_c = lax.broadcasted_iota(jnp.int32, (N_EXPERTS, N_EXPERTS), 1)
    before = jnp.where(e_r < e_c, 1.0, 0.0).astype(BF16)
    seg_off = jnp.dot(jnp.broadcast_to(seg, (SEG_ROWS, N_EXPERTS)).astype(BF16), before,
                      preferred_element_type=F32)[0:1, :]
    slots = ranks + seg_off
    l4 = lax.broadcasted_iota(jnp.int32, (tm, TOP_K), 1)
    lpos4 = jnp.zeros((tm, TOP_K), F32)
    gate4 = jnp.zeros((tm, TOP_K), F32)
    for j in range(TOP_K):
        sj = jnp.sum(jnp.where(hots[j], slots, 0.0), axis=-1, keepdims=True)
        lpos4 = jnp.where(l4 == j, sj, lpos4)
        gate4 = jnp.where(l4 == j, ex[j] / den, gate4)
    lpos_ref[...] = lpos4.astype(jnp.int32)
    gate_ref[...] = gate4
    seg_ref[...] = seg


def _outproj(attn, hgo, x2, wo_bf, bo, g2, rw, rb):
    t = x2.shape[0]
    row = lambda i: (i, 0)
    fixed = lambda i: (0, 0)
    return pl.pallas_call(
        _outproj_kernel,
        grid=(t // SORT_TILE,),
        in_specs=[pl.BlockSpec((SORT_TILE, ATTN_WIDTH), row),
                  pl.BlockSpec((SORT_TILE, HGRN_WIDTH), row),
                  pl.BlockSpec((SORT_TILE, D_MODEL), row),
                  pl.BlockSpec((ATTN_WIDTH + HGRN_WIDTH, D_MODEL), fixed),
                  pl.BlockSpec((1, D_MODEL), fixed),
                  pl.BlockSpec((1, D_MODEL), fixed),
                  pl.BlockSpec((D_MODEL, 2 * LANES), fixed),
                  pl.BlockSpec((1, N_EXPERTS), fixed)],
        out_specs=[pl.BlockSpec((SORT_TILE, D_MODEL), row),
                   pl.BlockSpec((SORT_TILE, D_MODEL), row),
                   pl.BlockSpec((SORT_TILE, TOP_K), row),
                   pl.BlockSpec((SORT_TILE, TOP_K), row),
                   pl.BlockSpec((None, 1, N_EXPERTS), lambda i: (i, 0, 0))],
        out_shape=[jax.ShapeDtypeStruct((t, D_MODEL), F32),
                   jax.ShapeDtypeStruct((t, D_MODEL), BF16),
                   jax.ShapeDtypeStruct((t, TOP_K), jnp.int32),
                   jax.ShapeDtypeStruct((t, TOP_K), F32),
                   jax.ShapeDtypeStruct((t // SORT_TILE, 1, N_EXPERTS), F32)],
        compiler_params=_params("parallel"),
        name="outproj",
    )(attn, hgo, x2, wo_bf, bo, g2, rw, rb)


def _rows(start, n):
    return pl.ds(pl.multiple_of(start, SEG_ROWS), n)


def _repeat(cnt, fn):
    def body(j, carry):
        fn(j)
        return carry

    lax.fori_loop(0, cnt, body, 0)


def _segment_pieces(big_cnt, small_cnt, fn):
    _repeat(big_cnt, lambda j: fn(j * BIG_PIECE, BIG_PIECE))
    _repeat(small_cnt, lambda j: fn(big_cnt * BIG_PIECE + j * SEG_ROWS, SEG_ROWS))


def _dispatch_kernel(loc_ref, glob_ref, big_ref, small_ref, tot_big_ref, tot_small_ref,
                     zdst_ref, zbig_ref, zsmall_ref,
                     h_ref, lpos_t_ref, xs_ref, lbuf_ref, zbuf_ref, sem, zsem):
    i = pl.program_id(0)
    last = pl.num_programs(0) - 1
    buf = i % 2

    def piece(b, local, glob, n):
        return pltpu.make_async_copy(lbuf_ref.at[b, _rows(local, n), :], xs_ref.at[_rows(glob, n), :], sem.at[b])

    def drain(step):
        _segment_pieces(tot_big_ref[step], tot_small_ref[step], lambda off, n: piece(step % 2, 0, 0, n).wait())

    def zero_piece(glob, n):
        return pltpu.make_async_copy(zbuf_ref.at[_rows(0, n), :], xs_ref.at[_rows(glob, n), :], zsem)

    @pl.when(i == 0)
    def _():
        zbuf_ref[...] = jnp.zeros_like(zbuf_ref)
        for e in range(N_EXPERTS + 1):
            _segment_pieces(zbig_ref[e], zsmall_ref[e],
                            lambda off, n, e=e: zero_piece(zdst_ref[e] + off, n).start())
        for e in range(N_EXPERTS + 1):
            _segment_pieces(zbig_ref[e], zsmall_ref[e], lambda off, n: zero_piece(0, n).wait())

    @pl.when(i >= 2)
    def _():
        drain(i - 2)

    slot = lax.broadcasted_iota(jnp.int32, (SORT_SLOTS, SORT_TILE), 0)
    onehot = jnp.zeros((SORT_SLOTS, SORT_TILE), F32)
    for k in range(TOP_K):
        onehot = onehot + jnp.where(slot == lpos_t_ref[k:k + 1, :], 1.0, 0.0)
    lbuf_ref[buf] = jnp.dot(onehot.astype(BF16), h_ref[...], preferred_element_type=F32)
    for e in range(N_EXPERTS):
        idx = i * N_EXPERTS + e
        _segment_pieces(big_ref[idx], small_ref[idx], lambda off, n, idx=idx: piece(
            buf, loc_ref[idx] + off, glob_ref[idx] + off, n).start())

    @pl.when(i == last)
    def _():
        @pl.when(i >= 1)
        def _():
            drain(i - 1)

        drain(i)


def _dispatch(tables, zero_tables, h2, lpos_t, n_rows):
    t = h2.shape[0]
    tile = lambda i, *_: (i, 0)
    return pl.pallas_call(
        _dispatch_kernel,
        grid_spec=pltpu.PrefetchScalarGridSpec(
            num_scalar_prefetch=9,
            grid=(t // SORT_TILE,),
            in_specs=[pl.BlockSpec((SORT_TILE, D_MODEL), tile),
                      pl.BlockSpec((TOP_K, SORT_TILE), lambda i, *_: (0, i))],
            out_specs=pl.BlockSpec(memory_space=pl.ANY),
            scratch_shapes=[pltpu.VMEM((2, SORT_SLOTS, D_MODEL), F32),
                            pltpu.VMEM((BIG_PIECE, D_MODEL), F32),
                            pltpu.SemaphoreType.DMA((2,)),
                            pltpu.SemaphoreType.DMA(())]),
        out_shape=jax.ShapeDtypeStruct((n_rows, D_MODEL), F32),
        compiler_params=_params("arbitrary"),
        name="dispatch",
    )(*tables, *zero_tables, h2, lpos_t)


CAST_ROWS = 128


def _expert_kernel(be_ref, nb_ref, first_ref, slot_ref, next_ref,
                   xs_ref, w1_hbm, b1_ref, w2_hbm, b2_ref, y_ref,
                   w1f_ref, w2f_ref, w1b_ref, w2i_ref, w2b_ref, sem):
    blk = pl.program_id(0)
    half = LANES // 2

    def weight_copies(e, s):
        return (pltpu.make_async_copy(w1_hbm.at[e], w1f_ref.at[s], sem.at[0, s]),
                pltpu.make_async_copy(w2_hbm.at[e], w2f_ref.at[s], sem.at[1, s]))

    @pl.when(first_ref[blk] == 1)
    def _():
        e = be_ref[blk]
        s = slot_ref[blk]

        @pl.when(blk == 0)
        def _():
            for c in weight_copies(e, s):
                c.start()

        @pl.when(next_ref[blk] >= 0)
        def _():
            for c in weight_copies(next_ref[blk], 1 - s):
                c.start()

        for c in weight_copies(e, s):
            c.wait()

        def cast_rows(r, carry):
            rows = pl.ds(pl.multiple_of(r * CAST_ROWS, CAST_ROWS), CAST_ROWS)
            w1b_ref[rows, :] = w1f_ref[s, rows, :].astype(BF16)
            return carry

        lax.fori_loop(0, D_MODEL // CAST_ROWS, cast_rows, 0)
        for c in range(D_MODEL // LANES):
            cols = slice(c * LANES, (c + 1) * LANES)
            for m in range(EXPERT_FF // LANES):
                lo = m * LANES
                w2i_ref[c, pl.ds(lo, half, stride=2), :] = w2f_ref[s, lo:lo + half, cols]
                w2i_ref[c, pl.ds(lo + 1, half, stride=2), :] = w2f_ref[s, lo + half:lo + LANES, cols]
            w2b_ref[:, cols] = w2i_ref[c].astype(BF16)

    @pl.when(blk < nb_ref[0])
    def _():
        x = xs_ref[...].astype(BF16)
        hid = jnp.dot(x, w1b_ref[...], preferred_element_type=F32) + b1_ref[...]
        even = (lax.broadcasted_iota(jnp.int32, (x.shape[0], LANES), 1) & 1) == 0
        glu, lin = [], []
        for m in range(EXPERT_FF // LANES):
            ha = hid[:, 2 * m * LANES:(2 * m + 1) * LANES]
            hb = hid[:, (2 * m + 1) * LANES:(2 * m + 2) * LANES]
            glu.append(jnp.where(even, ha, pltpu.roll(hb, 1, axis=1)))
            lin.append(jnp.where(even, pltpu.roll(ha, LANES - 1, axis=1), hb))
        glu = jnp.minimum(jnp.concatenate(glu, axis=1), SWIGLU_LIMIT)
        lin = jnp.clip(jnp.concatenate(lin, axis=1), -SWIGLU_LIMIT, SWIGLU_LIMIT)
        act = glu * jax.nn.sigmoid(SWIGLU_ALPHA * glu) * (lin + 1.0)
        y_ref[...] = jnp.dot(act.astype(BF16), w2b_ref[...], preferred_element_type=F32) + b2_ref[...]

    @pl.when(blk >= nb_ref[0])
    def _():
        y_ref[...] = jnp.zeros_like(y_ref)


def _experts(block_tables, xs, w1, b1, w2, b2):
    n_rows = xs.shape[0]
    nblk = n_rows // EXPERT_BLOCK
    rows = lambda b, *_: (b, 0)
    used_rows = lambda b, be, nb, *_: (jnp.minimum(b, nb[0] - 1), 0)
    bias = lambda b, be, *_: (be[b], 0, 0)
    return pl.pallas_call(
        _expert_kernel,
        grid_spec=pltpu.PrefetchScalarGridSpec(
            num_scalar_prefetch=5,
            grid=(nblk,),
            in_specs=[pl.BlockSpec((EXPERT_BLOCK, D_MODEL), used_rows),
                      pl.BlockSpec(memory_space=pl.ANY),
                      pl.BlockSpec((None, 1, 2 * EXPERT_FF), bias),
                      pl.BlockSpec(memory_space=pl.ANY),
                      pl.BlockSpec((None, 1, D_MODEL), bias)],
            out_specs=pl.BlockSpec((EXPERT_BLOCK, D_MODEL), rows),
            scratch_shapes=[pltpu.VMEM((2, D_MODEL, 2 * EXPERT_FF), F32),
                            pltpu.VMEM((2, EXPERT_FF, D_MODEL), F32),
                            pltpu.VMEM((D_MODEL, 2 * EXPERT_FF), BF16),
                            pltpu.VMEM((D_MODEL // LANES, EXPERT_FF, LANES), F32),
                            pltpu.VMEM((EXPERT_FF, D_MODEL), BF16),
                            pltpu.SemaphoreType.DMA((2, 2))]),
        out_shape=jax.ShapeDtypeStruct((n_rows, D_MODEL), F32),
        compiler_params=_params("arbitrary"),
        name="experts",
    )(*block_tables, xs, w1, b1, w2, b2)


def _combine_kernel(final_norm, loc_ref, glob_ref, big_ref, small_ref, tot_big_ref, tot_small_ref,
                    yb_ref, x1_ref, lpos_ref, gate_ref, g_ref, o_ref, gbuf_ref, sem):
    i = pl.program_id(0)
    last = pl.num_programs(0) - 1
    buf = i % 2

    def piece(b, local, glob, n):
        return pltpu.make_async_copy(yb_ref.at[_rows(glob, n), :], gbuf_ref.at[b, _rows(local, n), :], sem.at[b])

    def fetch(step):
        for e in range(N_EXPERTS):
            idx = step * N_EXPERTS + e
            _segment_pieces(big_ref[idx], small_ref[idx], lambda off, n, idx=idx: piece(
                step % 2, loc_ref[idx] + off, glob_ref[idx] + off, n).start())

    @pl.when(i == 0)
    def _():
        gbuf_ref[...] = jnp.zeros_like(gbuf_ref)
        fetch(0)

    @pl.when(i < last)
    def _():
        fetch(i + 1)

    _segment_pieces(tot_big_ref[i], tot_small_ref[i], lambda off, n: piece(buf, 0, 0, n).wait())
    slot = lax.broadcasted_iota(jnp.int32, (SORT_TILE, SORT_SLOTS), 1)
    lpos = lpos_ref[...]
    gates = gate_ref[...]
    weights = jnp.zeros((SORT_TILE, SORT_SLOTS), F32)
    for k in range(TOP_K):
        weights = weights + jnp.where(slot == lpos[:, k:k + 1], gates[:, k:k + 1], 0.0)
    y = x1_ref[...] + jnp.dot(weights.astype(BF16), gbuf_ref[buf].astype(BF16), preferred_element_type=F32)
    o_ref[...] = _rms(y, g_ref[...]) if final_norm else y


def _combine(tables, yb, x1, lpos, gates, g, final_norm):
    t = x1.shape[0]
    tile = lambda i, *_: (i, 0)
    return pl.pallas_call(
        functools.partial(_combine_kernel, final_norm),
        grid_spec=pltpu.PrefetchScalarGridSpec(
            num_scalar_prefetch=6,
            grid=(t // SORT_TILE,),
            in_specs=[pl.BlockSpec(memory_space=pl.ANY),
                      pl.BlockSpec((SORT_TILE, D_MODEL), tile),
                      pl.BlockSpec((SORT_TILE, TOP_K), tile),
                      pl.BlockSpec((SORT_TILE, TOP_K), tile),
                      pl.BlockSpec((1, D_MODEL), lambda i, *_: (0, 0))],
            out_specs=pl.BlockSpec((SORT_TILE, D_MODEL), tile),
            scratch_shapes=[pltpu.VMEM((2, SORT_SLOTS, D_MODEL), F32),
                            pltpu.SemaphoreType.DMA((2,))]),
        out_shape=jax.ShapeDtypeStruct((t, D_MODEL), F32),
        compiler_params=_params("arbitrary"),
        name="combine",
    )(*tables, yb, x1, lpos, gates, g)


def kernel(x, norm1_g, w_in, b_in, attn_sinks, attn_out_g, hgrn_lb_logits, hgrn_out_g, w_out, b_out,
           norm2_g, router_w, router_b, w1, b1, w2, b2, final_g):
    batch, seq, d = x.shape
    t = batch * seq
    depth = w_in.shape[0]
    lower_bounds = jnp.cumsum(jax.nn.softmax(hgrn_lb_logits.astype(F32), axis=0), axis=0)
    assert t % SORT_TILE == 0 and SORT_SLOTS >= SORT_TILE * TOP_K + N_EXPERTS * (SEG_ROWS - 1)
    ntiles = t // SORT_TILE
    nblk = (t * TOP_K + ntiles * N_EXPERTS * (SEG_ROWS - 1)) // EXPERT_BLOCK + N_EXPERTS
    n_rows = nblk * EXPERT_BLOCK
    x2 = x.reshape(t, d)
    for l in range(depth):
        aq, ak, av, hq, hf, hi, hg = _inproj(x2, norm1_g[l][None], w_in[l].astype(BF16), b_in[l][None])
        attn = _attention(aq, ak, av, attn_sinks[l], attn_out_g[l][None], batch, seq)
        hgo = _hgrn(hq, hf, hi, hg, lower_bounds[l][None], hgrn_out_g[l][None], batch, seq)
        rw_hi = router_w[l].astype(BF16)
        rw_lo = (router_w[l] - rw_hi.astype(F32)).astype(BF16)
        rw_cat = jnp.zeros((d, 2 * LANES), BF16)
        rw_cat = rw_cat.at[:, :N_EXPERTS].set(rw_hi).at[:, LANES:LANES + N_EXPERTS].set(rw_lo)
        x1, h2, lpos, gates, seg = _outproj(
            attn, hgo, x2, w_out[l].astype(BF16), b_out[l][None], norm2_g[l][None],
            rw_cat, router_b[l][None])
        seg = seg.reshape(ntiles, N_EXPERTS).astype(jnp.int32)
        rows_e = jnp.sum(seg, axis=0)
        padded = (rows_e + EXPERT_BLOCK - 1) // EXPERT_BLOCK * EXPERT_BLOCK
        ends = jnp.cumsum(padded)
        pstart = ends - padded
        seg_glob = pstart[None, :] + jnp.cumsum(seg, axis=0) - seg
        seg_loc = jnp.cumsum(seg, axis=1) - seg
        tail = jnp.append(padded - rows_e, n_rows - ends[-1])
        tables = (seg_loc.reshape(-1), seg_glob.reshape(-1),
                  (seg // BIG_PIECE).reshape(-1), (seg % BIG_PIECE // SEG_ROWS).reshape(-1),
                  jnp.sum(seg // BIG_PIECE, axis=1), jnp.sum(seg % BIG_PIECE // SEG_ROWS, axis=1))
        zero_tables = (jnp.append(pstart + rows_e, ends[-1]), tail // BIG_PIECE, tail % BIG_PIECE // SEG_ROWS)
        blk_ids = jnp.arange(nblk, dtype=jnp.int32)
        block_e = jnp.minimum(jnp.sum(blk_ids[:, None] * EXPERT_BLOCK >= ends[None, :], axis=-1), N_EXPERTS - 1)
        n_used = ends[-1] // EXPERT_BLOCK
        first = ((blk_ids == 0) | (block_e != jnp.roll(block_e, 1))) & (blk_ids < n_used)
        slot = (jnp.cumsum(first) - 1) % 2
        e_ids = jnp.arange(N_EXPERTS, dtype=jnp.int32)
        later_used = (e_ids[None, :] > e_ids[:, None]) & (padded[None, :] > 0)
        next_e = jnp.min(jnp.where(later_used, e_ids[None, :], N_EXPERTS), axis=1)
        next_e = jnp.where(next_e == N_EXPERTS, -1, next_e)
        block_tables = tuple(a.astype(jnp.int32) for a in (
            block_e, n_used.reshape(1), first, slot, next_e[block_e]))
        xs = _dispatch(tables, zero_tables, h2, lpos.T, n_rows)
        yb = _experts(block_tables, xs, w1[l], b1[l][:, None, :], w2[l], b2[l][:, None, :])
        x2 = _combine(tables, yb, x1, lpos, gates, final_g[None], l == depth - 1)
    return x2.reshape(batch, seq, d)
```

```python
import functools

import numpy as np
import jax
import jax.numpy as jnp
from jax import lax
from jax.experimental import pallas as pl
from jax.experimental.pallas import tpu as pltpu

F32 = jnp.float32
BF16 = jnp.bfloat16

D_MODEL = 1024
ATTN_Q_HEADS = 8
ATTN_KV_HEADS = 2
ATTN_HEAD_DIM = 64
ATTN_GROUP = ATTN_Q_HEADS // ATTN_KV_HEADS
ATTN_WIDTH = ATTN_Q_HEADS * ATTN_HEAD_DIM
ATTN_KV_WIDTH = ATTN_KV_HEADS * ATTN_HEAD_DIM
WINDOW = 128
HGRN_HEADS = 4
HGRN_DIM = 128
HGRN_WIDTH = HGRN_HEADS * HGRN_DIM
HGRN_CHUNK = 64
HGRN_SUB = 16
IN_WIDTH = ATTN_WIDTH + 2 * ATTN_KV_WIDTH + 4 * HGRN_WIDTH
N_EXPERTS = 32
TOP_K = 4
EXPERT_FF = D_MODEL
SWIGLU_LIMIT = 7.0
SWIGLU_ALPHA = 1.702
NORM_EPS = 1e-5

LANES = 128
ROW_TILE = 512
EXPERT_BLOCK = 256
SORT_TILE = 256
SEG_ROWS = 8
BIG_PIECE = 32
SORT_SLOTS = 1280
VMEM_LIMIT = 56 * 1024 * 1024

_ALIBI = [float(2.0 ** (-8.0 * (h + 1) / ATTN_Q_HEADS)) for h in range(ATTN_Q_HEADS)]


def _rms(x, g):
    return x * lax.rsqrt(jnp.mean(x * x, axis=-1, keepdims=True) + NORM_EPS) * g


def _params(*sem):
    return pltpu.CompilerParams(dimension_semantics=sem, vmem_limit_bytes=VMEM_LIMIT)


_IN_SPLITS = (ATTN_WIDTH, ATTN_KV_WIDTH, ATTN_KV_WIDTH, HGRN_WIDTH, HGRN_WIDTH, HGRN_WIDTH, HGRN_WIDTH)


def _inproj_kernel(x_ref, g_ref, w_ref, b_ref, *out_refs):
    h = _rms(x_ref[...], g_ref[...]).astype(BF16)
    lo = 0
    for ref, width in zip(out_refs, _IN_SPLITS):
        ref[...] = jnp.dot(h, w_ref[:, lo:lo + width], preferred_element_type=F32) + b_ref[:, lo:lo + width]
        lo += width


def _inproj(x2, g, w_bf, b):
    t = x2.shape[0]
    row = lambda i: (i, 0)
    fixed = lambda i: (0, 0)
    return pl.pallas_call(
        _inproj_kernel,
        grid=(t // ROW_TILE,),
        in_specs=[pl.BlockSpec((ROW_TILE, D_MODEL), row),
                  pl.BlockSpec((1, D_MODEL), fixed),
                  pl.BlockSpec((D_MODEL, IN_WIDTH), fixed),
                  pl.BlockSpec((1, IN_WIDTH), fixed)],
        out_specs=[pl.BlockSpec((ROW_TILE, w), row) for w in _IN_SPLITS],
        out_shape=[jax.ShapeDtypeStruct((t, w), F32) for w in _IN_SPLITS],
        compiler_params=_params("parallel"),
        name="inproj",
    )(x2, g, w_bf, b)


def _attn_bias_tables():
    qi = np.arange(WINDOW)[:, None]
    ki = np.arange(2 * WINDOW)[None, :]
    dist = WINDOW + qi - ki
    in_window = (dist >= 0) & (dist < WINDOW)
    slopes = np.asarray(_ALIBI, np.float32)[:, None, None]
    bias = -(slopes * dist.astype(np.float32)[None])
    tables = [np.where(in_window & (ki >= WINDOW), bias, -np.inf), np.where(in_window, bias, -np.inf)]
    return np.stack(tables).astype(np.float32)


def _pair_heads(a, axis):
    shape = a.shape
    a = a.reshape(shape[:axis] + (ATTN_KV_HEADS, ATTN_GROUP, ATTN_HEAD_DIM) + shape[axis + 1:])
    return jnp.swapaxes(a, axis, axis + 1).reshape(shape)


def _attn_kernel(sink_ref, q_ref, kp_ref, kc_ref, vp_ref, vc_ref, bias_ref, g_ref, o_ref):
    k2 = jnp.concatenate([kp_ref[...], kc_ref[...]], axis=0).astype(BF16)
    v2 = jnp.concatenate([vp_ref[...], vc_ref[...]], axis=0).astype(BF16)
    scale = ATTN_HEAD_DIM ** -0.5
    nt = (((1,), (1,)), ((), ()))
    low = lax.broadcasted_iota(jnp.int32, (WINDOW, LANES), 1) < ATTN_HEAD_DIM

    def softmax(s, h):
        s = s * scale + bias_ref[h]
        sink = sink_ref[h]
        m = jnp.maximum(jnp.max(s, axis=-1, keepdims=True), sink)
        p = jnp.exp(s - m)
        den = jnp.sum(p, axis=-1, keepdims=True) + jnp.exp(sink - m)
        return (p / den).astype(BF16)

    scores = []
    for j in range(ATTN_GROUP):
        q = q_ref[:, j * LANES:(j + 1) * LANES]
        scores.append((lax.dot_general(jnp.where(low, q, 0.0).astype(BF16), k2, nt, preferred_element_type=F32),
                       lax.dot_general(jnp.where(low, 0.0, q).astype(BF16), k2, nt, preferred_element_type=F32)))
    probs = [(softmax(s_lo, j), softmax(s_hi, ATTN_GROUP + j)) for j, (s_lo, s_hi) in enumerate(scores)]
    outs = [jnp.where(low, jnp.dot(p_lo, v2, preferred_element_type=F32),
                      jnp.dot(p_hi, v2, preferred_element_type=F32)) for p_lo, p_hi in probs]
    o_ref[...] = _rms(jnp.concatenate(outs, axis=1), g_ref[...])


def _attention(aq, ak, av, sinks, g, batch, seq):
    nb = seq // WINDOW
    aq = aq.reshape(batch, seq, ATTN_WIDTH)
    ak = ak.reshape(batch, seq, ATTN_KV_WIDTH)
    av = av.reshape(batch, seq, ATTN_KV_WIDTH)
    cur = lambda b, n, s: (b, n, 0)
    prev = lambda b, n, s: (b, jnp.maximum(n - 1, 0), 0)
    out = pl.pallas_call(
        _attn_kernel,
        grid_spec=pltpu.PrefetchScalarGridSpec(
            num_scalar_prefetch=1,
            grid=(batch, nb),
            in_specs=[pl.BlockSpec((None, WINDOW, ATTN_WIDTH), cur),
                      pl.BlockSpec((None, WINDOW, ATTN_KV_WIDTH), prev),
                      pl.BlockSpec((None, WINDOW, ATTN_KV_WIDTH), cur),
                      pl.BlockSpec((None, WINDOW, ATTN_KV_WIDTH), prev),
                      pl.BlockSpec((None, WINDOW, ATTN_KV_WIDTH), cur),
                      pl.BlockSpec((None, ATTN_Q_HEADS, WINDOW, 2 * WINDOW),
                                   lambda b, n, s: (jnp.minimum(n, 1), 0, 0, 0)),
                      pl.BlockSpec((1, ATTN_WIDTH), lambda b, n, s: (0, 0))],
            out_specs=pl.BlockSpec((None, WINDOW, ATTN_WIDTH), cur)),
        out_shape=jax.ShapeDtypeStruct((batch, seq, ATTN_WIDTH), F32),
        compiler_params=_params("parallel", "parallel"),
        name="attn",
    )(sinks, aq, ak, ak, av, av, jnp.asarray(_attn_bias_tables()), g)
    return out.reshape(batch * seq, ATTN_WIDTH)


HGRN_EXP_GUARD = 80.0


def _hgrn_kernel(guard, q_ref, f_ref, i_ref, gate_ref, lb_ref, og_ref, o_ref, st_ref, q_scr, k_scr, b_scr):
    c = pl.program_id(1)

    @pl.when(c == 0)
    def _():
        st_ref[...] = jnp.zeros_like(st_ref)

    C, S = HGRN_CHUNK, HGRN_SUB
    nsub = C // S
    r_i = lax.broadcasted_iota(jnp.int32, (C, C), 0)
    c_i = lax.broadcasted_iota(jnp.int32, (C, C), 1)
    nt = (((1,), (1,)), ((), ()))

    tri = (r_i >= c_i).astype(F32)
    qx = q_ref[...]
    q_scr[...] = qx * jax.nn.sigmoid(qx)
    lb = lb_ref[...]
    f = lb + (1.0 - lb) * jax.nn.sigmoid(f_ref[...])
    k_scr[...] = 1.0 - f
    logf = jnp.log(f)
    b_scr[...] = jnp.dot(tri, logf, preferred_element_type=F32, precision=lax.Precision.HIGHEST)
    decay = jnp.concatenate([-jnp.sum(logf[i * S:(i + 1) * S, :], axis=0, keepdims=True) for i in range(nsub)],
                            axis=0)
    risky = jnp.max(decay) > guard

    rowblk = lax.broadcasted_iota(jnp.int32, (C, HGRN_DIM), 0) // S
    heads = [slice(h * HGRN_DIM, (h + 1) * HGRN_DIM) for h in range(HGRN_HEADS)]

    def finish(intra):
        inter, new_state = [], []
        for h, sl in enumerate(heads):
            q, k, b = q_scr[:, sl], k_scr[:, sl], b_scr[:, sl]
            st = st_ref[h]
            inter.append(lax.dot_general((q * jnp.exp(b)).astype(BF16), st.astype(BF16), nt,
                                         preferred_element_type=F32))
            bl = b[C - 1:C, :]
            kd = (k * jnp.exp(bl - b)).astype(BF16)
            new_state.append(st * jnp.exp(bl) + lax.dot_general(
                i_ref[:, sl].astype(BF16), kd, (((0,), (0,)), ((), ())), preferred_element_type=F32))
        for h, sl in enumerate(heads):
            st_ref[h] = new_state[h]
            o = intra[h] + inter[h]
            gx = gate_ref[:, sl]
            o = o * lax.rsqrt(jnp.mean(o * o, axis=-1, keepdims=True) + NORM_EPS) * og_ref[:, sl]
            o_ref[:, sl] = o * (gx * jax.nn.sigmoid(gx))

    def sub_block_queries(q, b):
        return [jnp.where(rowblk == i, q * jnp.exp(jnp.minimum(b - b[i * S:i * S + 1, :], 0.0)), 0.0)
                for i in range(nsub)]

    @pl.when(jnp.logical_not(risky))
    def _():
        att = []
        for sl in heads:
            q, k, b = q_scr[:, sl], k_scr[:, sl], b_scr[:, sl]
            k_sub = [jnp.where(rowblk <= i, k * jnp.exp(jnp.minimum(b[i * S:i * S + 1, :] - b, guard)), 0.0)
                     for i in range(nsub)]
            att.append(lax.dot_general(jnp.concatenate(sub_block_queries(q, b), axis=1).astype(BF16),
                                       jnp.concatenate(k_sub, axis=1).astype(BF16), nt,
                                       preferred_element_type=F32))
        finish([jnp.dot(jnp.where(r_i >= c_i, att[h], 0.0).astype(BF16), i_ref[:, sl].astype(BF16),
                        preferred_element_type=F32) for h, sl in enumerate(heads)])

    @pl.when(risky)
    def _():
        sub_r = lax.broadcasted_iota(jnp.int32, (S, HGRN_DIM), 0)
        intra = []
        for sl in heads:
            q, k, b, v = q_scr[:, sl], k_scr[:, sl], b_scr[:, sl], i_ref[:, sl]
            k_sub = [jnp.where(rowblk < i, k * jnp.exp(jnp.minimum(b[i * S:i * S + 1, :] - b, 0.0)), 0.0)
                     for i in range(1, nsub)]
            att = lax.dot_general(jnp.concatenate(sub_block_queries(q, b)[1:], axis=1).astype(BF16),
                                  jnp.concatenate(k_sub, axis=1).astype(BF16), nt, preferred_element_type=F32)
            diag = []
            for i in range(nsub):
                b_blk = b[i * S:(i + 1) * S, :]
                q_blk = q[i * S:(i + 1) * S, :]
                acc = jnp.zeros((S, HGRN_DIM), F32)
                for s in range(S):
                    r = i * S + s
                    e = jnp.exp(jnp.minimum(b_blk - b[r:r + 1, :], 0.0))
                    a = jnp.where(sub_r >= s, q_blk * e * k[r:r + 1, :], 0.0)
                    acc = acc + jnp.sum(a, axis=-1, keepdims=True) * v[r:r + 1, :]
                diag.append(acc)
            intra.append(jnp.dot(att.astype(BF16), v.astype(BF16), preferred_element_type=F32)
                         + jnp.concatenate(diag, axis=0))
        finish(intra)


def _hgrn(hq, hf, hi, hg, lb, og, batch, seq, guard=HGRN_EXP_GUARD):
    nc = seq // HGRN_CHUNK
    shp = (batch, seq, HGRN_WIDTH)
    blk = pl.BlockSpec((None, HGRN_CHUNK, HGRN_WIDTH), lambda b, c: (b, c, 0))
    vec = pl.BlockSpec((1, HGRN_WIDTH), lambda b, c: (0, 0))
    out = pl.pallas_call(
        functools.partial(_hgrn_kernel, guard),
        grid=(batch, nc),
        in_specs=[blk, blk, blk, blk, vec, vec],
        out_specs=blk,
        out_shape=jax.ShapeDtypeStruct(shp, F32),
        scratch_shapes=[pltpu.VMEM((HGRN_HEADS, HGRN_DIM, HGRN_DIM), F32)]
        + [pltpu.VMEM((HGRN_CHUNK, HGRN_WIDTH), F32)] * 3,
        compiler_params=_params("parallel", "arbitrary"),
        name="hgrn",
    )(hq.reshape(shp), hf.reshape(shp), hi.reshape(shp), hg.reshape(shp), lb, og)
    return out.reshape(batch * seq, HGRN_WIDTH)


def _outproj_kernel(attn_ref, hg_ref, x_ref, wo_ref, bo_ref, g2_ref, rw_ref, rb_ref,
                    x1_ref, h2_ref, lpos_ref, gate_ref, seg_ref):
    tm = x_ref.shape[0]
    y = jnp.dot(attn_ref[...].astype(BF16), wo_ref[:ATTN_WIDTH, :], preferred_element_type=F32)
    y = y + jnp.dot(hg_ref[...].astype(BF16), wo_ref[ATTN_WIDTH:, :], preferred_element_type=F32)
    x1 = x_ref[...] + y + bo_ref[...]
    x1_ref[...] = x1
    h2 = _rms(x1, g2_ref[...])
    h2_ref[...] = h2.astype(BF16)
    h_hi = h2.astype(BF16)
    h_lo = (h2 - h_hi.astype(F32)).astype(BF16)
    p_hi = jnp.dot(h_hi, rw_ref[...], preferred_element_type=F32)
    p_lo = jnp.dot(h_lo, rw_ref[...], preferred_element_type=F32)
    logits = (p_hi[:, :N_EXPERTS] + p_hi[:, LANES:LANES + N_EXPERTS]) + p_lo[:, :N_EXPERTS] + rb_ref[...]
    lane = lax.broadcasted_iota(jnp.int32, (tm, N_EXPERTS), 1)
    work = logits
    vals, hots = [], []
    for _ in range(TOP_K):
        m = jnp.max(work, axis=-1, keepdims=True)
        idx = jnp.min(jnp.where(work == m, lane, N_EXPERTS), axis=-1, keepdims=True)
        hot = lane == idx
        vals.append(m)
        hots.append(hot)
        work = jnp.where(hot, -jnp.inf, work)
    ex = [jnp.exp(v - vals[0]) for v in vals]
    den = ex[0] + ex[1] + ex[2] + ex[3]
    sel = jnp.zeros((tm, N_EXPERTS), F32)
    for hot in hots:
        sel = sel + hot.astype(F32)
    r_i = lax.broadcasted_iota(jnp.int32, (tm, tm), 0)
    c_i = lax.broadcasted_iota(jnp.int32, (tm, tm), 1)
    strict = jnp.where(r_i > c_i, 1.0, 0.0).astype(BF16)
    ranks = jnp.dot(strict, sel.astype(BF16), preferred_element_type=F32)
    seg = jnp.floor((jnp.sum(sel, axis=0, keepdims=True) + (SEG_ROWS - 1.0)) * (1.0 / SEG_ROWS)) * SEG_ROWS
    e_r = lax.broadcasted_iota(jnp.int32, (N_EXPERTS, N_EXPERTS), 0)
    e_c = lax.broadcasted_iota(jnp.int32, (N_EXPERTS, N_EXPERTS), 1)
    before = jnp.where(e_r < e_c, 1.0, 0.0).astype(BF16)
    seg_off = jnp.dot(jnp.broadcast_to(seg, (SEG_ROWS, N_EXPERTS)).astype(BF16), before,
                      preferred_element_type=F32)[0:1, :]
    slots = ranks + seg_off
    l4 = lax.broadcasted_iota(jnp.int32, (tm, TOP_K), 1)
    lpos4 = jnp.zeros((tm, TOP_K), F32)
    gate4 = jnp.zeros((tm, TOP_K), F32)
    for j in range(TOP_K):
        sj = jnp.sum(jnp.where(hots[j], slots, 0.0), axis=-1, keepdims=True)
        lpos4 = jnp.where(l4 == j, sj, lpos4)
        gate4 = jnp.where(l4 == j, ex[j] / den, gate4)
    lpos_ref[...] = lpos4.astype(jnp.int32)
    gate_ref[...] = gate4
    seg_ref[...] = seg


def _outproj(attn, hgo, x2, wo_bf, bo, g2, rw, rb):
    t = x2.shape[0]
    row = lambda i: (i, 0)
    fixed = lambda i: (0, 0)
    return pl.pallas_call(
        _outproj_kernel,
        grid=(t // SORT_TILE,),
        in_specs=[pl.BlockSpec((SORT_TILE, ATTN_WIDTH), row),
                  pl.BlockSpec((SORT_TILE, HGRN_WIDTH), row),
                  pl.BlockSpec((SORT_TILE, D_MODEL), row),
                  pl.BlockSpec((ATTN_WIDTH + HGRN_WIDTH, D_MODEL), fixed),
                  pl.BlockSpec((1, D_MODEL), fixed),
                  pl.BlockSpec((1, D_MODEL), fixed),
                  pl.BlockSpec((D_MODEL, 2 * LANES), fixed),
                  pl.BlockSpec((1, N_EXPERTS), fixed)],
        out_specs=[pl.BlockSpec((SORT_TILE, D_MODEL), row),
                   pl.BlockSpec((SORT_TILE, D_MODEL), row),
                   pl.BlockSpec((SORT_TILE, TOP_K), row),
                   pl.BlockSpec((SORT_TILE, TOP_K), row),
                   pl.BlockSpec((None, 1, N_EXPERTS), lambda i: (i, 0, 0))],
        out_shape=[jax.ShapeDtypeStruct((t, D_MODEL), F32),
                   jax.ShapeDtypeStruct((t, D_MODEL), BF16),
                   jax.ShapeDtypeStruct((t, TOP_K), jnp.int32),
                   jax.ShapeDtypeStruct((t, TOP_K), F32),
                   jax.ShapeDtypeStruct((t // SORT_TILE, 1, N_EXPERTS), F32)],
        compiler_params=_params("parallel"),
        name="outproj",
    )(attn, hgo, x2, wo_bf, bo, g2, rw, rb)


def _rows(start, n):
    return pl.ds(pl.multiple_of(start, SEG_ROWS), n)


def _repeat(cnt, fn):
    def body(j, carry):
        fn(j)
        return carry

    lax.fori_loop(0, cnt, body, 0)


def _segment_pieces(big_cnt, small_cnt, fn):
    _repeat(big_cnt, lambda j: fn(j * BIG_PIECE, BIG_PIECE))
    _repeat(small_cnt, lambda j: fn(big_cnt * BIG_PIECE + j * SEG_ROWS, SEG_ROWS))


def _dispatch_kernel(loc_ref, glob_ref, big_ref, small_ref, tot_big_ref, tot_small_ref,
                     zdst_ref, zbig_ref, zsmall_ref,
                     h_ref, lpos_t_ref, xs_ref, lbuf_ref, zbuf_ref, sem, zsem):
    i = pl.program_id(0)
    last = pl.num_programs(0) - 1
    buf = i % 2

    def piece(b, local, glob, n):
        return pltpu.make_async_copy(lbuf_ref.at[b, _rows(local, n), :], xs_ref.at[_rows(glob, n), :], sem.at[b])

    def drain(step):
        _segment_pieces(tot_big_ref[step], tot_small_ref[step], lambda off, n: piece(step % 2, 0, 0, n).wait())

    def zero_piece(glob, n):
        return pltpu.make_async_copy(zbuf_ref.at[_rows(0, n), :], xs_ref.at[_rows(glob, n), :], zsem)

    @pl.when(i == 0)
    def _():
        zbuf_ref[...] = jnp.zeros_like(zbuf_ref)
        for e in range(N_EXPERTS + 1):
            _segment_pieces(zbig_ref[e], zsmall_ref[e],
                            lambda off, n, e=e: zero_piece(zdst_ref[e] + off, n).start())
        for e in range(N_EXPERTS + 1):
            _segment_pieces(zbig_ref[e], zsmall_ref[e], lambda off, n: zero_piece(0, n).wait())

    @pl.when(i >= 2)
    def _():
        drain(i - 2)

    slot = lax.broadcasted_iota(jnp.int32, (SORT_SLOTS, SORT_TILE), 0)
    onehot = jnp.zeros((SORT_SLOTS, SORT_TILE), F32)
    for k in range(TOP_K):
        onehot = onehot + jnp.where(slot == lpos_t_ref[k:k + 1, :], 1.0, 0.0)
    lbuf_ref[buf] = jnp.dot(onehot.astype(BF16), h_ref[...], preferred_element_type=F32)
    for e in range(N_EXPERTS):
        idx = i * N_EXPERTS + e
        _segment_pieces(big_ref[idx], small_ref[idx], lambda off, n, idx=idx: piece(
            buf, loc_ref[idx] + off, glob_ref[idx] + off, n).start())

    @pl.when(i == last)
    def _():
        @pl.when(i >= 1)
        def _():
            drain(i - 1)

        drain(i)


def _dispatch(tables, zero_tables, h2, lpos_t, n_rows):
    t = h2.shape[0]
    tile = lambda i, *_: (i, 0)
    return pl.pallas_call(
        _dispatch_kernel,
        grid_spec=pltpu.PrefetchScalarGridSpec(
            num_scalar_prefetch=9,
            grid=(t // SORT_TILE,),
            in_specs=[pl.BlockSpec((SORT_TILE, D_MODEL), tile),
                      pl.BlockSpec((TOP_K, SORT_TILE), lambda i, *_: (0, i))],
            out_specs=pl.BlockSpec(memory_space=pl.ANY),
            scratch_shapes=[pltpu.VMEM((2, SORT_SLOTS, D_MODEL), F32),
                            pltpu.VMEM((BIG_PIECE, D_MODEL), F32),
                            pltpu.SemaphoreType.DMA((2,)),
                            pltpu.SemaphoreType.DMA(())]),
        out_shape=jax.ShapeDtypeStruct((n_rows, D_MODEL), F32),
        compiler_params=_params("arbitrary"),
        name="dispatch",
    )(*tables, *zero_tables, h2, lpos_t)


CAST_ROWS = 128


def _expert_kernel(be_ref, nb_ref, first_ref, slot_ref, next_ref,
                   xs_ref, w1_hbm, b1_ref, w2_hbm, b2_ref, y_ref,
                   w1f_ref, w2f_ref, w1b_ref, w2i_ref, w2b_ref, sem):
    blk = pl.program_id(0)
    half = LANES // 2

    def weight_copies(e, s):
        return (pltpu.make_async_copy(w1_hbm.at[e], w1f_ref.at[s], sem.at[0, s]),
                pltpu.make_async_copy(w2_hbm.at[e], w2f_ref.at[s], sem.at[1, s]))

    @pl.when(first_ref[blk] == 1)
    def _():
        e = be_ref[blk]
        s = slot_ref[blk]

        @pl.when(blk == 0)
        def _():
            for c in weight_copies(e, s):
                c.start()

        @pl.when(next_ref[blk] >= 0)
        def _():
            for c in weight_copies(next_ref[blk], 1 - s):
                c.start()

        for c in weight_copies(e, s):
            c.wait()

        def cast_rows(r, carry):
            rows = pl.ds(pl.multiple_of(r * CAST_ROWS, CAST_ROWS), CAST_ROWS)
            w1b_ref[rows, :] = w1f_ref[s, rows, :].astype(BF16)
            return carry

        lax.fori_loop(0, D_MODEL // CAST_ROWS, cast_rows, 0)
        for c in range(D_MODEL // LANES):
            cols = slice(c * LANES, (c + 1) * LANES)
            for m in range(EXPERT_FF // LANES):
                lo = m * LANES
                w2i_ref[c, pl.ds(lo, half, stride=2), :] = w2f_ref[s, lo:lo + half, cols]
                w2i_ref[c, pl.ds(lo + 1, half, stride=2), :] = w2f_ref[s, lo + half:lo + LANES, cols]
            w2b_ref[:, cols] = w2i_ref[c].astype(BF16)

    @pl.when(blk < nb_ref[0])
    def _():
        x = xs_ref[...].astype(BF16)
        hid = jnp.dot(x, w1b_ref[...], preferred_element_type=F32) + b1_ref[...]
        even = (lax.broadcasted_iota(jnp.int32, (x.shape[0], LANES), 1) & 1) == 0
        glu, lin = [], []
        for m in range(EXPERT_FF // LANES):
            ha = hid[:, 2 * m * LANES:(2 * m + 1) * LANES]
            hb = hid[:, (2 * m + 1) * LANES:(2 * m + 2) * LANES]
            glu.append(jnp.where(even, ha, pltpu.roll(hb, 1, axis=1)))
            lin.append(jnp.where(even, pltpu.roll(ha, LANES - 1, axis=1), hb))
        glu = jnp.minimum(jnp.concatenate(glu, axis=1), SWIGLU_LIMIT)
        lin = jnp.clip(jnp.concatenate(lin, axis=1), -SWIGLU_LIMIT, SWIGLU_LIMIT)
        act = glu * jax.nn.sigmoid(SWIGLU_ALPHA * glu) * (lin + 1.0)
        y_ref[...] = jnp.dot(act.astype(BF16), w2b_ref[...], preferred_element_type=F32) + b2_ref[...]

    @pl.when(blk >= nb_ref[0])
    def _():
        y_ref[...] = jnp.zeros_like(y_ref)


def _experts(block_tables, xs, w1, b1, w2, b2):
    n_rows = xs.shape[0]
    nblk = n_rows // EXPERT_BLOCK
    rows = lambda b, *_: (b, 0)
    used_rows = lambda b, be, nb, *_: (jnp.minimum(b, nb[0] - 1), 0)
    bias = lambda b, be, *_: (be[b], 0, 0)
    return pl.pallas_call(
        _expert_kernel,
        grid_spec=pltpu.PrefetchScalarGridSpec(
            num_scalar_prefetch=5,
            grid=(nblk,),
            in_specs=[pl.BlockSpec((EXPERT_BLOCK, D_MODEL), used_rows),
                      pl.BlockSpec(memory_space=pl.ANY),
                      pl.BlockSpec((None, 1, 2 * EXPERT_FF), bias),
                      pl.BlockSpec(memory_space=pl.ANY),
                      pl.BlockSpec((None, 1, D_MODEL), bias)],
            out_specs=pl.BlockSpec((EXPERT_BLOCK, D_MODEL), rows),
            scratch_shapes=[pltpu.VMEM((2, D_MODEL, 2 * EXPERT_FF), F32),
                            pltpu.VMEM((2, EXPERT_FF, D_MODEL), F32),
                            pltpu.VMEM((D_MODEL, 2 * EXPERT_FF), BF16),
                            pltpu.VMEM((D_MODEL // LANES, EXPERT_FF, LANES), F32),
                            pltpu.VMEM((EXPERT_FF, D_MODEL), BF16),
                            pltpu.SemaphoreType.DMA((2, 2))]),
        out_shape=jax.ShapeDtypeStruct((n_rows, D_MODEL), F32),
        compiler_params=_params("arbitrary"),
        name="experts",
    )(*block_tables, xs, w1, b1, w2, b2)


def _combine_kernel(final_norm, loc_ref, glob_ref, big_ref, small_ref, tot_big_ref, tot_small_ref,
                    yb_ref, x1_ref, lpos_ref, gate_ref, g_ref, o_ref, gbuf_ref, sem):
    i = pl.program_id(0)
    last = pl.num_programs(0) - 1
    buf = i % 2

    def piece(b, local, glob, n):
        return pltpu.make_async_copy(yb_ref.at[_rows(glob, n), :], gbuf_ref.at[b, _rows(local, n), :], sem.at[b])

    def fetch(step):
        for e in range(N_EXPERTS):
            idx = step * N_EXPERTS + e
            _segment_pieces(big_ref[idx], small_ref[idx], lambda off, n, idx=idx: piece(
                step % 2, loc_ref[idx] + off, glob_ref[idx] + off, n).start())

    @pl.when(i == 0)
    def _():
        gbuf_ref[...] = jnp.zeros_like(gbuf_ref)
        fetch(0)

    @pl.when(i < last)
    def _():
        fetch(i + 1)

    _segment_pieces(tot_big_ref[i], tot_small_ref[i], lambda off, n: piece(buf, 0, 0, n).wait())
    slot = lax.broadcasted_iota(jnp.int32, (SORT_TILE, SORT_SLOTS), 1)
    lpos = lpos_ref[...]
    gates = gate_ref[...]
    weights = jnp.zeros((SORT_TILE, SORT_SLOTS), F32)
    for k in range(TOP_K):
        weights = weights + jnp.where(slot == lpos[:, k:k + 1], gates[:, k:k + 1], 0.0)
    y = x1_ref[...] + jnp.dot(weights.astype(BF16), gbuf_ref[buf].astype(BF16), preferred_element_type=F32)
    o_ref[...] = _rms(y, g_ref[...]) if final_norm else y


def _combine(tables, yb, x1, lpos, gates, g, final_norm):
    t = x1.shape[0]
    tile = lambda i, *_: (i, 0)
    return pl.pallas_call(
        functools.partial(_combine_kernel, final_norm),
        grid_spec=pltpu.PrefetchScalarGridSpec(
            num_scalar_prefetch=6,
            grid=(t // SORT_TILE,),
            in_specs=[pl.BlockSpec(memory_space=pl.ANY),
                      pl.BlockSpec((SORT_TILE, D_MODEL), tile),
                      pl.BlockSpec((SORT_TILE, TOP_K), tile),
                      pl.BlockSpec((SORT_TILE, TOP_K), tile),
                      pl.BlockSpec((1, D_MODEL), lambda i, *_: (0, 0))],
            out_specs=pl.BlockSpec((SORT_TILE, D_MODEL), tile),
            scratch_shapes=[pltpu.VMEM((2, SORT_SLOTS, D_MODEL), F32),
                            pltpu.SemaphoreType.DMA((2,))]),
        out_shape=jax.ShapeDtypeStruct((t, D_MODEL), F32),
        compiler_params=_params("arbitrary"),
        name="combine",
    )(*tables, yb, x1, lpos, gates, g)


def kernel(x, norm1_g, w_in, b_in, attn_sinks, attn_out_g, hgrn_lb_logits, hgrn_out_g, w_out, b_out,
           norm2_g, router_w, router_b, w1, b1, w2, b2, final_g):
    batch, seq, d = x.shape
    t = batch * seq
    depth = w_in.shape[0]
    lower_bounds = jnp.cumsum(jax.nn.softmax(hgrn_lb_logits.astype(F32), axis=0), axis=0)
    assert t % SORT_TILE == 0 and SORT_SLOTS >= SORT_TILE * TOP_K + N_EXPERTS * (SEG_ROWS - 1)
    ntiles = t // SORT_TILE
    nblk = (t * TOP_K + ntiles * N_EXPERTS * (SEG_ROWS - 1)) // EXPERT_BLOCK + N_EXPERTS
    n_rows = nblk * EXPERT_BLOCK
    x2 = x.reshape(t, d)
    for l in range(depth):
        w_in_l = jnp.concatenate([_pair_heads(w_in[l][:, :ATTN_WIDTH], 1), w_in[l][:, ATTN_WIDTH:]], axis=1)
        b_in_l = jnp.concatenate([_pair_heads(b_in[l][:ATTN_WIDTH], 0), b_in[l][ATTN_WIDTH:]])
        aq, ak, av, hq, hf, hi, hg = _inproj(x2, norm1_g[l][None], w_in_l.astype(BF16), b_in_l[None])
        attn = _attention(aq, ak, av, attn_sinks[l], _pair_heads(attn_out_g[l], 0)[None], batch, seq)
        hgo = _hgrn(hq, hf, hi, hg, lower_bounds[l][None], hgrn_out_g[l][None], batch, seq)
        w_out_l = jnp.concatenate([_pair_heads(w_out[l][:ATTN_WIDTH], 0), w_out[l][ATTN_WIDTH:]], axis=0)
        rw_hi = router_w[l].astype(BF16)
        rw_lo = (router_w[l] - rw_hi.astype(F32)).astype(BF16)
        rw_cat = jnp.zeros((d, 2 * LANES), BF16)
        rw_cat = rw_cat.at[:, :N_EXPERTS].set(rw_hi).at[:, LANES:LANES + N_EXPERTS].set(rw_lo)
        x1, h2, lpos, gates, seg = _outproj(
            attn, hgo, x2, w_out_l.astype(BF16), b_out[l][None], norm2_g[l][None],
            rw_cat, router_b[l][None])
        seg = seg.reshape(ntiles, N_EXPERTS).astype(jnp.int32)
        rows_e = jnp.sum(seg, axis=0)
        padded = (rows_e + EXPERT_BLOCK - 1) // EXPERT_BLOCK * EXPERT_BLOCK
        ends = jnp.cumsum(padded)
        pstart = ends - padded
        seg_glob = pstart[None, :] + jnp.cumsum(seg, axis=0) - seg
        seg_loc = jnp.cumsum(seg, axis=1) - seg
        tail = jnp.append(padded - rows_e, n_rows - ends[-1])
        tables = (seg_loc.reshape(-1), seg_glob.reshape(-1),
                  (seg // BIG_PIECE).reshape(-1), (seg % BIG_PIECE // SEG_ROWS).reshape(-1),
                  jnp.sum(seg // BIG_PIECE, axis=1), jnp.sum(seg % BIG_PIECE // SEG_ROWS, axis=1))
        zero_tables = (jnp.append(pstart + rows_e, ends[-1]), tail // BIG_PIECE, tail % BIG_PIECE // SEG_ROWS)
        blk_ids = jnp.arange(nblk, dtype=jnp.int32)
        block_e = jnp.minimum(jnp.sum(blk_ids[:, None] * EXPERT_BLOCK >= ends[None, :], axis=-1), N_EXPERTS - 1)
        n_used = ends[-1] // EXPERT_BLOCK
        first = ((blk_ids == 0) | (block_e != jnp.roll(block_e, 1))) & (blk_ids < n_used)
        slot = (jnp.cumsum(first) - 1) % 2
        e_ids = jnp.arange(N_EXPERTS, dtype=jnp.int32)
        later_used = (e_ids[None, :] > e_ids[:, None]) & (padded[None, :] > 0)
        next_e = jnp.min(jnp.where(later_used, e_ids[None, :], N_EXPERTS), axis=1)
        next_e = jnp.where(next_e == N_EXPERTS, -1, next_e)
        block_tables = tuple(a.astype(jnp.int32) for a in (
            block_e, n_used.reshape(1), first, slot, next_e[block_e]))
        xs = _dispatch(tables, zero_tables, h2, lpos.T, n_rows)
        yb = _experts(block_tables, xs, w1[l], b1[l][:, None, :], w2[l], b2[l][:, None, :])
        x2 = _combine(tables, yb, x1, lpos, gates, final_g[None], l == depth - 1)
    return x2.reshape(batch, seq, d)
```

```python
import functools

import numpy as np
import jax
import jax.numpy as jnp
from jax import lax
from jax.experimental import pallas as pl
from jax.experimental.pallas import tpu as pltpu

F32 = jnp.float32
BF16 = jnp.bfloat16

D_MODEL = 1024
ATTN_Q_HEADS = 8
ATTN_KV_HEADS = 2
ATTN_HEAD_DIM = 64
ATTN_GROUP = ATTN_Q_HEADS // ATTN_KV_HEADS
ATTN_WIDTH = ATTN_Q_HEADS * ATTN_HEAD_DIM
ATTN_KV_WIDTH = ATTN_KV_HEADS * ATTN_HEAD_DIM
WINDOW = 128
HGRN_HEADS = 4
HGRN_DIM = 128
HGRN_WIDTH = HGRN_HEADS * HGRN_DIM
HGRN_CHUNK = 64
HGRN_SUB = 16
HGRN_STEP = 256
IN_WIDTH = ATTN_WIDTH + 2 * ATTN_KV_WIDTH + 4 * HGRN_WIDTH
N_EXPERTS = 32
TOP_K = 4
EXPERT_FF = D_MODEL
SWIGLU_LIMIT = 7.0
SWIGLU_ALPHA = 1.702
NORM_EPS = 1e-5

LANES = 128
ROW_TILE = 512
EXPERT_BLOCK = 512
SORT_TILE = 256
SEG_ROWS = 8
BIG_PIECE = 32
SORT_SLOTS = 1280
VMEM_LIMIT = 56 * 1024 * 1024

_ALIBI = [float(2.0 ** (-8.0 * (h + 1) / ATTN_Q_HEADS)) for h in range(ATTN_Q_HEADS)]


def _rms(x, g):
    return x * lax.rsqrt(jnp.mean(x * x, axis=-1, keepdims=True) + NORM_EPS) * g


def _params(*sem):
    return pltpu.CompilerParams(dimension_semantics=sem, vmem_limit_bytes=VMEM_LIMIT)


_IN_SPLITS = (ATTN_WIDTH, ATTN_KV_WIDTH, ATTN_KV_WIDTH, HGRN_WIDTH, HGRN_WIDTH, HGRN_WIDTH, HGRN_WIDTH)


def _inproj_kernel(x_ref, g_ref, w_ref, b_ref, *out_refs):
    h = _rms(x_ref[...], g_ref[...]).astype(BF16)
    lo = 0
    for ref, width in zip(out_refs, _IN_SPLITS):
        ref[...] = jnp.dot(h, w_ref[:, lo:lo + width], preferred_element_type=F32) + b_ref[:, lo:lo + width]
        lo += width


def _inproj(x2, g, w_bf, b):
    t = x2.shape[0]
    row = lambda i: (i, 0)
    fixed = lambda i: (0, 0)
    return pl.pallas_call(
        _inproj_kernel,
        grid=(t // ROW_TILE,),
        in_specs=[pl.BlockSpec((ROW_TILE, D_MODEL), row),
                  pl.BlockSpec((1, D_MODEL), fixed),
                  pl.BlockSpec((D_MODEL, IN_WIDTH), fixed),
                  pl.BlockSpec((1, IN_WIDTH), fixed)],
        out_specs=[pl.BlockSpec((ROW_TILE, w), row) for w in _IN_SPLITS],
        out_shape=[jax.ShapeDtypeStruct((t, w), F32) for w in _IN_SPLITS],
        compiler_params=_params("parallel"),
        name="inproj",
    )(x2, g, w_bf, b)


def _attn_bias_tables():
    qi = np.arange(WINDOW)[:, None]
    ki = np.arange(2 * WINDOW)[None, :]
    dist = WINDOW + qi - ki
    in_window = (dist >= 0) & (dist < WINDOW)
    slopes = np.asarray(_ALIBI, np.float32)[:, None, None]
    bias = -(slopes * dist.astype(np.float32)[None])
    tables = [np.where(in_window & (ki >= WINDOW), bias, -np.inf), np.where(in_window, bias, -np.inf)]
    return np.stack(tables).astype(np.float32)


def _pair_heads(a, axis):
    shape = a.shape
    a = a.reshape(shape[:axis] + (ATTN_KV_HEADS, ATTN_GROUP, ATTN_HEAD_DIM) + shape[axis + 1:])
    return jnp.swapaxes(a, axis, axis + 1).reshape(shape)


def _attn_kernel(sink_ref, q_ref, kp_ref, kc_ref, vp_ref, vc_ref, bias_ref, g_ref, o_ref):
    k2 = jnp.concatenate([kp_ref[...], kc_ref[...]], axis=0).astype(BF16)
    v2 = jnp.concatenate([vp_ref[...], vc_ref[...]], axis=0).astype(BF16)
    scale = ATTN_HEAD_DIM ** -0.5
    nt = (((1,), (1,)), ((), ()))
    low = lax.broadcasted_iota(jnp.int32, (WINDOW, LANES), 1) < ATTN_HEAD_DIM

    def softmax(s, h):
        s = s * scale + bias_ref[h]
        sink = sink_ref[h]
        m = jnp.maximum(jnp.max(s, axis=-1, keepdims=True), sink)
        p = jnp.exp(s - m)
        den = jnp.sum(p, axis=-1, keepdims=True) + jnp.exp(sink - m)
        return (p / den).astype(BF16)

    scores = []
    for j in range(ATTN_GROUP):
        q = q_ref[:, j * LANES:(j + 1) * LANES]
        scores.append((lax.dot_general(jnp.where(low, q, 0.0).astype(BF16), k2, nt, preferred_element_type=F32),
                       lax.dot_general(jnp.where(low, 0.0, q).astype(BF16), k2, nt, preferred_element_type=F32)))
    probs = [(softmax(s_lo, j), softmax(s_hi, ATTN_GROUP + j)) for j, (s_lo, s_hi) in enumerate(scores)]
    outs = [jnp.where(low, jnp.dot(p_lo, v2, preferred_element_type=F32),
                      jnp.dot(p_hi, v2, preferred_element_type=F32)) for p_lo, p_hi in probs]
    o_ref[...] = _rms(jnp.concatenate(outs, axis=1), g_ref[...])


def _attention(aq, ak, av, sinks, g, batch, seq):
    nb = seq // WINDOW
    aq = aq.reshape(batch, seq, ATTN_WIDTH)
    ak = ak.reshape(batch, seq, ATTN_KV_WIDTH)
    av = av.reshape(batch, seq, ATTN_KV_WIDTH)
    cur = lambda b, n, s: (b, n, 0)
    prev = lambda b, n, s: (b, jnp.maximum(n - 1, 0), 0)
    out = pl.pallas_call(
        _attn_kernel,
        grid_spec=pltpu.PrefetchScalarGridSpec(
            num_scalar_prefetch=1,
            grid=(batch, nb),
            in_specs=[pl.BlockSpec((None, WINDOW, ATTN_WIDTH), cur),
                      pl.BlockSpec((None, WINDOW, ATTN_KV_WIDTH), prev),
                      pl.BlockSpec((None, WINDOW, ATTN_KV_WIDTH), cur),
                      pl.BlockSpec((None, WINDOW, ATTN_KV_WIDTH), prev),
                      pl.BlockSpec((None, WINDOW, ATTN_KV_WIDTH), cur),
                      pl.BlockSpec((None, ATTN_Q_HEADS, WINDOW, 2 * WINDOW),
                                   lambda b, n, s: (jnp.minimum(n, 1), 0, 0, 0)),
                      pl.BlockSpec((1, ATTN_WIDTH), lambda b, n, s: (0, 0))],
            out_specs=pl.BlockSpec((None, WINDOW, ATTN_WIDTH), cur)),
        out_shape=jax.ShapeDtypeStruct((batch, seq, ATTN_WIDTH), F32),
        compiler_params=_params("parallel", "parallel"),
        name="attn",
    )(sinks, aq, ak, ak, av, av, jnp.asarray(_attn_bias_tables()), g)
    return out.reshape(batch * seq, ATTN_WIDTH)


HGRN_EXP_GUARD = 80.0


def _hgrn_kernel(guard, q_ref, f_ref, i_ref, gate_ref, lb_ref, og_ref, o_ref, st_ref, q_scr, k_scr, b_scr):
    step = pl.program_id(1)

    @pl.when(step == 0)
    def _():
        st_ref[...] = jnp.zeros_like(st_ref)

    C, S = HGRN_CHUNK, HGRN_SUB
    nsub = C // S
    nchunk = HGRN_STEP // C
    r_i = lax.broadcasted_iota(jnp.int32, (C, C), 0)
    c_i = lax.broadcasted_iota(jnp.int32, (C, C), 1)
    nt = (((1,), (1,)), ((), ()))

    tri = (r_i >= c_i).astype(F32)
    qx = q_ref[...]
    q_scr[...] = qx * jax.nn.sigmoid(qx)
    lb = lb_ref[...]
    f = lb + (1.0 - lb) * jax.nn.sigmoid(f_ref[...])
    k_scr[...] = 1.0 - f
    logf = jnp.log(f)
    for c in range(nchunk):
        b_scr[c * C:(c + 1) * C, :] = jnp.dot(tri, logf[c * C:(c + 1) * C, :], preferred_element_type=F32,
                                              precision=lax.Precision.HIGHEST)
    decay = jnp.concatenate([-jnp.sum(logf[i * S:(i + 1) * S, :], axis=0, keepdims=True)
                             for i in range(HGRN_STEP // S)], axis=0)
    risky = jnp.max(decay) > guard

    rowblk = lax.broadcasted_iota(jnp.int32, (C, HGRN_DIM), 0) // S
    pairs = [(c, h) for c in range(nchunk) for h in range(HGRN_HEADS)]

    def view(ref, c, h):
        return ref[c * C:(c + 1) * C, h * HGRN_DIM:(h + 1) * HGRN_DIM]

    def finish(intra):
        q_dec, gain, update = {}, {}, {}
        for c, h in pairs:
            q, k, b = view(q_scr, c, h), view(k_scr, c, h), view(b_scr, c, h)
            q_dec[c, h] = (q * jnp.exp(b)).astype(BF16)
            bl = b[C - 1:C, :]
            gain[c, h] = jnp.exp(bl)
            update[c, h] = lax.dot_general(view(i_ref, c, h).astype(BF16), (k * jnp.exp(bl - b)).astype(BF16),
                                           (((0,), (0,)), ((), ())), preferred_element_type=F32)
        state = {}
        for h in range(HGRN_HEADS):
            st = st_ref[h]
            for c in range(nchunk):
                state[c, h] = st
                st = st * gain[c, h] + update[c, h]
            st_ref[h] = st
        for c, h in pairs:
            o = intra[c, h] + lax.dot_general(q_dec[c, h], state[c, h].astype(BF16), nt,
                                              preferred_element_type=F32)
            gx = view(gate_ref, c, h)
            og = og_ref[:, h * HGRN_DIM:(h + 1) * HGRN_DIM]
            o = o * lax.rsqrt(jnp.mean(o * o, axis=-1, keepdims=True) + NORM_EPS) * og
            o_ref[c * C:(c + 1) * C, h * HGRN_DIM:(h + 1) * HGRN_DIM] = o * (gx * jax.nn.sigmoid(gx))

    def sub_block_queries(q, b):
        return [jnp.where(rowblk == i, q * jnp.exp(jnp.minimum(b - b[i * S:i * S + 1, :], 0.0)), 0.0)
                for i in range(nsub)]

    @pl.when(jnp.logical_not(risky))
    def _():
        att = {}
        for c, h in pairs:
            q, k, b = view(q_scr, c, h), view(k_scr, c, h), view(b_scr, c, h)
            k_sub = [jnp.where(rowblk <= i, k * jnp.exp(jnp.minimum(b[i * S:i * S + 1, :] - b, guard)), 0.0)
                     for i in range(nsub)]
            att[c, h] = lax.dot_general(jnp.concatenate(sub_block_queries(q, b), axis=1).astype(BF16),
                                        jnp.concatenate(k_sub, axis=1).astype(BF16), nt,
                                        preferred_element_type=F32)
        finish({(c, h): jnp.dot(jnp.where(r_i >= c_i, att[c, h], 0.0).astype(BF16),
                                view(i_ref, c, h).astype(BF16), preferred_element_type=F32) for c, h in pairs})

    @pl.when(risky)
    def _():
        sub_r = lax.broadcasted_iota(jnp.int32, (S, HGRN_DIM), 0)
        intra = {}
        for c, h in pairs:
            q, k, b, v = view(q_scr, c, h), view(k_scr, c, h), view(b_scr, c, h), view(i_ref, c, h)
            k_sub = [jnp.where(rowblk < i, k * jnp.exp(jnp.minimum(b[i * S:i * S + 1, :] - b, 0.0)), 0.0)
                     for i in range(1, nsub)]
            att = lax.dot_general(jnp.concatenate(sub_block_queries(q, b)[1:], axis=1).astype(BF16),
                                  jnp.concatenate(k_sub, axis=1).astype(BF16), nt, preferred_element_type=F32)
            diag = []
            for i in range(nsub):
                b_blk = b[i * S:(i + 1) * S, :]
                q_blk = q[i * S:(i + 1) * S, :]
                acc = jnp.zeros((S, HGRN_DIM), F32)
                for s_ in range(S):
                    r = i * S + s_
                    e = jnp.exp(jnp.minimum(b_blk - b[r:r + 1, :], 0.0))
                    a = jnp.where(sub_r >= s_, q_blk * e * k[r:r + 1, :], 0.0)
                    acc = acc + jnp.sum(a, axis=-1, keepdims=True) * v[r:r + 1, :]
                diag.append(acc)
            intra[c, h] = (jnp.dot(att.astype(BF16), v.astype(BF16), preferred_element_type=F32)
                           + jnp.concatenate(diag, axis=0))
        finish(intra)


def _hgrn(hq, hf, hi, hg, lb, og, batch, seq, guard=HGRN_EXP_GUARD):
    shp = (batch, seq, HGRN_WIDTH)
    blk = pl.BlockSpec((None, HGRN_STEP, HGRN_WIDTH), lambda b, c: (b, c, 0))
    vec = pl.BlockSpec((1, HGRN_WIDTH), lambda b, c: (0, 0))
    out = pl.pallas_call(
        functools.partial(_hgrn_kernel, guard),
        grid=(batch, seq // HGRN_STEP),
        in_specs=[blk, blk, blk, blk, vec, vec],
        out_specs=blk,
        out_shape=jax.ShapeDtypeStruct(shp, F32),
        scratch_shapes=[pltpu.VMEM((HGRN_HEADS, HGRN_DIM, HGRN_DIM), F32)]
        + [pltpu.VMEM((HGRN_STEP, HGRN_WIDTH), F32)] * 3,
        compiler_params=_params("parallel", "arbitrary"),
        name="hgrn",
    )(hq.reshape(shp), hf.reshape(shp), hi.reshape(shp), hg.reshape(shp), lb, og)
    return out.reshape(batch * seq, HGRN_WIDTH)


def _outproj_kernel(attn_ref, hg_ref, x_ref, wo_ref, bo_ref, g2_ref, rw_ref, rb_ref,
                    x1_ref, h2_ref, lpos_ref, gate_ref, seg_ref):
    tm = x_ref.shape[0]
    y = jnp.dot(attn_ref[...].astype(BF16), wo_ref[:ATTN_WIDTH, :], preferred_element_type=F32)
    y = y + jnp.dot(hg_ref[...].astype(BF16), wo_ref[ATTN_WIDTH:, :], preferred_element_type=F32)
    x1 = x_ref[...] + y + bo_ref[...]
    x1_ref[...] = x1
    h2 = _rms(x1, g2_ref[...])
    h2_ref[...] = h2.astype(BF16)
    h_hi = h2.astype(BF16)
    h_lo = (h2 - h_hi.astype(F32)).astype(BF16)
    p_hi = jnp.dot(h_hi, rw_ref[...], preferred_element_type=F32)
    p_lo = jnp.dot(h_lo, rw_ref[...], preferred_element_type=F32)
    logits = (p_hi[:, :N_EXPERTS] + p_hi[:, LANES:LANES + N_EXPERTS]) + p_lo[:, :N_EXPERTS] + rb_ref[...]
    lane = lax.broadcasted_iota(jnp.int32, (tm, N_EXPERTS), 1)
    work = logits
    vals, hots = [], []
    for _ in range(TOP_K):
        m = jnp.max(work, axis=-1, keepdims=True)
        idx = jnp.min(jnp.where(work == m, lane, N_EXPERTS), axis=-1, keepdims=True)
        hot = lane == idx
        vals.append(m)
        hots.append(hot)
        work = jnp.where(hot, -jnp.inf, work)
    ex = [jnp.exp(v - vals[0]) for v in vals]
    den = ex[0] + ex[1] + ex[2] + ex[3]
    sel = jnp.zeros((tm, N_EXPERTS), F32)
    for hot in hots:
        sel = sel + hot.astype(F32)
    r_i = lax.broadcasted_iota(jnp.int32, (tm, tm), 0)
    c_i = lax.broadcasted_iota(jnp.int32, (tm, tm), 1)
    strict = jnp.where(r_i > c_i, 1.0, 0.0).astype(BF16)
    ranks = jnp.dot(strict, sel.astype(BF16), preferred_element_type=F32)
    seg = jnp.floor((jnp.sum(sel, axis=0, keepdims=True) + (SEG_ROWS - 1.0)) * (1.0 / SEG_ROWS)) * SEG_ROWS
    e_r = lax.broadcasted_iota(jnp.int32, (N_EXPERTS, N_EXPERTS), 0)
    e_c = lax.broadcasted_iota(jnp.int32, (N_EXPERTS, N_EXPERTS), 1)
    before = jnp.where(e_r < e_c, 1.0, 0.0).astype(BF16)
    seg_off = jnp.dot(jnp.broadcast_to(seg, (SEG_ROWS, N_EXPERTS)).astype(BF16), before,
                      preferred_element_type=F32)[0:1, :]
    slots = ranks + seg_off
    l4 = lax.broadcasted_iota(jnp.int32, (tm, TOP_K), 1)
    lpos4 = jnp.zeros((tm, TOP_K), F32)
    gate4 = jnp.zeros((tm, TOP_K), F32)
    for j in range(TOP_K):
        sj = jnp.sum(jnp.where(hots[j], slots, 0.0), axis=-1, keepdims=True)
        lpos4 = jnp.where(l4 == j, sj, lpos4)
        gate4 = jnp.where(l4 == j, ex[j] / den, gate4)
    lpos_ref[...] = lpos4.astype(jnp.int32)
    gate_ref[...] = gate4
    seg_ref[...] = seg


def _outproj(attn, hgo, x2, wo_bf, bo, g2, rw, rb):
    t = x2.shape[0]
    row = lambda i: (i, 0)
    fixed = lambda i: (0, 0)
    return pl.pallas_call(
        _outproj_kernel,
        grid=(t // SORT_TILE,),
        in_specs=[pl.BlockSpec((SORT_TILE, ATTN_WIDTH), row),
                  pl.BlockSpec((SORT_TILE, HGRN_WIDTH), row),
                  pl.BlockSpec((SORT_TILE, D_MODEL), row),
                  pl.BlockSpec((ATTN_WIDTH + HGRN_WIDTH, D_MODEL), fixed),
                  pl.BlockSpec((1, D_MODEL), fixed),
                  pl.BlockSpec((1, D_MODEL), fixed),
                  pl.BlockSpec((D_MODEL, 2 * LANES), fixed),
                  pl.BlockSpec((1, N_EXPERTS), fixed)],
        out_specs=[pl.BlockSpec((SORT_TILE, D_MODEL), row),
                   pl.BlockSpec((SORT_TILE, D_MODEL), row),
                   pl.BlockSpec((SORT_TILE, TOP_K), row),
                   pl.BlockSpec((SORT_TILE, TOP_K), row),
                   pl.BlockSpec((None, 1, N_EXPERTS), lambda i: (i, 0, 0))],
        out_shape=[jax.ShapeDtypeStruct((t, D_MODEL), F32),
                   jax.ShapeDtypeStruct((t, D_MODEL), BF16),
                   jax.ShapeDtypeStruct((t, TOP_K), jnp.int32),
                   jax.ShapeDtypeStruct((t, TOP_K), F32),
                   jax.ShapeDtypeStruct((t // SORT_TILE, 1, N_EXPERTS), F32)],
        compiler_params=_params("parallel"),
        name="outproj",
    )(attn, hgo, x2, wo_bf, bo, g2, rw, rb)


def _rows(start, n):
    return pl.ds(pl.multiple_of(start, SEG_ROWS), n)


def _repeat(cnt, fn):
    def body(j, carry):
        fn(j)
        return carry

    lax.fori_loop(0, cnt, body, 0)


def _segment_pieces(big_cnt, small_cnt, fn):
    _repeat(big_cnt, lambda j: fn(j * BIG_PIECE, BIG_PIECE))
    _repeat(small_cnt, lambda j: fn(big_cnt * BIG_PIECE + j * SEG_ROWS, SEG_ROWS))


def _dispatch_kernel(loc_ref, glob_ref, big_ref, small_ref, tot_big_ref, tot_small_ref,
                     zdst_ref, zbig_ref, zsmall_ref,
                     h_ref, lpos_t_ref, xs_ref, lbuf_ref, zbuf_ref, sem, zsem):
    i = pl.program_id(0)
    last = pl.num_programs(0) - 1
    buf = i % 2

    def piece(b, local, glob, n):
        return pltpu.make_async_copy(lbuf_ref.at[b, _rows(local, n), :], xs_ref.at[_rows(glob, n), :], sem.at[b])

    def drain(step):
        _segment_pieces(tot_big_ref[step], tot_small_ref[step], lambda off, n: piece(step % 2, 0, 0, n).wait())

    def zero_piece(glob, n):
        return pltpu.make_async_copy(zbuf_ref.at[_rows(0, n), :], xs_ref.at[_rows(glob, n), :], zsem)

    @pl.when(i == 0)
    def _():
        zbuf_ref[...] = jnp.zeros_like(zbuf_ref)
        for e in range(N_EXPERTS + 1):
            _segment_pieces(zbig_ref[e], zsmall_ref[e],
                            lambda off, n, e=e: zero_piece(zdst_ref[e] + off, n).start())
        for e in range(N_EXPERTS + 1):
            _segment_pieces(zbig_ref[e], zsmall_ref[e], lambda off, n: zero_piece(0, n).wait())

    @pl.when(i >= 2)
    def _():
        drain(i - 2)

    slot = lax.broadcasted_iota(jnp.int32, (SORT_SLOTS, SORT_TILE), 0)
    onehot = jnp.zeros((SORT_SLOTS, SORT_TILE), F32)
    for k in range(TOP_K):
        onehot = onehot + jnp.where(slot == lpos_t_ref[k:k + 1, :], 1.0, 0.0)
    lbuf_ref[buf] = jnp.dot(onehot.astype(BF16), h_ref[...], preferred_element_type=F32)
    for e in range(N_EXPERTS):
        idx = i * N_EXPERTS + e
        _segment_pieces(big_ref[idx], small_ref[idx], lambda off, n, idx=idx: piece(
            buf, loc_ref[idx] + off, glob_ref[idx] + off, n).start())

    @pl.when(i == last)
    def _():
        @pl.when(i >= 1)
        def _():
            drain(i - 1)

        drain(i)


def _dispatch(tables, zero_tables, h2, lpos_t, n_rows):
    t = h2.shape[0]
    tile = lambda i, *_: (i, 0)
    return pl.pallas_call(
        _dispatch_kernel,
        grid_spec=pltpu.PrefetchScalarGridSpec(
            num_scalar_prefetch=9,
            grid=(t // SORT_TILE,),
            in_specs=[pl.BlockSpec((SORT_TILE, D_MODEL), tile),
                      pl.BlockSpec((TOP_K, SORT_TILE), lambda i, *_: (0, i))],
            out_specs=pl.BlockSpec(memory_space=pl.ANY),
            scratch_shapes=[pltpu.VMEM((2, SORT_SLOTS, D_MODEL), F32),
                            pltpu.VMEM((BIG_PIECE, D_MODEL), F32),
                            pltpu.SemaphoreType.DMA((2,)),
                            pltpu.SemaphoreType.DMA(())]),
        out_shape=jax.ShapeDtypeStruct((n_rows, D_MODEL), F32),
        compiler_params=_params("arbitrary"),
        name="dispatch",
    )(*tables, *zero_tables, h2, lpos_t)


CAST_ROWS = 128


def _expert_kernel(be_ref, nb_ref, first_ref, slot_ref, next_ref,
                   xs_ref, w1_hbm, b1_ref, w2_hbm, b2_ref, y_ref,
                   w1f_ref, w2f_ref, w1b_ref, w2i_ref, w2b_ref, sem):
    blk = pl.program_id(0)
    half = LANES // 2

    def weight_copies(e, s):
        return (pltpu.make_async_copy(w1_hbm.at[e], w1f_ref.at[s], sem.at[0, s]),
                pltpu.make_async_copy(w2_hbm.at[e], w2f_ref.at[s], sem.at[1, s]))

    @pl.when(first_ref[blk] == 1)
    def _():
        e = be_ref[blk]
        s = slot_ref[blk]

        @pl.when(blk == 0)
        def _():
            for c in weight_copies(e, s):
                c.start()

        @pl.when(next_ref[blk] >= 0)
        def _():
            for c in weight_copies(next_ref[blk], 1 - s):
                c.start()

        for c in weight_copies(e, s):
            c.wait()

        def cast_rows(r, carry):
            rows = pl.ds(pl.multiple_of(r * CAST_ROWS, CAST_ROWS), CAST_ROWS)
            w1b_ref[rows, :] = w1f_ref[s, rows, :].astype(BF16)
            return carry

        lax.fori_loop(0, D_MODEL // CAST_ROWS, cast_rows, 0)
        for c in range(D_MODEL // LANES):
            cols = slice(c * LANES, (c + 1) * LANES)
            for m in range(EXPERT_FF // LANES):
                lo = m * LANES
                w2i_ref[c, pl.ds(lo, half, stride=2), :] = w2f_ref[s, lo:lo + half, cols]
                w2i_ref[c, pl.ds(lo + 1, half, stride=2), :] = w2f_ref[s, lo + half:lo + LANES, cols]
            w2b_ref[:, cols] = w2i_ref[c].astype(BF16)

    @pl.when(blk < nb_ref[0])
    def _():
        x = xs_ref[...].astype(BF16)
        hid = jnp.dot(x, w1b_ref[...], preferred_element_type=F32) + b1_ref[...]
        even = (lax.broadcasted_iota(jnp.int32, (x.shape[0], LANES), 1) & 1) == 0
        glu, lin = [], []
        for m in range(EXPERT_FF // LANES):
            ha = hid[:, 2 * m * LANES:(2 * m + 1) * LANES]
            hb = hid[:, (2 * m + 1) * LANES:(2 * m + 2) * LANES]
            glu.append(jnp.where(even, ha, pltpu.roll(hb, 1, axis=1)))
            lin.append(jnp.where(even, pltpu.roll(ha, LANES - 1, axis=1), hb))
        glu = jnp.minimum(jnp.concatenate(glu, axis=1), SWIGLU_LIMIT)
        lin = jnp.clip(jnp.concatenate(lin, axis=1), -SWIGLU_LIMIT, SWIGLU_LIMIT)
        act = glu * jax.nn.sigmoid(SWIGLU_ALPHA * glu) * (lin + 1.0)
        y_ref[...] = jnp.dot(act.astype(BF16), w2b_ref[...], preferred_element_type=F32) + b2_ref[...]

    @pl.when(blk >= nb_ref[0])
    def _():
        y_ref[...] = jnp.zeros_like(y_ref)


def _experts(block_tables, xs, w1, b1, w2, b2):
    n_rows = xs.shape[0]
    nblk = n_rows // EXPERT_BLOCK
    rows = lambda b, *_: (b, 0)
    used_rows = lambda b, be, nb, *_: (jnp.minimum(b, nb[0] - 1), 0)
    bias = lambda b, be, *_: (be[b], 0, 0)
    return pl.pallas_call(
        _expert_kernel,
        grid_spec=pltpu.PrefetchScalarGridSpec(
            num_scalar_prefetch=5,
            grid=(nblk,),
            in_specs=[pl.BlockSpec((EXPERT_BLOCK, D_MODEL), used_rows),
                      pl.BlockSpec(memory_space=pl.ANY),
                      pl.BlockSpec((None, 1, 2 * EXPERT_FF), bias),
                      pl.BlockSpec(memory_space=pl.ANY),
                      pl.BlockSpec((None, 1, D_MODEL), bias)],
            out_specs=pl.BlockSpec((EXPERT_BLOCK, D_MODEL), rows),
            scratch_shapes=[pltpu.VMEM((2, D_MODEL, 2 * EXPERT_FF), F32),
                            pltpu.VMEM((2, EXPERT_FF, D_MODEL), F32),
                            pltpu.VMEM((D_MODEL, 2 * EXPERT_FF), BF16),
                            pltpu.VMEM((D_MODEL // LANES, EXPERT_FF, LANES), F32),
                            pltpu.VMEM((EXPERT_FF, D_MODEL), BF16),
                            pltpu.SemaphoreType.DMA((2, 2))]),
        out_shape=jax.ShapeDtypeStruct((n_rows, D_MODEL), F32),
        compiler_params=_params("arbitrary"),
        name="experts",
    )(*block_tables, xs, w1, b1, w2, b2)


def _combine_kernel(final_norm, loc_ref, glob_ref, big_ref, small_ref, tot_big_ref, tot_small_ref,
                    yb_ref, x1_ref, lpos_ref, gate_ref, g_ref, o_ref, gbuf_ref, sem):
    i = pl.program_id(0)
    last = pl.num_programs(0) - 1
    buf = i % 2

    def piece(b, local, glob, n):
        return pltpu.make_async_copy(yb_ref.at[_rows(glob, n), :], gbuf_ref.at[b, _rows(local, n), :], sem.at[b])

    def fetch(step):
        for e in range(N_EXPERTS):
            idx = step * N_EXPERTS + e
            _segment_pieces(big_ref[idx], small_ref[idx], lambda off, n, idx=idx: piece(
                step % 2, loc_ref[idx] + off, glob_ref[idx] + off, n).start())

    @pl.when(i == 0)
    def _():
        gbuf_ref[...] = jnp.zeros_like(gbuf_ref)
        fetch(0)

    @pl.when(i < last)
    def _():
        fetch(i + 1)

    _segment_pieces(tot_big_ref[i], tot_small_ref[i], lambda off, n: piece(buf, 0, 0, n).wait())
    slot = lax.broadcasted_iota(jnp.int32, (SORT_TILE, SORT_SLOTS), 1)
    lpos = lpos_ref[...]
    gates = gate_ref[...]
    weights = jnp.zeros((SORT_TILE, SORT_SLOTS), F32)
    for k in range(TOP_K):
        weights = weights + jnp.where(slot == lpos[:, k:k + 1], gates[:, k:k + 1], 0.0)
    y = x1_ref[...] + jnp.dot(weights.astype(BF16), gbuf_ref[buf].astype(BF16), preferred_element_type=F32)
    o_ref[...] = _rms(y, g_ref[...]) if final_norm else y


def _combine(tables, yb, x1, lpos, gates, g, final_norm):
    t = x1.shape[0]
    tile = lambda i, *_: (i, 0)
    return pl.pallas_call(
        functools.partial(_combine_kernel, final_norm),
        grid_spec=pltpu.PrefetchScalarGridSpec(
            num_scalar_prefetch=6,
            grid=(t // SORT_TILE,),
            in_specs=[pl.BlockSpec(memory_space=pl.ANY),
                      pl.BlockSpec((SORT_TILE, D_MODEL), tile),
                      pl.BlockSpec((SORT_TILE, TOP_K), tile),
                      pl.BlockSpec((SORT_TILE, TOP_K), tile),
                      pl.BlockSpec((1, D_MODEL), lambda i, *_: (0, 0))],
            out_specs=pl.BlockSpec((SORT_TILE, D_MODEL), tile),
            scratch_shapes=[pltpu.VMEM((2, SORT_SLOTS, D_MODEL), F32),
                            pltpu.SemaphoreType.DMA((2,))]),
        out_shape=jax.ShapeDtypeStruct((t, D_MODEL), F32),
        compiler_params=_params("arbitrary"),
        name="combine",
    )(*tables, yb, x1, lpos, gates, g)


def kernel(x, norm1_g, w_in, b_in, attn_sinks, attn_out_g, hgrn_lb_logits, hgrn_out_g, w_out, b_out,
           norm2_g, router_w, router_b, w1, b1, w2, b2, final_g):
    batch, seq, d = x.shape
    t = batch * seq
    depth = w_in.shape[0]
    lower_bounds = jnp.cumsum(jax.nn.softmax(hgrn_lb_logits.astype(F32), axis=0), axis=0)
    assert t % SORT_TILE == 0 and SORT_SLOTS >= SORT_TILE * TOP_K + N_EXPERTS * (SEG_ROWS - 1)
    ntiles = t // SORT_TILE
    nblk = (t * TOP_K + ntiles * N_EXPERTS * (SEG_ROWS - 1)) // EXPERT_BLOCK + N_EXPERTS
    n_rows = nblk * EXPERT_BLOCK
    x2 = x.reshape(t, d)
    for l in range(depth):
        w_in_l = jnp.concatenate([_pair_heads(w_in[l][:, :ATTN_WIDTH], 1), w_in[l][:, ATTN_WIDTH:]], axis=1)
        b_in_l = jnp.concatenate([_pair_heads(b_in[l][:ATTN_WIDTH], 0), b_in[l][ATTN_WIDTH:]])
        aq, ak, av, hq, hf, hi, hg = _inproj(x2, norm1_g[l][None], w_in_l.astype(BF16), b_in_l[None])
        attn = _attention(aq, ak, av, attn_sinks[l], _pair_heads(attn_out_g[l], 0)[None], batch, seq)
        hgo = _hgrn(hq, hf, hi, hg, lower_bounds[l][None], hgrn_out_g[l][None], batch, seq)
        w_out_l = jnp.concatenate([_pair_heads(w_out[l][:ATTN_WIDTH], 0), w_out[l][ATTN_WIDTH:]], axis=0)
        rw_hi = router_w[l].astype(BF16)
        rw_lo = (router_w[l] - rw_hi.astype(F32)).astype(BF16)
        rw_cat = jnp.zeros((d, 2 * LANES), BF16)
        rw_cat = rw_cat.at[:, :N_EXPERTS].set(rw_hi).at[:, LANES:LANES + N_EXPERTS].set(rw_lo)
        x1, h2, lpos, gates, seg = _outproj(
            attn, hgo, x2, w_out_l.astype(BF16), b_out[l][None], norm2_g[l][None],
            rw_cat, router_b[l][None])
        seg = seg.reshape(ntiles, N_EXPERTS).astype(jnp.int32)
        rows_e = jnp.sum(seg, axis=0)
        padded = (rows_e + EXPERT_BLOCK - 1) // EXPERT_BLOCK * EXPERT_BLOCK
        ends = jnp.cumsum(padded)
        pstart = ends - padded
        seg_glob = pstart[None, :] + jnp.cumsum(seg, axis=0) - seg
        seg_loc = jnp.cumsum(seg, axis=1) - seg
        tail = jnp.append(padded - rows_e, n_rows - ends[-1])
        tables = (seg_loc.reshape(-1), seg_glob.reshape(-1),
                  (seg // BIG_PIECE).reshape(-1), (seg % BIG_PIECE // SEG_ROWS).reshape(-1),
                  jnp.sum(seg // BIG_PIECE, axis=1), jnp.sum(seg % BIG_PIECE // SEG_ROWS, axis=1))
        zero_tables = (jnp.append(pstart + rows_e, ends[-1]), tail // BIG_PIECE, tail % BIG_PIECE // SEG_ROWS)
        blk_ids = jnp.arange(nblk, dtype=jnp.int32)
        block_e = jnp.minimum(jnp.sum(blk_ids[:, None] * EXPERT_BLOCK >= ends[None, :], axis=-1), N_EXPERTS - 1)
        n_used = ends[-1] // EXPERT_BLOCK
        first = ((blk_ids == 0) | (block_e != jnp.roll(block_e, 1))) & (blk_ids < n_used)
        slot = (jnp.cumsum(first) - 1) % 2
        e_ids = jnp.arange(N_EXPERTS, dtype=jnp.int32)
        later_used = (e_ids[None, :] > e_ids[:, None]) & (padded[None, :] > 0)
        next_e = jnp.min(jnp.where(later_used, e_ids[None, :], N_EXPERTS), axis=1)
        next_e = jnp.where(next_e == N_EXPERTS, -1, next_e)
        block_tables = tuple(a.astype(jnp.int32) for a in (
            block_e, n_used.reshape(1), first, slot, next_e[block_e]))
        xs = _dispatch(tables, zero_tables, h2, lpos.T, n_rows)
        yb = _experts(block_tables, xs, w1[l], b1[l][:, None, :], w2[l], b2[l][:, None, :])
        x2 = _combine(tables, yb, x1, lpos, gates, final_g[None], l == depth - 1)
    return x2.reshape(batch, seq, d)
```

```python
import functools

import numpy as np
import jax
import jax.numpy as jnp
from jax import lax
from jax.experimental import pallas as pl
from jax.experimental.pallas import tpu as pltpu

F32 = jnp.float32
BF16 = jnp.bfloat16

D_MODEL = 1024
ATTN_Q_HEADS = 8
ATTN_KV_HEADS = 2
ATTN_HEAD_DIM = 64
ATTN_GROUP = ATTN_Q_HEADS // ATTN_KV_HEADS
ATTN_WIDTH = ATTN_Q_HEADS * ATTN_HEAD_DIM
ATTN_KV_WIDTH = ATTN_KV_HEADS * ATTN_HEAD_DIM
WINDOW = 128
HGRN_HEADS = 4
HGRN_DIM = 128
HGRN_WIDTH = HGRN_HEADS * HGRN_DIM
HGRN_CHUNK = 64
HGRN_SUB = 16
HGRN_STEP = 256
IN_WIDTH = ATTN_WIDTH + 2 * ATTN_KV_WIDTH + 4 * HGRN_WIDTH
N_EXPERTS = 32
TOP_K = 4
EXPERT_FF = D_MODEL
SWIGLU_LIMIT = 7.0
SWIGLU_ALPHA = 1.702
NORM_EPS = 1e-5

LANES = 128
ROW_TILE = 512
EXPERT_BLOCK = 512
SORT_TILE = 256
SEG_ROWS = 8
BIG_PIECE = 32
SORT_SLOTS = 1280
VMEM_LIMIT = 56 * 1024 * 1024

_ALIBI = [float(2.0 ** (-8.0 * (h + 1) / ATTN_Q_HEADS)) for h in range(ATTN_Q_HEADS)]


def _rms(x, g):
    return x * lax.rsqrt(jnp.mean(x * x, axis=-1, keepdims=True) + NORM_EPS) * g


def _params(*sem):
    return pltpu.CompilerParams(dimension_semantics=sem, vmem_limit_bytes=VMEM_LIMIT)


_IN_SPLITS = (ATTN_WIDTH, ATTN_KV_WIDTH, ATTN_KV_WIDTH, HGRN_WIDTH, HGRN_WIDTH, HGRN_WIDTH, HGRN_WIDTH)


def _inproj_kernel(x_ref, g_ref, w_ref, b_ref, *out_refs):
    h = _rms(x_ref[...], g_ref[...]).astype(BF16)
    lo = 0
    for ref, width in zip(out_refs, _IN_SPLITS):
        ref[...] = jnp.dot(h, w_ref[:, lo:lo + width], preferred_element_type=F32) + b_ref[:, lo:lo + width]
        lo += width


def _inproj(x2, g, w_bf, b):
    t = x2.shape[0]
    row = lambda i: (i, 0)
    fixed = lambda i: (0, 0)
    return pl.pallas_call(
        _inproj_kernel,
        grid=(t // ROW_TILE,),
        in_specs=[pl.BlockSpec((ROW_TILE, D_MODEL), row),
                  pl.BlockSpec((1, D_MODEL), fixed),
                  pl.BlockSpec((D_MODEL, IN_WIDTH), fixed),
                  pl.BlockSpec((1, IN_WIDTH), fixed)],
        out_specs=[pl.BlockSpec((ROW_TILE, w), row) for w in _IN_SPLITS],
        out_shape=[jax.ShapeDtypeStruct((t, w), F32) for w in _IN_SPLITS],
        compiler_params=_params("parallel"),
        name="inproj",
    )(x2, g, w_bf, b)


def _attn_bias_tables():
    qi = np.arange(WINDOW)[:, None]
    ki = np.arange(2 * WINDOW)[None, :]
    dist = WINDOW + qi - ki
    in_window = (dist >= 0) & (dist < WINDOW)
    slopes = np.asarray(_ALIBI, np.float32)[:, None, None]
    bias = -(slopes * dist.astype(np.float32)[None])
    tables = [np.where(in_window & (ki >= WINDOW), bias, -np.inf), np.where(in_window, bias, -np.inf)]
    return np.stack(tables).astype(np.float32)


def _pair_heads(a, axis):
    shape = a.shape
    a = a.reshape(shape[:axis] + (ATTN_KV_HEADS, ATTN_GROUP, ATTN_HEAD_DIM) + shape[axis + 1:])
    return jnp.swapaxes(a, axis, axis + 1).reshape(shape)


def _attn_kernel(sink_ref, q_ref, kp_ref, kc_ref, vp_ref, vc_ref, bias_ref, g_ref, o_ref):
    k2 = jnp.concatenate([kp_ref[...], kc_ref[...]], axis=0).astype(BF16)
    v2 = jnp.concatenate([vp_ref[...], vc_ref[...]], axis=0).astype(BF16)
    scale = ATTN_HEAD_DIM ** -0.5
    nt = (((1,), (1,)), ((), ()))
    low = lax.broadcasted_iota(jnp.int32, (WINDOW, LANES), 1) < ATTN_HEAD_DIM

    def softmax(s, h):
        s = s * scale + bias_ref[h]
        sink = sink_ref[h]
        m = jnp.maximum(jnp.max(s, axis=-1, keepdims=True), sink)
        p = jnp.exp(s - m)
        den = jnp.sum(p, axis=-1, keepdims=True) + jnp.exp(sink - m)
        return (p / den).astype(BF16)

    scores = []
    for j in range(ATTN_GROUP):
        q = q_ref[:, j * LANES:(j + 1) * LANES]
        scores.append((lax.dot_general(jnp.where(low, q, 0.0).astype(BF16), k2, nt, preferred_element_type=F32),
                       lax.dot_general(jnp.where(low, 0.0, q).astype(BF16), k2, nt, preferred_element_type=F32)))
    probs = [(softmax(s_lo, j), softmax(s_hi, ATTN_GROUP + j)) for j, (s_lo, s_hi) in enumerate(scores)]
    outs = [jnp.where(low, jnp.dot(p_lo, v2, preferred_element_type=F32),
                      jnp.dot(p_hi, v2, preferred_element_type=F32)) for p_lo, p_hi in probs]
    o_ref[...] = _rms(jnp.concatenate(outs, axis=1), g_ref[...])


def _attention(aq, ak, av, sinks, g, batch, seq):
    nb = seq // WINDOW
    aq = aq.reshape(batch, seq, ATTN_WIDTH)
    ak = ak.reshape(batch, seq, ATTN_KV_WIDTH)
    av = av.reshape(batch, seq, ATTN_KV_WIDTH)
    cur = lambda b, n, s: (b, n, 0)
    prev = lambda b, n, s: (b, jnp.maximum(n - 1, 0), 0)
    out = pl.pallas_call(
        _attn_kernel,
        grid_spec=pltpu.PrefetchScalarGridSpec(
            num_scalar_prefetch=1,
            grid=(batch, nb),
            in_specs=[pl.BlockSpec((None, WINDOW, ATTN_WIDTH), cur),
                      pl.BlockSpec((None, WINDOW, ATTN_KV_WIDTH), prev),
                      pl.BlockSpec((None, WINDOW, ATTN_KV_WIDTH), cur),
                      pl.BlockSpec((None, WINDOW, ATTN_KV_WIDTH), prev),
                      pl.BlockSpec((None, WINDOW, ATTN_KV_WIDTH), cur),
                      pl.BlockSpec((None, ATTN_Q_HEADS, WINDOW, 2 * WINDOW),
                                   lambda b, n, s: (jnp.minimum(n, 1), 0, 0, 0)),
                      pl.BlockSpec((1, ATTN_WIDTH), lambda b, n, s: (0, 0))],
            out_specs=pl.BlockSpec((None, WINDOW, ATTN_WIDTH), cur)),
        out_shape=jax.ShapeDtypeStruct((batch, seq, ATTN_WIDTH), F32),
        compiler_params=_params("parallel", "parallel"),
        name="attn",
    )(sinks, aq, ak, ak, av, av, jnp.asarray(_attn_bias_tables()), g)
    return out.reshape(batch * seq, ATTN_WIDTH)


HGRN_EXP_GUARD = 80.0


def _hgrn_kernel(guard, q_ref, f_ref, i_ref, gate_ref, lb_ref, og_ref, o_ref, st_ref, q_scr, k_scr, b_scr):
    step = pl.program_id(1)

    @pl.when(step == 0)
    def _():
        st_ref[...] = jnp.zeros_like(st_ref)

    C, S = HGRN_CHUNK, HGRN_SUB
    nsub = C // S
    nchunk = HGRN_STEP // C
    r_i = lax.broadcasted_iota(jnp.int32, (C, C), 0)
    c_i = lax.broadcasted_iota(jnp.int32, (C, C), 1)
    nt = (((1,), (1,)), ((), ()))

    tri = (r_i >= c_i).astype(F32)
    qx = q_ref[...]
    q_scr[...] = qx * jax.nn.sigmoid(qx)
    lb = lb_ref[...]
    f = lb + (1.0 - lb) * jax.nn.sigmoid(f_ref[...])
    k_scr[...] = 1.0 - f
    logf = jnp.log(f)
    for c in range(nchunk):
        b_scr[c * C:(c + 1) * C, :] = jnp.dot(tri, logf[c * C:(c + 1) * C, :], preferred_element_type=F32,
                                              precision=lax.Precision.HIGHEST)
    decay = jnp.concatenate([-jnp.sum(logf[i * S:(i + 1) * S, :], axis=0, keepdims=True)
                             for i in range(HGRN_STEP // S)], axis=0)
    risky = jnp.max(decay) > guard

    rowblk = lax.broadcasted_iota(jnp.int32, (C, HGRN_DIM), 0) // S
    pairs = [(c, h) for c in range(nchunk) for h in range(HGRN_HEADS)]

    def view(ref, c, h):
        return ref[c * C:(c + 1) * C, h * HGRN_DIM:(h + 1) * HGRN_DIM]

    def finish(intra):
        q_dec, gain, update = {}, {}, {}
        for c, h in pairs:
            q, k, b = view(q_scr, c, h), view(k_scr, c, h), view(b_scr, c, h)
            q_dec[c, h] = (q * jnp.exp(b)).astype(BF16)
            bl = b[C - 1:C, :]
            gain[c, h] = jnp.exp(bl)
            update[c, h] = lax.dot_general(view(i_ref, c, h).astype(BF16), (k * jnp.exp(bl - b)).astype(BF16),
                                           (((0,), (0,)), ((), ())), preferred_element_type=F32)
        state = {}
        for h in range(HGRN_HEADS):
            st = st_ref[h]
            for c in range(nchunk):
                state[c, h] = st
                st = st * gain[c, h] + update[c, h]
            st_ref[h] = st
        for c, h in pairs:
            o = intra[c, h] + lax.dot_general(q_dec[c, h], state[c, h].astype(BF16), nt,
                                              preferred_element_type=F32)
            gx = view(gate_ref, c, h)
            og = og_ref[:, h * HGRN_DIM:(h + 1) * HGRN_DIM]
            o = o * lax.rsqrt(jnp.mean(o * o, axis=-1, keepdims=True) + NORM_EPS) * og
            o_ref[c * C:(c + 1) * C, h * HGRN_DIM:(h + 1) * HGRN_DIM] = o * (gx * jax.nn.sigmoid(gx))

    def sub_block_queries(q, b):
        return [jnp.where(rowblk == i, q * jnp.exp(jnp.minimum(b - b[i * S:i * S + 1, :], 0.0)), 0.0)
                for i in range(nsub)]

    @pl.when(jnp.logical_not(risky))
    def _():
        att = {}
        for c, h in pairs:
            q, k, b = view(q_scr, c, h), view(k_scr, c, h), view(b_scr, c, h)
            k_sub = [jnp.where(rowblk <= i, k * jnp.exp(jnp.minimum(b[i * S:i * S + 1, :] - b, guard)), 0.0)
                     for i in range(nsub)]
            att[c, h] = lax.dot_general(jnp.concatenate(sub_block_queries(q, b), axis=1).astype(BF16),
                                        jnp.concatenate(k_sub, axis=1).astype(BF16), nt,
                                        preferred_element_type=F32)
        finish({(c, h): jnp.dot(jnp.where(r_i >= c_i, att[c, h], 0.0).astype(BF16),
                                view(i_ref, c, h).astype(BF16), preferred_element_type=F32) for c, h in pairs})

    @pl.when(risky)
    def _():
        sub_r = lax.broadcasted_iota(jnp.int32, (S, HGRN_DIM), 0)
        intra = {}
        for c, h in pairs:
            q, k, b, v = view(q_scr, c, h), view(k_scr, c, h), view(b_scr, c, h), view(i_ref, c, h)
            k_sub = [jnp.where(rowblk < i, k * jnp.exp(jnp.minimum(b[i * S:i * S + 1, :] - b, 0.0)), 0.0)
                     for i in range(1, nsub)]
            att = lax.dot_general(jnp.concatenate(sub_block_queries(q, b)[1:], axis=1).astype(BF16),
                                  jnp.concatenate(k_sub, axis=1).astype(BF16), nt, preferred_element_type=F32)
            diag = []
            for i in range(nsub):
                b_blk = b[i * S:(i + 1) * S, :]
                q_blk = q[i * S:(i + 1) * S, :]
                acc = jnp.zeros((S, HGRN_DIM), F32)
                for s_ in range(S):
                    r = i * S + s_
                    e = jnp.exp(jnp.minimum(b_blk - b[r:r + 1, :], 0.0))
                    a = jnp.where(sub_r >= s_, q_blk * e * k[r:r + 1, :], 0.0)
                    acc = acc + jnp.sum(a, axis=-1, keepdims=True) * v[r:r + 1, :]
                diag.append(acc)
            intra[c, h] = (jnp.dot(att.astype(BF16), v.astype(BF16), preferred_element_type=F32)
                           + jnp.concatenate(diag, axis=0))
        finish(intra)


def _hgrn(hq, hf, hi, hg, lb, og, batch, seq, guard=HGRN_EXP_GUARD):
    shp = (batch, seq, HGRN_WIDTH)
    blk = pl.BlockSpec((None, HGRN_STEP, HGRN_WIDTH), lambda b, c: (b, c, 0))
    vec = pl.BlockSpec((1, HGRN_WIDTH), lambda b, c: (0, 0))
    out = pl.pallas_call(
        functools.partial(_hgrn_kernel, guard),
        grid=(batch, seq // HGRN_STEP),
        in_specs=[blk, blk, blk, blk, vec, vec],
        out_specs=blk,
        out_shape=jax.ShapeDtypeStruct(shp, F32),
        scratch_shapes=[pltpu.VMEM((HGRN_HEADS, HGRN_DIM, HGRN_DIM), F32)]
        + [pltpu.VMEM((HGRN_STEP, HGRN_WIDTH), F32)] * 3,
        compiler_params=_params("parallel", "arbitrary"),
        name="hgrn",
    )(hq.reshape(shp), hf.reshape(shp), hi.reshape(shp), hg.reshape(shp), lb, og)
    return out.reshape(batch * seq, HGRN_WIDTH)


def _outproj_kernel(attn_ref, hg_ref, x_ref, wo_ref, bo_ref, g2_ref, rw_ref, rb_ref,
                    x1_ref, h2_ref, lpos_ref, gate_ref, seg_ref):
    tm = x_ref.shape[0]
    y = jnp.dot(attn_ref[...].astype(BF16), wo_ref[:ATTN_WIDTH, :], preferred_element_type=F32)
    y = y + jnp.dot(hg_ref[...].astype(BF16), wo_ref[ATTN_WIDTH:, :], preferred_element_type=F32)
    x1 = x_ref[...] + y + bo_ref[...]
    x1_ref[...] = x1
    h2 = _rms(x1, g2_ref[...])
    h2_ref[...] = h2.astype(BF16)
    h_hi = h2.astype(BF16)
    h_lo = (h2 - h_hi.astype(F32)).astype(BF16)
    p_hi = jnp.dot(h_hi, rw_ref[...], preferred_element_type=F32)
    p_lo = jnp.dot(h_lo, rw_ref[...], preferred_element_type=F32)
    logits = (p_hi[:, :N_EXPERTS] + p_hi[:, LANES:LANES + N_EXPERTS]) + p_lo[:, :N_EXPERTS] + rb_ref[...]
    lane = lax.broadcasted_iota(jnp.int32, (tm, N_EXPERTS), 1)
    work = logits
    vals, hots = [], []
    for _ in range(TOP_K):
        m = jnp.max(work, axis=-1, keepdims=True)
        idx = jnp.min(jnp.where(work == m, lane, N_EXPERTS), axis=-1, keepdims=True)
        hot = lane == idx
        vals.append(m)
        hots.append(hot)
        work = jnp.where(hot, -jnp.inf, work)
    ex = [jnp.exp(v - vals[0]) for v in vals]
    den = ex[0] + ex[1] + ex[2] + ex[3]
    sel = jnp.zeros((tm, N_EXPERTS), F32)
    for hot in hots:
        sel = sel + hot.astype(F32)
    r_i = lax.broadcasted_iota(jnp.int32, (tm, tm), 0)
    c_i = lax.broadcasted_iota(jnp.int32, (tm, tm), 1)
    strict = jnp.where(r_i > c_i, 1.0, 0.0).astype(BF16)
    ranks = jnp.dot(strict, sel.astype(BF16), preferred_element_type=F32)
    seg = jnp.floor((jnp.sum(sel, axis=0, keepdims=True) + (SEG_ROWS - 1.0)) * (1.0 / SEG_ROWS)) * SEG_ROWS
    e_r = lax.broadcasted_iota(jnp.int32, (N_EXPERTS, N_EXPERTS), 0)
    e_c = lax.broadcasted_iota(jnp.int32, (N_EXPERTS, N_EXPERTS), 1)
    before = jnp.where(e_r < e_c, 1.0, 0.0).astype(BF16)
    seg_off = jnp.dot(jnp.broadcast_to(seg, (SEG_ROWS, N_EXPERTS)).astype(BF16), before,
                      preferred_element_type=F32)[0:1, :]
    slots = ranks + seg_off
    l4 = lax.broadcasted_iota(jnp.int32, (tm, TOP_K), 1)
    lpos4 = jnp.zeros((tm, TOP_K), F32)
    gate4 = jnp.zeros((tm, TOP_K), F32)
    for j in range(TOP_K):
        sj = jnp.sum(jnp.where(hots[j], slots, 0.0), axis=-1, keepdims=True)
        lpos4 = jnp.where(l4 == j, sj, lpos4)
        gate4 = jnp.where(l4 == j, ex[j] / den, gate4)
    lpos_ref[...] = lpos4.astype(jnp.int32)
    gate_ref[...] = gate4
    seg_ref[...] = seg


def _outproj(attn, hgo, x2, wo_bf, bo, g2, rw, rb):
    t = x2.shape[0]
    row = lambda i: (i, 0)
    fixed = lambda i: (0, 0)
    return pl.pallas_call(
        _outproj_kernel,
        grid=(t // SORT_TILE,),
        in_specs=[pl.BlockSpec((SORT_TILE, ATTN_WIDTH), row),
                  pl.BlockSpec((SORT_TILE, HGRN_WIDTH), row),
                  pl.BlockSpec((SORT_TILE, D_MODEL), row),
                  pl.BlockSpec((ATTN_WIDTH + HGRN_WIDTH, D_MODEL), fixed),
                  pl.BlockSpec((1, D_MODEL), fixed),
                  pl.BlockSpec((1, D_MODEL), fixed),
                  pl.BlockSpec((D_MODEL, 2 * LANES), fixed),
                  pl.BlockSpec((1, N_EXPERTS), fixed)],
        out_specs=[pl.BlockSpec((SORT_TILE, D_MODEL), row),
                   pl.BlockSpec((SORT_TILE, D_MODEL), row),
                   pl.BlockSpec((SORT_TILE, TOP_K), row),
                   pl.BlockSpec((SORT_TILE, TOP_K), row),
                   pl.BlockSpec((None, 1, N_EXPERTS), lambda i: (i, 0, 0))],
        out_shape=[jax.ShapeDtypeStruct((t, D_MODEL), F32),
                   jax.ShapeDtypeStruct((t, D_MODEL), BF16),
                   jax.ShapeDtypeStruct((t, TOP_K), jnp.int32),
                   jax.ShapeDtypeStruct((t, TOP_K), F32),
                   jax.ShapeDtypeStruct((t // SORT_TILE, 1, N_EXPERTS), F32)],
        compiler_params=_params("parallel"),
        name="outproj",
    )(attn, hgo, x2, wo_bf, bo, g2, rw, rb)


def _rows(start, n):
    return pl.ds(pl.multiple_of(start, SEG_ROWS), n)


def _repeat(cnt, fn):
    def body(j, carry):
        fn(j)
        return carry

    lax.fori_loop(0, cnt, body, 0)


def _tile_pieces(lists, step, fn):
    big_loc, big_glob, big_base, big_cnt, small_loc, small_glob, small_base, small_cnt = lists
    b0 = big_base[step]
    _repeat(big_cnt[step], lambda j: fn(big_loc[b0 + j], big_glob[b0 + j], BIG_PIECE))
    s0 = small_base[step]
    _repeat(small_cnt[step], lambda j: fn(small_loc[s0 + j], small_glob[s0 + j], SEG_ROWS))


def _tile_piece_waits(lists, step, wait):
    _repeat(lists[3][step], lambda j: wait(BIG_PIECE))
    _repeat(lists[7][step], lambda j: wait(SEG_ROWS))


def _dispatch_kernel(*refs):
    lists, (zdst_ref, zbig_ref, zsmall_ref) = refs[:8], refs[8:11]
    h_ref, lpos_t_ref, xs_ref, lbuf_ref, zbuf_ref, sem, zsem = refs[11:]
    i = pl.program_id(0)
    last = pl.num_programs(0) - 1
    buf = i % 2

    def piece(b, local, glob, n):
        return pltpu.make_async_copy(lbuf_ref.at[b, _rows(local, n), :], xs_ref.at[_rows(glob, n), :], sem.at[b])

    def drain(step):
        _tile_piece_waits(lists, step, lambda n: piece(step % 2, 0, 0, n).wait())

    def zero_piece(glob, n):
        return pltpu.make_async_copy(zbuf_ref.at[_rows(0, n), :], xs_ref.at[_rows(glob, n), :], zsem)

    @pl.when(i == 0)
    def _():
        zbuf_ref[...] = jnp.zeros_like(zbuf_ref)
        for e in range(N_EXPERTS + 1):
            _repeat(zbig_ref[e], lambda j, e=e: zero_piece(zdst_ref[e] + j * BIG_PIECE, BIG_PIECE).start())
            _repeat(zsmall_ref[e], lambda j, e=e: zero_piece(
                zdst_ref[e] + zbig_ref[e] * BIG_PIECE + j * SEG_ROWS, SEG_ROWS).start())
        for e in range(N_EXPERTS + 1):
            _repeat(zbig_ref[e], lambda j: zero_piece(0, BIG_PIECE).wait())
            _repeat(zsmall_ref[e], lambda j: zero_piece(0, SEG_ROWS).wait())

    @pl.when(i >= 2)
    def _():
        drain(i - 2)

    slot = lax.broadcasted_iota(jnp.int32, (SORT_SLOTS, SORT_TILE), 0)
    onehot = jnp.zeros((SORT_SLOTS, SORT_TILE), F32)
    for k in range(TOP_K):
        onehot = onehot + jnp.where(slot == lpos_t_ref[k:k + 1, :], 1.0, 0.0)
    lbuf_ref[buf] = jnp.dot(onehot.astype(BF16), h_ref[...], preferred_element_type=F32)
    _tile_pieces(lists, i, lambda local, glob, n: piece(buf, local, glob, n).start())

    @pl.when(i == last)
    def _():
        @pl.when(i >= 1)
        def _():
            drain(i - 1)

        drain(i)


def _dispatch(tables, zero_tables, h2, lpos_t, n_rows):
    t = h2.shape[0]
    tile = lambda i, *_: (i, 0)
    return pl.pallas_call(
        _dispatch_kernel,
        grid_spec=pltpu.PrefetchScalarGridSpec(
            num_scalar_prefetch=11,
            grid=(t // SORT_TILE,),
            in_specs=[pl.BlockSpec((SORT_TILE, D_MODEL), tile),
                      pl.BlockSpec((TOP_K, SORT_TILE), lambda i, *_: (0, i))],
            out_specs=pl.BlockSpec(memory_space=pl.ANY),
            scratch_shapes=[pltpu.VMEM((2, SORT_SLOTS, D_MODEL), F32),
                            pltpu.VMEM((BIG_PIECE, D_MODEL), F32),
                            pltpu.SemaphoreType.DMA((2,)),
                            pltpu.SemaphoreType.DMA(())]),
        out_shape=jax.ShapeDtypeStruct((n_rows, D_MODEL), F32),
        compiler_params=_params("arbitrary"),
        name="dispatch",
    )(*tables, *zero_tables, h2, lpos_t)


CAST_ROWS = 128


def _expert_kernel(be_ref, nb_ref, first_ref, slot_ref, next_ref,
                   xs_ref, w1_hbm, b1_ref, w2_hbm, b2_ref, y_ref,
                   w1f_ref, w2f_ref, w1b_ref, w2i_ref, w2b_ref, sem):
    blk = pl.program_id(0)
    half = LANES // 2

    def weight_copies(e, s):
        return (pltpu.make_async_copy(w1_hbm.at[e], w1f_ref.at[s], sem.at[0, s]),
                pltpu.make_async_copy(w2_hbm.at[e], w2f_ref.at[s], sem.at[1, s]))

    @pl.when(first_ref[blk] == 1)
    def _():
        e = be_ref[blk]
        s = slot_ref[blk]

        @pl.when(blk == 0)
        def _():
            for c in weight_copies(e, s):
                c.start()

        @pl.when(next_ref[blk] >= 0)
        def _():
            for c in weight_copies(next_ref[blk], 1 - s):
                c.start()

        for c in weight_copies(e, s):
            c.wait()

        def cast_rows(r, carry):
            rows = pl.ds(pl.multiple_of(r * CAST_ROWS, CAST_ROWS), CAST_ROWS)
            w1b_ref[rows, :] = w1f_ref[s, rows, :].astype(BF16)
            return carry

        lax.fori_loop(0, D_MODEL // CAST_ROWS, cast_rows, 0)
        for c in range(D_MODEL // LANES):
            cols = slice(c * LANES, (c + 1) * LANES)
            for m in range(EXPERT_FF // LANES):
                lo = m * LANES
                w2i_ref[c, pl.ds(lo, half, stride=2), :] = w2f_ref[s, lo:lo + half, cols]
                w2i_ref[c, pl.ds(lo + 1, half, stride=2), :] = w2f_ref[s, lo + half:lo + LANES, cols]
            w2b_ref[:, cols] = w2i_ref[c].astype(BF16)

    @pl.when(blk < nb_ref[0])
    def _():
        x = xs_ref[...].astype(BF16)
        hid = jnp.dot(x, w1b_ref[...], preferred_element_type=F32) + b1_ref[...]
        even = (lax.broadcasted_iota(jnp.int32, (x.shape[0], LANES), 1) & 1) == 0
        glu, lin = [], []
        for m in range(EXPERT_FF // LANES):
            ha = hid[:, 2 * m * LANES:(2 * m + 1) * LANES]
            hb = hid[:, (2 * m + 1) * LANES:(2 * m + 2) * LANES]
            glu.append(jnp.where(even, ha, pltpu.roll(hb, 1, axis=1)))
            lin.append(jnp.where(even, pltpu.roll(ha, LANES - 1, axis=1), hb))
        glu = jnp.minimum(jnp.concatenate(glu, axis=1), SWIGLU_LIMIT)
        lin = jnp.clip(jnp.concatenate(lin, axis=1), -SWIGLU_LIMIT, SWIGLU_LIMIT)
        act = glu * jax.nn.sigmoid(SWIGLU_ALPHA * glu) * (lin + 1.0)
        y_ref[...] = jnp.dot(act.astype(BF16), w2b_ref[...], preferred_element_type=F32) + b2_ref[...]

    @pl.when(blk >= nb_ref[0])
    def _():
        y_ref[...] = jnp.zeros_like(y_ref)


def _experts(block_tables, xs, w1, b1, w2, b2):
    n_rows = xs.shape[0]
    nblk = n_rows // EXPERT_BLOCK
    rows = lambda b, *_: (b, 0)
    used_rows = lambda b, be, nb, *_: (jnp.minimum(b, nb[0] - 1), 0)
    bias = lambda b, be, *_: (be[b], 0, 0)
    return pl.pallas_call(
        _expert_kernel,
        grid_spec=pltpu.PrefetchScalarGridSpec(
            num_scalar_prefetch=5,
            grid=(nblk,),
            in_specs=[pl.BlockSpec((EXPERT_BLOCK, D_MODEL), used_rows),
                      pl.BlockSpec(memory_space=pl.ANY),
                      pl.BlockSpec((None, 1, 2 * EXPERT_FF), bias),
                      pl.BlockSpec(memory_space=pl.ANY),
                      pl.BlockSpec((None, 1, D_MODEL), bias)],
            out_specs=pl.BlockSpec((EXPERT_BLOCK, D_MODEL), rows),
            scratch_shapes=[pltpu.VMEM((2, D_MODEL, 2 * EXPERT_FF), F32),
                            pltpu.VMEM((2, EXPERT_FF, D_MODEL), F32),
                            pltpu.VMEM((D_MODEL, 2 * EXPERT_FF), BF16),
                            pltpu.VMEM((D_MODEL // LANES, EXPERT_FF, LANES), F32),
                            pltpu.VMEM((EXPERT_FF, D_MODEL), BF16),
                            pltpu.SemaphoreType.DMA((2, 2))]),
        out_shape=jax.ShapeDtypeStruct((n_rows, D_MODEL), F32),
        compiler_params=_params("arbitrary"),
        name="experts",
    )(*block_tables, xs, w1, b1, w2, b2)


def _combine_kernel(final_norm, *refs):
    lists = refs[:8]
    yb_ref, x1_ref, lpos_ref, gate_ref, g_ref, o_ref, gbuf_ref, sem = refs[8:]
    i = pl.program_id(0)
    last = pl.num_programs(0) - 1
    buf = i % 2

    def piece(b, local, glob, n):
        return pltpu.make_async_copy(yb_ref.at[_rows(glob, n), :], gbuf_ref.at[b, _rows(local, n), :], sem.at[b])

    def fetch(step):
        _tile_pieces(lists, step, lambda local, glob, n: piece(step % 2, local, glob, n).start())

    @pl.when(i == 0)
    def _():
        gbuf_ref[...] = jnp.zeros_like(gbuf_ref)
        fetch(0)

    @pl.when(i < last)
    def _():
        fetch(i + 1)

    _tile_piece_waits(lists, i, lambda n: piece(buf, 0, 0, n).wait())
    slot = lax.broadcasted_iota(jnp.int32, (SORT_TILE, SORT_SLOTS), 1)
    lpos = lpos_ref[...]
    gates = gate_ref[...]
    weights = jnp.zeros((SORT_TILE, SORT_SLOTS), F32)
    for k in range(TOP_K):
        weights = weights + jnp.where(slot == lpos[:, k:k + 1], gates[:, k:k + 1], 0.0)
    y = x1_ref[...] + jnp.dot(weights.astype(BF16), gbuf_ref[buf].astype(BF16), preferred_element_type=F32)
    o_ref[...] = _rms(y, g_ref[...]) if final_norm else y


def _combine(tables, yb, x1, lpos, gates, g, final_norm):
    t = x1.shape[0]
    tile = lambda i, *_: (i, 0)
    return pl.pallas_call(
        functools.partial(_combine_kernel, final_norm),
        grid_spec=pltpu.PrefetchScalarGridSpec(
            num_scalar_prefetch=8,
            grid=(t // SORT_TILE,),
            in_specs=[pl.BlockSpec(memory_space=pl.ANY),
                      pl.BlockSpec((SORT_TILE, D_MODEL), tile),
                      pl.BlockSpec((SORT_TILE, TOP_K), tile),
                      pl.BlockSpec((SORT_TILE, TOP_K), tile),
                      pl.BlockSpec((1, D_MODEL), lambda i, *_: (0, 0))],
            out_specs=pl.BlockSpec((SORT_TILE, D_MODEL), tile),
            scratch_shapes=[pltpu.VMEM((2, SORT_SLOTS, D_MODEL), F32),
                            pltpu.SemaphoreType.DMA((2,))]),
        out_shape=jax.ShapeDtypeStruct((t, D_MODEL), F32),
        compiler_params=_params("arbitrary"),
        name="combine",
    )(*tables, yb, x1, lpos, gates, g)


def _piece_lists(seg_loc, seg_glob, cnt, skip, rows, max_pieces):
    cnt_flat = cnt.reshape(-1)
    first = jnp.cumsum(cnt_flat) - cnt_flat
    p = jnp.arange(max_pieces, dtype=jnp.int32)
    started = first[None, :] <= p[:, None]

    def per_piece(rows0):
        base = rows0.reshape(-1) + skip.reshape(-1) - first * rows
        step = base - jnp.concatenate([jnp.zeros((1,), base.dtype), base[:-1]])
        return jnp.sum(jnp.where(started, step[None, :], 0), axis=1) + p * rows

    per_tile = jnp.sum(cnt, axis=1)
    return per_piece(seg_loc), per_piece(seg_glob), jnp.cumsum(per_tile) - per_tile, per_tile


def kernel(x, norm1_g, w_in, b_in, attn_sinks, attn_out_g, hgrn_lb_logits, hgrn_out_g, w_out, b_out,
           norm2_g, router_w, router_b, w1, b1, w2, b2, final_g):
    batch, seq, d = x.shape
    t = batch * seq
    depth = w_in.shape[0]
    lower_bounds = jnp.cumsum(jax.nn.softmax(hgrn_lb_logits.astype(F32), axis=0), axis=0)
    assert t % SORT_TILE == 0 and SORT_SLOTS >= SORT_TILE * TOP_K + N_EXPERTS * (SEG_ROWS - 1)
    ntiles = t // SORT_TILE
    seg_rows_max = t * TOP_K + ntiles * N_EXPERTS * (SEG_ROWS - 1)
    nblk = seg_rows_max // EXPERT_BLOCK + N_EXPERTS
    n_rows = nblk * EXPERT_BLOCK
    x2 = x.reshape(t, d)
    for l in range(depth):
        w_in_l = jnp.concatenate([_pair_heads(w_in[l][:, :ATTN_WIDTH], 1), w_in[l][:, ATTN_WIDTH:]], axis=1)
        b_in_l = jnp.concatenate([_pair_heads(b_in[l][:ATTN_WIDTH], 0), b_in[l][ATTN_WIDTH:]])
        aq, ak, av, hq, hf, hi, hg = _inproj(x2, norm1_g[l][None], w_in_l.astype(BF16), b_in_l[None])
        attn = _attention(aq, ak, av, attn_sinks[l], _pair_heads(attn_out_g[l], 0)[None], batch, seq)
        hgo = _hgrn(hq, hf, hi, hg, lower_bounds[l][None], hgrn_out_g[l][None], batch, seq)
        w_out_l = jnp.concatenate([_pair_heads(w_out[l][:ATTN_WIDTH], 0), w_out[l][ATTN_WIDTH:]], axis=0)
        rw_hi = router_w[l].astype(BF16)
        rw_lo = (router_w[l] - rw_hi.astype(F32)).astype(BF16)
        rw_cat = jnp.zeros((d, 2 * LANES), BF16)
        rw_cat = rw_cat.at[:, :N_EXPERTS].set(rw_hi).at[:, LANES:LANES + N_EXPERTS].set(rw_lo)
        x1, h2, lpos, gates, seg = _outproj(
            attn, hgo, x2, w_out_l.astype(BF16), b_out[l][None], norm2_g[l][None],
            rw_cat, router_b[l][None])
        seg = seg.reshape(ntiles, N_EXPERTS).astype(jnp.int32)
        rows_e = jnp.sum(seg, axis=0)
        padded = (rows_e + EXPERT_BLOCK - 1) // EXPERT_BLOCK * EXPERT_BLOCK
        ends = jnp.cumsum(padded)
        pstart = ends - padded
        seg_glob = pstart[None, :] + jnp.cumsum(seg, axis=0) - seg
        seg_loc = jnp.cumsum(seg, axis=1) - seg
        tail = jnp.append(padded - rows_e, n_rows - ends[-1])
        zero_tables = (jnp.append(pstart + rows_e, ends[-1]), tail // BIG_PIECE, tail % BIG_PIECE // SEG_ROWS)
        big = seg // BIG_PIECE
        tables = (_piece_lists(seg_loc, seg_glob, big, jnp.zeros_like(seg), BIG_PIECE, seg_rows_max // BIG_PIECE)
                  + _piece_lists(seg_loc, seg_glob, seg % BIG_PIECE // SEG_ROWS, big * BIG_PIECE, SEG_ROWS,
                                 ntiles * N_EXPERTS * (BIG_PIECE // SEG_ROWS - 1)))
        blk_ids = jnp.arange(nblk, dtype=jnp.int32)
        block_e = jnp.minimum(jnp.sum(blk_ids[:, None] * EXPERT_BLOCK >= ends[None, :], axis=-1), N_EXPERTS - 1)
        n_used = ends[-1] // EXPERT_BLOCK
        first = ((blk_ids == 0) | (block_e != jnp.roll(block_e, 1))) & (blk_ids < n_used)
        slot = (jnp.cumsum(first) - 1) % 2
        e_ids = jnp.arange(N_EXPERTS, dtype=jnp.int32)
        later_used = (e_ids[None, :] > e_ids[:, None]) & (padded[None, :] > 0)
        next_e = jnp.min(jnp.where(later_used, e_ids[None, :], N_EXPERTS), axis=1)
        next_e = jnp.where(next_e == N_EXPERTS, -1, next_e)
        block_tables = tuple(a.astype(jnp.int32) for a in (
            block_e, n_used.reshape(1), first, slot, next_e[block_e]))
        xs = _dispatch(tables, zero_tables, h2, lpos.T, n_rows)
        yb = _experts(block_tables, xs, w1[l], b1[l][:, None, :], w2[l], b2[l][:, None, :])
        x2 = _combine(tables, yb, x1, lpos, gates, final_g[None], l == depth - 1)
    return x2.reshape(batch, seq, d)
```

```python
import functools

import numpy as np
import jax
import jax.numpy as jnp
from jax import lax
from jax.experimental import pallas as pl
from jax.experimental.pallas import tpu as pltpu

F32 = jnp.float32
BF16 = jnp.bfloat16

D_MODEL = 1024
ATTN_Q_HEADS = 8
ATTN_KV_HEADS = 2
ATTN_HEAD_DIM = 64
ATTN_GROUP = ATTN_Q_HEADS // ATTN_KV_HEADS
ATTN_WIDTH = ATTN_Q_HEADS * ATTN_HEAD_DIM
ATTN_KV_WIDTH = ATTN_KV_HEADS * ATTN_HEAD_DIM
WINDOW = 128
HGRN_HEADS = 4
HGRN_DIM = 128
HGRN_WIDTH = HGRN_HEADS * HGRN_DIM
HGRN_CHUNK = 64
HGRN_SUB = 16
HGRN_STEP = 256
IN_WIDTH = ATTN_WIDTH + 2 * ATTN_KV_WIDTH + 4 * HGRN_WIDTH
N_EXPERTS = 32
TOP_K = 4
EXPERT_FF = D_MODEL
SWIGLU_LIMIT = 7.0
SWIGLU_ALPHA = 1.702
NORM_EPS = 1e-5

LANES = 128
ROW_TILE = 512
EXPERT_BLOCK = 512
SORT_TILE = 256
SEG_ROWS = 8
BIG_PIECE = 32
SORT_SLOTS = 1280
VMEM_LIMIT = 56 * 1024 * 1024

_ALIBI = [float(2.0 ** (-8.0 * (h + 1) / ATTN_Q_HEADS)) for h in range(ATTN_Q_HEADS)]


def _rms(x, g):
    return x * lax.rsqrt(jnp.mean(x * x, axis=-1, keepdims=True) + NORM_EPS) * g


def _params(*sem):
    return pltpu.CompilerParams(dimension_semantics=sem, vmem_limit_bytes=VMEM_LIMIT)


_IN_SPLITS = (ATTN_WIDTH, ATTN_KV_WIDTH, ATTN_KV_WIDTH, HGRN_WIDTH, HGRN_WIDTH, HGRN_WIDTH, HGRN_WIDTH)


def _inproj_kernel(x_ref, g_ref, w_ref, b_ref, *out_refs):
    h = _rms(x_ref[...], g_ref[...]).astype(BF16)
    lo = 0
    for ref, width in zip(out_refs, _IN_SPLITS):
        ref[...] = jnp.dot(h, w_ref[:, lo:lo + width], preferred_element_type=F32) + b_ref[:, lo:lo + width]
        lo += width


def _inproj(x2, g, w_bf, b):
    t = x2.shape[0]
    row = lambda i: (i, 0)
    fixed = lambda i: (0, 0)
    return pl.pallas_call(
        _inproj_kernel,
        grid=(t // ROW_TILE,),
        in_specs=[pl.BlockSpec((ROW_TILE, D_MODEL), row),
                  pl.BlockSpec((1, D_MODEL), fixed),
                  pl.BlockSpec((D_MODEL, IN_WIDTH), fixed),
                  pl.BlockSpec((1, IN_WIDTH), fixed)],
        out_specs=[pl.BlockSpec((ROW_TILE, w), row) for w in _IN_SPLITS],
        out_shape=[jax.ShapeDtypeStruct((t, w), F32) for w in _IN_SPLITS],
        compiler_params=_params("parallel"),
        name="inproj",
    )(x2, g, w_bf, b)


def _attn_bias_tables():
    qi = np.arange(WINDOW)[:, None]
    ki = np.arange(2 * WINDOW)[None, :]
    dist = WINDOW + qi - ki
    in_window = (dist >= 0) & (dist < WINDOW)
    slopes = np.asarray(_ALIBI, np.float32)[:, None, None]
    bias = -(slopes * dist.astype(np.float32)[None])
    tables = [np.where(in_window & (ki >= WINDOW), bias, -np.inf), np.where(in_window, bias, -np.inf)]
    return np.stack(tables).astype(np.float32)


def _pair_heads(a, axis):
    shape = a.shape
    a = a.reshape(shape[:axis] + (ATTN_KV_HEADS, ATTN_GROUP, ATTN_HEAD_DIM) + shape[axis + 1:])
    return jnp.swapaxes(a, axis, axis + 1).reshape(shape)


def _attn_kernel(sink_ref, q_ref, kp_ref, kc_ref, vp_ref, vc_ref, bias_ref, g_ref, o_ref):
    k2 = jnp.concatenate([kp_ref[...], kc_ref[...]], axis=0).astype(BF16)
    v2 = jnp.concatenate([vp_ref[...], vc_ref[...]], axis=0).astype(BF16)
    scale = ATTN_HEAD_DIM ** -0.5
    nt = (((1,), (1,)), ((), ()))
    low = lax.broadcasted_iota(jnp.int32, (WINDOW, LANES), 1) < ATTN_HEAD_DIM

    def softmax(s, h):
        s = s * scale + bias_ref[h]
        sink = sink_ref[h]
        m = jnp.maximum(jnp.max(s, axis=-1, keepdims=True), sink)
        p = jnp.exp(s - m)
        den = jnp.sum(p, axis=-1, keepdims=True) + jnp.exp(sink - m)
        return (p / den).astype(BF16)

    scores = []
    for j in range(ATTN_GROUP):
        q = q_ref[:, j * LANES:(j + 1) * LANES]
        scores.append((lax.dot_general(jnp.where(low, q, 0.0).astype(BF16), k2, nt, preferred_element_type=F32),
                       lax.dot_general(jnp.where(low, 0.0, q).astype(BF16), k2, nt, preferred_element_type=F32)))
    probs = [(softmax(s_lo, j), softmax(s_hi, ATTN_GROUP + j)) for j, (s_lo, s_hi) in enumerate(scores)]
    outs = [jnp.where(low, jnp.dot(p_lo, v2, preferred_element_type=F32),
                      jnp.dot(p_hi, v2, preferred_element_type=F32)) for p_lo, p_hi in probs]
    o_ref[...] = _rms(jnp.concatenate(outs, axis=1), g_ref[...])


def _attention(aq, ak, av, sinks, g, batch, seq):
    nb = seq // WINDOW
    aq = aq.reshape(batch, seq, ATTN_WIDTH)
    ak = ak.reshape(batch, seq, ATTN_KV_WIDTH)
    av = av.reshape(batch, seq, ATTN_KV_WIDTH)
    cur = lambda b, n, s: (b, n, 0)
    prev = lambda b, n, s: (b, jnp.maximum(n - 1, 0), 0)
    out = pl.pallas_call(
        _attn_kernel,
        grid_spec=pltpu.PrefetchScalarGridSpec(
            num_scalar_prefetch=1,
            grid=(batch, nb),
            in_specs=[pl.BlockSpec((None, WINDOW, ATTN_WIDTH), cur),
                      pl.BlockSpec((None, WINDOW, ATTN_KV_WIDTH), prev),
                      pl.BlockSpec((None, WINDOW, ATTN_KV_WIDTH), cur),
                      pl.BlockSpec((None, WINDOW, ATTN_KV_WIDTH), prev),
                      pl.BlockSpec((None, WINDOW, ATTN_KV_WIDTH), cur),
                      pl.BlockSpec((None, ATTN_Q_HEADS, WINDOW, 2 * WINDOW),
                                   lambda b, n, s: (jnp.minimum(n, 1), 0, 0, 0)),
                      pl.BlockSpec((1, ATTN_WIDTH), lambda b, n, s: (0, 0))],
            out_specs=pl.BlockSpec((None, WINDOW, ATTN_WIDTH), cur)),
        out_shape=jax.ShapeDtypeStruct((batch, seq, ATTN_WIDTH), F32),
        compiler_params=_params("parallel", "parallel"),
        name="attn",
    )(sinks, aq, ak, ak, av, av, jnp.asarray(_attn_bias_tables()), g)
    return out.reshape(batch * seq, ATTN_WIDTH)


HGRN_EXP_GUARD = 80.0


def _hgrn_kernel(guard, q_ref, f_ref, i_ref, gate_ref, lb_ref, og_ref, o_ref, st_ref, q_scr, k_scr, b_scr):
    step = pl.program_id(1)

    @pl.when(step == 0)
    def _():
        st_ref[...] = jnp.zeros_like(st_ref)

    C, S = HGRN_CHUNK, HGRN_SUB
    nsub = C // S
    nchunk = HGRN_STEP // C
    r_i = lax.broadcasted_iota(jnp.int32, (C, C), 0)
    c_i = lax.broadcasted_iota(jnp.int32, (C, C), 1)
    nt = (((1,), (1,)), ((), ()))

    tri = (r_i >= c_i).astype(F32)
    qx = q_ref[...]
    q_scr[...] = qx * jax.nn.sigmoid(qx)
    lb = lb_ref[...]
    f = lb + (1.0 - lb) * jax.nn.sigmoid(f_ref[...])
    k_scr[...] = 1.0 - f
    logf = jnp.log(f)
    for c in range(nchunk):
        b_scr[c * C:(c + 1) * C, :] = jnp.dot(tri, logf[c * C:(c + 1) * C, :], preferred_element_type=F32,
                                              precision=lax.Precision.HIGHEST)
    decay = jnp.concatenate([-jnp.sum(logf[i * S:(i + 1) * S, :], axis=0, keepdims=True)
                             for i in range(HGRN_STEP // S)], axis=0)
    risky = jnp.max(decay) > guard

    rowblk = lax.broadcasted_iota(jnp.int32, (C, HGRN_DIM), 0) // S
    pairs = [(c, h) for c in range(nchunk) for h in range(HGRN_HEADS)]

    def view(ref, c, h):
        return ref[c * C:(c + 1) * C, h * HGRN_DIM:(h + 1) * HGRN_DIM]

    def finish(intra):
        q_dec, gain, update = {}, {}, {}
        for c, h in pairs:
            q, k, b = view(q_scr, c, h), view(k_scr, c, h), view(b_scr, c, h)
            q_dec[c, h] = (q * jnp.exp(b)).astype(BF16)
            bl = b[C - 1:C, :]
            gain[c, h] = jnp.exp(bl)
            update[c, h] = lax.dot_general(view(i_ref, c, h).astype(BF16), (k * jnp.exp(bl - b)).astype(BF16),
                                           (((0,), (0,)), ((), ())), preferred_element_type=F32)
        state = {}
        for h in range(HGRN_HEADS):
            st = st_ref[h]
            for c in range(nchunk):
                state[c, h] = st
                st = st * gain[c, h] + update[c, h]
            st_ref[h] = st
        for c, h in pairs:
            o = intra[c, h] + lax.dot_general(q_dec[c, h], state[c, h].astype(BF16), nt,
                                              preferred_element_type=F32)
            gx = view(gate_ref, c, h)
            og = og_ref[:, h * HGRN_DIM:(h + 1) * HGRN_DIM]
            o = o * lax.rsqrt(jnp.mean(o * o, axis=-1, keepdims=True) + NORM_EPS) * og
            o_ref[c * C:(c + 1) * C, h * HGRN_DIM:(h + 1) * HGRN_DIM] = o * (gx * jax.nn.sigmoid(gx))

    def sub_block_queries(q, b):
        return [jnp.where(rowblk == i, q * jnp.exp(jnp.minimum(b - b[i * S:i * S + 1, :], 0.0)), 0.0)
                for i in range(nsub)]

    @pl.when(jnp.logical_not(risky))
    def _():
        att = {}
        for c, h in pairs:
            q, k, b = view(q_scr, c, h), view(k_scr, c, h), view(b_scr, c, h)
            k_sub = [jnp.where(rowblk <= i, k * jnp.exp(jnp.minimum(b[i * S:i * S + 1, :] - b, guard)), 0.0)
                     for i in range(nsub)]
            att[c, h] = lax.dot_general(jnp.concatenate(sub_block_queries(q, b), axis=1).astype(BF16),
                                        jnp.concatenate(k_sub, axis=1).astype(BF16), nt,
                                        preferred_element_type=F32)
        finish({(c, h): jnp.dot(jnp.where(r_i >= c_i, att[c, h], 0.0).astype(BF16),
                                view(i_ref, c, h).astype(BF16), preferred_element_type=F32) for c, h in pairs})

    @pl.when(risky)
    def _():
        sub_r = lax.broadcasted_iota(jnp.int32, (S, HGRN_DIM), 0)
        intra = {}
        for c, h in pairs:
            q, k, b, v = view(q_scr, c, h), view(k_scr, c, h), view(b_scr, c, h), view(i_ref, c, h)
            k_sub = [jnp.where(rowblk < i, k * jnp.exp(jnp.minimum(b[i * S:i * S + 1, :] - b, 0.0)), 0.0)
                     for i in range(1, nsub)]
            att = lax.dot_general(jnp.concatenate(sub_block_queries(q, b)[1:], axis=1).astype(BF16),
                                  jnp.concatenate(k_sub, axis=1).astype(BF16), nt, preferred_element_type=F32)
            diag = []
            for i in range(nsub):
                b_blk = b[i * S:(i + 1) * S, :]
                q_blk = q[i * S:(i + 1) * S, :]
                acc = jnp.zeros((S, HGRN_DIM), F32)
                for s_ in range(S):
                    r = i * S + s_
                    e = jnp.exp(jnp.minimum(b_blk - b[r:r + 1, :], 0.0))
                    a = jnp.where(sub_r >= s_, q_blk * e * k[r:r + 1, :], 0.0)
                    acc = acc + jnp.sum(a, axis=-1, keepdims=True) * v[r:r + 1, :]
                diag.append(acc)
            intra[c, h] = (jnp.dot(att.astype(BF16), v.astype(BF16), preferred_element_type=F32)
                           + jnp.concatenate(diag, axis=0))
        finish(intra)


def _hgrn(hq, hf, hi, hg, lb, og, batch, seq, guard=HGRN_EXP_GUARD):
    shp = (batch, seq, HGRN_WIDTH)
    blk = pl.BlockSpec((None, HGRN_STEP, HGRN_WIDTH), lambda b, c: (b, c, 0))
    vec = pl.BlockSpec((1, HGRN_WIDTH), lambda b, c: (0, 0))
    out = pl.pallas_call(
        functools.partial(_hgrn_kernel, guard),
        grid=(batch, seq // HGRN_STEP),
        in_specs=[blk, blk, blk, blk, vec, vec],
        out_specs=blk,
        out_shape=jax.ShapeDtypeStruct(shp, F32),
        scratch_shapes=[pltpu.VMEM((HGRN_HEADS, HGRN_DIM, HGRN_DIM), F32)]
        + [pltpu.VMEM((HGRN_STEP, HGRN_WIDTH), F32)] * 3,
        compiler_params=_params("parallel", "arbitrary"),
        name="hgrn",
    )(hq.reshape(shp), hf.reshape(shp), hi.reshape(shp), hg.reshape(shp), lb, og)
    return out.reshape(batch * seq, HGRN_WIDTH)


def _outproj_kernel(attn_ref, hg_ref, x_ref, wo_ref, bo_ref, g2_ref, rw_ref, rb_ref,
                    x1_ref, h2_ref, lpos_ref, gate_ref, seg_ref):
    tm = x_ref.shape[0]
    y = jnp.dot(attn_ref[...].astype(BF16), wo_ref[:ATTN_WIDTH, :], preferred_element_type=F32)
    y = y + jnp.dot(hg_ref[...].astype(BF16), wo_ref[ATTN_WIDTH:, :], preferred_element_type=F32)
    x1 = x_ref[...] + y + bo_ref[...]
    x1_ref[...] = x1
    h2 = _rms(x1, g2_ref[...])
    h2_ref[...] = h2.astype(BF16)
    h_hi = h2.astype(BF16)
    h_lo = (h2 - h_hi.astype(F32)).astype(BF16)
    p_hi = jnp.dot(h_hi, rw_ref[...], preferred_element_type=F32)
    p_lo = jnp.dot(h_lo, rw_ref[...], preferred_element_type=F32)
    logits = (p_hi[:, :N_EXPERTS] + p_hi[:, LANES:LANES + N_EXPERTS]) + p_lo[:, :N_EXPERTS] + rb_ref[...]
    lane = lax.broadcasted_iota(jnp.int32, (tm, N_EXPERTS), 1).astype(F32)
    work = logits
    vals, hots = [], []
    for _ in range(TOP_K):
        m = jnp.max(work, axis=-1, keepdims=True)
        idx = jnp.min(jnp.where(work == m, lane, float(N_EXPERTS)), axis=-1, keepdims=True)
        hot = lane == idx
        vals.append(m)
        hots.append(hot)
        work = jnp.where(hot, -jnp.inf, work)
    ex = [jnp.exp(v - vals[0]) for v in vals]
    den = ex[0] + ex[1] + ex[2] + ex[3]
    sel = jnp.zeros((tm, N_EXPERTS), F32)
    for hot in hots:
        sel = sel + hot.astype(F32)
    r_i = lax.broadcasted_iota(jnp.int32, (tm, tm), 0)
    c_i = lax.broadcasted_iota(jnp.int32, (tm, tm), 1)
    strict = jnp.where(r_i > c_i, 1.0, 0.0).astype(BF16)
    ranks = jnp.dot(strict, sel.astype(BF16), preferred_element_type=F32)
    seg = jnp.floor((jnp.sum(sel, axis=0, keepdims=True) + (SEG_ROWS - 1.0)) * (1.0 / SEG_ROWS)) * SEG_ROWS
    e_r = lax.broadcasted_iota(jnp.int32, (N_EXPERTS, N_EXPERTS), 0)
    e_c = lax.broadcasted_iota(jnp.int32, (N_EXPERTS, N_EXPERTS), 1)
    before = jnp.where(e_r < e_c, 1.0, 0.0).astype(BF16)
    seg_off = jnp.dot(jnp.broadcast_to(seg, (SEG_ROWS, N_EXPERTS)).astype(BF16), before,
                      preferred_element_type=F32)[0:1, :]
    slots = ranks + seg_off
    l4 = lax.broadcasted_iota(jnp.int32, (tm, TOP_K), 1)
    lpos4 = jnp.zeros((tm, TOP_K), F32)
    gate4 = jnp.zeros((tm, TOP_K), F32)
    for j in range(TOP_K):
        sj = jnp.sum(jnp.where(hots[j], slots, 0.0), axis=-1, keepdims=True)
        lpos4 = jnp.where(l4 == j, sj, lpos4)
        gate4 = jnp.where(l4 == j, ex[j] / den, gate4)
    lpos_ref[...] = lpos4.astype(jnp.int32)
    gate_ref[...] = gate4
    seg_ref[...] = seg


def _outproj(attn, hgo, x2, wo_bf, bo, g2, rw, rb):
    t = x2.shape[0]
    row = lambda i: (i, 0)
    fixed = lambda i: (0, 0)
    return pl.pallas_call(
        _outproj_kernel,
        grid=(t // SORT_TILE,),
        in_specs=[pl.BlockSpec((SORT_TILE, ATTN_WIDTH), row),
                  pl.BlockSpec((SORT_TILE, HGRN_WIDTH), row),
                  pl.BlockSpec((SORT_TILE, D_MODEL), row),
                  pl.BlockSpec((ATTN_WIDTH + HGRN_WIDTH, D_MODEL), fixed),
                  pl.BlockSpec((1, D_MODEL), fixed),
                  pl.BlockSpec((1, D_MODEL), fixed),
                  pl.BlockSpec((D_MODEL, 2 * LANES), fixed),
                  pl.BlockSpec((1, N_EXPERTS), fixed)],
        out_specs=[pl.BlockSpec((SORT_TILE, D_MODEL), row),
                   pl.BlockSpec((SORT_TILE, D_MODEL), row),
                   pl.BlockSpec((SORT_TILE, TOP_K), row),
                   pl.BlockSpec((SORT_TILE, TOP_K), row),
                   pl.BlockSpec((None, 1, N_EXPERTS), lambda i: (i, 0, 0))],
        out_shape=[jax.ShapeDtypeStruct((t, D_MODEL), F32),
                   jax.ShapeDtypeStruct((t, D_MODEL), BF16),
                   jax.ShapeDtypeStruct((t, TOP_K), jnp.int32),
                   jax.ShapeDtypeStruct((t, TOP_K), F32),
                   jax.ShapeDtypeStruct((t // SORT_TILE, 1, N_EXPERTS), F32)],
        compiler_params=_params("parallel"),
        name="outproj",
    )(attn, hgo, x2, wo_bf, bo, g2, rw, rb)


def _rows(start, n):
    return pl.ds(pl.multiple_of(start, SEG_ROWS), n)


def _repeat(cnt, fn):
    def body(j, carry):
        fn(j)
        return carry

    lax.fori_loop(0, cnt, body, 0)


BIG_CAP = SORT_SLOTS // BIG_PIECE
SMALL_CAP = N_EXPERTS * (BIG_PIECE // SEG_ROWS - 1)


def _tile_pieces(lists, step, fn):
    big_loc, big_glob, big_cnt, small_loc, small_glob, small_cnt = lists
    b0 = step * BIG_CAP
    _repeat(big_cnt[step], lambda j: fn(big_loc[b0 + j], big_glob[b0 + j], BIG_PIECE))
    s0 = step * SMALL_CAP
    _repeat(small_cnt[step], lambda j: fn(small_loc[s0 + j], small_glob[s0 + j], SEG_ROWS))


def _tile_piece_waits(lists, step, wait):
    _repeat(lists[2][step], lambda j: wait(BIG_PIECE))
    _repeat(lists[5][step], lambda j: wait(SEG_ROWS))


def _dispatch_kernel(*refs):
    lists, (zdst_ref, zbig_ref, zsmall_ref) = refs[:6], refs[6:9]
    h_ref, lpos_t_ref, xs_ref, lbuf_ref, zbuf_ref, sem, zsem = refs[9:]
    i = pl.program_id(0)
    last = pl.num_programs(0) - 1
    buf = i % 2

    def piece(b, local, glob, n):
        return pltpu.make_async_copy(lbuf_ref.at[b, _rows(local, n), :], xs_ref.at[_rows(glob, n), :], sem.at[b])

    def drain(step):
        _tile_piece_waits(lists, step, lambda n: piece(step % 2, 0, 0, n).wait())

    def zero_piece(glob, n):
        return pltpu.make_async_copy(zbuf_ref.at[_rows(0, n), :], xs_ref.at[_rows(glob, n), :], zsem)

    @pl.when(i == 0)
    def _():
        zbuf_ref[...] = jnp.zeros_like(zbuf_ref)
        for e in range(N_EXPERTS + 1):
            _repeat(zbig_ref[e], lambda j, e=e: zero_piece(zdst_ref[e] + j * BIG_PIECE, BIG_PIECE).start())
            _repeat(zsmall_ref[e], lambda j, e=e: zero_piece(
                zdst_ref[e] + zbig_ref[e] * BIG_PIECE + j * SEG_ROWS, SEG_ROWS).start())
        for e in range(N_EXPERTS + 1):
            _repeat(zbig_ref[e], lambda j: zero_piece(0, BIG_PIECE).wait())
            _repeat(zsmall_ref[e], lambda j: zero_piece(0, SEG_ROWS).wait())

    @pl.when(i >= 2)
    def _():
        drain(i - 2)

    slot = lax.broadcasted_iota(jnp.int32, (SORT_SLOTS, SORT_TILE), 0)
    onehot = jnp.zeros((SORT_SLOTS, SORT_TILE), F32)
    for k in range(TOP_K):
        onehot = onehot + jnp.where(slot == lpos_t_ref[k:k + 1, :], 1.0, 0.0)
    lbuf_ref[buf] = jnp.dot(onehot.astype(BF16), h_ref[...], preferred_element_type=F32)
    _tile_pieces(lists, i, lambda local, glob, n: piece(buf, local, glob, n).start())

    @pl.when(i == last)
    def _():
        @pl.when(i >= 1)
        def _():
            drain(i - 1)

        drain(i)


def _dispatch(tables, zero_tables, h2, lpos_t, n_rows):
    t = h2.shape[0]
    tile = lambda i, *_: (i, 0)
    return pl.pallas_call(
        _dispatch_kernel,
        grid_spec=pltpu.PrefetchScalarGridSpec(
            num_scalar_prefetch=9,
            grid=(t // SORT_TILE,),
            in_specs=[pl.BlockSpec((SORT_TILE, D_MODEL), tile),
                      pl.BlockSpec((TOP_K, SORT_TILE), lambda i, *_: (0, i))],
            out_specs=pl.BlockSpec(memory_space=pl.ANY),
            scratch_shapes=[pltpu.VMEM((2, SORT_SLOTS, D_MODEL), F32),
                            pltpu.VMEM((BIG_PIECE, D_MODEL), F32),
                            pltpu.SemaphoreType.DMA((2,)),
                            pltpu.SemaphoreType.DMA(())]),
        out_shape=jax.ShapeDtypeStruct((n_rows, D_MODEL), F32),
        compiler_params=_params("arbitrary"),
        name="dispatch",
    )(*tables, *zero_tables, h2, lpos_t)


CAST_ROWS = 128


def _expert_kernel(be_ref, nb_ref, eb_ref,
                   xs_ref, w1_hbm, b1_ref, w2_hbm, b2_ref, y_ref,
                   w1f_ref, w2f_ref, w1b_ref, w2i_ref, w2b_ref, started_ref, sem):
    blk = pl.program_id(0)
    half = LANES // 2

    def weight_copies(e, s):
        return (pltpu.make_async_copy(w1_hbm.at[e], w1f_ref.at[s], sem.at[0, s]),
                pltpu.make_async_copy(w2_hbm.at[e], w2f_ref.at[s], sem.at[1, s]))

    e = be_ref[blk]
    first_of_expert = (blk < nb_ref[0]) & ((blk == 0) | (e != be_ref[jnp.maximum(blk - 1, 0)]))

    @pl.when(first_of_expert)
    def _():
        done = jnp.where(blk == 0, 0, started_ref[0])
        started_ref[0] = done + 1
        s = done % 2
        next_blk = blk + eb_ref[e]

        @pl.when(blk == 0)
        def _():
            for c in weight_copies(e, s):
                c.start()

        @pl.when(next_blk < nb_ref[0])
        def _():
            for c in weight_copies(be_ref[next_blk], 1 - s):
                c.start()

        for c in weight_copies(e, s):
            c.wait()

        def cast_rows(r, carry):
            rows = pl.ds(pl.multiple_of(r * CAST_ROWS, CAST_ROWS), CAST_ROWS)
            w1b_ref[rows, :] = w1f_ref[s, rows, :].astype(BF16)
            return carry

        lax.fori_loop(0, D_MODEL // CAST_ROWS, cast_rows, 0)
        for c in range(D_MODEL // LANES):
            cols = slice(c * LANES, (c + 1) * LANES)
            for m in range(EXPERT_FF // LANES):
                lo = m * LANES
                w2i_ref[c, pl.ds(lo, half, stride=2), :] = w2f_ref[s, lo:lo + half, cols]
                w2i_ref[c, pl.ds(lo + 1, half, stride=2), :] = w2f_ref[s, lo + half:lo + LANES, cols]
            w2b_ref[:, cols] = w2i_ref[c].astype(BF16)

    @pl.when(blk < nb_ref[0])
    def _():
        x = xs_ref[...].astype(BF16)
        hid = jnp.dot(x, w1b_ref[...], preferred_element_type=F32) + b1_ref[...]
        even = (lax.broadcasted_iota(jnp.int32, (x.shape[0], LANES), 1) & 1) == 0
        glu, lin = [], []
        for m in range(EXPERT_FF // LANES):
            ha = hid[:, 2 * m * LANES:(2 * m + 1) * LANES]
            hb = hid[:, (2 * m + 1) * LANES:(2 * m + 2) * LANES]
            glu.append(jnp.where(even, ha, pltpu.roll(hb, 1, axis=1)))
            lin.append(jnp.where(even, pltpu.roll(ha, LANES - 1, axis=1), hb))
        glu = jnp.minimum(jnp.concatenate(glu, axis=1), SWIGLU_LIMIT)
        lin = jnp.clip(jnp.concatenate(lin, axis=1), -SWIGLU_LIMIT, SWIGLU_LIMIT)
        act = glu * jax.nn.sigmoid(SWIGLU_ALPHA * glu) * (lin + 1.0)
        y_ref[...] = jnp.dot(act.astype(BF16), w2b_ref[...], preferred_element_type=F32) + b2_ref[...]

    @pl.when(blk >= nb_ref[0])
    def _():
        y_ref[...] = jnp.zeros_like(y_ref)


def _experts(block_tables, xs, w1, b1, w2, b2):
    n_rows = xs.shape[0]
    nblk = n_rows // EXPERT_BLOCK
    rows = lambda b, *_: (b, 0)
    used_rows = lambda b, be, nb, *_: (jnp.minimum(b, nb[0] - 1), 0)
    bias = lambda b, be, *_: (be[b], 0, 0)
    return pl.pallas_call(
        _expert_kernel,
        grid_spec=pltpu.PrefetchScalarGridSpec(
            num_scalar_prefetch=3,
            grid=(nblk,),
            in_specs=[pl.BlockSpec((EXPERT_BLOCK, D_MODEL), used_rows),
                      pl.BlockSpec(memory_space=pl.ANY),
                      pl.BlockSpec((None, 1, 2 * EXPERT_FF), bias),
                      pl.BlockSpec(memory_space=pl.ANY),
                      pl.BlockSpec((None, 1, D_MODEL), bias)],
            out_specs=pl.BlockSpec((EXPERT_BLOCK, D_MODEL), rows),
            scratch_shapes=[pltpu.VMEM((2, D_MODEL, 2 * EXPERT_FF), F32),
                            pltpu.VMEM((2, EXPERT_FF, D_MODEL), F32),
                            pltpu.VMEM((D_MODEL, 2 * EXPERT_FF), BF16),
                            pltpu.VMEM((D_MODEL // LANES, EXPERT_FF, LANES), F32),
                            pltpu.VMEM((EXPERT_FF, D_MODEL), BF16),
                            pltpu.SMEM((1,), jnp.int32),
                            pltpu.SemaphoreType.DMA((2, 2))]),
        out_shape=jax.ShapeDtypeStruct((n_rows, D_MODEL), F32),
        compiler_params=_params("arbitrary"),
        name="experts",
    )(*block_tables, xs, w1, b1, w2, b2)


def _combine_kernel(final_norm, *refs):
    lists = refs[:6]
    yb_ref, x1_ref, lpos_ref, gate_ref, g_ref, o_ref, gbuf_ref, sem = refs[6:]
    i = pl.program_id(0)
    last = pl.num_programs(0) - 1
    buf = i % 2

    def piece(b, local, glob, n):
        return pltpu.make_async_copy(yb_ref.at[_rows(glob, n), :], gbuf_ref.at[b, _rows(local, n), :], sem.at[b])

    def fetch(step):
        _tile_pieces(lists, step, lambda local, glob, n: piece(step % 2, local, glob, n).start())

    @pl.when(i == 0)
    def _():
        gbuf_ref[...] = jnp.zeros_like(gbuf_ref)
        fetch(0)

    @pl.when(i < last)
    def _():
        fetch(i + 1)

    _tile_piece_waits(lists, i, lambda n: piece(buf, 0, 0, n).wait())
    slot = lax.broadcasted_iota(jnp.int32, (SORT_TILE, SORT_SLOTS), 1)
    lpos = lpos_ref[...]
    gates = gate_ref[...]
    weights = jnp.zeros((SORT_TILE, SORT_SLOTS), F32)
    for k in range(TOP_K):
        weights = weights + jnp.where(slot == lpos[:, k:k + 1], gates[:, k:k + 1], 0.0)
    y = x1_ref[...] + jnp.dot(weights.astype(BF16), gbuf_ref[buf].astype(BF16), preferred_element_type=F32)
    o_ref[...] = _rms(y, g_ref[...]) if final_norm else y


def _combine(tables, yb, x1, lpos, gates, g, final_norm):
    t = x1.shape[0]
    tile = lambda i, *_: (i, 0)
    return pl.pallas_call(
        functools.partial(_combine_kernel, final_norm),
        grid_spec=pltpu.PrefetchScalarGridSpec(
            num_scalar_prefetch=6,
            grid=(t // SORT_TILE,),
            in_specs=[pl.BlockSpec(memory_space=pl.ANY),
                      pl.BlockSpec((SORT_TILE, D_MODEL), tile),
                      pl.BlockSpec((SORT_TILE, TOP_K), tile),
                      pl.BlockSpec((SORT_TILE, TOP_K), tile),
                      pl.BlockSpec((1, D_MODEL), lambda i, *_: (0, 0))],
            out_specs=pl.BlockSpec((SORT_TILE, D_MODEL), tile),
            scratch_shapes=[pltpu.VMEM((2, SORT_SLOTS, D_MODEL), F32),
                            pltpu.SemaphoreType.DMA((2,))]),
        out_shape=jax.ShapeDtypeStruct((t, D_MODEL), F32),
        compiler_params=_params("arbitrary"),
        name="combine",
    )(*tables, yb, x1, lpos, gates, g)


def _piece_lists(seg_loc, seg_glob, cnt, skip, rows, cap):
    first = jnp.cumsum(cnt, axis=1) - cnt
    p = jnp.arange(cap, dtype=jnp.int32)
    started = first[:, None, :] <= p[None, :, None]

    def per_piece(rows0):
        base = rows0 + skip - first * rows
        step = base - jnp.concatenate([jnp.zeros_like(base[:, :1]), base[:, :-1]], axis=1)
        return (jnp.sum(jnp.where(started, step[:, None, :], 0), axis=2) + p[None, :] * rows).reshape(-1)

    return per_piece(seg_loc), per_piece(seg_glob), jnp.sum(cnt, axis=1)


def kernel(x, norm1_g, w_in, b_in, attn_sinks, attn_out_g, hgrn_lb_logits, hgrn_out_g, w_out, b_out,
           norm2_g, router_w, router_b, w1, b1, w2, b2, final_g):
    batch, seq, d = x.shape
    t = batch * seq
    depth = w_in.shape[0]
    lower_bounds = jnp.cumsum(jax.nn.softmax(hgrn_lb_logits.astype(F32), axis=0), axis=0)
    assert t % SORT_TILE == 0 and SORT_SLOTS >= SORT_TILE * TOP_K + N_EXPERTS * (SEG_ROWS - 1)
    ntiles = t // SORT_TILE
    nblk = (t * TOP_K + ntiles * N_EXPERTS * (SEG_ROWS - 1)) // EXPERT_BLOCK + N_EXPERTS
    n_rows = nblk * EXPERT_BLOCK
    x2 = x.reshape(t, d)
    for l in range(depth):
        w_in_l = jnp.concatenate([_pair_heads(w_in[l][:, :ATTN_WIDTH], 1), w_in[l][:, ATTN_WIDTH:]], axis=1)
        b_in_l = jnp.concatenate([_pair_heads(b_in[l][:ATTN_WIDTH], 0), b_in[l][ATTN_WIDTH:]])
        aq, ak, av, hq, hf, hi, hg = _inproj(x2, norm1_g[l][None], w_in_l.astype(BF16), b_in_l[None])
        attn = _attention(aq, ak, av, attn_sinks[l], _pair_heads(attn_out_g[l], 0)[None], batch, seq)
        hgo = _hgrn(hq, hf, hi, hg, lower_bounds[l][None], hgrn_out_g[l][None], batch, seq)
        w_out_l = jnp.concatenate([_pair_heads(w_out[l][:ATTN_WIDTH], 0), w_out[l][ATTN_WIDTH:]], axis=0)
        rw_hi = router_w[l].astype(BF16)
        rw_lo = (router_w[l] - rw_hi.astype(F32)).astype(BF16)
        rw_cat = jnp.zeros((d, 2 * LANES), BF16)
        rw_cat = rw_cat.at[:, :N_EXPERTS].set(rw_hi).at[:, LANES:LANES + N_EXPERTS].set(rw_lo)
        x1, h2, lpos, gates, seg = _outproj(
            attn, hgo, x2, w_out_l.astype(BF16), b_out[l][None], norm2_g[l][None],
            rw_cat, router_b[l][None])
        seg = seg.reshape(ntiles, N_EXPERTS).astype(jnp.int32)
        rows_e = jnp.sum(seg, axis=0)
        padded = (rows_e + EXPERT_BLOCK - 1) // EXPERT_BLOCK * EXPERT_BLOCK
        ends = jnp.cumsum(padded)
        pstart = ends - padded
        seg_glob = pstart[None, :] + jnp.cumsum(seg, axis=0) - seg
        seg_loc = jnp.cumsum(seg, axis=1) - seg
        tail = jnp.append(padded - rows_e, n_rows - ends[-1])
        zero_tables = (jnp.append(pstart + rows_e, ends[-1]), tail // BIG_PIECE, tail % BIG_PIECE // SEG_ROWS)
        big = seg // BIG_PIECE
        tables = (_piece_lists(seg_loc, seg_glob, big, jnp.zeros_like(seg), BIG_PIECE, BIG_CAP)
                  + _piece_lists(seg_loc, seg_glob, seg % BIG_PIECE // SEG_ROWS, big * BIG_PIECE, SEG_ROWS,
                                 SMALL_CAP))
        blk_ids = jnp.arange(nblk, dtype=jnp.int32)
        block_e = jnp.minimum(jnp.sum(blk_ids[:, None] * EXPERT_BLOCK >= ends[None, :], axis=-1), N_EXPERTS - 1)
        block_tables = (block_e.astype(jnp.int32), (ends[-1:] // EXPERT_BLOCK).astype(jnp.int32),
                        (padded // EXPERT_BLOCK).astype(jnp.int32))
        xs = _dispatch(tables, zero_tables, h2, lpos.T, n_rows)
        yb = _experts(block_tables, xs, w1[l], b1[l][:, None, :], w2[l], b2[l][:, None, :])
        x2 = _combine(tables, yb, x1, lpos, gates, final_g[None], l == depth - 1)
    return x2.reshape(batch, seq, d)
```

```python
import functools

import numpy as np
import jax
import jax.numpy as jnp
from jax import lax
from jax.experimental import pallas as pl
from jax.experimental.pallas import tpu as pltpu

F32 = jnp.float32
BF16 = jnp.bfloat16

D_MODEL = 1024
ATTN_Q_HEADS = 8
ATTN_KV_HEADS = 2
ATTN_HEAD_DIM = 64
ATTN_GROUP = ATTN_Q_HEADS // ATTN_KV_HEADS
ATTN_WIDTH = ATTN_Q_HEADS * ATTN_HEAD_DIM
ATTN_KV_WIDTH = ATTN_KV_HEADS * ATTN_HEAD_DIM
WINDOW = 128
ATTN_STEP_BLOCKS = 4
HGRN_HEADS = 4
HGRN_DIM = 128
HGRN_WIDTH = HGRN_HEADS * HGRN_DIM
HGRN_CHUNK = 64
HGRN_SUB = 16
HGRN_STEP = 256
IN_WIDTH = ATTN_WIDTH + 2 * ATTN_KV_WIDTH + 4 * HGRN_WIDTH
N_EXPERTS = 32
TOP_K = 4
EXPERT_FF = D_MODEL
SWIGLU_LIMIT = 7.0
SWIGLU_ALPHA = 1.702
NORM_EPS = 1e-5

LANES = 128
ROW_TILE = 1024
EXPERT_BLOCK = 512
SORT_TILE = 256
SEG_ROWS = 8
BIG_PIECE = 32
SORT_SLOTS = 1280
VMEM_LIMIT = 56 * 1024 * 1024

_ALIBI = [float(2.0 ** (-8.0 * (h + 1) / ATTN_Q_HEADS)) for h in range(ATTN_Q_HEADS)]


def _rms(x, g):
    return x * lax.rsqrt(jnp.mean(x * x, axis=-1, keepdims=True) + NORM_EPS) * g


def _params(*sem):
    return pltpu.CompilerParams(dimension_semantics=sem, vmem_limit_bytes=VMEM_LIMIT)


_IN_SPLITS = (ATTN_WIDTH, ATTN_KV_WIDTH, ATTN_KV_WIDTH, HGRN_WIDTH, HGRN_WIDTH, HGRN_WIDTH, HGRN_WIDTH)


def _inproj_kernel(x_ref, g_ref, w_ref, b_ref, *out_refs):
    h = _rms(x_ref[...], g_ref[...]).astype(BF16)
    lo = 0
    for ref, width in zip(out_refs, _IN_SPLITS):
        ref[...] = jnp.dot(h, w_ref[:, lo:lo + width], preferred_element_type=F32) + b_ref[:, lo:lo + width]
        lo += width


def _inproj(x2, g, w_bf, b):
    t = x2.shape[0]
    row = lambda i: (i, 0)
    fixed = lambda i: (0, 0)
    return pl.pallas_call(
        _inproj_kernel,
        grid=(t // ROW_TILE,),
        in_specs=[pl.BlockSpec((ROW_TILE, D_MODEL), row),
                  pl.BlockSpec((1, D_MODEL), fixed),
                  pl.BlockSpec((D_MODEL, IN_WIDTH), fixed),
                  pl.BlockSpec((1, IN_WIDTH), fixed)],
        out_specs=[pl.BlockSpec((ROW_TILE, w), row) for w in _IN_SPLITS],
        out_shape=[jax.ShapeDtypeStruct((t, w), F32) for w in _IN_SPLITS],
        compiler_params=_params("parallel"),
        name="inproj",
    )(x2, g, w_bf, b)


def _attn_bias_tables():
    qi = np.arange(WINDOW)[:, None]
    ki = np.arange(2 * WINDOW)[None, :]
    dist = WINDOW + qi - ki
    in_window = (dist >= 0) & (dist < WINDOW)
    slopes = np.asarray(_ALIBI, np.float32)[:, None, None]
    bias = -(slopes * dist.astype(np.float32)[None])
    tables = [np.where(in_window & (ki >= WINDOW), bias, -np.inf), np.where(in_window, bias, -np.inf)]
    return np.stack(tables).astype(np.float32)


def _pair_heads(a, axis):
    shape = a.shape
    a = a.reshape(shape[:axis] + (ATTN_KV_HEADS, ATTN_GROUP, ATTN_HEAD_DIM) + shape[axis + 1:])
    return jnp.swapaxes(a, axis, axis + 1).reshape(shape)


def _attn_kernel(sink_ref, q_ref, kp_ref, kc_ref, vp_ref, vc_ref, bias_ref, g_ref, o_ref):
    n = pl.program_id(1)
    kc = kc_ref[...].astype(BF16)
    vc = vc_ref[...].astype(BF16)
    keys = [jnp.concatenate([kp_ref[...].astype(BF16), kc[:WINDOW]], axis=0)]
    vals = [jnp.concatenate([vp_ref[...].astype(BF16), vc[:WINDOW]], axis=0)]
    for i in range(1, ATTN_STEP_BLOCKS):
        keys.append(kc[(i - 1) * WINDOW:(i + 1) * WINDOW])
        vals.append(vc[(i - 1) * WINDOW:(i + 1) * WINDOW])
    tables = [jnp.minimum(n, 1)] + [1] * (ATTN_STEP_BLOCKS - 1)
    scale = ATTN_HEAD_DIM ** -0.5
    nt = (((1,), (1,)), ((), ()))
    low = lax.broadcasted_iota(jnp.int32, (WINDOW, LANES), 1) < ATTN_HEAD_DIM

    def softmax(s, table, h):
        s = s * scale + bias_ref[table, h]
        sink = sink_ref[h]
        m = jnp.maximum(jnp.max(s, axis=-1, keepdims=True), sink)
        p = jnp.exp(s - m)
        den = jnp.sum(p, axis=-1, keepdims=True) + jnp.exp(sink - m)
        return (p / den).astype(BF16)

    items = [(i, j) for i in range(ATTN_STEP_BLOCKS) for j in range(ATTN_GROUP)]
    scores = {}
    for i, j in items:
        q = q_ref[i * WINDOW:(i + 1) * WINDOW, j * LANES:(j + 1) * LANES]
        scores[i, j] = (
            lax.dot_general(jnp.where(low, q, 0.0).astype(BF16), keys[i], nt, preferred_element_type=F32),
            lax.dot_general(jnp.where(low, 0.0, q).astype(BF16), keys[i], nt, preferred_element_type=F32))
    probs = {(i, j): (softmax(scores[i, j][0], tables[i], j), softmax(scores[i, j][1], tables[i], ATTN_GROUP + j))
             for i, j in items}
    for i in range(ATTN_STEP_BLOCKS):
        outs = [jnp.where(low, jnp.dot(probs[i, j][0], vals[i], preferred_element_type=F32),
                          jnp.dot(probs[i, j][1], vals[i], preferred_element_type=F32))
                for j in range(ATTN_GROUP)]
        o_ref[i * WINDOW:(i + 1) * WINDOW, :] = _rms(jnp.concatenate(outs, axis=1), g_ref[...])


def _attention(aq, ak, av, sinks, g, batch, seq):
    step = ATTN_STEP_BLOCKS * WINDOW
    aq = aq.reshape(batch, seq, ATTN_WIDTH)
    ak = ak.reshape(batch, seq, ATTN_KV_WIDTH)
    av = av.reshape(batch, seq, ATTN_KV_WIDTH)
    cur = lambda b, n, s: (b, n, 0)
    prev = lambda b, n, s: (b, jnp.maximum(n * ATTN_STEP_BLOCKS - 1, 0), 0)
    out = pl.pallas_call(
        _attn_kernel,
        grid_spec=pltpu.PrefetchScalarGridSpec(
            num_scalar_prefetch=1,
            grid=(batch, seq // step),
            in_specs=[pl.BlockSpec((None, step, ATTN_WIDTH), cur),
                      pl.BlockSpec((None, WINDOW, ATTN_KV_WIDTH), prev),
                      pl.BlockSpec((None, step, ATTN_KV_WIDTH), cur),
                      pl.BlockSpec((None, WINDOW, ATTN_KV_WIDTH), prev),
                      pl.BlockSpec((None, step, ATTN_KV_WIDTH), cur),
                      pl.BlockSpec((2, ATTN_Q_HEADS, WINDOW, 2 * WINDOW), lambda b, n, s: (0, 0, 0, 0)),
                      pl.BlockSpec((1, ATTN_WIDTH), lambda b, n, s: (0, 0))],
            out_specs=pl.BlockSpec((None, step, ATTN_WIDTH), cur)),
        out_shape=jax.ShapeDtypeStruct((batch, seq, ATTN_WIDTH), F32),
        compiler_params=_params("parallel", "parallel"),
        name="attn",
    )(sinks, aq, ak, ak, av, av, jnp.asarray(_attn_bias_tables()), g)
    return out.reshape(batch * seq, ATTN_WIDTH)


HGRN_EXP_GUARD = 80.0


def _hgrn_kernel(guard, q_ref, f_ref, i_ref, gate_ref, lb_ref, og_ref, o_ref, st_ref, q_scr, k_scr, b_scr):
    step = pl.program_id(1)

    @pl.when(step == 0)
    def _():
        st_ref[...] = jnp.zeros_like(st_ref)

    C, S = HGRN_CHUNK, HGRN_SUB
    nsub = C // S
    nchunk = HGRN_STEP // C
    r_i = lax.broadcasted_iota(jnp.int32, (C, C), 0)
    c_i = lax.broadcasted_iota(jnp.int32, (C, C), 1)
    nt = (((1,), (1,)), ((), ()))

    tri = (r_i >= c_i).astype(F32)
    qx = q_ref[...]
    q_scr[...] = qx * jax.nn.sigmoid(qx)
    lb = lb_ref[...]
    f = lb + (1.0 - lb) * jax.nn.sigmoid(f_ref[...])
    k_scr[...] = 1.0 - f
    logf = jnp.log(f)
    for c in range(nchunk):
        b_scr[c * C:(c + 1) * C, :] = jnp.dot(tri, logf[c * C:(c + 1) * C, :], preferred_element_type=F32,
                                              precision=lax.Precision.HIGHEST)
    decay = jnp.concatenate([-jnp.sum(logf[i * S:(i + 1) * S, :], axis=0, keepdims=True)
                             for i in range(HGRN_STEP // S)], axis=0)
    risky = jnp.max(decay) > guard

    rowblk = lax.broadcasted_iota(jnp.int32, (C, HGRN_DIM), 0) // S
    pairs = [(c, h) for c in range(nchunk) for h in range(HGRN_HEADS)]

    def view(ref, c, h):
        return ref[c * C:(c + 1) * C, h * HGRN_DIM:(h + 1) * HGRN_DIM]

    def finish(intra):
        q_dec, gain, update = {}, {}, {}
        for c, h in pairs:
            q, k, b = view(q_scr, c, h), view(k_scr, c, h), view(b_scr, c, h)
            q_dec[c, h] = (q * jnp.exp(b)).astype(BF16)
            bl = b[C - 1:C, :]
            gain[c, h] = jnp.exp(bl)
            update[c, h] = lax.dot_general(view(i_ref, c, h).astype(BF16), (k * jnp.exp(bl - b)).astype(BF16),
                                           (((0,), (0,)), ((), ())), preferred_element_type=F32)
        state = {}
        for h in range(HGRN_HEADS):
            st = st_ref[h]
            for c in range(nchunk):
                state[c, h] = st
                st = st * gain[c, h] + update[c, h]
            st_ref[h] = st
        for c, h in pairs:
            o = intra[c, h] + lax.dot_general(q_dec[c, h], state[c, h].astype(BF16), nt,
                                              preferred_element_type=F32)
            gx = view(gate_ref, c, h)
            og = og_ref[:, h * HGRN_DIM:(h + 1) * HGRN_DIM]
            o = o * lax.rsqrt(jnp.mean(o * o, axis=-1, keepdims=True) + NORM_EPS) * og
            o_ref[c * C:(c + 1) * C, h * HGRN_DIM:(h + 1) * HGRN_DIM] = o * (gx * jax.nn.sigmoid(gx))

    def sub_block_queries(q, b):
        return [jnp.where(rowblk == i, q * jnp.exp(jnp.minimum(b - b[i * S:i * S + 1, :], 0.0)), 0.0)
                for i in range(nsub)]

    @pl.when(jnp.logical_not(risky))
    def _():
        att = {}
        for c, h in pairs:
            q, k, b = view(q_scr, c, h), view(k_scr, c, h), view(b_scr, c, h)
            k_sub = [jnp.where(rowblk <= i, k * jnp.exp(jnp.minimum(b[i * S:i * S + 1, :] - b, guard)), 0.0)
                     for i in range(nsub)]
            att[c, h] = lax.dot_general(jnp.concatenate(sub_block_queries(q, b), axis=1).astype(BF16),
                                        jnp.concatenate(k_sub, axis=1).astype(BF16), nt,
                                        preferred_element_type=F32)
        finish({(c, h): jnp.dot(jnp.where(r_i >= c_i, att[c, h], 0.0).astype(BF16),
                                view(i_ref, c, h).astype(BF16), preferred_element_type=F32) for c, h in pairs})

    @pl.when(risky)
    def _():
        sub_r = lax.broadcasted_iota(jnp.int32, (S, HGRN_DIM), 0)
        intra = {}
        for c, h in pairs:
            q, k, b, v = view(q_scr, c, h), view(k_scr, c, h), view(b_scr, c, h), view(i_ref, c, h)
            k_sub = [jnp.where(rowblk < i, k * jnp.exp(jnp.minimum(b[i * S:i * S + 1, :] - b, 0.0)), 0.0)
                     for i in range(1, nsub)]
            att = lax.dot_general(jnp.concatenate(sub_block_queries(q, b)[1:], axis=1).astype(BF16),
                                  jnp.concatenate(k_sub, axis=1).astype(BF16), nt, preferred_element_type=F32)
            diag = []
            for i in range(nsub):
                b_blk = b[i * S:(i + 1) * S, :]
                q_blk = q[i * S:(i + 1) * S, :]
                acc = jnp.zeros((S, HGRN_DIM), F32)
                for s_ in range(S):
                    r = i * S + s_
                    e = jnp.exp(jnp.minimum(b_blk - b[r:r + 1, :], 0.0))
                    a = jnp.where(sub_r >= s_, q_blk * e * k[r:r + 1, :], 0.0)
                    acc = acc + jnp.sum(a, axis=-1, keepdims=True) * v[r:r + 1, :]
                diag.append(acc)
            intra[c, h] = (jnp.dot(att.astype(BF16), v.astype(BF16), preferred_element_type=F32)
                           + jnp.concatenate(diag, axis=0))
        finish(intra)


def _hgrn(hq, hf, hi, hg, lb, og, batch, seq, guard=HGRN_EXP_GUARD):
    shp = (batch, seq, HGRN_WIDTH)
    blk = pl.BlockSpec((None, HGRN_STEP, HGRN_WIDTH), lambda b, c: (b, c, 0))
    vec = pl.BlockSpec((1, HGRN_WIDTH), lambda b, c: (0, 0))
    out = pl.pallas_call(
        functools.partial(_hgrn_kernel, guard),
        grid=(batch, seq // HGRN_STEP),
        in_specs=[blk, blk, blk, blk, vec, vec],
        out_specs=blk,
        out_shape=jax.ShapeDtypeStruct(shp, F32),
        scratch_shapes=[pltpu.VMEM((HGRN_HEADS, HGRN_DIM, HGRN_DIM), F32)]
        + [pltpu.VMEM((HGRN_STEP, HGRN_WIDTH), F32)] * 3,
        compiler_params=_params("parallel", "arbitrary"),
        name="hgrn",
    )(hq.reshape(shp), hf.reshape(shp), hi.reshape(shp), hg.reshape(shp), lb, og)
    return out.reshape(batch * seq, HGRN_WIDTH)


def _outproj_kernel(attn_ref, hg_ref, x_ref, wo_ref, bo_ref, g2_ref, rw_ref, rb_ref,
                    x1_ref, h2_ref, lpos_ref, gate_ref, seg_ref):
    tm = x_ref.shape[0]
    y = jnp.dot(attn_ref[...].astype(BF16), wo_ref[:ATTN_WIDTH, :], preferred_element_type=F32)
    y = y + jnp.dot(hg_ref[...].astype(BF16), wo_ref[ATTN_WIDTH:, :], preferred_element_type=F32)
    x1 = x_ref[...] + y + bo_ref[...]
    x1_ref[...] = x1
    h2 = _rms(x1, g2_ref[...])
    h2_ref[...] = h2.astype(BF16)
    h_hi = h2.astype(BF16)
    h_lo = (h2 - h_hi.astype(F32)).astype(BF16)
    p_hi = jnp.dot(h_hi, rw_ref[...], preferred_element_type=F32)
    p_lo = jnp.dot(h_lo, rw_ref[...], preferred_element_type=F32)
    logits = (p_hi[:, :N_EXPERTS] + p_hi[:, LANES:LANES + N_EXPERTS]) + p_lo[:, :N_EXPERTS] + rb_ref[...]
    lane = lax.broadcasted_iota(jnp.int32, (tm, N_EXPERTS), 1).astype(F32)
    work = logits
    vals, hots = [], []
    for _ in range(TOP_K):
        m = jnp.max(work, axis=-1, keepdims=True)
        idx = jnp.min(jnp.where(work == m, lane, float(N_EXPERTS)), axis=-1, keepdims=True)
        hot = lane == idx
        vals.append(m)
        hots.append(hot)
        work = jnp.where(hot, -jnp.inf, work)
    ex = [jnp.exp(v - vals[0]) for v in vals]
    den = ex[0] + ex[1] + ex[2] + ex[3]
    sel = jnp.zeros((tm, N_EXPERTS), F32)
    for hot in hots:
        sel = sel + hot.astype(F32)
    r_i = lax.broadcasted_iota(jnp.int32, (tm, tm), 0)
    c_i = lax.broadcasted_iota(jnp.int32, (tm, tm), 1)
    strict = jnp.where(r_i > c_i, 1.0, 0.0).astype(BF16)
    ranks = jnp.dot(strict, sel.astype(BF16), preferred_element_type=F32)
    seg = jnp.floor((jnp.sum(sel, axis=0, keepdims=True) + (SEG_ROWS - 1.0)) * (1.0 / SEG_ROWS)) * SEG_ROWS
    e_r = lax.broadcasted_iota(jnp.int32, (N_EXPERTS, N_EXPERTS), 0)
    e_c = lax.broadcasted_iota(jnp.int32, (N_EXPERTS, N_EXPERTS), 1)
    before = jnp.where(e_r < e_c, 1.0, 0.0).astype(BF16)
    seg_off = jnp.dot(jnp.broadcast_to(seg, (SEG_ROWS, N_EXPERTS)).astype(BF16), before,
                      preferred_element_type=F32)[0:1, :]
    slots = ranks + seg_off
    l4 = lax.broadcasted_iota(jnp.int32, (tm, TOP_K), 1)
    lpos4 = jnp.zeros((tm, TOP_K), F32)
    gate4 = jnp.zeros((tm, TOP_K), F32)
    for j in range(TOP_K):
        sj = jnp.sum(jnp.where(hots[j], slots, 0.0), axis=-1, keepdims=True)
        lpos4 = jnp.where(l4 == j, sj, lpos4)
        gate4 = jnp.where(l4 == j, ex[j] / den, gate4)
    lpos_ref[...] = lpos4.astype(jnp.int32)
    gate_ref[...] = gate4
    seg_ref[...] = seg


def _outproj(attn, hgo, x2, wo_bf, bo, g2, rw, rb):
    t = x2.shape[0]
    row = lambda i: (i, 0)
    fixed = lambda i: (0, 0)
    return pl.pallas_call(
        _outproj_kernel,
        grid=(t // SORT_TILE,),
        in_specs=[pl.BlockSpec((SORT_TILE, ATTN_WIDTH), row),
                  pl.BlockSpec((SORT_TILE, HGRN_WIDTH), row),
                  pl.BlockSpec((SORT_TILE, D_MODEL), row),
                  pl.BlockSpec((ATTN_WIDTH + HGRN_WIDTH, D_MODEL), fixed),
                  pl.BlockSpec((1, D_MODEL), fixed),
                  pl.BlockSpec((1, D_MODEL), fixed),
                  pl.BlockSpec((D_MODEL, 2 * LANES), fixed),
                  pl.BlockSpec((1, N_EXPERTS), fixed)],
        out_specs=[pl.BlockSpec((SORT_TILE, D_MODEL), row),
                   pl.BlockSpec((SORT_TILE, D_MODEL), row),
                   pl.BlockSpec((SORT_TILE, TOP_K), row),
                   pl.BlockSpec((SORT_TILE, TOP_K), row),
                   pl.BlockSpec((None, 1, N_EXPERTS), lambda i: (i, 0, 0))],
        out_shape=[jax.ShapeDtypeStruct((t, D_MODEL), F32),
                   jax.ShapeDtypeStruct((t, D_MODEL), BF16),
                   jax.ShapeDtypeStruct((t, TOP_K), jnp.int32),
                   jax.ShapeDtypeStruct((t, TOP_K), F32),
                   jax.ShapeDtypeStruct((t // SORT_TILE, 1, N_EXPERTS), F32)],
        compiler_params=_params("parallel"),
        name="outproj",
    )(attn, hgo, x2, wo_bf, bo, g2, rw, rb)


def _rows(start, n):
    return pl.ds(pl.multiple_of(start, SEG_ROWS), n)


def _repeat(cnt, fn):
    def body(j, carry):
        fn(j)
        return carry

    lax.fori_loop(0, cnt, body, 0)


BIG_CAP = SORT_SLOTS // BIG_PIECE
SMALL_CAP = N_EXPERTS * (BIG_PIECE // SEG_ROWS - 1)


def _tile_pieces(lists, step, fn):
    big_loc, big_glob, big_cnt, small_loc, small_glob, small_cnt = lists
    b0 = step * BIG_CAP
    _repeat(big_cnt[step], lambda j: fn(big_loc[b0 + j], big_glob[b0 + j], BIG_PIECE))
    s0 = step * SMALL_CAP
    _repeat(small_cnt[step], lambda j: fn(small_loc[s0 + j], small_glob[s0 + j], SEG_ROWS))


def _tile_piece_waits(lists, step, wait):
    _repeat(lists[2][step], lambda j: wait(BIG_PIECE))
    _repeat(lists[5][step], lambda j: wait(SEG_ROWS))


def _dispatch_kernel(*refs):
    lists, (zdst_ref, zbig_ref, zsmall_ref) = refs[:6], refs[6:9]
    h_ref, lpos_t_ref, xs_ref, lbuf_ref, zbuf_ref, sem, zsem = refs[9:]
    i = pl.program_id(0)
    last = pl.num_programs(0) - 1
    buf = i % 2

    def piece(b, local, glob, n):
        return pltpu.make_async_copy(lbuf_ref.at[b, _rows(local, n), :], xs_ref.at[_rows(glob, n), :], sem.at[b])

    def drain(step):
        _tile_piece_waits(lists, step, lambda n: piece(step % 2, 0, 0, n).wait())

    def zero_piece(glob, n):
        return pltpu.make_async_copy(zbuf_ref.at[_rows(0, n), :], xs_ref.at[_rows(glob, n), :], zsem)

    @pl.when(i == 0)
    def _():
        zbuf_ref[...] = jnp.zeros_like(zbuf_ref)
        for e in range(N_EXPERTS + 1):
            _repeat(zbig_ref[e], lambda j, e=e: zero_piece(zdst_ref[e] + j * BIG_PIECE, BIG_PIECE).start())
            _repeat(zsmall_ref[e], lambda j, e=e: zero_piece(
                zdst_ref[e] + zbig_ref[e] * BIG_PIECE + j * SEG_ROWS, SEG_ROWS).start())
        for e in range(N_EXPERTS + 1):
            _repeat(zbig_ref[e], lambda j: zero_piece(0, BIG_PIECE).wait())
            _repeat(zsmall_ref[e], lambda j: zero_piece(0, SEG_ROWS).wait())

    @pl.when(i >= 2)
    def _():
        drain(i - 2)

    slot = lax.broadcasted_iota(jnp.int32, (SORT_SLOTS, SORT_TILE), 0)
    onehot = jnp.zeros((SORT_SLOTS, SORT_TILE), F32)
    for k in range(TOP_K):
        onehot = onehot + jnp.where(slot == lpos_t_ref[k:k + 1, :], 1.0, 0.0)
    lbuf_ref[buf] = jnp.dot(onehot.astype(BF16), h_ref[...], preferred_element_type=F32)
    _tile_pieces(lists, i, lambda local, glob, n: piece(buf, local, glob, n).start())

    @pl.when(i == last)
    def _():
        @pl.when(i >= 1)
        def _():
            drain(i - 1)

        drain(i)


def _dispatch(tables, zero_tables, h2, lpos_t, n_rows):
    t = h2.shape[0]
    tile = lambda i, *_: (i, 0)
    return pl.pallas_call(
        _dispatch_kernel,
        grid_spec=pltpu.PrefetchScalarGridSpec(
            num_scalar_prefetch=9,
            grid=(t // SORT_TILE,),
            in_specs=[pl.BlockSpec((SORT_TILE, D_MODEL), tile),
                      pl.BlockSpec((TOP_K, SORT_TILE), lambda i, *_: (0, i))],
            out_specs=pl.BlockSpec(memory_space=pl.ANY),
            scratch_shapes=[pltpu.VMEM((2, SORT_SLOTS, D_MODEL), F32),
                            pltpu.VMEM((BIG_PIECE, D_MODEL), F32),
                            pltpu.SemaphoreType.DMA((2,)),
                            pltpu.SemaphoreType.DMA(())]),
        out_shape=jax.ShapeDtypeStruct((n_rows, D_MODEL), F32),
        compiler_params=_params("arbitrary"),
        name="dispatch",
    )(*tables, *zero_tables, h2, lpos_t)


CAST_ROWS = 128


def _expert_kernel(be_ref, nb_ref, eb_ref, filled_ref,
                   xs_ref, w1_hbm, b1_ref, w2_hbm, b2_ref, y_ref,
                   w1f_ref, w2f_ref, w1b_ref, w2i_ref, w2b_ref, started_ref, sem):
    blk = pl.program_id(0)
    half = LANES // 2

    def weight_copies(e, s):
        return (pltpu.make_async_copy(w1_hbm.at[e], w1f_ref.at[s], sem.at[0, s]),
                pltpu.make_async_copy(w2_hbm.at[e], w2f_ref.at[s], sem.at[1, s]))

    e = be_ref[blk]
    first_of_expert = (blk < nb_ref[0]) & ((blk == 0) | (e != be_ref[jnp.maximum(blk - 1, 0)]))

    @pl.when(first_of_expert)
    def _():
        done = jnp.where(blk == 0, 0, started_ref[0])
        started_ref[0] = done + 1
        s = done % 2
        next_blk = blk + eb_ref[e]

        @pl.when(blk == 0)
        def _():
            for c in weight_copies(e, s):
                c.start()

        @pl.when(next_blk < nb_ref[0])
        def _():
            for c in weight_copies(be_ref[next_blk], 1 - s):
                c.start()

        for c in weight_copies(e, s):
            c.wait()

        def cast_rows(r, carry):
            rows = pl.ds(pl.multiple_of(r * CAST_ROWS, CAST_ROWS), CAST_ROWS)
            w1b_ref[rows, :] = w1f_ref[s, rows, :].astype(BF16)
            return carry

        lax.fori_loop(0, D_MODEL // CAST_ROWS, cast_rows, 0)
        for c in range(D_MODEL // LANES):
            cols = slice(c * LANES, (c + 1) * LANES)
            for m in range(EXPERT_FF // LANES):
                lo = m * LANES
                w2i_ref[c, pl.ds(lo, half, stride=2), :] = w2f_ref[s, lo:lo + half, cols]
                w2i_ref[c, pl.ds(lo + 1, half, stride=2), :] = w2f_ref[s, lo + half:lo + LANES, cols]
            w2b_ref[:, cols] = w2i_ref[c].astype(BF16)

    def ffn(rows):
        x = xs_ref[:rows, :].astype(BF16)
        hid = jnp.dot(x, w1b_ref[...], preferred_element_type=F32) + b1_ref[...]
        even = (lax.broadcasted_iota(jnp.int32, (rows, LANES), 1) & 1) == 0
        glu, lin = [], []
        for m in range(EXPERT_FF // LANES):
            ha = hid[:, 2 * m * LANES:(2 * m + 1) * LANES]
            hb = hid[:, (2 * m + 1) * LANES:(2 * m + 2) * LANES]
            glu.append(jnp.where(even, ha, pltpu.roll(hb, 1, axis=1)))
            lin.append(jnp.where(even, pltpu.roll(ha, LANES - 1, axis=1), hb))
        glu = jnp.minimum(jnp.concatenate(glu, axis=1), SWIGLU_LIMIT)
        lin = jnp.clip(jnp.concatenate(lin, axis=1), -SWIGLU_LIMIT, SWIGLU_LIMIT)
        act = glu * jax.nn.sigmoid(SWIGLU_ALPHA * glu) * (lin + 1.0)
        y_ref[:rows, :] = jnp.dot(act.astype(BF16), w2b_ref[...], preferred_element_type=F32) + b2_ref[...]

    half_block = EXPERT_BLOCK // 2

    @pl.when(filled_ref[blk] > half_block)
    def _():
        ffn(EXPERT_BLOCK)

    @pl.when((filled_ref[blk] > 0) & (filled_ref[blk] <= half_block))
    def _():
        ffn(half_block)
        y_ref[half_block:, :] = jnp.zeros((EXPERT_BLOCK - half_block, D_MODEL), F32)

    @pl.when(filled_ref[blk] == 0)
    def _():
        y_ref[...] = jnp.zeros_like(y_ref)


def _experts(block_tables, xs, w1, b1, w2, b2):
    n_rows = xs.shape[0]
    nblk = n_rows // EXPERT_BLOCK
    rows = lambda b, *_: (b, 0)
    used_rows = lambda b, be, nb, *_: (jnp.minimum(b, nb[0] - 1), 0)
    bias = lambda b, be, *_: (be[b], 0, 0)
    return pl.pallas_call(
        _expert_kernel,
        grid_spec=pltpu.PrefetchScalarGridSpec(
            num_scalar_prefetch=4,
            grid=(nblk,),
            in_specs=[pl.BlockSpec((EXPERT_BLOCK, D_MODEL), used_rows),
                      pl.BlockSpec(memory_space=pl.ANY),
                      pl.BlockSpec((None, 1, 2 * EXPERT_FF), bias),
                      pl.BlockSpec(memory_space=pl.ANY),
                      pl.BlockSpec((None, 1, D_MODEL), bias)],
            out_specs=pl.BlockSpec((EXPERT_BLOCK, D_MODEL), rows),
            scratch_shapes=[pltpu.VMEM((2, D_MODEL, 2 * EXPERT_FF), F32),
                            pltpu.VMEM((2, EXPERT_FF, D_MODEL), F32),
                            pltpu.VMEM((D_MODEL, 2 * EXPERT_FF), BF16),
                            pltpu.VMEM((D_MODEL // LANES, EXPERT_FF, LANES), F32),
                            pltpu.VMEM((EXPERT_FF, D_MODEL), BF16),
                            pltpu.SMEM((1,), jnp.int32),
                            pltpu.SemaphoreType.DMA((2, 2))]),
        out_shape=jax.ShapeDtypeStruct((n_rows, D_MODEL), F32),
        compiler_params=_params("arbitrary"),
        name="experts",
    )(*block_tables, xs, w1, b1, w2, b2)


def _combine_kernel(final_norm, *refs):
    lists = refs[:6]
    yb_ref, x1_ref, lpos_ref, gate_ref, g_ref, o_ref, gbuf_ref, sem = refs[6:]
    i = pl.program_id(0)
    last = pl.num_programs(0) - 1
    buf = i % 2

    def piece(b, local, glob, n):
        return pltpu.make_async_copy(yb_ref.at[_rows(glob, n), :], gbuf_ref.at[b, _rows(local, n), :], sem.at[b])

    def fetch(step):
        _tile_pieces(lists, step, lambda local, glob, n: piece(step % 2, local, glob, n).start())

    @pl.when(i == 0)
    def _():
        gbuf_ref[...] = jnp.zeros_like(gbuf_ref)
        fetch(0)

    @pl.when(i < last)
    def _():
        fetch(i + 1)

    _tile_piece_waits(lists, i, lambda n: piece(buf, 0, 0, n).wait())
    slot = lax.broadcasted_iota(jnp.int32, (SORT_TILE, SORT_SLOTS), 1)
    lpos = lpos_ref[...]
    gates = gate_ref[...]
    weights = jnp.zeros((SORT_TILE, SORT_SLOTS), F32)
    for k in range(TOP_K):
        weights = weights + jnp.where(slot == lpos[:, k:k + 1], gates[:, k:k + 1], 0.0)
    y = x1_ref[...] + jnp.dot(weights.astype(BF16), gbuf_ref[buf].astype(BF16), preferred_element_type=F32)
    o_ref[...] = _rms(y, g_ref[...]) if final_norm else y


def _combine(tables, yb, x1, lpos, gates, g, final_norm):
    t = x1.shape[0]
    tile = lambda i, *_: (i, 0)
    return pl.pallas_call(
        functools.partial(_combine_kernel, final_norm),
        grid_spec=pltpu.PrefetchScalarGridSpec(
            num_scalar_prefetch=6,
            grid=(t // SORT_TILE,),
            in_specs=[pl.BlockSpec(memory_space=pl.ANY),
                      pl.BlockSpec((SORT_TILE, D_MODEL), tile),
                      pl.BlockSpec((SORT_TILE, TOP_K), tile),
                      pl.BlockSpec((SORT_TILE, TOP_K), tile),
                      pl.BlockSpec((1, D_MODEL), lambda i, *_: (0, 0))],
            out_specs=pl.BlockSpec((SORT_TILE, D_MODEL), tile),
            scratch_shapes=[pltpu.VMEM((2, SORT_SLOTS, D_MODEL), F32),
                            pltpu.SemaphoreType.DMA((2,))]),
        out_shape=jax.ShapeDtypeStruct((t, D_MODEL), F32),
        compiler_params=_params("arbitrary"),
        name="combine",
    )(*tables, yb, x1, lpos, gates, g)


def _piece_lists(seg_loc, seg_glob, cnt, skip, rows, cap):
    first = jnp.cumsum(cnt, axis=1) - cnt
    p = jnp.arange(cap, dtype=jnp.int32)
    started = first[:, None, :] <= p[None, :, None]

    def per_piece(rows0):
        base = rows0 + skip - first * rows
        step = base - jnp.concatenate([jnp.zeros_like(base[:, :1]), base[:, :-1]], axis=1)
        return (jnp.sum(jnp.where(started, step[:, None, :], 0), axis=2) + p[None, :] * rows).reshape(-1)

    return per_piece(seg_loc), per_piece(seg_glob), jnp.sum(cnt, axis=1)


def kernel(x, norm1_g, w_in, b_in, attn_sinks, attn_out_g, hgrn_lb_logits, hgrn_out_g, w_out, b_out,
           norm2_g, router_w, router_b, w1, b1, w2, b2, final_g):
    batch, seq, d = x.shape
    t = batch * seq
    depth = w_in.shape[0]
    lower_bounds = jnp.cumsum(jax.nn.softmax(hgrn_lb_logits.astype(F32), axis=0), axis=0)
    assert t % SORT_TILE == 0 and SORT_SLOTS >= SORT_TILE * TOP_K + N_EXPERTS * (SEG_ROWS - 1)
    ntiles = t // SORT_TILE
    nblk = (t * TOP_K + ntiles * N_EXPERTS * (SEG_ROWS - 1)) // EXPERT_BLOCK + N_EXPERTS
    n_rows = nblk * EXPERT_BLOCK
    x2 = x.reshape(t, d)
    for l in range(depth):
        w_in_l = jnp.concatenate([_pair_heads(w_in[l][:, :ATTN_WIDTH], 1), w_in[l][:, ATTN_WIDTH:]], axis=1)
        b_in_l = jnp.concatenate([_pair_heads(b_in[l][:ATTN_WIDTH], 0), b_in[l][ATTN_WIDTH:]])
        aq, ak, av, hq, hf, hi, hg = _inproj(x2, norm1_g[l][None], w_in_l.astype(BF16), b_in_l[None])
        attn = _attention(aq, ak, av, attn_sinks[l], _pair_heads(attn_out_g[l], 0)[None], batch, seq)
        hgo = _hgrn(hq, hf, hi, hg, lower_bounds[l][None], hgrn_out_g[l][None], batch, seq)
        w_out_l = jnp.concatenate([_pair_heads(w_out[l][:ATTN_WIDTH], 0), w_out[l][ATTN_WIDTH:]], axis=0)
        rw_hi = router_w[l].astype(BF16)
        rw_lo = (router_w[l] - rw_hi.astype(F32)).astype(BF16)
        rw_cat = jnp.zeros((d, 2 * LANES), BF16)
        rw_cat = rw_cat.at[:, :N_EXPERTS].set(rw_hi).at[:, LANES:LANES + N_EXPERTS].set(rw_lo)
        x1, h2, lpos, gates, seg = _outproj(
            attn, hgo, x2, w_out_l.astype(BF16), b_out[l][None], norm2_g[l][None],
            rw_cat, router_b[l][None])
        seg = seg.reshape(ntiles, N_EXPERTS).astype(jnp.int32)
        rows_e = jnp.sum(seg, axis=0)
        padded = (rows_e + EXPERT_BLOCK - 1) // EXPERT_BLOCK * EXPERT_BLOCK
        ends = jnp.cumsum(padded)
        pstart = ends - padded
        seg_glob = pstart[None, :] + jnp.cumsum(seg, axis=0) - seg
        seg_loc = jnp.cumsum(seg, axis=1) - seg
        tail = jnp.append(padded - rows_e, n_rows - ends[-1])
        zero_tables = (jnp.append(pstart + rows_e, ends[-1]), tail // BIG_PIECE, tail % BIG_PIECE // SEG_ROWS)
        big = seg // BIG_PIECE
        tables = (_piece_lists(seg_loc, seg_glob, big, jnp.zeros_like(seg), BIG_PIECE, BIG_CAP)
                  + _piece_lists(seg_loc, seg_glob, seg % BIG_PIECE // SEG_ROWS, big * BIG_PIECE, SEG_ROWS,
                                 SMALL_CAP))
        blk_ids = jnp.arange(nblk, dtype=jnp.int32)
        block_e = jnp.minimum(jnp.sum(blk_ids[:, None] * EXPERT_BLOCK >= ends[None, :], axis=-1), N_EXPERTS - 1)
        in_expert = rows_e[None, :] - (blk_ids[:, None] * EXPERT_BLOCK - pstart[None, :])
        filled = jnp.sum(jnp.where((block_e[:, None] == jnp.arange(N_EXPERTS)[None, :])
                                   & (blk_ids[:, None] * EXPERT_BLOCK < ends[-1]),
                                   jnp.clip(in_expert, 0, EXPERT_BLOCK), 0), axis=1)
        block_tables = (block_e.astype(jnp.int32), (ends[-1:] // EXPERT_BLOCK).astype(jnp.int32),
                        (padded // EXPERT_BLOCK).astype(jnp.int32), filled.astype(jnp.int32))
        xs = _dispatch(tables, zero_tables, h2, lpos.T, n_rows)
        yb = _experts(block_tables, xs, w1[l], b1[l][:, None, :], w2[l], b2[l][:, None, :])
        x2 = _combine(tables, yb, x1, lpos, gates, final_g[None], l == depth - 1)
    return x2.reshape(batch, seq, d)
```

```python
import functools

import numpy as np
import jax
import jax.numpy as jnp
from jax import lax
from jax.experimental import pallas as pl
from jax.experimental.pallas import tpu as pltpu

F32 = jnp.float32
BF16 = jnp.bfloat16

D_MODEL = 1024
ATTN_Q_HEADS = 8
ATTN_KV_HEADS = 2
ATTN_HEAD_DIM = 64
ATTN_GROUP = ATTN_Q_HEADS // ATTN_KV_HEADS
ATTN_WIDTH = ATTN_Q_HEADS * ATTN_HEAD_DIM
ATTN_KV_WIDTH = ATTN_KV_HEADS * ATTN_HEAD_DIM
WINDOW = 128
ATTN_STEP_BLOCKS = 4
HGRN_HEADS = 4
HGRN_DIM = 128
HGRN_WIDTH = HGRN_HEADS * HGRN_DIM
HGRN_CHUNK = 64
HGRN_SUB = 16
HGRN_STEP = 256
IN_WIDTH = ATTN_WIDTH + 2 * ATTN_KV_WIDTH + 4 * HGRN_WIDTH
N_EXPERTS = 32
TOP_K = 4
EXPERT_FF = D_MODEL
SWIGLU_LIMIT = 7.0
SWIGLU_ALPHA = 1.702
NORM_EPS = 1e-5

LANES = 128
ROW_TILE = 1024
EXPERT_BLOCK = 512
SORT_TILE = 256
OUTPROJ_TILES = 4
SEG_ROWS = 8
BIG_PIECE = 32
SORT_SLOTS = 1280
VMEM_LIMIT = 56 * 1024 * 1024

_ALIBI = [float(2.0 ** (-8.0 * (h + 1) / ATTN_Q_HEADS)) for h in range(ATTN_Q_HEADS)]


def _rms(x, g):
    return x * lax.rsqrt(jnp.mean(x * x, axis=-1, keepdims=True) + NORM_EPS) * g


def _params(*sem):
    return pltpu.CompilerParams(dimension_semantics=sem, vmem_limit_bytes=VMEM_LIMIT)


_IN_SPLITS = (ATTN_WIDTH, ATTN_KV_WIDTH, ATTN_KV_WIDTH, HGRN_WIDTH, HGRN_WIDTH, HGRN_WIDTH, HGRN_WIDTH)


def _inproj_kernel(x_ref, g_ref, w_ref, b_ref, *out_refs):
    h = _rms(x_ref[...], g_ref[...]).astype(BF16)
    lo = 0
    for ref, width in zip(out_refs, _IN_SPLITS):
        ref[...] = jnp.dot(h, w_ref[:, lo:lo + width], preferred_element_type=F32) + b_ref[:, lo:lo + width]
        lo += width


def _inproj(x2, g, w_bf, b):
    t = x2.shape[0]
    row = lambda i: (i, 0)
    fixed = lambda i: (0, 0)
    return pl.pallas_call(
        _inproj_kernel,
        grid=(t // ROW_TILE,),
        in_specs=[pl.BlockSpec((ROW_TILE, D_MODEL), row),
                  pl.BlockSpec((1, D_MODEL), fixed),
                  pl.BlockSpec((D_MODEL, IN_WIDTH), fixed),
                  pl.BlockSpec((1, IN_WIDTH), fixed)],
        out_specs=[pl.BlockSpec((ROW_TILE, w), row) for w in _IN_SPLITS],
        out_shape=[jax.ShapeDtypeStruct((t, w), F32) for w in _IN_SPLITS],
        compiler_params=_params("parallel"),
        name="inproj",
    )(x2, g, w_bf, b)


def _attn_bias_tables():
    qi = np.arange(WINDOW)[:, None]
    ki = np.arange(2 * WINDOW)[None, :]
    dist = WINDOW + qi - ki
    in_window = (dist >= 0) & (dist < WINDOW)
    slopes = np.asarray(_ALIBI, np.float32)[:, None, None]
    bias = -(slopes * dist.astype(np.float32)[None])
    tables = [np.where(in_window & (ki >= WINDOW), bias, -np.inf), np.where(in_window, bias, -np.inf)]
    return np.stack(tables).astype(np.float32)


def _pair_heads(a, axis):
    shape = a.shape
    a = a.reshape(shape[:axis] + (ATTN_KV_HEADS, ATTN_GROUP, ATTN_HEAD_DIM) + shape[axis + 1:])
    return jnp.swapaxes(a, axis, axis + 1).reshape(shape)


def _attn_kernel(sink_ref, q_ref, kp_ref, kc_ref, vp_ref, vc_ref, bias_ref, g_ref, o_ref):
    n = pl.program_id(1)
    kc = kc_ref[...].astype(BF16)
    vc = vc_ref[...].astype(BF16)
    keys = [jnp.concatenate([kp_ref[...].astype(BF16), kc[:WINDOW]], axis=0)]
    vals = [jnp.concatenate([vp_ref[...].astype(BF16), vc[:WINDOW]], axis=0)]
    for i in range(1, ATTN_STEP_BLOCKS):
        keys.append(kc[(i - 1) * WINDOW:(i + 1) * WINDOW])
        vals.append(vc[(i - 1) * WINDOW:(i + 1) * WINDOW])
    tables = [jnp.minimum(n, 1)] + [1] * (ATTN_STEP_BLOCKS - 1)
    scale = ATTN_HEAD_DIM ** -0.5
    nt = (((1,), (1,)), ((), ()))
    low = lax.broadcasted_iota(jnp.int32, (WINDOW, LANES), 1) < ATTN_HEAD_DIM

    def softmax(s, table, h):
        s = s * scale + bias_ref[table, h]
        sink = sink_ref[h]
        m = jnp.maximum(jnp.max(s, axis=-1, keepdims=True), sink)
        p = jnp.exp(s - m)
        den = jnp.sum(p, axis=-1, keepdims=True) + jnp.exp(sink - m)
        return (p / den).astype(BF16)

    items = [(i, j) for i in range(ATTN_STEP_BLOCKS) for j in range(ATTN_GROUP)]
    scores = {}
    for i, j in items:
        q = q_ref[i * WINDOW:(i + 1) * WINDOW, j * LANES:(j + 1) * LANES]
        scores[i, j] = (
            lax.dot_general(jnp.where(low, q, 0.0).astype(BF16), keys[i], nt, preferred_element_type=F32),
            lax.dot_general(jnp.where(low, 0.0, q).astype(BF16), keys[i], nt, preferred_element_type=F32))
    probs = {(i, j): (softmax(scores[i, j][0], tables[i], j), softmax(scores[i, j][1], tables[i], ATTN_GROUP + j))
             for i, j in items}
    for i in range(ATTN_STEP_BLOCKS):
        outs = [jnp.where(low, jnp.dot(probs[i, j][0], vals[i], preferred_element_type=F32),
                          jnp.dot(probs[i, j][1], vals[i], preferred_element_type=F32))
                for j in range(ATTN_GROUP)]
        o_ref[i * WINDOW:(i + 1) * WINDOW, :] = _rms(jnp.concatenate(outs, axis=1), g_ref[...])


def _attention(aq, ak, av, sinks, g, batch, seq):
    step = ATTN_STEP_BLOCKS * WINDOW
    aq = aq.reshape(batch, seq, ATTN_WIDTH)
    ak = ak.reshape(batch, seq, ATTN_KV_WIDTH)
    av = av.reshape(batch, seq, ATTN_KV_WIDTH)
    cur = lambda b, n, s: (b, n, 0)
    prev = lambda b, n, s: (b, jnp.maximum(n * ATTN_STEP_BLOCKS - 1, 0), 0)
    out = pl.pallas_call(
        _attn_kernel,
        grid_spec=pltpu.PrefetchScalarGridSpec(
            num_scalar_prefetch=1,
            grid=(batch, seq // step),
            in_specs=[pl.BlockSpec((None, step, ATTN_WIDTH), cur),
                      pl.BlockSpec((None, WINDOW, ATTN_KV_WIDTH), prev),
                      pl.BlockSpec((None, step, ATTN_KV_WIDTH), cur),
                      pl.BlockSpec((None, WINDOW, ATTN_KV_WIDTH), prev),
                      pl.BlockSpec((None, step, ATTN_KV_WIDTH), cur),
                      pl.BlockSpec((2, ATTN_Q_HEADS, WINDOW, 2 * WINDOW), lambda b, n, s: (0, 0, 0, 0)),
                      pl.BlockSpec((1, ATTN_WIDTH), lambda b, n, s: (0, 0))],
            out_specs=pl.BlockSpec((None, step, ATTN_WIDTH), cur)),
        out_shape=jax.ShapeDtypeStruct((batch, seq, ATTN_WIDTH), F32),
        compiler_params=_params("parallel", "parallel"),
        name="attn",
    )(sinks, aq, ak, ak, av, av, jnp.asarray(_attn_bias_tables()), g)
    return out.reshape(batch * seq, ATTN_WIDTH)


HGRN_EXP_GUARD = 80.0


def _hgrn_kernel(guard, q_ref, f_ref, i_ref, gate_ref, lb_ref, og_ref, o_ref, st_ref, q_scr, k_scr, b_scr):
    step = pl.program_id(1)

    @pl.when(step == 0)
    def _():
        st_ref[...] = jnp.zeros_like(st_ref)

    C, S = HGRN_CHUNK, HGRN_SUB
    nsub = C // S
    nchunk = HGRN_STEP // C
    r_i = lax.broadcasted_iota(jnp.int32, (C, C), 0)
    c_i = lax.broadcasted_iota(jnp.int32, (C, C), 1)
    nt = (((1,), (1,)), ((), ()))

    tri = (r_i >= c_i).astype(F32)
    qx = q_ref[...]
    q_scr[...] = qx * jax.nn.sigmoid(qx)
    lb = lb_ref[...]
    f = lb + (1.0 - lb) * jax.nn.sigmoid(f_ref[...])
    k_scr[...] = 1.0 - f
    logf = jnp.log(f)
    for c in range(nchunk):
        b_scr[c * C:(c + 1) * C, :] = jnp.dot(tri, logf[c * C:(c + 1) * C, :], preferred_element_type=F32,
                                              precision=lax.Precision.HIGHEST)
    decay = jnp.concatenate([-jnp.sum(logf[i * S:(i + 1) * S, :], axis=0, keepdims=True)
                             for i in range(HGRN_STEP // S)], axis=0)
    risky = jnp.max(decay) > guard

    rowblk = lax.broadcasted_iota(jnp.int32, (C, HGRN_DIM), 0) // S
    pairs = [(c, h) for c in range(nchunk) for h in range(HGRN_HEADS)]

    def view(ref, c, h):
        return ref[c * C:(c + 1) * C, h * HGRN_DIM:(h + 1) * HGRN_DIM]

    def finish(intra):
        q_dec, gain, update = {}, {}, {}
        for c, h in pairs:
            q, k, b = view(q_scr, c, h), view(k_scr, c, h), view(b_scr, c, h)
            q_dec[c, h] = (q * jnp.exp(b)).astype(BF16)
            bl = b[C - 1:C, :]
            gain[c, h] = jnp.exp(bl)
            update[c, h] = lax.dot_general(view(i_ref, c, h).astype(BF16), (k * jnp.exp(bl - b)).astype(BF16),
                                           (((0,), (0,)), ((), ())), preferred_element_type=F32)
        state = {}
        for h in range(HGRN_HEADS):
            st = st_ref[h]
            for c in range(nchunk):
                state[c, h] = st
                st = st * gain[c, h] + update[c, h]
            st_ref[h] = st
        for c, h in pairs:
            o = intra[c, h] + lax.dot_general(q_dec[c, h], state[c, h].astype(BF16), nt,
                                              preferred_element_type=F32)
            gx = view(gate_ref, c, h)
            og = og_ref[:, h * HGRN_DIM:(h + 1) * HGRN_DIM]
            o = o * lax.rsqrt(jnp.mean(o * o, axis=-1, keepdims=True) + NORM_EPS) * og
            o_ref[c * C:(c + 1) * C, h * HGRN_DIM:(h + 1) * HGRN_DIM] = o * (gx * jax.nn.sigmoid(gx))

    def sub_block_queries(q, b):
        return [jnp.where(rowblk == i, q * jnp.exp(jnp.minimum(b - b[i * S:i * S + 1, :], 0.0)), 0.0)
                for i in range(nsub)]

    @pl.when(jnp.logical_not(risky))
    def _():
        att = {}
        for c, h in pairs:
            q, k, b = view(q_scr, c, h), view(k_scr, c, h), view(b_scr, c, h)
            k_sub = [jnp.where(rowblk <= i, k * jnp.exp(jnp.minimum(b[i * S:i * S + 1, :] - b, guard)), 0.0)
                     for i in range(nsub)]
            att[c, h] = lax.dot_general(jnp.concatenate(sub_block_queries(q, b), axis=1).astype(BF16),
                                        jnp.concatenate(k_sub, axis=1).astype(BF16), nt,
                                        preferred_element_type=F32)
        finish({(c, h): jnp.dot(jnp.where(r_i >= c_i, att[c, h], 0.0).astype(BF16),
                                view(i_ref, c, h).astype(BF16), preferred_element_type=F32) for c, h in pairs})

    @pl.when(risky)
    def _():
        sub_r = lax.broadcasted_iota(jnp.int32, (S, HGRN_DIM), 0)
        intra = {}
        for c, h in pairs:
            q, k, b, v = view(q_scr, c, h), view(k_scr, c, h), view(b_scr, c, h), view(i_ref, c, h)
            k_sub = [jnp.where(rowblk < i, k * jnp.exp(jnp.minimum(b[i * S:i * S + 1, :] - b, 0.0)), 0.0)
                     for i in range(1, nsub)]
            att = lax.dot_general(jnp.concatenate(sub_block_queries(q, b)[1:], axis=1).astype(BF16),
                                  jnp.concatenate(k_sub, axis=1).astype(BF16), nt, preferred_element_type=F32)
            diag = []
            for i in range(nsub):
                b_blk = b[i * S:(i + 1) * S, :]
                q_blk = q[i * S:(i + 1) * S, :]
                acc = jnp.zeros((S, HGRN_DIM), F32)
                for s_ in range(S):
                    r = i * S + s_
                    e = jnp.exp(jnp.minimum(b_blk - b[r:r + 1, :], 0.0))
                    a = jnp.where(sub_r >= s_, q_blk * e * k[r:r + 1, :], 0.0)
                    acc = acc + jnp.sum(a, axis=-1, keepdims=True) * v[r:r + 1, :]
                diag.append(acc)
            intra[c, h] = (jnp.dot(att.astype(BF16), v.astype(BF16), preferred_element_type=F32)
                           + jnp.concatenate(diag, axis=0))
        finish(intra)


def _hgrn(hq, hf, hi, hg, lb, og, batch, seq, guard=HGRN_EXP_GUARD):
    shp = (batch, seq, HGRN_WIDTH)
    blk = pl.BlockSpec((None, HGRN_STEP, HGRN_WIDTH), lambda b, c: (b, c, 0))
    vec = pl.BlockSpec((1, HGRN_WIDTH), lambda b, c: (0, 0))
    out = pl.pallas_call(
        functools.partial(_hgrn_kernel, guard),
        grid=(batch, seq // HGRN_STEP),
        in_specs=[blk, blk, blk, blk, vec, vec],
        out_specs=blk,
        out_shape=jax.ShapeDtypeStruct(shp, F32),
        scratch_shapes=[pltpu.VMEM((HGRN_HEADS, HGRN_DIM, HGRN_DIM), F32)]
        + [pltpu.VMEM((HGRN_STEP, HGRN_WIDTH), F32)] * 3,
        compiler_params=_params("parallel", "arbitrary"),
        name="hgrn",
    )(hq.reshape(shp), hf.reshape(shp), hi.reshape(shp), hg.reshape(shp), lb, og)
    return out.reshape(batch * seq, HGRN_WIDTH)


def _outproj_kernel(attn_ref, hg_ref, x_ref, wo_ref, bo_ref, g2_ref, rw_ref, rb_ref,
                    x1_ref, h2_ref, lpos_ref, gate_ref, seg_ref):
    tm = SORT_TILE
    tiles = range(OUTPROJ_TILES)
    y = jnp.dot(attn_ref[...].astype(BF16), wo_ref[:ATTN_WIDTH, :], preferred_element_type=F32)
    y = y + jnp.dot(hg_ref[...].astype(BF16), wo_ref[ATTN_WIDTH:, :], preferred_element_type=F32)
    x1 = x_ref[...] + y + bo_ref[...]
    x1_ref[...] = x1
    h2 = _rms(x1, g2_ref[...])
    h2_ref[...] = h2.astype(BF16)
    nt = (((1,), (1,)), ((), ()))
    h_hi = h2.astype(BF16)
    h_lo = (h2 - h_hi.astype(F32)).astype(BF16)
    work = []
    for u in tiles:
        p_hi = lax.dot_general(rw_ref[...], h_hi[u * tm:(u + 1) * tm], nt, preferred_element_type=F32)
        p_lo = lax.dot_general(rw_ref[...], h_lo[u * tm:(u + 1) * tm], nt, preferred_element_type=F32)
        work.append((p_hi[:N_EXPERTS] + p_hi[LANES:LANES + N_EXPERTS]) + p_lo[:N_EXPERTS] + rb_ref[...])
    eid = lax.broadcasted_iota(jnp.int32, (N_EXPERTS, tm), 0).astype(F32)
    vals = [[] for _ in tiles]
    hots = [[] for _ in tiles]
    for _ in range(TOP_K):
        for u in tiles:
            m = jnp.max(work[u], axis=0, keepdims=True)
            idx = jnp.min(jnp.where(work[u] == m, eid, float(N_EXPERTS)), axis=0, keepdims=True)
            hot = eid == idx
            vals[u].append(m)
            hots[u].append(hot)
            work[u] = jnp.where(hot, -jnp.inf, work[u])
    t_r = lax.broadcasted_iota(jnp.int32, (tm, tm), 0)
    t_c = lax.broadcasted_iota(jnp.int32, (tm, tm), 1)
    earlier = jnp.where(t_r < t_c, 1.0, 0.0).astype(BF16)
    e_r = lax.broadcasted_iota(jnp.int32, (N_EXPERTS, N_EXPERTS), 0)
    e_c = lax.broadcasted_iota(jnp.int32, (N_EXPERTS, N_EXPERTS), 1)
    before = jnp.where(e_c < e_r, 1.0, 0.0).astype(BF16)

    def round_up(n):
        return jnp.floor((n + (SEG_ROWS - 1.0)) * (1.0 / SEG_ROWS)) * SEG_ROWS

    sel = [sum(jnp.where(hot, 1.0, 0.0) for hot in hots[u]) for u in tiles]
    ranks = [jnp.dot(sel[u].astype(BF16), earlier, preferred_element_type=F32) for u in tiles]
    seg_col = [round_up(jnp.sum(sel[u], axis=1, keepdims=True)) for u in tiles]
    seg_off = [jnp.dot(before, jnp.broadcast_to(seg_col[u], (N_EXPERTS, LANES)).astype(BF16),
                       preferred_element_type=F32)[:, 0:1] for u in tiles]
    seg_row = [round_up(lax.dot_general(jnp.ones((SEG_ROWS, tm), BF16), sel[u].astype(BF16), nt,
                                        preferred_element_type=F32)[0:1, :]) for u in tiles]
    for u in tiles:
        slots = ranks[u] + seg_off[u]
        ex = [jnp.exp(v - vals[u][0]) for v in vals[u]]
        den = ex[0] + ex[1] + ex[2] + ex[3]
        lpos_ref[:, u * tm:(u + 1) * tm] = jnp.concatenate(
            [jnp.sum(jnp.where(hot, slots, 0.0), axis=0, keepdims=True) for hot in hots[u]],
            axis=0).astype(jnp.int32)
        gate_ref[:, u * tm:(u + 1) * tm] = jnp.concatenate([e / den for e in ex], axis=0)
        seg_ref[u] = seg_row[u]


def _outproj(attn, hgo, x2, wo_bf, bo, g2, rw, rb):
    t = x2.shape[0]
    step = OUTPROJ_TILES * SORT_TILE
    row = lambda i: (i, 0)
    col = lambda i: (0, i)
    fixed = lambda i: (0, 0)
    return pl.pallas_call(
        _outproj_kernel,
        grid=(t // step,),
        in_specs=[pl.BlockSpec((step, ATTN_WIDTH), row),
                  pl.BlockSpec((step, HGRN_WIDTH), row),
                  pl.BlockSpec((step, D_MODEL), row),
                  pl.BlockSpec((ATTN_WIDTH + HGRN_WIDTH, D_MODEL), fixed),
                  pl.BlockSpec((1, D_MODEL), fixed),
                  pl.BlockSpec((1, D_MODEL), fixed),
                  pl.BlockSpec((2 * LANES, D_MODEL), fixed),
                  pl.BlockSpec((N_EXPERTS, 1), fixed)],
        out_specs=[pl.BlockSpec((step, D_MODEL), row),
                   pl.BlockSpec((step, D_MODEL), row),
                   pl.BlockSpec((TOP_K, step), col),
                   pl.BlockSpec((TOP_K, step), col),
                   pl.BlockSpec((OUTPROJ_TILES, 1, N_EXPERTS), lambda i: (i, 0, 0))],
        out_shape=[jax.ShapeDtypeStruct((t, D_MODEL), F32),
                   jax.ShapeDtypeStruct((t, D_MODEL), BF16),
                   jax.ShapeDtypeStruct((TOP_K, t), jnp.int32),
                   jax.ShapeDtypeStruct((TOP_K, t), F32),
                   jax.ShapeDtypeStruct((t // SORT_TILE, 1, N_EXPERTS), F32)],
        compiler_params=_params("parallel"),
        name="outproj",
    )(attn, hgo, x2, wo_bf, bo, g2, rw, rb)


def _rows(start, n):
    return pl.ds(pl.multiple_of(start, SEG_ROWS), n)


def _repeat(cnt, fn):
    def body(j, carry):
        fn(j)
        return carry

    lax.fori_loop(0, cnt, body, 0)


BIG_CAP = SORT_SLOTS // BIG_PIECE
SMALL_CAP = N_EXPERTS * (BIG_PIECE // SEG_ROWS - 1)


def _tile_pieces(lists, step, fn):
    big_loc, big_glob, big_cnt, small_loc, small_glob, small_cnt = lists
    b0 = step * BIG_CAP
    _repeat(big_cnt[step], lambda j: fn(big_loc[b0 + j], big_glob[b0 + j], BIG_PIECE))
    s0 = step * SMALL_CAP
    _repeat(small_cnt[step], lambda j: fn(small_loc[s0 + j], small_glob[s0 + j], SEG_ROWS))


def _tile_piece_waits(lists, step, wait):
    _repeat(lists[2][step], lambda j: wait(BIG_PIECE))
    _repeat(lists[5][step], lambda j: wait(SEG_ROWS))


def _dispatch_kernel(*refs):
    lists, (zdst_ref, zbig_ref, zsmall_ref) = refs[:6], refs[6:9]
    h_ref, lpos_t_ref, xs_ref, lbuf_ref, zbuf_ref, sem, zsem = refs[9:]
    i = pl.program_id(0)
    last = pl.num_programs(0) - 1
    buf = i % 2

    def piece(b, local, glob, n):
        return pltpu.make_async_copy(lbuf_ref.at[b, _rows(local, n), :], xs_ref.at[_rows(glob, n), :], sem.at[b])

    def drain(step):
        _tile_piece_waits(lists, step, lambda n: piece(step % 2, 0, 0, n).wait())

    def zero_piece(glob, n):
        return pltpu.make_async_copy(zbuf_ref.at[_rows(0, n), :], xs_ref.at[_rows(glob, n), :], zsem)

    @pl.when(i == 0)
    def _():
        zbuf_ref[...] = jnp.zeros_like(zbuf_ref)
        for e in range(N_EXPERTS + 1):
            _repeat(zbig_ref[e], lambda j, e=e: zero_piece(zdst_ref[e] + j * BIG_PIECE, BIG_PIECE).start())
            _repeat(zsmall_ref[e], lambda j, e=e: zero_piece(
                zdst_ref[e] + zbig_ref[e] * BIG_PIECE + j * SEG_ROWS, SEG_ROWS).start())
        for e in range(N_EXPERTS + 1):
            _repeat(zbig_ref[e], lambda j: zero_piece(0, BIG_PIECE).wait())
            _repeat(zsmall_ref[e], lambda j: zero_piece(0, SEG_ROWS).wait())

    @pl.when(i >= 2)
    def _():
        drain(i - 2)

    slot = lax.broadcasted_iota(jnp.int32, (SORT_SLOTS, SORT_TILE), 0)
    onehot = jnp.zeros((SORT_SLOTS, SORT_TILE), F32)
    for k in range(TOP_K):
        onehot = onehot + jnp.where(slot == lpos_t_ref[k:k + 1, :], 1.0, 0.0)
    lbuf_ref[buf] = jnp.dot(onehot.astype(BF16), h_ref[...], preferred_element_type=F32)
    _tile_pieces(lists, i, lambda local, glob, n: piece(buf, local, glob, n).start())

    @pl.when(i == last)
    def _():
        @pl.when(i >= 1)
        def _():
            drain(i - 1)

        drain(i)


def _dispatch(tables, zero_tables, h2, lpos_t, n_rows):
    t = h2.shape[0]
    tile = lambda i, *_: (i, 0)
    return pl.pallas_call(
        _dispatch_kernel,
        grid_spec=pltpu.PrefetchScalarGridSpec(
            num_scalar_prefetch=9,
            grid=(t // SORT_TILE,),
            in_specs=[pl.BlockSpec((SORT_TILE, D_MODEL), tile),
                      pl.BlockSpec((TOP_K, SORT_TILE), lambda i, *_: (0, i))],
            out_specs=pl.BlockSpec(memory_space=pl.ANY),
            scratch_shapes=[pltpu.VMEM((2, SORT_SLOTS, D_MODEL), F32),
                            pltpu.VMEM((BIG_PIECE, D_MODEL), F32),
                            pltpu.SemaphoreType.DMA((2,)),
                            pltpu.SemaphoreType.DMA(())]),
        out_shape=jax.ShapeDtypeStruct((n_rows, D_MODEL), F32),
        compiler_params=_params("arbitrary"),
        name="dispatch",
    )(*tables, *zero_tables, h2, lpos_t)


CAST_ROWS = 128


def _expert_kernel(be_ref, nb_ref, eb_ref, filled_ref,
                   xs_ref, w1_hbm, b1_ref, w2_hbm, b2_ref, y_ref,
                   w1f_ref, w2f_ref, w1b_ref, w2i_ref, w2b_ref, started_ref, sem):
    blk = pl.program_id(0)
    half = LANES // 2

    def weight_copies(e, s):
        return (pltpu.make_async_copy(w1_hbm.at[e], w1f_ref.at[s], sem.at[0, s]),
                pltpu.make_async_copy(w2_hbm.at[e], w2f_ref.at[s], sem.at[1, s]))

    e = be_ref[blk]
    first_of_expert = (blk < nb_ref[0]) & ((blk == 0) | (e != be_ref[jnp.maximum(blk - 1, 0)]))

    @pl.when(first_of_expert)
    def _():
        done = jnp.where(blk == 0, 0, started_ref[0])
        started_ref[0] = done + 1
        s = done % 2
        next_blk = blk + eb_ref[e]

        @pl.when(blk == 0)
        def _():
            for c in weight_copies(e, s):
                c.start()

        @pl.when(next_blk < nb_ref[0])
        def _():
            for c in weight_copies(be_ref[next_blk], 1 - s):
                c.start()

        for c in weight_copies(e, s):
            c.wait()

        def cast_rows(r, carry):
            rows = pl.ds(pl.multiple_of(r * CAST_ROWS, CAST_ROWS), CAST_ROWS)
            w1b_ref[rows, :] = w1f_ref[s, rows, :].astype(BF16)
            return carry

        lax.fori_loop(0, D_MODEL // CAST_ROWS, cast_rows, 0)
        for c in range(D_MODEL // LANES):
            cols = slice(c * LANES, (c + 1) * LANES)
            for m in range(EXPERT_FF // LANES):
                lo = m * LANES
                w2i_ref[c, pl.ds(lo, half, stride=2), :] = w2f_ref[s, lo:lo + half, cols]
                w2i_ref[c, pl.ds(lo + 1, half, stride=2), :] = w2f_ref[s, lo + half:lo + LANES, cols]
            w2b_ref[:, cols] = w2i_ref[c].astype(BF16)

    def ffn(rows):
        x = xs_ref[:rows, :].astype(BF16)
        hid = jnp.dot(x, w1b_ref[...], preferred_element_type=F32) + b1_ref[...]
        even = (lax.broadcasted_iota(jnp.int32, (rows, LANES), 1) & 1) == 0
        glu, lin = [], []
        for m in range(EXPERT_FF // LANES):
            ha = hid[:, 2 * m * LANES:(2 * m + 1) * LANES]
            hb = hid[:, (2 * m + 1) * LANES:(2 * m + 2) * LANES]
            glu.append(jnp.where(even, ha, pltpu.roll(hb, 1, axis=1)))
            lin.append(jnp.where(even, pltpu.roll(ha, LANES - 1, axis=1), hb))
        glu = jnp.minimum(jnp.concatenate(glu, axis=1), SWIGLU_LIMIT)
        lin = jnp.clip(jnp.concatenate(lin, axis=1), -SWIGLU_LIMIT, SWIGLU_LIMIT)
        act = glu * jax.nn.sigmoid(SWIGLU_ALPHA * glu) * (lin + 1.0)
        y_ref[:rows, :] = jnp.dot(act.astype(BF16), w2b_ref[...], preferred_element_type=F32) + b2_ref[...]

    half_block = EXPERT_BLOCK // 2

    @pl.when(filled_ref[blk] > half_block)
    def _():
        ffn(EXPERT_BLOCK)

    @pl.when((filled_ref[blk] > 0) & (filled_ref[blk] <= half_block))
    def _():
        ffn(half_block)
        y_ref[half_block:, :] = jnp.zeros((EXPERT_BLOCK - half_block, D_MODEL), F32)

    @pl.when(filled_ref[blk] == 0)
    def _():
        y_ref[...] = jnp.zeros_like(y_ref)


def _experts(block_tables, xs, w1, b1, w2, b2):
    n_rows = xs.shape[0]
    nblk = n_rows // EXPERT_BLOCK
    rows = lambda b, *_: (b, 0)
    used_rows = lambda b, be, nb, *_: (jnp.minimum(b, nb[0] - 1), 0)
    bias = lambda b, be, *_: (be[b], 0, 0)
    return pl.pallas_call(
        _expert_kernel,
        grid_spec=pltpu.PrefetchScalarGridSpec(
            num_scalar_prefetch=4,
            grid=(nblk,),
            in_specs=[pl.BlockSpec((EXPERT_BLOCK, D_MODEL), used_rows),
                      pl.BlockSpec(memory_space=pl.ANY),
                      pl.BlockSpec((None, 1, 2 * EXPERT_FF), bias),
                      pl.BlockSpec(memory_space=pl.ANY),
                      pl.BlockSpec((None, 1, D_MODEL), bias)],
            out_specs=pl.BlockSpec((EXPERT_BLOCK, D_MODEL), rows),
            scratch_shapes=[pltpu.VMEM((2, D_MODEL, 2 * EXPERT_FF), F32),
                            pltpu.VMEM((2, EXPERT_FF, D_MODEL), F32),
                            pltpu.VMEM((D_MODEL, 2 * EXPERT_FF), BF16),
                            pltpu.VMEM((D_MODEL // LANES, EXPERT_FF, LANES), F32),
                            pltpu.VMEM((EXPERT_FF, D_MODEL), BF16),
                            pltpu.SMEM((1,), jnp.int32),
                            pltpu.SemaphoreType.DMA((2, 2))]),
        out_shape=jax.ShapeDtypeStruct((n_rows, D_MODEL), F32),
        compiler_params=_params("arbitrary"),
        name="experts",
    )(*block_tables, xs, w1, b1, w2, b2)


def _combine_kernel(final_norm, *refs):
    lists = refs[:6]
    yb_ref, x1_ref, lpos_ref, gate_ref, g_ref, o_ref, gbuf_ref, sem = refs[6:]
    i = pl.program_id(0)
    last = pl.num_programs(0) - 1
    buf = i % 2

    def piece(b, local, glob, n):
        return pltpu.make_async_copy(yb_ref.at[_rows(glob, n), :], gbuf_ref.at[b, _rows(local, n), :], sem.at[b])

    def fetch(step):
        _tile_pieces(lists, step, lambda local, glob, n: piece(step % 2, local, glob, n).start())

    @pl.when(i == 0)
    def _():
        gbuf_ref[...] = jnp.zeros_like(gbuf_ref)
        fetch(0)

    @pl.when(i < last)
    def _():
        fetch(i + 1)

    _tile_piece_waits(lists, i, lambda n: piece(buf, 0, 0, n).wait())
    slot = lax.broadcasted_iota(jnp.int32, (SORT_TILE, SORT_SLOTS), 1)
    lpos = lpos_ref[...]
    gates = gate_ref[...]
    weights = jnp.zeros((SORT_TILE, SORT_SLOTS), F32)
    for k in range(TOP_K):
        weights = weights + jnp.where(slot == lpos[:, k:k + 1], gates[:, k:k + 1], 0.0)
    y = x1_ref[...] + jnp.dot(weights.astype(BF16), gbuf_ref[buf].astype(BF16), preferred_element_type=F32)
    o_ref[...] = _rms(y, g_ref[...]) if final_norm else y


def _combine(tables, yb, x1, lpos, gates, g, final_norm):
    t = x1.shape[0]
    tile = lambda i, *_: (i, 0)
    return pl.pallas_call(
        functools.partial(_combine_kernel, final_norm),
        grid_spec=pltpu.PrefetchScalarGridSpec(
            num_scalar_prefetch=6,
            grid=(t // SORT_TILE,),
            in_specs=[pl.BlockSpec(memory_space=pl.ANY),
                      pl.BlockSpec((SORT_TILE, D_MODEL), tile),
                      pl.BlockSpec((SORT_TILE, TOP_K), tile),
                      pl.BlockSpec((SORT_TILE, TOP_K), tile),
                      pl.BlockSpec((1, D_MODEL), lambda i, *_: (0, 0))],
            out_specs=pl.BlockSpec((SORT_TILE, D_MODEL), tile),
            scratch_shapes=[pltpu.VMEM((2, SORT_SLOTS, D_MODEL), F32),
                            pltpu.SemaphoreType.DMA((2,))]),
        out_shape=jax.ShapeDtypeStruct((t, D_MODEL), F32),
        compiler_params=_params("arbitrary"),
        name="combine",
    )(*tables, yb, x1, lpos, gates, g)


def _piece_lists(seg_loc, seg_glob, cnt, skip, rows, cap):
    first = jnp.cumsum(cnt, axis=1) - cnt
    p = jnp.arange(cap, dtype=jnp.int32)
    started = first[:, None, :] <= p[None, :, None]

    def per_piece(rows0):
        base = rows0 + skip - first * rows
        step = base - jnp.concatenate([jnp.zeros_like(base[:, :1]), base[:, :-1]], axis=1)
        return (jnp.sum(jnp.where(started, step[:, None, :], 0), axis=2) + p[None, :] * rows).reshape(-1)

    return per_piece(seg_loc), per_piece(seg_glob), jnp.sum(cnt, axis=1)


def kernel(x, norm1_g, w_in, b_in, attn_sinks, attn_out_g, hgrn_lb_logits, hgrn_out_g, w_out, b_out,
           norm2_g, router_w, router_b, w1, b1, w2, b2, final_g):
    batch, seq, d = x.shape
    t = batch * seq
    depth = w_in.shape[0]
    lower_bounds = jnp.cumsum(jax.nn.softmax(hgrn_lb_logits.astype(F32), axis=0), axis=0)
    assert t % SORT_TILE == 0 and SORT_SLOTS >= SORT_TILE * TOP_K + N_EXPERTS * (SEG_ROWS - 1)
    ntiles = t // SORT_TILE
    nblk = (t * TOP_K + ntiles * N_EXPERTS * (SEG_ROWS - 1)) // EXPERT_BLOCK + N_EXPERTS
    n_rows = nblk * EXPERT_BLOCK
    x2 = x.reshape(t, d)
    for l in range(depth):
        w_in_l = jnp.concatenate([_pair_heads(w_in[l][:, :ATTN_WIDTH], 1), w_in[l][:, ATTN_WIDTH:]], axis=1)
        b_in_l = jnp.concatenate([_pair_heads(b_in[l][:ATTN_WIDTH], 0), b_in[l][ATTN_WIDTH:]])
        aq, ak, av, hq, hf, hi, hg = _inproj(x2, norm1_g[l][None], w_in_l.astype(BF16), b_in_l[None])
        attn = _attention(aq, ak, av, attn_sinks[l], _pair_heads(attn_out_g[l], 0)[None], batch, seq)
        hgo = _hgrn(hq, hf, hi, hg, lower_bounds[l][None], hgrn_out_g[l][None], batch, seq)
        w_out_l = jnp.concatenate([_pair_heads(w_out[l][:ATTN_WIDTH], 0), w_out[l][ATTN_WIDTH:]], axis=0)
        rw_hi = router_w[l].astype(BF16)
        rw_lo = (router_w[l] - rw_hi.astype(F32)).astype(BF16)
        rw_cat = jnp.zeros((2 * LANES, d), BF16)
        rw_cat = rw_cat.at[:N_EXPERTS].set(rw_hi.T).at[LANES:LANES + N_EXPERTS].set(rw_lo.T)
        x1, h2, lpos_t, gates_t, seg = _outproj(
            attn, hgo, x2, w_out_l.astype(BF16), b_out[l][None], norm2_g[l][None],
            rw_cat, router_b[l][:, None])
        seg = seg.reshape(ntiles, N_EXPERTS).astype(jnp.int32)
        rows_e = jnp.sum(seg, axis=0)
        padded = (rows_e + EXPERT_BLOCK - 1) // EXPERT_BLOCK * EXPERT_BLOCK
        ends = jnp.cumsum(padded)
        pstart = ends - padded
        seg_glob = pstart[None, :] + jnp.cumsum(seg, axis=0) - seg
        seg_loc = jnp.cumsum(seg, axis=1) - seg
        tail = jnp.append(padded - rows_e, n_rows - ends[-1])
        zero_tables = (jnp.append(pstart + rows_e, ends[-1]), tail // BIG_PIECE, tail % BIG_PIECE // SEG_ROWS)
        big = seg // BIG_PIECE
        tables = (_piece_lists(seg_loc, seg_glob, big, jnp.zeros_like(seg), BIG_PIECE, BIG_CAP)
                  + _piece_lists(seg_loc, seg_glob, seg % BIG_PIECE // SEG_ROWS, big * BIG_PIECE, SEG_ROWS,
                                 SMALL_CAP))
        blk_ids = jnp.arange(nblk, dtype=jnp.int32)
        block_e = jnp.minimum(jnp.sum(blk_ids[:, None] * EXPERT_BLOCK >= ends[None, :], axis=-1), N_EXPERTS - 1)
        in_expert = rows_e[None, :] - (blk_ids[:, None] * EXPERT_BLOCK - pstart[None, :])
        filled = jnp.sum(jnp.where((block_e[:, None] == jnp.arange(N_EXPERTS)[None, :])
                                   & (blk_ids[:, None] * EXPERT_BLOCK < ends[-1]),
                                   jnp.clip(in_expert, 0, EXPERT_BLOCK), 0), axis=1)
        block_tables = (block_e.astype(jnp.int32), (ends[-1:] // EXPERT_BLOCK).astype(jnp.int32),
                        (padded // EXPERT_BLOCK).astype(jnp.int32), filled.astype(jnp.int32))
        xs = _dispatch(tables, zero_tables, h2, lpos_t, n_rows)
        yb = _experts(block_tables, xs, w1[l], b1[l][:, None, :], w2[l], b2[l][:, None, :])
        x2 = _combine(tables, yb, x1, lpos_t.T, gates_t.T, final_g[None], l == depth - 1)
    return x2.reshape(batch, seq, d)
```

```python
import functools

import numpy as np
import jax
import jax.numpy as jnp
from jax import lax
from jax.experimental import pallas as pl
from jax.experimental.pallas import tpu as pltpu

F32 = jnp.float32
BF16 = jnp.bfloat16

D_MODEL = 1024
ATTN_Q_HEADS = 8
ATTN_KV_HEADS = 2
ATTN_HEAD_DIM = 64
ATTN_GROUP = ATTN_Q_HEADS // ATTN_KV_HEADS
ATTN_WIDTH = ATTN_Q_HEADS * ATTN_HEAD_DIM
ATTN_KV_WIDTH = ATTN_KV_HEADS * ATTN_HEAD_DIM
WINDOW = 128
ATTN_STEP_BLOCKS = 4
HGRN_HEADS = 4
HGRN_DIM = 128
HGRN_WIDTH = HGRN_HEADS * HGRN_DIM
HGRN_CHUNK = 64
HGRN_SUB = 16
HGRN_STEP = 512
IN_WIDTH = ATTN_WIDTH + 2 * ATTN_KV_WIDTH + 4 * HGRN_WIDTH
N_EXPERTS = 32
TOP_K = 4
EXPERT_FF = D_MODEL
SWIGLU_LIMIT = 7.0
SWIGLU_ALPHA = 1.702
NORM_EPS = 1e-5

LANES = 128
ROW_TILE = 1024
EXPERT_BLOCK = 512
SORT_TILE = 256
OUTPROJ_TILES = 4
SEG_ROWS = 8
BIG_PIECE = 32
SORT_SLOTS = 1280
VMEM_LIMIT = 56 * 1024 * 1024

_ALIBI = [float(2.0 ** (-8.0 * (h + 1) / ATTN_Q_HEADS)) for h in range(ATTN_Q_HEADS)]


def _rms(x, g):
    return x * lax.rsqrt(jnp.mean(x * x, axis=-1, keepdims=True) + NORM_EPS) * g


def _params(*sem):
    return pltpu.CompilerParams(dimension_semantics=sem, vmem_limit_bytes=VMEM_LIMIT)


_IN_SPLITS = (ATTN_WIDTH, ATTN_KV_WIDTH, ATTN_KV_WIDTH, HGRN_WIDTH, HGRN_WIDTH, HGRN_WIDTH, HGRN_WIDTH)


def _inproj_kernel(x_ref, g_ref, w_ref, b_ref, *out_refs):
    h = _rms(x_ref[...], g_ref[...]).astype(BF16)
    lo = 0
    for ref, width in zip(out_refs, _IN_SPLITS):
        ref[...] = jnp.dot(h, w_ref[:, lo:lo + width], preferred_element_type=F32) + b_ref[:, lo:lo + width]
        lo += width


def _inproj(x2, g, w_bf, b):
    t = x2.shape[0]
    row = lambda i: (i, 0)
    fixed = lambda i: (0, 0)
    return pl.pallas_call(
        _inproj_kernel,
        grid=(t // ROW_TILE,),
        in_specs=[pl.BlockSpec((ROW_TILE, D_MODEL), row),
                  pl.BlockSpec((1, D_MODEL), fixed),
                  pl.BlockSpec((D_MODEL, IN_WIDTH), fixed),
                  pl.BlockSpec((1, IN_WIDTH), fixed)],
        out_specs=[pl.BlockSpec((ROW_TILE, w), row) for w in _IN_SPLITS],
        out_shape=[jax.ShapeDtypeStruct((t, w), F32) for w in _IN_SPLITS],
        compiler_params=_params("parallel"),
        name="inproj",
    )(x2, g, w_bf, b)


def _attn_bias_tables():
    qi = np.arange(WINDOW)[:, None]
    ki = np.arange(2 * WINDOW)[None, :]
    dist = WINDOW + qi - ki
    in_window = (dist >= 0) & (dist < WINDOW)
    slopes = np.asarray(_ALIBI, np.float32)[:, None, None]
    bias = -(slopes * dist.astype(np.float32)[None])
    tables = [np.where(in_window & (ki >= WINDOW), bias, -np.inf), np.where(in_window, bias, -np.inf)]
    return np.stack(tables).astype(np.float32)


def _pair_heads(a, axis):
    shape = a.shape
    a = a.reshape(shape[:axis] + (ATTN_KV_HEADS, ATTN_GROUP, ATTN_HEAD_DIM) + shape[axis + 1:])
    return jnp.swapaxes(a, axis, axis + 1).reshape(shape)


def _attn_kernel(sink_ref, q_ref, kp_ref, kc_ref, vp_ref, vc_ref, bias_ref, g_ref, o_ref):
    n = pl.program_id(1)
    kc = kc_ref[...].astype(BF16)
    vc = vc_ref[...].astype(BF16)
    keys = [jnp.concatenate([kp_ref[...].astype(BF16), kc[:WINDOW]], axis=0)]
    vals = [jnp.concatenate([vp_ref[...].astype(BF16), vc[:WINDOW]], axis=0)]
    for i in range(1, ATTN_STEP_BLOCKS):
        keys.append(kc[(i - 1) * WINDOW:(i + 1) * WINDOW])
        vals.append(vc[(i - 1) * WINDOW:(i + 1) * WINDOW])
    tables = [jnp.minimum(n, 1)] + [1] * (ATTN_STEP_BLOCKS - 1)
    scale = ATTN_HEAD_DIM ** -0.5
    nt = (((1,), (1,)), ((), ()))
    low = lax.broadcasted_iota(jnp.int32, (WINDOW, LANES), 1) < ATTN_HEAD_DIM

    def softmax(s, table, h):
        s = s * scale + bias_ref[table, h]
        sink = sink_ref[h]
        m = jnp.maximum(jnp.max(s, axis=-1, keepdims=True), sink)
        p = jnp.exp(s - m)
        den = jnp.sum(p, axis=-1, keepdims=True) + jnp.exp(sink - m)
        return (p / den).astype(BF16)

    items = [(i, j) for i in range(ATTN_STEP_BLOCKS) for j in range(ATTN_GROUP)]
    scores = {}
    for i, j in items:
        q = q_ref[i * WINDOW:(i + 1) * WINDOW, j * LANES:(j + 1) * LANES]
        scores[i, j] = (
            lax.dot_general(jnp.where(low, q, 0.0).astype(BF16), keys[i], nt, preferred_element_type=F32),
            lax.dot_general(jnp.where(low, 0.0, q).astype(BF16), keys[i], nt, preferred_element_type=F32))
    probs = {(i, j): (softmax(scores[i, j][0], tables[i], j), softmax(scores[i, j][1], tables[i], ATTN_GROUP + j))
             for i, j in items}
    for i in range(ATTN_STEP_BLOCKS):
        outs = [jnp.where(low, jnp.dot(probs[i, j][0], vals[i], preferred_element_type=F32),
                          jnp.dot(probs[i, j][1], vals[i], preferred_element_type=F32))
                for j in range(ATTN_GROUP)]
        o_ref[i * WINDOW:(i + 1) * WINDOW, :] = _rms(jnp.concatenate(outs, axis=1), g_ref[...])


def _attention(aq, ak, av, sinks, g, batch, seq):
    step = ATTN_STEP_BLOCKS * WINDOW
    aq = aq.reshape(batch, seq, ATTN_WIDTH)
    ak = ak.reshape(batch, seq, ATTN_KV_WIDTH)
    av = av.reshape(batch, seq, ATTN_KV_WIDTH)
    cur = lambda b, n, s: (b, n, 0)
    prev = lambda b, n, s: (b, jnp.maximum(n * ATTN_STEP_BLOCKS - 1, 0), 0)
    out = pl.pallas_call(
        _attn_kernel,
        grid_spec=pltpu.PrefetchScalarGridSpec(
            num_scalar_prefetch=1,
            grid=(batch, seq // step),
            in_specs=[pl.BlockSpec((None, step, ATTN_WIDTH), cur),
                      pl.BlockSpec((None, WINDOW, ATTN_KV_WIDTH), prev),
                      pl.BlockSpec((None, step, ATTN_KV_WIDTH), cur),
                      pl.BlockSpec((None, WINDOW, ATTN_KV_WIDTH), prev),
                      pl.BlockSpec((None, step, ATTN_KV_WIDTH), cur),
                      pl.BlockSpec((2, ATTN_Q_HEADS, WINDOW, 2 * WINDOW), lambda b, n, s: (0, 0, 0, 0)),
                      pl.BlockSpec((1, ATTN_WIDTH), lambda b, n, s: (0, 0))],
            out_specs=pl.BlockSpec((None, step, ATTN_WIDTH), cur)),
        out_shape=jax.ShapeDtypeStruct((batch, seq, ATTN_WIDTH), F32),
        compiler_params=_params("parallel", "parallel"),
        name="attn",
    )(sinks, aq, ak, ak, av, av, jnp.asarray(_attn_bias_tables()), g)
    return out.reshape(batch * seq, ATTN_WIDTH)


HGRN_EXP_GUARD = 80.0


def _hgrn_kernel(guard, q_ref, f_ref, i_ref, gate_ref, lb_ref, og_ref, o_ref, st_ref, q_scr, k_scr, b_scr):
    step = pl.program_id(1)

    @pl.when(step == 0)
    def _():
        st_ref[...] = jnp.zeros_like(st_ref)

    C, S = HGRN_CHUNK, HGRN_SUB
    nsub = C // S
    nchunk = HGRN_STEP // C
    r_i = lax.broadcasted_iota(jnp.int32, (C, C), 0)
    c_i = lax.broadcasted_iota(jnp.int32, (C, C), 1)
    nt = (((1,), (1,)), ((), ()))

    tri = (r_i >= c_i).astype(F32)
    qx = q_ref[...]
    q_scr[...] = qx * jax.nn.sigmoid(qx)
    lb = lb_ref[...]
    f = lb + (1.0 - lb) * jax.nn.sigmoid(f_ref[...])
    k_scr[...] = 1.0 - f
    logf = jnp.log(f)
    for c in range(nchunk):
        b_scr[c * C:(c + 1) * C, :] = jnp.dot(tri, logf[c * C:(c + 1) * C, :], preferred_element_type=F32,
                                              precision=lax.Precision.HIGHEST)
    decay = jnp.concatenate([-jnp.sum(logf[i * S:(i + 1) * S, :], axis=0, keepdims=True)
                             for i in range(HGRN_STEP // S)], axis=0)
    risky = jnp.max(decay) > guard

    rowblk = lax.broadcasted_iota(jnp.int32, (C, HGRN_DIM), 0) // S
    pairs = [(c, h) for c in range(nchunk) for h in range(HGRN_HEADS)]

    def view(ref, c, h):
        return ref[c * C:(c + 1) * C, h * HGRN_DIM:(h + 1) * HGRN_DIM]

    def finish(intra):
        q_dec, gain, update = {}, {}, {}
        for c, h in pairs:
            q, k, b = view(q_scr, c, h), view(k_scr, c, h), view(b_scr, c, h)
            q_dec[c, h] = (q * jnp.exp(b)).astype(BF16)
            bl = b[C - 1:C, :]
            gain[c, h] = jnp.exp(bl)
            update[c, h] = lax.dot_general(view(i_ref, c, h).astype(BF16), (k * jnp.exp(bl - b)).astype(BF16),
                                           (((0,), (0,)), ((), ())), preferred_element_type=F32)
        state = {}
        for h in range(HGRN_HEADS):
            st = st_ref[h]
            for c in range(nchunk):
                state[c, h] = st
                st = st * gain[c, h] + update[c, h]
            st_ref[h] = st
        for c, h in pairs:
            o = intra[c, h] + lax.dot_general(q_dec[c, h], state[c, h].astype(BF16), nt,
                                              preferred_element_type=F32)
            gx = view(gate_ref, c, h)
            og = og_ref[:, h * HGRN_DIM:(h + 1) * HGRN_DIM]
            o = o * lax.rsqrt(jnp.mean(o * o, axis=-1, keepdims=True) + NORM_EPS) * og
            o_ref[c * C:(c + 1) * C, h * HGRN_DIM:(h + 1) * HGRN_DIM] = o * (gx * jax.nn.sigmoid(gx))

    def sub_block_queries(q, b):
        return [jnp.where(rowblk == i, q * jnp.exp(jnp.minimum(b - b[i * S:i * S + 1, :], 0.0)), 0.0)
                for i in range(nsub)]

    @pl.when(jnp.logical_not(risky))
    def _():
        att = {}
        for c, h in pairs:
            q, k, b = view(q_scr, c, h), view(k_scr, c, h), view(b_scr, c, h)
            k_sub = [jnp.where(rowblk <= i, k * jnp.exp(jnp.minimum(b[i * S:i * S + 1, :] - b, guard)), 0.0)
                     for i in range(nsub)]
            att[c, h] = lax.dot_general(jnp.concatenate(sub_block_queries(q, b), axis=1).astype(BF16),
                                        jnp.concatenate(k_sub, axis=1).astype(BF16), nt,
                                        preferred_element_type=F32)
        finish({(c, h): jnp.dot(jnp.where(r_i >= c_i, att[c, h], 0.0).astype(BF16),
                                view(i_ref, c, h).astype(BF16), preferred_element_type=F32) for c, h in pairs})

    @pl.when(risky)
    def _():
        sub_r = lax.broadcasted_iota(jnp.int32, (S, HGRN_DIM), 0)
        intra = {}
        for c, h in pairs:
            q, k, b, v = view(q_scr, c, h), view(k_scr, c, h), view(b_scr, c, h), view(i_ref, c, h)
            k_sub = [jnp.where(rowblk < i, k * jnp.exp(jnp.minimum(b[i * S:i * S + 1, :] - b, 0.0)), 0.0)
                     for i in range(1, nsub)]
            att = lax.dot_general(jnp.concatenate(sub_block_queries(q, b)[1:], axis=1).astype(BF16),
                                  jnp.concatenate(k_sub, axis=1).astype(BF16), nt, preferred_element_type=F32)
            diag = []
            for i in range(nsub):
                b_blk = b[i * S:(i + 1) * S, :]
                q_blk = q[i * S:(i + 1) * S, :]
                acc = jnp.zeros((S, HGRN_DIM), F32)
                for s_ in range(S):
                    r = i * S + s_
                    e = jnp.exp(jnp.minimum(b_blk - b[r:r + 1, :], 0.0))
                    a = jnp.where(sub_r >= s_, q_blk * e * k[r:r + 1, :], 0.0)
                    acc = acc + jnp.sum(a, axis=-1, keepdims=True) * v[r:r + 1, :]
                diag.append(acc)
            intra[c, h] = (jnp.dot(att.astype(BF16), v.astype(BF16), preferred_element_type=F32)
                           + jnp.concatenate(diag, axis=0))
        finish(intra)


def _hgrn(hq, hf, hi, hg, lb, og, batch, seq, guard=HGRN_EXP_GUARD):
    shp = (batch, seq, HGRN_WIDTH)
    blk = pl.BlockSpec((None, HGRN_STEP, HGRN_WIDTH), lambda b, c: (b, c, 0))
    vec = pl.BlockSpec((1, HGRN_WIDTH), lambda b, c: (0, 0))
    out = pl.pallas_call(
        functools.partial(_hgrn_kernel, guard),
        grid=(batch, seq // HGRN_STEP),
        in_specs=[blk, blk, blk, blk, vec, vec],
        out_specs=blk,
        out_shape=jax.ShapeDtypeStruct(shp, F32),
        scratch_shapes=[pltpu.VMEM((HGRN_HEADS, HGRN_DIM, HGRN_DIM), F32)]
        + [pltpu.VMEM((HGRN_STEP, HGRN_WIDTH), F32)] * 3,
        compiler_params=_params("parallel", "arbitrary"),
        name="hgrn",
    )(hq.reshape(shp), hf.reshape(shp), hi.reshape(shp), hg.reshape(shp), lb, og)
    return out.reshape(batch * seq, HGRN_WIDTH)


def _outproj_kernel(attn_ref, hg_ref, x_ref, wo_ref, bo_ref, g2_ref, rw_ref, rb_ref,
                    x1_ref, h2_ref, lpos_ref, gate_ref, seg_ref):
    tm = SORT_TILE
    tiles = range(OUTPROJ_TILES)
    y = jnp.dot(attn_ref[...].astype(BF16), wo_ref[:ATTN_WIDTH, :], preferred_element_type=F32)
    y = y + jnp.dot(hg_ref[...].astype(BF16), wo_ref[ATTN_WIDTH:, :], preferred_element_type=F32)
    x1 = x_ref[...] + y + bo_ref[...]
    x1_ref[...] = x1
    h2 = _rms(x1, g2_ref[...])
    h2_ref[...] = h2.astype(BF16)
    nt = (((1,), (1,)), ((), ()))
    h_hi = h2.astype(BF16)
    h_lo = (h2 - h_hi.astype(F32)).astype(BF16)
    work = []
    for u in tiles:
        p_hi = lax.dot_general(rw_ref[...], h_hi[u * tm:(u + 1) * tm], nt, preferred_element_type=F32)
        p_lo = lax.dot_general(rw_ref[...], h_lo[u * tm:(u + 1) * tm], nt, preferred_element_type=F32)
        work.append((p_hi[:N_EXPERTS] + p_hi[LANES:LANES + N_EXPERTS]) + p_lo[:N_EXPERTS] + rb_ref[...])
    eid = lax.broadcasted_iota(jnp.int32, (N_EXPERTS, tm), 0).astype(F32)
    vals = [[] for _ in tiles]
    hots = [[] for _ in tiles]
    for _ in range(TOP_K):
        for u in tiles:
            m = jnp.max(work[u], axis=0, keepdims=True)
            idx = jnp.min(jnp.where(work[u] == m, eid, float(N_EXPERTS)), axis=0, keepdims=True)
            hot = eid == idx
            vals[u].append(m)
            hots[u].append(hot)
            work[u] = jnp.where(hot, -jnp.inf, work[u])
    t_r = lax.broadcasted_iota(jnp.int32, (tm, tm), 0)
    t_c = lax.broadcasted_iota(jnp.int32, (tm, tm), 1)
    earlier = jnp.where(t_r < t_c, 1.0, 0.0).astype(BF16)
    e_r = lax.broadcasted_iota(jnp.int32, (N_EXPERTS, N_EXPERTS), 0)
    e_c = lax.broadcasted_iota(jnp.int32, (N_EXPERTS, N_EXPERTS), 1)
    before = jnp.where(e_c < e_r, 1.0, 0.0).astype(BF16)

    def round_up(n):
        return jnp.floor((n + (SEG_ROWS - 1.0)) * (1.0 / SEG_ROWS)) * SEG_ROWS

    sel = [sum(jnp.where(hot, 1.0, 0.0) for hot in hots[u]) for u in tiles]
    ranks = [jnp.dot(sel[u].astype(BF16), earlier, preferred_element_type=F32) for u in tiles]
    seg_col = [round_up(jnp.sum(sel[u], axis=1, keepdims=True)) for u in tiles]
    seg_off = [jnp.dot(before, jnp.broadcast_to(seg_col[u], (N_EXPERTS, LANES)).astype(BF16),
                       preferred_element_type=F32)[:, 0:1] for u in tiles]
    seg_row = [round_up(lax.dot_general(jnp.ones((SEG_ROWS, tm), BF16), sel[u].astype(BF16), nt,
                                        preferred_element_type=F32)[0:1, :]) for u in tiles]
    for u in tiles:
        slots = ranks[u] + seg_off[u]
        ex = [jnp.exp(v - vals[u][0]) for v in vals[u]]
        den = ex[0] + ex[1] + ex[2] + ex[3]
        lpos_ref[:, u * tm:(u + 1) * tm] = jnp.concatenate(
            [jnp.sum(jnp.where(hot, slots, 0.0), axis=0, keepdims=True) for hot in hots[u]],
            axis=0).astype(jnp.int32)
        gate_ref[:, u * tm:(u + 1) * tm] = jnp.concatenate([e / den for e in ex], axis=0)
        seg_ref[u] = seg_row[u]


def _outproj(attn, hgo, x2, wo_bf, bo, g2, rw, rb):
    t = x2.shape[0]
    step = OUTPROJ_TILES * SORT_TILE
    row = lambda i: (i, 0)
    col = lambda i: (0, i)
    fixed = lambda i: (0, 0)
    return pl.pallas_call(
        _outproj_kernel,
        grid=(t // step,),
        in_specs=[pl.BlockSpec((step, ATTN_WIDTH), row),
                  pl.BlockSpec((step, HGRN_WIDTH), row),
                  pl.BlockSpec((step, D_MODEL), row),
                  pl.BlockSpec((ATTN_WIDTH + HGRN_WIDTH, D_MODEL), fixed),
                  pl.BlockSpec((1, D_MODEL), fixed),
                  pl.BlockSpec((1, D_MODEL), fixed),
                  pl.BlockSpec((2 * LANES, D_MODEL), fixed),
                  pl.BlockSpec((N_EXPERTS, 1), fixed)],
        out_specs=[pl.BlockSpec((step, D_MODEL), row),
                   pl.BlockSpec((step, D_MODEL), row),
                   pl.BlockSpec((TOP_K, step), col),
                   pl.BlockSpec((TOP_K, step), col),
                   pl.BlockSpec((OUTPROJ_TILES, 1, N_EXPERTS), lambda i: (i, 0, 0))],
        out_shape=[jax.ShapeDtypeStruct((t, D_MODEL), F32),
                   jax.ShapeDtypeStruct((t, D_MODEL), BF16),
                   jax.ShapeDtypeStruct((TOP_K, t), jnp.int32),
                   jax.ShapeDtypeStruct((TOP_K, t), F32),
                   jax.ShapeDtypeStruct((t // SORT_TILE, 1, N_EXPERTS), F32)],
        compiler_params=_params("parallel"),
        name="outproj",
    )(attn, hgo, x2, wo_bf, bo, g2, rw, rb)


def _rows(start, n):
    return pl.ds(pl.multiple_of(start, SEG_ROWS), n)


def _repeat(cnt, fn):
    def body(j, carry):
        fn(j)
        return carry

    lax.fori_loop(0, cnt, body, 0)


BIG_CAP = SORT_SLOTS // BIG_PIECE
SMALL_CAP = N_EXPERTS * (BIG_PIECE // SEG_ROWS - 1)


def _tile_pieces(lists, step, fn):
    big_loc, big_glob, big_cnt, small_loc, small_glob, small_cnt = lists
    b0 = step * BIG_CAP
    _repeat(big_cnt[step], lambda j: fn(big_loc[b0 + j], big_glob[b0 + j], BIG_PIECE, 0))
    s0 = step * SMALL_CAP
    _repeat(small_cnt[step], lambda j: fn(small_loc[s0 + j], small_glob[s0 + j], SEG_ROWS, 1))


def _tile_piece_waits(lists, step, wait):
    _repeat(lists[2][step], lambda j: wait(BIG_PIECE))
    _repeat(lists[5][step], lambda j: wait(SEG_ROWS))


def _dispatch_kernel(*refs):
    lists, (zdst_ref, zbig_ref, zsmall_ref) = refs[:6], refs[6:9]
    h_ref, lpos_t_ref, xs_ref, lbuf_ref, zbuf_ref, sem, zsem = refs[9:]
    i = pl.program_id(0)
    last = pl.num_programs(0) - 1
    buf = i % 2

    def piece(b, local, glob, n):
        return pltpu.make_async_copy(lbuf_ref.at[b, _rows(local, n), :], xs_ref.at[_rows(glob, n), :], sem.at[b])

    def drain(step):
        _tile_piece_waits(lists, step, lambda n: piece(step % 2, 0, 0, n).wait())

    def zero_piece(glob, n):
        return pltpu.make_async_copy(zbuf_ref.at[_rows(0, n), :], xs_ref.at[_rows(glob, n), :], zsem)

    @pl.when(i == 0)
    def _():
        zbuf_ref[...] = jnp.zeros_like(zbuf_ref)
        for e in range(N_EXPERTS + 1):
            _repeat(zbig_ref[e], lambda j, e=e: zero_piece(zdst_ref[e] + j * BIG_PIECE, BIG_PIECE).start())
            _repeat(zsmall_ref[e], lambda j, e=e: zero_piece(
                zdst_ref[e] + zbig_ref[e] * BIG_PIECE + j * SEG_ROWS, SEG_ROWS).start())
        for e in range(N_EXPERTS + 1):
            _repeat(zbig_ref[e], lambda j: zero_piece(0, BIG_PIECE).wait())
            _repeat(zsmall_ref[e], lambda j: zero_piece(0, SEG_ROWS).wait())

    @pl.when(i >= 2)
    def _():
        drain(i - 2)

    slot = lax.broadcasted_iota(jnp.int32, (SORT_SLOTS, SORT_TILE), 0)
    onehot = jnp.zeros((SORT_SLOTS, SORT_TILE), F32)
    for k in range(TOP_K):
        onehot = onehot + jnp.where(slot == lpos_t_ref[k:k + 1, :], 1.0, 0.0)
    lbuf_ref[buf] = jnp.dot(onehot.astype(BF16), h_ref[...], preferred_element_type=F32)
    _tile_pieces(lists, i, lambda local, glob, n, prio: piece(buf, local, glob, n).start(priority=prio))

    @pl.when(i == last)
    def _():
        @pl.when(i >= 1)
        def _():
            drain(i - 1)

        drain(i)


def _dispatch(tables, zero_tables, h2, lpos_t, n_rows):
    t = h2.shape[0]
    tile = lambda i, *_: (i, 0)
    return pl.pallas_call(
        _dispatch_kernel,
        grid_spec=pltpu.PrefetchScalarGridSpec(
            num_scalar_prefetch=9,
            grid=(t // SORT_TILE,),
            in_specs=[pl.BlockSpec((SORT_TILE, D_MODEL), tile),
                      pl.BlockSpec((TOP_K, SORT_TILE), lambda i, *_: (0, i))],
            out_specs=pl.BlockSpec(memory_space=pl.ANY),
            scratch_shapes=[pltpu.VMEM((2, SORT_SLOTS, D_MODEL), F32),
                            pltpu.VMEM((BIG_PIECE, D_MODEL), F32),
                            pltpu.SemaphoreType.DMA((2,)),
                            pltpu.SemaphoreType.DMA(())]),
        out_shape=jax.ShapeDtypeStruct((n_rows, D_MODEL), F32),
        compiler_params=_params("arbitrary"),
        name="dispatch",
    )(*tables, *zero_tables, h2, lpos_t)


CAST_ROWS = 128


def _expert_kernel(be_ref, nb_ref, eb_ref, filled_ref,
                   xs_ref, w1_hbm, b1_ref, w2_hbm, b2_ref, y_ref,
                   w1f_ref, w2f_ref, w1b_ref, w2i_ref, w2b_ref, started_ref, sem):
    blk = pl.program_id(0)
    half = LANES // 2

    def weight_copies(e, s):
        return (pltpu.make_async_copy(w1_hbm.at[e], w1f_ref.at[s], sem.at[0, s]),
                pltpu.make_async_copy(w2_hbm.at[e], w2f_ref.at[s], sem.at[1, s]))

    e = be_ref[blk]
    first_of_expert = (blk < nb_ref[0]) & ((blk == 0) | (e != be_ref[jnp.maximum(blk - 1, 0)]))

    @pl.when(first_of_expert)
    def _():
        done = jnp.where(blk == 0, 0, started_ref[0])
        started_ref[0] = done + 1
        s = done % 2
        next_blk = blk + eb_ref[e]

        @pl.when(blk == 0)
        def _():
            for c in weight_copies(e, s):
                c.start()

        @pl.when(next_blk < nb_ref[0])
        def _():
            for c in weight_copies(be_ref[next_blk], 1 - s):
                c.start()

        for c in weight_copies(e, s):
            c.wait()

        def cast_rows(r, carry):
            rows = pl.ds(pl.multiple_of(r * CAST_ROWS, CAST_ROWS), CAST_ROWS)
            w1b_ref[rows, :] = w1f_ref[s, rows, :].astype(BF16)
            return carry

        lax.fori_loop(0, D_MODEL // CAST_ROWS, cast_rows, 0)
        for c in range(D_MODEL // LANES):
            cols = slice(c * LANES, (c + 1) * LANES)
            for m in range(EXPERT_FF // LANES):
                lo = m * LANES
                w2i_ref[c, pl.ds(lo, half, stride=2), :] = w2f_ref[s, lo:lo + half, cols]
                w2i_ref[c, pl.ds(lo + 1, half, stride=2), :] = w2f_ref[s, lo + half:lo + LANES, cols]
            w2b_ref[:, cols] = w2i_ref[c].astype(BF16)

    def ffn(rows):
        x = xs_ref[:rows, :].astype(BF16)
        hid = jnp.dot(x, w1b_ref[...], preferred_element_type=F32) + b1_ref[...]
        even = (lax.broadcasted_iota(jnp.int32, (rows, LANES), 1) & 1) == 0
        glu, lin = [], []
        for m in range(EXPERT_FF // LANES):
            ha = hid[:, 2 * m * LANES:(2 * m + 1) * LANES]
            hb = hid[:, (2 * m + 1) * LANES:(2 * m + 2) * LANES]
            glu.append(jnp.where(even, ha, pltpu.roll(hb, 1, axis=1)))
            lin.append(jnp.where(even, pltpu.roll(ha, LANES - 1, axis=1), hb))
        glu = jnp.minimum(jnp.concatenate(glu, axis=1), SWIGLU_LIMIT)
        lin = jnp.clip(jnp.concatenate(lin, axis=1), -SWIGLU_LIMIT, SWIGLU_LIMIT)
        act = glu * jax.nn.sigmoid(SWIGLU_ALPHA * glu) * (lin + 1.0)
        y_ref[:rows, :] = jnp.dot(act.astype(BF16), w2b_ref[...], preferred_element_type=F32) + b2_ref[...]

    half_block = EXPERT_BLOCK // 2

    @pl.when(filled_ref[blk] > half_block)
    def _():
        ffn(EXPERT_BLOCK)

    @pl.when((filled_ref[blk] > 0) & (filled_ref[blk] <= half_block))
    def _():
        ffn(half_block)
        y_ref[half_block:, :] = jnp.zeros((EXPERT_BLOCK - half_block, D_MODEL), F32)

    @pl.when(filled_ref[blk] == 0)
    def _():
        y_ref[...] = jnp.zeros_like(y_ref)


def _experts(block_tables, xs, w1, b1, w2, b2):
    n_rows = xs.shape[0]
    nblk = n_rows // EXPERT_BLOCK
    rows = lambda b, *_: (b, 0)
    used_rows = lambda b, be, nb, *_: (jnp.minimum(b, nb[0] - 1), 0)
    bias = lambda b, be, *_: (be[b], 0, 0)
    return pl.pallas_call(
        _expert_kernel,
        grid_spec=pltpu.PrefetchScalarGridSpec(
            num_scalar_prefetch=4,
            grid=(nblk,),
            in_specs=[pl.BlockSpec((EXPERT_BLOCK, D_MODEL), used_rows),
                      pl.BlockSpec(memory_space=pl.ANY),
                      pl.BlockSpec((None, 1, 2 * EXPERT_FF), bias),
                      pl.BlockSpec(memory_space=pl.ANY),
                      pl.BlockSpec((None, 1, D_MODEL), bias)],
            out_specs=pl.BlockSpec((EXPERT_BLOCK, D_MODEL), rows),
            scratch_shapes=[pltpu.VMEM((2, D_MODEL, 2 * EXPERT_FF), F32),
                            pltpu.VMEM((2, EXPERT_FF, D_MODEL), F32),
                            pltpu.VMEM((D_MODEL, 2 * EXPERT_FF), BF16),
                            pltpu.VMEM((D_MODEL // LANES, EXPERT_FF, LANES), F32),
                            pltpu.VMEM((EXPERT_FF, D_MODEL), BF16),
                            pltpu.SMEM((1,), jnp.int32),
                            pltpu.SemaphoreType.DMA((2, 2))]),
        out_shape=jax.ShapeDtypeStruct((n_rows, D_MODEL), F32),
        compiler_params=_params("arbitrary"),
        name="experts",
    )(*block_tables, xs, w1, b1, w2, b2)


def _combine_kernel(final_norm, *refs):
    lists = refs[:6]
    yb_ref, x1_ref, lpos_ref, gate_ref, g_ref, o_ref, gbuf_ref, sem = refs[6:]
    i = pl.program_id(0)
    last = pl.num_programs(0) - 1
    buf = i % 2

    def piece(b, local, glob, n):
        return pltpu.make_async_copy(yb_ref.at[_rows(glob, n), :], gbuf_ref.at[b, _rows(local, n), :], sem.at[b])

    def fetch(step):
        _tile_pieces(lists, step, lambda local, glob, n, prio: piece(step % 2, local, glob, n).start(priority=prio))

    @pl.when(i == 0)
    def _():
        gbuf_ref[...] = jnp.zeros_like(gbuf_ref)
        fetch(0)

    @pl.when(i < last)
    def _():
        fetch(i + 1)

    _tile_piece_waits(lists, i, lambda n: piece(buf, 0, 0, n).wait())
    slot = lax.broadcasted_iota(jnp.int32, (SORT_TILE, SORT_SLOTS), 1)
    lpos = lpos_ref[...]
    gates = gate_ref[...]
    weights = jnp.zeros((SORT_TILE, SORT_SLOTS), F32)
    for k in range(TOP_K):
        weights = weights + jnp.where(slot == lpos[:, k:k + 1], gates[:, k:k + 1], 0.0)
    y = x1_ref[...] + jnp.dot(weights.astype(BF16), gbuf_ref[buf].astype(BF16), preferred_element_type=F32)
    o_ref[...] = _rms(y, g_ref[...]) if final_norm else y


def _combine(tables, yb, x1, lpos, gates, g, final_norm):
    t = x1.shape[0]
    tile = lambda i, *_: (i, 0)
    return pl.pallas_call(
        functools.partial(_combine_kernel, final_norm),
        grid_spec=pltpu.PrefetchScalarGridSpec(
            num_scalar_prefetch=6,
            grid=(t // SORT_TILE,),
            in_specs=[pl.BlockSpec(memory_space=pl.ANY),
                      pl.BlockSpec((SORT_TILE, D_MODEL), tile),
                      pl.BlockSpec((SORT_TILE, TOP_K), tile),
                      pl.BlockSpec((SORT_TILE, TOP_K), tile),
                      pl.BlockSpec((1, D_MODEL), lambda i, *_: (0, 0))],
            out_specs=pl.BlockSpec((SORT_TILE, D_MODEL), tile),
            scratch_shapes=[pltpu.VMEM((2, SORT_SLOTS, D_MODEL), F32),
                            pltpu.SemaphoreType.DMA((2,))]),
        out_shape=jax.ShapeDtypeStruct((t, D_MODEL), F32),
        compiler_params=_params("arbitrary"),
        name="combine",
    )(*tables, yb, x1, lpos, gates, g)


def _piece_lists(seg_loc, seg_glob, cnt, skip, rows, cap):
    first = jnp.cumsum(cnt, axis=1) - cnt
    p = jnp.arange(cap, dtype=jnp.int32)
    started = first[:, None, :] <= p[None, :, None]

    def per_piece(rows0):
        base = rows0 + skip - first * rows
        step = base - jnp.concatenate([jnp.zeros_like(base[:, :1]), base[:, :-1]], axis=1)
        return (jnp.sum(jnp.where(started, step[:, None, :], 0), axis=2) + p[None, :] * rows).reshape(-1)

    return per_piece(seg_loc), per_piece(seg_glob), jnp.sum(cnt, axis=1)


def kernel(x, norm1_g, w_in, b_in, attn_sinks, attn_out_g, hgrn_lb_logits, hgrn_out_g, w_out, b_out,
           norm2_g, router_w, router_b, w1, b1, w2, b2, final_g):
    batch, seq, d = x.shape
    t = batch * seq
    depth = w_in.shape[0]
    lower_bounds = jnp.cumsum(jax.nn.softmax(hgrn_lb_logits.astype(F32), axis=0), axis=0)
    assert t % SORT_TILE == 0 and SORT_SLOTS >= SORT_TILE * TOP_K + N_EXPERTS * (SEG_ROWS - 1)
    ntiles = t // SORT_TILE
    nblk = (t * TOP_K + ntiles * N_EXPERTS * (SEG_ROWS - 1)) // EXPERT_BLOCK + N_EXPERTS
    n_rows = nblk * EXPERT_BLOCK
    x2 = x.reshape(t, d)
    for l in range(depth):
        w_in_l = jnp.concatenate([_pair_heads(w_in[l][:, :ATTN_WIDTH], 1), w_in[l][:, ATTN_WIDTH:]], axis=1)
        b_in_l = jnp.concatenate([_pair_heads(b_in[l][:ATTN_WIDTH], 0), b_in[l][ATTN_WIDTH:]])
        aq, ak, av, hq, hf, hi, hg = _inproj(x2, norm1_g[l][None], w_in_l.astype(BF16), b_in_l[None])
        attn = _attention(aq, ak, av, attn_sinks[l], _pair_heads(attn_out_g[l], 0)[None], batch, seq)
        hgo = _hgrn(hq, hf, hi, hg, lower_bounds[l][None], hgrn_out_g[l][None], batch, seq)
        w_out_l = jnp.concatenate([_pair_heads(w_out[l][:ATTN_WIDTH], 0), w_out[l][ATTN_WIDTH:]], axis=0)
        rw_hi = router_w[l].astype(BF16)
        rw_lo = (router_w[l] - rw_hi.astype(F32)).astype(BF16)
        rw_cat = jnp.zeros((2 * LANES, d), BF16)
        rw_cat = rw_cat.at[:N_EXPERTS].set(rw_hi.T).at[LANES:LANES + N_EXPERTS].set(rw_lo.T)
        x1, h2, lpos_t, gates_t, seg = _outproj(
            attn, hgo, x2, w_out_l.astype(BF16), b_out[l][None], norm2_g[l][None],
            rw_cat, router_b[l][:, None])
        seg = seg.reshape(ntiles, N_EXPERTS).astype(jnp.int32)
        rows_e = jnp.sum(seg, axis=0)
        padded = (rows_e + EXPERT_BLOCK - 1) // EXPERT_BLOCK * EXPERT_BLOCK
        ends = jnp.cumsum(padded)
        pstart = ends - padded
        seg_glob = pstart[None, :] + jnp.cumsum(seg, axis=0) - seg
        seg_loc = jnp.cumsum(seg, axis=1) - seg
        tail = jnp.append(padded - rows_e, n_rows - ends[-1])
        zero_tables = (jnp.append(pstart + rows_e, ends[-1]), tail // BIG_PIECE, tail % BIG_PIECE // SEG_ROWS)
        big = seg // BIG_PIECE
        tables = (_piece_lists(seg_loc, seg_glob, big, jnp.zeros_like(seg), BIG_PIECE, BIG_CAP)
                  + _piece_lists(seg_loc, seg_glob, seg % BIG_PIECE // SEG_ROWS, big * BIG_PIECE, SEG_ROWS,
                                 SMALL_CAP))
        blk_ids = jnp.arange(nblk, dtype=jnp.int32)
        block_e = jnp.minimum(jnp.sum(blk_ids[:, None] * EXPERT_BLOCK >= ends[None, :], axis=-1), N_EXPERTS - 1)
        in_expert = rows_e[None, :] - (blk_ids[:, None] * EXPERT_BLOCK - pstart[None, :])
        filled = jnp.sum(jnp.where((block_e[:, None] == jnp.arange(N_EXPERTS)[None, :])
                                   & (blk_ids[:, None] * EXPERT_BLOCK < ends[-1]),
                                   jnp.clip(in_expert, 0, EXPERT_BLOCK), 0), axis=1)
        block_tables = (block_e.astype(jnp.int32), (ends[-1:] // EXPERT_BLOCK).astype(jnp.int32),
                        (padded // EXPERT_BLOCK).astype(jnp.int32), filled.astype(jnp.int32))
        xs = _dispatch(tables, zero_tables, h2, lpos_t, n_rows)
        yb = _experts(block_tables, xs, w1[l], b1[l][:, None, :], w2[l], b2[l][:, None, :])
        x2 = _combine(tables, yb, x1, lpos_t.T, gates_t.T, final_g[None], l == depth - 1)
    return x2.reshape(batch, seq, d)
```

```python
import functools

import numpy as np
import jax
import jax.numpy as jnp
from jax import lax
from jax.experimental import pallas as pl
from jax.experimental.pallas import tpu as pltpu

F32 = jnp.float32
BF16 = jnp.bfloat16

D_MODEL = 1024
ATTN_Q_HEADS = 8
ATTN_KV_HEADS = 2
ATTN_HEAD_DIM = 64
ATTN_GROUP = ATTN_Q_HEADS // ATTN_KV_HEADS
ATTN_WIDTH = ATTN_Q_HEADS * ATTN_HEAD_DIM
ATTN_KV_WIDTH = ATTN_KV_HEADS * ATTN_HEAD_DIM
WINDOW = 128
ATTN_STEP_BLOCKS = 4
HGRN_HEADS = 4
HGRN_DIM = 128
HGRN_WIDTH = HGRN_HEADS * HGRN_DIM
HGRN_CHUNK = 64
HGRN_SUB = 16
HGRN_STEP = 512
IN_WIDTH = ATTN_WIDTH + 2 * ATTN_KV_WIDTH + 4 * HGRN_WIDTH
N_EXPERTS = 32
TOP_K = 4
EXPERT_FF = D_MODEL
SWIGLU_LIMIT = 7.0
SWIGLU_ALPHA = 1.702
NORM_EPS = 1e-5

LANES = 128
ROW_TILE = 1024
EXPERT_BLOCK = 512
SORT_TILE = 256
OUTPROJ_TILES = 4
SEG_ROWS = 8
BIG_PIECE = 32
SORT_SLOTS = 1280
VMEM_LIMIT = 56 * 1024 * 1024

_ALIBI = [float(2.0 ** (-8.0 * (h + 1) / ATTN_Q_HEADS)) for h in range(ATTN_Q_HEADS)]


def _rms(x, g):
    return x * lax.rsqrt(jnp.mean(x * x, axis=-1, keepdims=True) + NORM_EPS) * g


def _params(*sem):
    return pltpu.CompilerParams(dimension_semantics=sem, vmem_limit_bytes=VMEM_LIMIT)


_IN_SPLITS = (ATTN_WIDTH, ATTN_KV_WIDTH, ATTN_KV_WIDTH, HGRN_WIDTH, HGRN_WIDTH, HGRN_WIDTH, HGRN_WIDTH)
_IN_DTYPES = (BF16, BF16, BF16, F32, F32, F32, F32)


def _inproj_kernel(x_ref, g_ref, w_ref, b_ref, *out_refs):
    h = _rms(x_ref[...], g_ref[...]).astype(BF16)
    lo = 0
    for ref, width in zip(out_refs, _IN_SPLITS):
        out = jnp.dot(h, w_ref[:, lo:lo + width], preferred_element_type=F32) + b_ref[:, lo:lo + width]
        ref[...] = out.astype(ref.dtype)
        lo += width


def _inproj(x2, g, w_bf, b):
    t = x2.shape[0]
    row = lambda i: (i, 0)
    fixed = lambda i: (0, 0)
    return pl.pallas_call(
        _inproj_kernel,
        grid=(t // ROW_TILE,),
        in_specs=[pl.BlockSpec((ROW_TILE, D_MODEL), row),
                  pl.BlockSpec((1, D_MODEL), fixed),
                  pl.BlockSpec((D_MODEL, IN_WIDTH), fixed),
                  pl.BlockSpec((1, IN_WIDTH), fixed)],
        out_specs=[pl.BlockSpec((ROW_TILE, w), row) for w in _IN_SPLITS],
        out_shape=[jax.ShapeDtypeStruct((t, w), dt) for w, dt in zip(_IN_SPLITS, _IN_DTYPES)],
        compiler_params=_params("parallel"),
        name="inproj",
    )(x2, g, w_bf, b)


def _attn_bias_tables():
    qi = np.arange(WINDOW)[:, None]
    ki = np.arange(2 * WINDOW)[None, :]
    dist = WINDOW + qi - ki
    in_window = (dist >= 0) & (dist < WINDOW)
    slopes = np.asarray(_ALIBI, np.float32)[:, None, None]
    bias = -(slopes * dist.astype(np.float32)[None])
    tables = [np.where(in_window & (ki >= WINDOW), bias, -np.inf), np.where(in_window, bias, -np.inf)]
    return np.stack(tables).astype(np.float32)


def _pair_heads(a, axis):
    shape = a.shape
    a = a.reshape(shape[:axis] + (ATTN_KV_HEADS, ATTN_GROUP, ATTN_HEAD_DIM) + shape[axis + 1:])
    return jnp.swapaxes(a, axis, axis + 1).reshape(shape)


def _attn_kernel(sink_ref, q_ref, kp_ref, kc_ref, vp_ref, vc_ref, bias_ref, g_ref, o_ref):
    n = pl.program_id(1)
    kc = kc_ref[...].astype(BF16)
    vc = vc_ref[...].astype(BF16)
    keys = [jnp.concatenate([kp_ref[...].astype(BF16), kc[:WINDOW]], axis=0)]
    vals = [jnp.concatenate([vp_ref[...].astype(BF16), vc[:WINDOW]], axis=0)]
    for i in range(1, ATTN_STEP_BLOCKS):
        keys.append(kc[(i - 1) * WINDOW:(i + 1) * WINDOW])
        vals.append(vc[(i - 1) * WINDOW:(i + 1) * WINDOW])
    tables = [jnp.minimum(n, 1)] + [1] * (ATTN_STEP_BLOCKS - 1)
    scale = ATTN_HEAD_DIM ** -0.5
    nt = (((1,), (1,)), ((), ()))
    low = lax.broadcasted_iota(jnp.int32, (WINDOW, LANES), 1) < ATTN_HEAD_DIM

    def softmax(s, table, h):
        s = s * scale + bias_ref[table, h]
        sink = sink_ref[h]
        m = jnp.maximum(jnp.max(s, axis=-1, keepdims=True), sink)
        p = jnp.exp(s - m)
        den = jnp.sum(p, axis=-1, keepdims=True) + jnp.exp(sink - m)
        return (p / den).astype(BF16)

    items = [(i, j) for i in range(ATTN_STEP_BLOCKS) for j in range(ATTN_GROUP)]
    scores = {}
    for i, j in items:
        q = q_ref[i * WINDOW:(i + 1) * WINDOW, j * LANES:(j + 1) * LANES]
        scores[i, j] = (
            lax.dot_general(jnp.where(low, q, 0.0).astype(BF16), keys[i], nt, preferred_element_type=F32),
            lax.dot_general(jnp.where(low, 0.0, q).astype(BF16), keys[i], nt, preferred_element_type=F32))
    probs = {(i, j): (softmax(scores[i, j][0], tables[i], j), softmax(scores[i, j][1], tables[i], ATTN_GROUP + j))
             for i, j in items}
    for i in range(ATTN_STEP_BLOCKS):
        outs = [jnp.where(low, jnp.dot(probs[i, j][0], vals[i], preferred_element_type=F32),
                          jnp.dot(probs[i, j][1], vals[i], preferred_element_type=F32))
                for j in range(ATTN_GROUP)]
        o_ref[i * WINDOW:(i + 1) * WINDOW, :] = _rms(jnp.concatenate(outs, axis=1), g_ref[...])


def _attention(aq, ak, av, sinks, g, batch, seq):
    step = ATTN_STEP_BLOCKS * WINDOW
    aq = aq.reshape(batch, seq, ATTN_WIDTH)
    ak = ak.reshape(batch, seq, ATTN_KV_WIDTH)
    av = av.reshape(batch, seq, ATTN_KV_WIDTH)
    cur = lambda b, n, s: (b, n, 0)
    prev = lambda b, n, s: (b, jnp.maximum(n * ATTN_STEP_BLOCKS - 1, 0), 0)
    out = pl.pallas_call(
        _attn_kernel,
        grid_spec=pltpu.PrefetchScalarGridSpec(
            num_scalar_prefetch=1,
            grid=(batch, seq // step),
            in_specs=[pl.BlockSpec((None, step, ATTN_WIDTH), cur),
                      pl.BlockSpec((None, WINDOW, ATTN_KV_WIDTH), prev),
                      pl.BlockSpec((None, step, ATTN_KV_WIDTH), cur),
                      pl.BlockSpec((None, WINDOW, ATTN_KV_WIDTH), prev),
                      pl.BlockSpec((None, step, ATTN_KV_WIDTH), cur),
                      pl.BlockSpec((2, ATTN_Q_HEADS, WINDOW, 2 * WINDOW), lambda b, n, s: (0, 0, 0, 0)),
                      pl.BlockSpec((1, ATTN_WIDTH), lambda b, n, s: (0, 0))],
            out_specs=pl.BlockSpec((None, step, ATTN_WIDTH), cur)),
        out_shape=jax.ShapeDtypeStruct((batch, seq, ATTN_WIDTH), F32),
        compiler_params=_params("parallel", "parallel"),
        name="attn",
    )(sinks, aq, ak, ak, av, av, jnp.asarray(_attn_bias_tables()), g)
    return out.reshape(batch * seq, ATTN_WIDTH)


HGRN_EXP_GUARD = 80.0


def _hgrn_kernel(guard, q_ref, f_ref, i_ref, gate_ref, lb_ref, og_ref, o_ref, st_ref, q_scr, k_scr, b_scr):
    step = pl.program_id(1)

    @pl.when(step == 0)
    def _():
        st_ref[...] = jnp.zeros_like(st_ref)

    C, S = HGRN_CHUNK, HGRN_SUB
    nsub = C // S
    nchunk = HGRN_STEP // C
    r_i = lax.broadcasted_iota(jnp.int32, (C, C), 0)
    c_i = lax.broadcasted_iota(jnp.int32, (C, C), 1)
    nt = (((1,), (1,)), ((), ()))

    tri = (r_i >= c_i).astype(F32)
    qx = q_ref[...]
    q_scr[...] = qx * jax.nn.sigmoid(qx)
    lb = lb_ref[...]
    f = lb + (1.0 - lb) * jax.nn.sigmoid(f_ref[...])
    k_scr[...] = 1.0 - f
    logf = jnp.log(f)
    for c in range(nchunk):
        b_scr[c * C:(c + 1) * C, :] = jnp.dot(tri, logf[c * C:(c + 1) * C, :], preferred_element_type=F32,
                                              precision=lax.Precision.HIGHEST)
    decay = jnp.concatenate([-jnp.sum(logf[i * S:(i + 1) * S, :], axis=0, keepdims=True)
                             for i in range(HGRN_STEP // S)], axis=0)
    risky = jnp.max(decay) > guard

    rowblk = lax.broadcasted_iota(jnp.int32, (C, HGRN_DIM), 0) // S
    pairs = [(c, h) for c in range(nchunk) for h in range(HGRN_HEADS)]

    def view(ref, c, h):
        return ref[c * C:(c + 1) * C, h * HGRN_DIM:(h + 1) * HGRN_DIM]

    def finish(intra):
        q_dec, gain, update = {}, {}, {}
        for c, h in pairs:
            q, k, b = view(q_scr, c, h), view(k_scr, c, h), view(b_scr, c, h)
            q_dec[c, h] = (q * jnp.exp(b)).astype(BF16)
            bl = b[C - 1:C, :]
            gain[c, h] = jnp.exp(bl)
            update[c, h] = lax.dot_general(view(i_ref, c, h).astype(BF16), (k * jnp.exp(bl - b)).astype(BF16),
                                           (((0,), (0,)), ((), ())), preferred_element_type=F32)
        state = {}
        for h in range(HGRN_HEADS):
            st = st_ref[h]
            for c in range(nchunk):
                state[c, h] = st
                st = st * gain[c, h] + update[c, h]
            st_ref[h] = st
        for c, h in pairs:
            o = intra[c, h] + lax.dot_general(q_dec[c, h], state[c, h].astype(BF16), nt,
                                              preferred_element_type=F32)
            gx = view(gate_ref, c, h)
            og = og_ref[:, h * HGRN_DIM:(h + 1) * HGRN_DIM]
            o = o * lax.rsqrt(jnp.mean(o * o, axis=-1, keepdims=True) + NORM_EPS) * og
            o_ref[c * C:(c + 1) * C, h * HGRN_DIM:(h + 1) * HGRN_DIM] = o * (gx * jax.nn.sigmoid(gx))

    def sub_block_queries(q, b):
        return [jnp.where(rowblk == i, q * jnp.exp(jnp.minimum(b - b[i * S:i * S + 1, :], 0.0)), 0.0)
                for i in range(nsub)]

    @pl.when(jnp.logical_not(risky))
    def _():
        att = {}
        for c, h in pairs:
            q, k, b = view(q_scr, c, h), view(k_scr, c, h), view(b_scr, c, h)
            k_sub = [jnp.where(rowblk <= i, k * jnp.exp(jnp.minimum(b[i * S:i * S + 1, :] - b, guard)), 0.0)
                     for i in range(nsub)]
            att[c, h] = lax.dot_general(jnp.concatenate(sub_block_queries(q, b), axis=1).astype(BF16),
                                        jnp.concatenate(k_sub, axis=1).astype(BF16), nt,
                                        preferred_element_type=F32)
        finish({(c, h): jnp.dot(jnp.where(r_i >= c_i, att[c, h], 0.0).astype(BF16),
                                view(i_ref, c, h).astype(BF16), preferred_element_type=F32) for c, h in pairs})

    @pl.when(risky)
    def _():
        sub_r = lax.broadcasted_iota(jnp.int32, (S, HGRN_DIM), 0)
        intra = {}
        for c, h in pairs:
            q, k, b, v = view(q_scr, c, h), view(k_scr, c, h), view(b_scr, c, h), view(i_ref, c, h)
            k_sub = [jnp.where(rowblk < i, k * jnp.exp(jnp.minimum(b[i * S:i * S + 1, :] - b, 0.0)), 0.0)
                     for i in range(1, nsub)]
            att = lax.dot_general(jnp.concatenate(sub_block_queries(q, b)[1:], axis=1).astype(BF16),
                                  jnp.concatenate(k_sub, axis=1).astype(BF16), nt, preferred_element_type=F32)
            diag = []
            for i in range(nsub):
                b_blk = b[i * S:(i + 1) * S, :]
                q_blk = q[i * S:(i + 1) * S, :]
                acc = jnp.zeros((S, HGRN_DIM), F32)
                for s_ in range(S):
                    r = i * S + s_
                    e = jnp.exp(jnp.minimum(b_blk - b[r:r + 1, :], 0.0))
                    a = jnp.where(sub_r >= s_, q_blk * e * k[r:r + 1, :], 0.0)
                    acc = acc + jnp.sum(a, axis=-1, keepdims=True) * v[r:r + 1, :]
                diag.append(acc)
            intra[c, h] = (jnp.dot(att.astype(BF16), v.astype(BF16), preferred_element_type=F32)
                           + jnp.concatenate(diag, axis=0))
        finish(intra)


def _hgrn(hq, hf, hi, hg, lb, og, batch, seq, guard=HGRN_EXP_GUARD):
    shp = (batch, seq, HGRN_WIDTH)
    blk = pl.BlockSpec((None, HGRN_STEP, HGRN_WIDTH), lambda b, c: (b, c, 0))
    vec = pl.BlockSpec((1, HGRN_WIDTH), lambda b, c: (0, 0))
    out = pl.pallas_call(
        functools.partial(_hgrn_kernel, guard),
        grid=(batch, seq // HGRN_STEP),
        in_specs=[blk, blk, blk, blk, vec, vec],
        out_specs=blk,
        out_shape=jax.ShapeDtypeStruct(shp, F32),
        scratch_shapes=[pltpu.VMEM((HGRN_HEADS, HGRN_DIM, HGRN_DIM), F32)]
        + [pltpu.VMEM((HGRN_STEP, HGRN_WIDTH), F32)] * 3,
        compiler_params=_params("parallel", "arbitrary"),
        name="hgrn",
    )(hq.reshape(shp), hf.reshape(shp), hi.reshape(shp), hg.reshape(shp), lb, og)
    return out.reshape(batch * seq, HGRN_WIDTH)


def _outproj_kernel(attn_ref, hg_ref, x_ref, wo_ref, bo_ref, g2_ref, rw_ref, rb_ref,
                    x1_ref, h2_ref, lpos_ref, gate_ref, seg_ref):
    tm = SORT_TILE
    tiles = range(OUTPROJ_TILES)
    y = jnp.dot(attn_ref[...].astype(BF16), wo_ref[:ATTN_WIDTH, :], preferred_element_type=F32)
    y = y + jnp.dot(hg_ref[...].astype(BF16), wo_ref[ATTN_WIDTH:, :], preferred_element_type=F32)
    x1 = x_ref[...] + y + bo_ref[...]
    x1_ref[...] = x1
    h2 = _rms(x1, g2_ref[...])
    h2_ref[...] = h2.astype(BF16)
    nt = (((1,), (1,)), ((), ()))
    h_hi = h2.astype(BF16)
    h_lo = (h2 - h_hi.astype(F32)).astype(BF16)
    work = []
    for u in tiles:
        p_hi = lax.dot_general(rw_ref[...], h_hi[u * tm:(u + 1) * tm], nt, preferred_element_type=F32)
        p_lo = lax.dot_general(rw_ref[...], h_lo[u * tm:(u + 1) * tm], nt, preferred_element_type=F32)
        work.append((p_hi[:N_EXPERTS] + p_hi[LANES:LANES + N_EXPERTS]) + p_lo[:N_EXPERTS] + rb_ref[...])
    eid = lax.broadcasted_iota(jnp.int32, (N_EXPERTS, tm), 0).astype(F32)
    vals = [[] for _ in tiles]
    hots = [[] for _ in tiles]
    for _ in range(TOP_K):
        for u in tiles:
            m = jnp.max(work[u], axis=0, keepdims=True)
            idx = jnp.min(jnp.where(work[u] == m, eid, float(N_EXPERTS)), axis=0, keepdims=True)
            hot = eid == idx
            vals[u].append(m)
            hots[u].append(hot)
            work[u] = jnp.where(hot, -jnp.inf, work[u])
    t_r = lax.broadcasted_iota(jnp.int32, (tm, tm), 0)
    t_c = lax.broadcasted_iota(jnp.int32, (tm, tm), 1)
    earlier = jnp.where(t_r < t_c, 1.0, 0.0).astype(BF16)
    e_r = lax.broadcasted_iota(jnp.int32, (N_EXPERTS, N_EXPERTS), 0)
    e_c = lax.broadcasted_iota(jnp.int32, (N_EXPERTS, N_EXPERTS), 1)
    before = jnp.where(e_c < e_r, 1.0, 0.0).astype(BF16)

    def round_up(n):
        return jnp.floor((n + (SEG_ROWS - 1.0)) * (1.0 / SEG_ROWS)) * SEG_ROWS

    sel = [sum(jnp.where(hot, 1.0, 0.0) for hot in hots[u]) for u in tiles]
    ranks = [jnp.dot(sel[u].astype(BF16), earlier, preferred_element_type=F32) for u in tiles]
    seg_col = [round_up(jnp.sum(sel[u], axis=1, keepdims=True)) for u in tiles]
    seg_off = [jnp.dot(before, jnp.broadcast_to(seg_col[u], (N_EXPERTS, LANES)).astype(BF16),
                       preferred_element_type=F32)[:, 0:1] for u in tiles]
    seg_row = [round_up(lax.dot_general(jnp.ones((SEG_ROWS, tm), BF16), sel[u].astype(BF16), nt,
                                        preferred_element_type=F32)[0:1, :]) for u in tiles]
    for u in tiles:
        slots = ranks[u] + seg_off[u]
        ex = [jnp.exp(v - vals[u][0]) for v in vals[u]]
        den = ex[0] + ex[1] + ex[2] + ex[3]
        lpos_ref[:, u * tm:(u + 1) * tm] = jnp.concatenate(
            [jnp.sum(jnp.where(hot, slots, 0.0), axis=0, keepdims=True) for hot in hots[u]],
            axis=0).astype(jnp.int32)
        gate_ref[:, u * tm:(u + 1) * tm] = jnp.concatenate([e / den for e in ex], axis=0)
        seg_ref[u] = seg_row[u]


def _outproj(attn, hgo, x2, wo_bf, bo, g2, rw, rb):
    t = x2.shape[0]
    step = OUTPROJ_TILES * SORT_TILE
    row = lambda i: (i, 0)
    col = lambda i: (0, i)
    fixed = lambda i: (0, 0)
    return pl.pallas_call(
        _outproj_kernel,
        grid=(t // step,),
        in_specs=[pl.BlockSpec((step, ATTN_WIDTH), row),
                  pl.BlockSpec((step, HGRN_WIDTH), row),
                  pl.BlockSpec((step, D_MODEL), row),
                  pl.BlockSpec((ATTN_WIDTH + HGRN_WIDTH, D_MODEL), fixed),
                  pl.BlockSpec((1, D_MODEL), fixed),
                  pl.BlockSpec((1, D_MODEL), fixed),
                  pl.BlockSpec((2 * LANES, D_MODEL), fixed),
                  pl.BlockSpec((N_EXPERTS, 1), fixed)],
        out_specs=[pl.BlockSpec((step, D_MODEL), row),
                   pl.BlockSpec((step, D_MODEL), row),
                   pl.BlockSpec((TOP_K, step), col),
                   pl.BlockSpec((TOP_K, step), col),
                   pl.BlockSpec((OUTPROJ_TILES, 1, N_EXPERTS), lambda i: (i, 0, 0))],
        out_shape=[jax.ShapeDtypeStruct((t, D_MODEL), F32),
                   jax.ShapeDtypeStruct((t, D_MODEL), BF16),
                   jax.ShapeDtypeStruct((TOP_K, t), jnp.int32),
                   jax.ShapeDtypeStruct((TOP_K, t), F32),
                   jax.ShapeDtypeStruct((t // SORT_TILE, 1, N_EXPERTS), F32)],
        compiler_params=_params("parallel"),
        name="outproj",
    )(attn, hgo, x2, wo_bf, bo, g2, rw, rb)


def _rows(start, n):
    return pl.ds(pl.multiple_of(start, SEG_ROWS), n)


def _repeat(cnt, fn):
    def body(j, carry):
        fn(j)
        return carry

    lax.fori_loop(0, cnt, body, 0)


BIG_CAP = SORT_SLOTS // BIG_PIECE
SMALL_CAP = N_EXPERTS * (BIG_PIECE // SEG_ROWS - 1)


def _tile_pieces(lists, step, fn):
    big_loc, big_glob, big_cnt, small_loc, small_glob, small_cnt = lists
    b0 = step * BIG_CAP
    _repeat(big_cnt[step], lambda j: fn(big_loc[b0 + j], big_glob[b0 + j], BIG_PIECE))
    s0 = step * SMALL_CAP
    _repeat(small_cnt[step], lambda j: fn(small_loc[s0 + j], small_glob[s0 + j], SEG_ROWS))


def _tile_piece_waits(lists, step, wait):
    _repeat(lists[2][step], lambda j: wait(BIG_PIECE))
    _repeat(lists[5][step], lambda j: wait(SEG_ROWS))


def _dispatch_kernel(*refs):
    lists, (zdst_ref, zbig_ref, zsmall_ref) = refs[:6], refs[6:9]
    h_ref, lpos_t_ref, xs_ref, lbuf_ref, zbuf_ref, sem, zsem = refs[9:]
    i = pl.program_id(0)
    last = pl.num_programs(0) - 1
    buf = i % 2

    def piece(b, local, glob, n):
        return pltpu.make_async_copy(lbuf_ref.at[b, _rows(local, n), :], xs_ref.at[_rows(glob, n), :], sem.at[b])

    def drain(step):
        _tile_piece_waits(lists, step, lambda n: piece(step % 2, 0, 0, n).wait())

    def zero_piece(glob, n):
        return pltpu.make_async_copy(zbuf_ref.at[_rows(0, n), :], xs_ref.at[_rows(glob, n), :], zsem)

    @pl.when(i == 0)
    def _():
        zbuf_ref[...] = jnp.zeros_like(zbuf_ref)
        for e in range(N_EXPERTS + 1):
            _repeat(zbig_ref[e], lambda j, e=e: zero_piece(zdst_ref[e] + j * BIG_PIECE, BIG_PIECE).start())
            _repeat(zsmall_ref[e], lambda j, e=e: zero_piece(
                zdst_ref[e] + zbig_ref[e] * BIG_PIECE + j * SEG_ROWS, SEG_ROWS).start())
        for e in range(N_EXPERTS + 1):
            _repeat(zbig_ref[e], lambda j: zero_piece(0, BIG_PIECE).wait())
            _repeat(zsmall_ref[e], lambda j: zero_piece(0, SEG_ROWS).wait())

    @pl.when(i >= 2)
    def _():
        drain(i - 2)

    slot = lax.broadcasted_iota(jnp.int32, (SORT_SLOTS, SORT_TILE), 0)
    onehot = jnp.zeros((SORT_SLOTS, SORT_TILE), F32)
    for k in range(TOP_K):
        onehot = onehot + jnp.where(slot == lpos_t_ref[k:k + 1, :], 1.0, 0.0)
    lbuf_ref[buf] = jnp.dot(onehot.astype(BF16), h_ref[...], preferred_element_type=F32)
    _tile_pieces(lists, i, lambda local, glob, n: piece(buf, local, glob, n).start())

    @pl.when(i == last)
    def _():
        @pl.when(i >= 1)
        def _():
            drain(i - 1)

        drain(i)


def _dispatch(tables, zero_tables, h2, lpos_t, n_rows):
    t = h2.shape[0]
    tile = lambda i, *_: (i, 0)
    return pl.pallas_call(
        _dispatch_kernel,
        grid_spec=pltpu.PrefetchScalarGridSpec(
            num_scalar_prefetch=9,
            grid=(t // SORT_TILE,),
            in_specs=[pl.BlockSpec((SORT_TILE, D_MODEL), tile),
                      pl.BlockSpec((TOP_K, SORT_TILE), lambda i, *_: (0, i))],
            out_specs=pl.BlockSpec(memory_space=pl.ANY),
            scratch_shapes=[pltpu.VMEM((2, SORT_SLOTS, D_MODEL), F32),
                            pltpu.VMEM((BIG_PIECE, D_MODEL), F32),
                            pltpu.SemaphoreType.DMA((2,)),
                            pltpu.SemaphoreType.DMA(())]),
        out_shape=jax.ShapeDtypeStruct((n_rows, D_MODEL), F32),
        compiler_params=_params("arbitrary"),
        name="dispatch",
    )(*tables, *zero_tables, h2, lpos_t)


CAST_ROWS = 128


def _expert_kernel(be_ref, nb_ref, eb_ref, filled_ref,
                   xs_ref, w1_hbm, b1_ref, w2_hbm, b2_ref, y_ref,
                   w1f_ref, w2f_ref, w1b_ref, w2i_ref, w2b_ref, started_ref, sem):
    blk = pl.program_id(0)
    half = LANES // 2

    def weight_copies(e, s):
        return (pltpu.make_async_copy(w1_hbm.at[e], w1f_ref.at[s], sem.at[0, s]),
                pltpu.make_async_copy(w2_hbm.at[e], w2f_ref.at[s], sem.at[1, s]))

    e = be_ref[blk]
    first_of_expert = (blk < nb_ref[0]) & ((blk == 0) | (e != be_ref[jnp.maximum(blk - 1, 0)]))

    @pl.when(first_of_expert)
    def _():
        done = jnp.where(blk == 0, 0, started_ref[0])
        started_ref[0] = done + 1
        s = done % 2
        next_blk = blk + eb_ref[e]

        @pl.when(blk == 0)
        def _():
            for c in weight_copies(e, s):
                c.start()

        @pl.when(next_blk < nb_ref[0])
        def _():
            for c in weight_copies(be_ref[next_blk], 1 - s):
                c.start()

        for c in weight_copies(e, s):
            c.wait()

        def cast_rows(r, carry):
            rows = pl.ds(pl.multiple_of(r * CAST_ROWS, CAST_ROWS), CAST_ROWS)
            w1b_ref[rows, :] = w1f_ref[s, rows, :].astype(BF16)
            return carry

        lax.fori_loop(0, D_MODEL // CAST_ROWS, cast_rows, 0)
        for c in range(D_MODEL // LANES):
            cols = slice(c * LANES, (c + 1) * LANES)
            for m in range(EXPERT_FF // LANES):
                lo = m * LANES
                w2i_ref[c, pl.ds(lo, half, stride=2), :] = w2f_ref[s, lo:lo + half, cols]
                w2i_ref[c, pl.ds(lo + 1, half, stride=2), :] = w2f_ref[s, lo + half:lo + LANES, cols]
            w2b_ref[:, cols] = w2i_ref[c].astype(BF16)

    def ffn(rows):
        x = xs_ref[:rows, :].astype(BF16)
        hid = jnp.dot(x, w1b_ref[...], preferred_element_type=F32) + b1_ref[...]
        even = (lax.broadcasted_iota(jnp.int32, (rows, LANES), 1) & 1) == 0
        glu, lin = [], []
        for m in range(EXPERT_FF // LANES):
            ha = hid[:, 2 * m * LANES:(2 * m + 1) * LANES]
            hb = hid[:, (2 * m + 1) * LANES:(2 * m + 2) * LANES]
            glu.append(jnp.where(even, ha, pltpu.roll(hb, 1, axis=1)))
            lin.append(jnp.where(even, pltpu.roll(ha, LANES - 1, axis=1), hb))
        glu = jnp.minimum(jnp.concatenate(glu, axis=1), SWIGLU_LIMIT)
        lin = jnp.clip(jnp.concatenate(lin, axis=1), -SWIGLU_LIMIT, SWIGLU_LIMIT)
        act = glu * jax.nn.sigmoid(SWIGLU_ALPHA * glu) * (lin + 1.0)
        y_ref[:rows, :] = jnp.dot(act.astype(BF16), w2b_ref[...], preferred_element_type=F32) + b2_ref[...]

    quarter = EXPERT_BLOCK // 4
    filled_quarters = (filled_ref[blk] + quarter - 1) // quarter
    for nq in range(1, 5):
        @pl.when(filled_quarters == nq)
        def _(rows=nq * quarter):
            ffn(rows)
            if rows < EXPERT_BLOCK:
                y_ref[rows:, :] = jnp.zeros((EXPERT_BLOCK - rows, D_MODEL), F32)

    @pl.when(filled_quarters == 0)
    def _():
        y_ref[...] = jnp.zeros_like(y_ref)


def _experts(block_tables, xs, w1, b1, w2, b2):
    n_rows = xs.shape[0]
    nblk = n_rows // EXPERT_BLOCK
    rows = lambda b, *_: (b, 0)
    used_rows = lambda b, be, nb, *_: (jnp.minimum(b, nb[0] - 1), 0)
    bias = lambda b, be, *_: (be[b], 0, 0)
    return pl.pallas_call(
        _expert_kernel,
        grid_spec=pltpu.PrefetchScalarGridSpec(
            num_scalar_prefetch=4,
            grid=(nblk,),
            in_specs=[pl.BlockSpec((EXPERT_BLOCK, D_MODEL), used_rows),
                      pl.BlockSpec(memory_space=pl.ANY),
                      pl.BlockSpec((None, 1, 2 * EXPERT_FF), bias),
                      pl.BlockSpec(memory_space=pl.ANY),
                      pl.BlockSpec((None, 1, D_MODEL), bias)],
            out_specs=pl.BlockSpec((EXPERT_BLOCK, D_MODEL), rows),
            scratch_shapes=[pltpu.VMEM((2, D_MODEL, 2 * EXPERT_FF), F32),
                            pltpu.VMEM((2, EXPERT_FF, D_MODEL), F32),
                            pltpu.VMEM((D_MODEL, 2 * EXPERT_FF), BF16),
                            pltpu.VMEM((D_MODEL // LANES, EXPERT_FF, LANES), F32),
                            pltpu.VMEM((EXPERT_FF, D_MODEL), BF16),
                            pltpu.SMEM((1,), jnp.int32),
                            pltpu.SemaphoreType.DMA((2, 2))]),
        out_shape=jax.ShapeDtypeStruct((n_rows, D_MODEL), F32),
        compiler_params=_params("arbitrary"),
        name="experts",
    )(*block_tables, xs, w1, b1, w2, b2)


def _combine_kernel(final_norm, *refs):
    lists = refs[:6]
    yb_ref, x1_ref, lpos_ref, gate_ref, g_ref, o_ref, gbuf_ref, sem = refs[6:]
    i = pl.program_id(0)
    last = pl.num_programs(0) - 1
    buf = i % 2

    def piece(b, local, glob, n):
        return pltpu.make_async_copy(yb_ref.at[_rows(glob, n), :], gbuf_ref.at[b, _rows(local, n), :], sem.at[b])

    def fetch(step):
        _tile_pieces(lists, step, lambda local, glob, n: piece(step % 2, local, glob, n).start())

    @pl.when(i == 0)
    def _():
        gbuf_ref[...] = jnp.zeros_like(gbuf_ref)
        fetch(0)

    @pl.when(i < last)
    def _():
        fetch(i + 1)

    _tile_piece_waits(lists, i, lambda n: piece(buf, 0, 0, n).wait())
    slot = lax.broadcasted_iota(jnp.int32, (SORT_TILE, SORT_SLOTS), 1)
    lpos = lpos_ref[...]
    gates = gate_ref[...]
    weights = jnp.zeros((SORT_TILE, SORT_SLOTS), F32)
    for k in range(TOP_K):
        weights = weights + jnp.where(slot == lpos[:, k:k + 1], gates[:, k:k + 1], 0.0)
    y = x1_ref[...] + jnp.dot(weights.astype(BF16), gbuf_ref[buf].astype(BF16), preferred_element_type=F32)
    o_ref[...] = _rms(y, g_ref[...]) if final_norm else y


def _combine(tables, yb, x1, lpos, gates, g, final_norm):
    t = x1.shape[0]
    tile = lambda i, *_: (i, 0)
    return pl.pallas_call(
        functools.partial(_combine_kernel, final_norm),
        grid_spec=pltpu.PrefetchScalarGridSpec(
            num_scalar_prefetch=6,
            grid=(t // SORT_TILE,),
            in_specs=[pl.BlockSpec(memory_space=pl.ANY),
                      pl.BlockSpec((SORT_TILE, D_MODEL), tile),
                      pl.BlockSpec((SORT_TILE, TOP_K), tile),
                      pl.BlockSpec((SORT_TILE, TOP_K), tile),
                      pl.BlockSpec((1, D_MODEL), lambda i, *_: (0, 0))],
            out_specs=pl.BlockSpec((SORT_TILE, D_MODEL), tile),
            scratch_shapes=[pltpu.VMEM((2, SORT_SLOTS, D_MODEL), F32),
                            pltpu.SemaphoreType.DMA((2,))]),
        out_shape=jax.ShapeDtypeStruct((t, D_MODEL), F32),
        compiler_params=_params("arbitrary"),
        name="combine",
    )(*tables, yb, x1, lpos, gates, g)


def _piece_lists(seg_loc, seg_glob, cnt, skip, rows, cap):
    first = jnp.cumsum(cnt, axis=1) - cnt
    p = jnp.arange(cap, dtype=jnp.int32)
    started = first[:, None, :] <= p[None, :, None]

    def per_piece(rows0):
        base = rows0 + skip - first * rows
        step = base - jnp.concatenate([jnp.zeros_like(base[:, :1]), base[:, :-1]], axis=1)
        return (jnp.sum(jnp.where(started, step[:, None, :], 0), axis=2) + p[None, :] * rows).reshape(-1)

    return per_piece(seg_loc), per_piece(seg_glob), jnp.sum(cnt, axis=1)


def kernel(x, norm1_g, w_in, b_in, attn_sinks, attn_out_g, hgrn_lb_logits, hgrn_out_g, w_out, b_out,
           norm2_g, router_w, router_b, w1, b1, w2, b2, final_g):
    batch, seq, d = x.shape
    t = batch * seq
    depth = w_in.shape[0]
    lower_bounds = jnp.cumsum(jax.nn.softmax(hgrn_lb_logits.astype(F32), axis=0), axis=0)
    assert t % SORT_TILE == 0 and SORT_SLOTS >= SORT_TILE * TOP_K + N_EXPERTS * (SEG_ROWS - 1)
    ntiles = t // SORT_TILE
    nblk = (t * TOP_K + ntiles * N_EXPERTS * (SEG_ROWS - 1)) // EXPERT_BLOCK + N_EXPERTS
    n_rows = nblk * EXPERT_BLOCK
    x2 = x.reshape(t, d)
    for l in range(depth):
        w_in_l = jnp.concatenate([_pair_heads(w_in[l][:, :ATTN_WIDTH], 1), w_in[l][:, ATTN_WIDTH:]], axis=1)
        b_in_l = jnp.concatenate([_pair_heads(b_in[l][:ATTN_WIDTH], 0), b_in[l][ATTN_WIDTH:]])
        aq, ak, av, hq, hf, hi, hg = _inproj(x2, norm1_g[l][None], w_in_l.astype(BF16), b_in_l[None])
        attn = _attention(aq, ak, av, attn_sinks[l], _pair_heads(attn_out_g[l], 0)[None], batch, seq)
        hgo = _hgrn(hq, hf, hi, hg, lower_bounds[l][None], hgrn_out_g[l][None], batch, seq)
        w_out_l = jnp.concatenate([_pair_heads(w_out[l][:ATTN_WIDTH], 0), w_out[l][ATTN_WIDTH:]], axis=0)
        rw_hi = router_w[l].astype(BF16)
        rw_lo = (router_w[l] - rw_hi.astype(F32)).astype(BF16)
        rw_cat = jnp.zeros((2 * LANES, d), BF16)
        rw_cat = rw_cat.at[:N_EXPERTS].set(rw_hi.T).at[LANES:LANES + N_EXPERTS].set(rw_lo.T)
        x1, h2, lpos_t, gates_t, seg = _outproj(
            attn, hgo, x2, w_out_l.astype(BF16), b_out[l][None], norm2_g[l][None],
            rw_cat, router_b[l][:, None])
        seg = seg.reshape(ntiles, N_EXPERTS).astype(jnp.int32)
        rows_e = jnp.sum(seg, axis=0)
        padded = (rows_e + EXPERT_BLOCK - 1) // EXPERT_BLOCK * EXPERT_BLOCK
        ends = jnp.cumsum(padded)
        pstart = ends - padded
        seg_glob = pstart[None, :] + jnp.cumsum(seg, axis=0) - seg
        seg_loc = jnp.cumsum(seg, axis=1) - seg
        tail = jnp.append(padded - rows_e, n_rows - ends[-1])
        zero_tables = (jnp.append(pstart + rows_e, ends[-1]), tail // BIG_PIECE, tail % BIG_PIECE // SEG_ROWS)
        big = seg // BIG_PIECE
        tables = (_piece_lists(seg_loc, seg_glob, big, jnp.zeros_like(seg), BIG_PIECE, BIG_CAP)
                  + _piece_lists(seg_loc, seg_glob, seg % BIG_PIECE // SEG_ROWS, big * BIG_PIECE, SEG_ROWS,
                                 SMALL_CAP))
        blk_ids = jnp.arange(nblk, dtype=jnp.int32)
        block_e = jnp.minimum(jnp.sum(blk_ids[:, None] * EXPERT_BLOCK >= ends[None, :], axis=-1), N_EXPERTS - 1)
        in_expert = rows_e[None, :] - (blk_ids[:, None] * EXPERT_BLOCK - pstart[None, :])
        filled = jnp.sum(jnp.where((block_e[:, None] == jnp.arange(N_EXPERTS)[None, :])
                                   & (blk_ids[:, None] * EXPERT_BLOCK < ends[-1]),
                                   jnp.clip(in_expert, 0, EXPERT_BLOCK), 0), axis=1)
        block_tables = (block_e.astype(jnp.int32), (ends[-1:] // EXPERT_BLOCK).astype(jnp.int32),
                        (padded // EXPERT_BLOCK).astype(jnp.int32), filled.astype(jnp.int32))
        xs = _dispatch(tables, zero_tables, h2, lpos_t, n_rows)
        yb = _experts(block_tables, xs, w1[l], b1[l][:, None, :], w2[l], b2[l][:, None, :])
        x2 = _combine(tables, yb, x1, lpos_t.T, gates_t.T, final_g[None], l == depth - 1)
    return x2.reshape(batch, seq, d)
```

```python
import functools

import numpy as np
import jax
import jax.numpy as jnp
from jax import lax
from jax.experimental import pallas as pl
from jax.experimental.pallas import tpu as pltpu

F32 = jnp.float32
BF16 = jnp.bfloat16

D_MODEL = 1024
ATTN_Q_HEADS = 8
ATTN_KV_HEADS = 2
ATTN_HEAD_DIM = 64
ATTN_GROUP = ATTN_Q_HEADS // ATTN_KV_HEADS
ATTN_WIDTH = ATTN_Q_HEADS * ATTN_HEAD_DIM
ATTN_KV_WIDTH = ATTN_KV_HEADS * ATTN_HEAD_DIM
WINDOW = 128
ATTN_STEP_BLOCKS = 4
HGRN_HEADS = 4
HGRN_DIM = 128
HGRN_WIDTH = HGRN_HEADS * HGRN_DIM
HGRN_CHUNK = 64
HGRN_SUB = 16
HGRN_STEP = 512
IN_WIDTH = ATTN_WIDTH + 2 * ATTN_KV_WIDTH + 4 * HGRN_WIDTH
N_EXPERTS = 32
TOP_K = 4
EXPERT_FF = D_MODEL
SWIGLU_LIMIT = 7.0
SWIGLU_ALPHA = 1.702
NORM_EPS = 1e-5

LANES = 128
ROW_TILE = 1024
EXPERT_BLOCK = 512
SORT_TILE = 256
OUTPROJ_TILES = 4
SEG_ROWS = 8
BIG_PIECE = 32
SORT_SLOTS = 1280
VMEM_LIMIT = 56 * 1024 * 1024

_ALIBI = [float(2.0 ** (-8.0 * (h + 1) / ATTN_Q_HEADS)) for h in range(ATTN_Q_HEADS)]


def _rms(x, g):
    return x * lax.rsqrt(jnp.mean(x * x, axis=-1, keepdims=True) + NORM_EPS) * g


def _params(*sem):
    return pltpu.CompilerParams(dimension_semantics=sem, vmem_limit_bytes=VMEM_LIMIT)


_IN_SPLITS = (ATTN_WIDTH, ATTN_KV_WIDTH, ATTN_KV_WIDTH, HGRN_WIDTH, HGRN_WIDTH, HGRN_WIDTH, HGRN_WIDTH)
_IN_DTYPES = (BF16, BF16, BF16, F32, F32, F32, F32)


def _inproj_kernel(x_ref, g_ref, w_ref, b_ref, *out_refs):
    h = _rms(x_ref[...], g_ref[...]).astype(BF16)
    lo = 0
    for ref, width in zip(out_refs, _IN_SPLITS):
        out = jnp.dot(h, w_ref[:, lo:lo + width], preferred_element_type=F32) + b_ref[:, lo:lo + width]
        ref[...] = out.astype(ref.dtype)
        lo += width


def _inproj(x2, g, w_bf, b):
    t = x2.shape[0]
    row = lambda i: (i, 0)
    fixed = lambda i: (0, 0)
    return pl.pallas_call(
        _inproj_kernel,
        grid=(t // ROW_TILE,),
        in_specs=[pl.BlockSpec((ROW_TILE, D_MODEL), row),
                  pl.BlockSpec((1, D_MODEL), fixed),
                  pl.BlockSpec((D_MODEL, IN_WIDTH), fixed),
                  pl.BlockSpec((1, IN_WIDTH), fixed)],
        out_specs=[pl.BlockSpec((ROW_TILE, w), row) for w in _IN_SPLITS],
        out_shape=[jax.ShapeDtypeStruct((t, w), dt) for w, dt in zip(_IN_SPLITS, _IN_DTYPES)],
        compiler_params=_params("parallel"),
        name="inproj",
    )(x2, g, w_bf, b)


def _attn_bias_tables():
    qi = np.arange(WINDOW)[:, None]
    ki = np.arange(2 * WINDOW)[None, :]
    dist = WINDOW + qi - ki
    in_window = (dist >= 0) & (dist < WINDOW)
    slopes = np.asarray(_ALIBI, np.float32)[:, None, None]
    bias = -(slopes * dist.astype(np.float32)[None])
    tables = [np.where(in_window & (ki >= WINDOW), bias, -np.inf), np.where(in_window, bias, -np.inf)]
    return np.stack(tables).astype(np.float32)


def _pair_heads(a, axis):
    shape = a.shape
    a = a.reshape(shape[:axis] + (ATTN_KV_HEADS, ATTN_GROUP, ATTN_HEAD_DIM) + shape[axis + 1:])
    return jnp.swapaxes(a, axis, axis + 1).reshape(shape)


def _attn_kernel(sink_ref, q_ref, kp_ref, kc_ref, vp_ref, vc_ref, bias_ref, g_ref, o_ref):
    n = pl.program_id(1)
    kc = kc_ref[...].astype(BF16)
    vc = vc_ref[...].astype(BF16)
    keys = [jnp.concatenate([kp_ref[...].astype(BF16), kc[:WINDOW]], axis=0)]
    vals = [jnp.concatenate([vp_ref[...].astype(BF16), vc[:WINDOW]], axis=0)]
    for i in range(1, ATTN_STEP_BLOCKS):
        keys.append(kc[(i - 1) * WINDOW:(i + 1) * WINDOW])
        vals.append(vc[(i - 1) * WINDOW:(i + 1) * WINDOW])
    tables = [jnp.minimum(n, 1)] + [1] * (ATTN_STEP_BLOCKS - 1)
    scale = ATTN_HEAD_DIM ** -0.5
    nt = (((1,), (1,)), ((), ()))
    low = lax.broadcasted_iota(jnp.int32, (WINDOW, LANES), 1) < ATTN_HEAD_DIM

    def softmax(s, table, h):
        s = s * scale + bias_ref[table, h]
        sink = sink_ref[h]
        m = jnp.maximum(jnp.max(s, axis=-1, keepdims=True), sink)
        p = jnp.exp(s - m)
        den = jnp.sum(p, axis=-1, keepdims=True) + jnp.exp(sink - m)
        return (p / den).astype(BF16)

    items = [(i, j) for i in range(ATTN_STEP_BLOCKS) for j in range(ATTN_GROUP)]
    scores = {}
    for i, j in items:
        q = q_ref[i * WINDOW:(i + 1) * WINDOW, j * LANES:(j + 1) * LANES]
        scores[i, j] = (
            lax.dot_general(jnp.where(low, q, 0.0).astype(BF16), keys[i], nt, preferred_element_type=F32),
            lax.dot_general(jnp.where(low, 0.0, q).astype(BF16), keys[i], nt, preferred_element_type=F32))
    probs = {(i, j): (softmax(scores[i, j][0], tables[i], j), softmax(scores[i, j][1], tables[i], ATTN_GROUP + j))
             for i, j in items}
    for i in range(ATTN_STEP_BLOCKS):
        outs = [jnp.where(low, jnp.dot(probs[i, j][0], vals[i], preferred_element_type=F32),
                          jnp.dot(probs[i, j][1], vals[i], preferred_element_type=F32))
                for j in range(ATTN_GROUP)]
        o_ref[i * WINDOW:(i + 1) * WINDOW, :] = _rms(jnp.concatenate(outs, axis=1), g_ref[...])


def _attention(aq, ak, av, sinks, g, batch, seq):
    step = ATTN_STEP_BLOCKS * WINDOW
    aq = aq.reshape(batch, seq, ATTN_WIDTH)
    ak = ak.reshape(batch, seq, ATTN_KV_WIDTH)
    av = av.reshape(batch, seq, ATTN_KV_WIDTH)
    cur = lambda b, n, s: (b, n, 0)
    prev = lambda b, n, s: (b, jnp.maximum(n * ATTN_STEP_BLOCKS - 1, 0), 0)
    out = pl.pallas_call(
        _attn_kernel,
        grid_spec=pltpu.PrefetchScalarGridSpec(
            num_scalar_prefetch=1,
            grid=(batch, seq // step),
            in_specs=[pl.BlockSpec((None, step, ATTN_WIDTH), cur),
                      pl.BlockSpec((None, WINDOW, ATTN_KV_WIDTH), prev),
                      pl.BlockSpec((None, step, ATTN_KV_WIDTH), cur),
                      pl.BlockSpec((None, WINDOW, ATTN_KV_WIDTH), prev),
                      pl.BlockSpec((None, step, ATTN_KV_WIDTH), cur),
                      pl.BlockSpec((2, ATTN_Q_HEADS, WINDOW, 2 * WINDOW), lambda b, n, s: (0, 0, 0, 0)),
                      pl.BlockSpec((1, ATTN_WIDTH), lambda b, n, s: (0, 0))],
            out_specs=pl.BlockSpec((None, step, ATTN_WIDTH), cur)),
        out_shape=jax.ShapeDtypeStruct((batch, seq, ATTN_WIDTH), F32),
        compiler_params=_params("parallel", "parallel"),
        name="attn",
    )(sinks, aq, ak, ak, av, av, jnp.asarray(_attn_bias_tables()), g)
    return out.reshape(batch * seq, ATTN_WIDTH)


HGRN_EXP_GUARD = 80.0


def _hgrn_kernel(guard, q_ref, f_ref, i_ref, gate_ref, lb_ref, og_ref, o_ref, st_ref, q_scr, k_scr, b_scr):
    step = pl.program_id(1)

    @pl.when(step == 0)
    def _():
        st_ref[...] = jnp.zeros_like(st_ref)

    C, S = HGRN_CHUNK, HGRN_SUB
    nsub = C // S
    nchunk = HGRN_STEP // C
    r_i = lax.broadcasted_iota(jnp.int32, (C, C), 0)
    c_i = lax.broadcasted_iota(jnp.int32, (C, C), 1)
    nt = (((1,), (1,)), ((), ()))

    tri = (r_i >= c_i).astype(F32)
    qx = q_ref[...]
    q_scr[...] = qx * jax.nn.sigmoid(qx)
    lb = lb_ref[...]
    f = lb + (1.0 - lb) * jax.nn.sigmoid(f_ref[...])
    k_scr[...] = 1.0 - f
    logf = jnp.log(f)
    for c in range(nchunk):
        b_scr[c * C:(c + 1) * C, :] = jnp.dot(tri, logf[c * C:(c + 1) * C, :], preferred_element_type=F32,
                                              precision=lax.Precision.HIGHEST)
    decay = jnp.concatenate([-jnp.sum(logf[i * S:(i + 1) * S, :], axis=0, keepdims=True)
                             for i in range(HGRN_STEP // S)], axis=0)
    risky = jnp.max(decay) > guard

    rowblk = lax.broadcasted_iota(jnp.int32, (C, HGRN_DIM), 0) // S
    pairs = [(c, h) for c in range(nchunk) for h in range(HGRN_HEADS)]

    def view(ref, c, h):
        return ref[c * C:(c + 1) * C, h * HGRN_DIM:(h + 1) * HGRN_DIM]

    def finish(intra):
        q_dec, gain, update = {}, {}, {}
        for c, h in pairs:
            q, k, b = view(q_scr, c, h), view(k_scr, c, h), view(b_scr, c, h)
            q_dec[c, h] = (q * jnp.exp(b)).astype(BF16)
            bl = b[C - 1:C, :]
            gain[c, h] = jnp.exp(bl)
            update[c, h] = lax.dot_general(view(i_ref, c, h).astype(BF16), (k * jnp.exp(bl - b)).astype(BF16),
                                           (((0,), (0,)), ((), ())), preferred_element_type=F32)
        state = {}
        for h in range(HGRN_HEADS):
            st = st_ref[h]
            for c in range(nchunk):
                state[c, h] = st
                st = st * gain[c, h] + update[c, h]
            st_ref[h] = st
        for c, h in pairs:
            o = intra[c, h] + lax.dot_general(q_dec[c, h], state[c, h].astype(BF16), nt,
                                              preferred_element_type=F32)
            gx = view(gate_ref, c, h)
            og = og_ref[:, h * HGRN_DIM:(h + 1) * HGRN_DIM]
            o = o * lax.rsqrt(jnp.mean(o * o, axis=-1, keepdims=True) + NORM_EPS) * og
            o_ref[c * C:(c + 1) * C, h * HGRN_DIM:(h + 1) * HGRN_DIM] = o * (gx * jax.nn.sigmoid(gx))

    def sub_block_queries(q, b):
        return [jnp.where(rowblk == i, q * jnp.exp(jnp.minimum(b - b[i * S:i * S + 1, :], 0.0)), 0.0)
                for i in range(nsub)]

    @pl.when(jnp.logical_not(risky))
    def _():
        att = {}
        for c, h in pairs:
            q, k, b = view(q_scr, c, h), view(k_scr, c, h), view(b_scr, c, h)
            k_sub = [jnp.where(rowblk <= i, k * jnp.exp(jnp.minimum(b[i * S:i * S + 1, :] - b, guard)), 0.0)
                     for i in range(nsub)]
            att[c, h] = lax.dot_general(jnp.concatenate(sub_block_queries(q, b), axis=1).astype(BF16),
                                        jnp.concatenate(k_sub, axis=1).astype(BF16), nt,
                                        preferred_element_type=F32)
        finish({(c, h): jnp.dot(jnp.where(r_i >= c_i, att[c, h], 0.0).astype(BF16),
                                view(i_ref, c, h).astype(BF16), preferred_element_type=F32) for c, h in pairs})

    @pl.when(risky)
    def _():
        sub_r = lax.broadcasted_iota(jnp.int32, (S, HGRN_DIM), 0)
        intra = {}
        for c, h in pairs:
            q, k, b, v = view(q_scr, c, h), view(k_scr, c, h), view(b_scr, c, h), view(i_ref, c, h)
            k_sub = [jnp.where(rowblk < i, k * jnp.exp(jnp.minimum(b[i * S:i * S + 1, :] - b, 0.0)), 0.0)
                     for i in range(1, nsub)]
            att = lax.dot_general(jnp.concatenate(sub_block_queries(q, b)[1:], axis=1).astype(BF16),
                                  jnp.concatenate(k_sub, axis=1).astype(BF16), nt, preferred_element_type=F32)
            diag = []
            for i in range(nsub):
                b_blk = b[i * S:(i + 1) * S, :]
                q_blk = q[i * S:(i + 1) * S, :]
                acc = jnp.zeros((S, HGRN_DIM), F32)
                for s_ in range(S):
                    r = i * S + s_
                    e = jnp.exp(jnp.minimum(b_blk - b[r:r + 1, :], 0.0))
                    a = jnp.where(sub_r >= s_, q_blk * e * k[r:r + 1, :], 0.0)
                    acc = acc + jnp.sum(a, axis=-1, keepdims=True) * v[r:r + 1, :]
                diag.append(acc)
            intra[c, h] = (jnp.dot(att.astype(BF16), v.astype(BF16), preferred_element_type=F32)
                           + jnp.concatenate(diag, axis=0))
        finish(intra)


def _hgrn(hq, hf, hi, hg, lb, og, batch, seq, guard=HGRN_EXP_GUARD):
    shp = (batch, seq, HGRN_WIDTH)
    blk = pl.BlockSpec((None, HGRN_STEP, HGRN_WIDTH), lambda b, c: (b, c, 0))
    vec = pl.BlockSpec((1, HGRN_WIDTH), lambda b, c: (0, 0))
    out = pl.pallas_call(
        functools.partial(_hgrn_kernel, guard),
        grid=(batch, seq // HGRN_STEP),
        in_specs=[blk, blk, blk, blk, vec, vec],
        out_specs=blk,
        out_shape=jax.ShapeDtypeStruct(shp, F32),
        scratch_shapes=[pltpu.VMEM((HGRN_HEADS, HGRN_DIM, HGRN_DIM), F32)]
        + [pltpu.VMEM((HGRN_STEP, HGRN_WIDTH), F32)] * 3,
        compiler_params=_params("parallel", "arbitrary"),
        name="hgrn",
    )(hq.reshape(shp), hf.reshape(shp), hi.reshape(shp), hg.reshape(shp), lb, og)
    return out.reshape(batch * seq, HGRN_WIDTH)


def _outproj_kernel(attn_ref, hg_ref, x_ref, wo_ref, bo_ref, g2_ref, rw_ref, rb_ref,
                    x1_ref, h2_ref, lpos_ref, gate_ref, seg_ref):
    tm = SORT_TILE
    tiles = range(OUTPROJ_TILES)
    y = jnp.dot(attn_ref[...].astype(BF16), wo_ref[:ATTN_WIDTH, :], preferred_element_type=F32)
    y = y + jnp.dot(hg_ref[...].astype(BF16), wo_ref[ATTN_WIDTH:, :], preferred_element_type=F32)
    x1 = x_ref[...] + y + bo_ref[...]
    x1_ref[...] = x1
    h2 = _rms(x1, g2_ref[...])
    h2_ref[...] = h2.astype(BF16)
    nt = (((1,), (1,)), ((), ()))
    h_hi = h2.astype(BF16)
    h_lo = (h2 - h_hi.astype(F32)).astype(BF16)
    work = []
    for u in tiles:
        p_hi = lax.dot_general(rw_ref[...], h_hi[u * tm:(u + 1) * tm], nt, preferred_element_type=F32)
        p_lo = lax.dot_general(rw_ref[...], h_lo[u * tm:(u + 1) * tm], nt, preferred_element_type=F32)
        work.append((p_hi[:N_EXPERTS] + p_hi[LANES:LANES + N_EXPERTS]) + p_lo[:N_EXPERTS] + rb_ref[...])
    eid = lax.broadcasted_iota(jnp.int32, (N_EXPERTS, tm), 0).astype(F32)
    vals = [[] for _ in tiles]
    hots = [[] for _ in tiles]
    for _ in range(TOP_K):
        for u in tiles:
            m = jnp.max(work[u], axis=0, keepdims=True)
            idx = jnp.min(jnp.where(work[u] == m, eid, float(N_EXPERTS)), axis=0, keepdims=True)
            hot = eid == idx
            vals[u].append(m)
            hots[u].append(hot)
            work[u] = jnp.where(hot, -jnp.inf, work[u])
    t_r = lax.broadcasted_iota(jnp.int32, (tm, tm), 0)
    t_c = lax.broadcasted_iota(jnp.int32, (tm, tm), 1)
    earlier = jnp.where(t_r < t_c, 1.0, 0.0).astype(BF16)
    e_r = lax.broadcasted_iota(jnp.int32, (N_EXPERTS, N_EXPERTS), 0)
    e_c = lax.broadcasted_iota(jnp.int32, (N_EXPERTS, N_EXPERTS), 1)
    before = jnp.where(e_c < e_r, 1.0, 0.0).astype(BF16)

    def round_up(n):
        return jnp.floor((n + (SEG_ROWS - 1.0)) * (1.0 / SEG_ROWS)) * SEG_ROWS

    sel = [sum(jnp.where(hot, 1.0, 0.0) for hot in hots[u]) for u in tiles]
    ranks = [jnp.dot(sel[u].astype(BF16), earlier, preferred_element_type=F32) for u in tiles]
    seg_col = [round_up(jnp.sum(sel[u], axis=1, keepdims=True)) for u in tiles]
    seg_off = [jnp.dot(before, jnp.broadcast_to(seg_col[u], (N_EXPERTS, LANES)).astype(BF16),
                       preferred_element_type=F32)[:, 0:1] for u in tiles]
    seg_row = [round_up(lax.dot_general(jnp.ones((SEG_ROWS, tm), BF16), sel[u].astype(BF16), nt,
                                        preferred_element_type=F32)[0:1, :]) for u in tiles]
    for u in tiles:
        slots = ranks[u] + seg_off[u]
        ex = [jnp.exp(v - vals[u][0]) for v in vals[u]]
        den = ex[0] + ex[1] + ex[2] + ex[3]
        lpos_ref[:, u * tm:(u + 1) * tm] = jnp.concatenate(
            [jnp.sum(jnp.where(hot, slots, 0.0), axis=0, keepdims=True) for hot in hots[u]],
            axis=0).astype(jnp.int32)
        gate_ref[:, u * tm:(u + 1) * tm] = jnp.concatenate([e / den for e in ex], axis=0)
        seg_ref[u] = seg_row[u]


def _outproj(attn, hgo, x2, wo_bf, bo, g2, rw, rb):
    t = x2.shape[0]
    step = OUTPROJ_TILES * SORT_TILE
    row = lambda i: (i, 0)
    col = lambda i: (0, i)
    fixed = lambda i: (0, 0)
    return pl.pallas_call(
        _outproj_kernel,
        grid=(t // step,),
        in_specs=[pl.BlockSpec((step, ATTN_WIDTH), row),
                  pl.BlockSpec((step, HGRN_WIDTH), row),
                  pl.BlockSpec((step, D_MODEL), row),
                  pl.BlockSpec((ATTN_WIDTH + HGRN_WIDTH, D_MODEL), fixed),
                  pl.BlockSpec((1, D_MODEL), fixed),
                  pl.BlockSpec((1, D_MODEL), fixed),
                  pl.BlockSpec((2 * LANES, D_MODEL), fixed),
                  pl.BlockSpec((N_EXPERTS, 1), fixed)],
        out_specs=[pl.BlockSpec((step, D_MODEL), row),
                   pl.BlockSpec((step, D_MODEL), row),
                   pl.BlockSpec((TOP_K, step), col),
                   pl.BlockSpec((TOP_K, step), col),
                   pl.BlockSpec((OUTPROJ_TILES, 1, N_EXPERTS), lambda i: (i, 0, 0))],
        out_shape=[jax.ShapeDtypeStruct((t, D_MODEL), F32),
                   jax.ShapeDtypeStruct((t, D_MODEL), BF16),
                   jax.ShapeDtypeStruct((TOP_K, t), jnp.int32),
                   jax.ShapeDtypeStruct((TOP_K, t), F32),
                   jax.ShapeDtypeStruct((t // SORT_TILE, 1, N_EXPERTS), F32)],
        compiler_params=_params("parallel"),
        name="outproj",
    )(attn, hgo, x2, wo_bf, bo, g2, rw, rb)


def _rows(start, n):
    return pl.ds(pl.multiple_of(start, SEG_ROWS), n)


def _repeat(cnt, fn):
    def body(j, carry):
        fn(j)
        return carry

    lax.fori_loop(0, cnt, body, 0)


BIG_CAP = SORT_SLOTS // BIG_PIECE
SMALL_CAP = N_EXPERTS * (BIG_PIECE // SEG_ROWS - 1)


def _tile_pieces(lists, step, fn):
    big_loc, big_glob, big_cnt, small_loc, small_glob, small_cnt = lists
    b0 = step * BIG_CAP
    _repeat(big_cnt[step], lambda j: fn(big_loc[b0 + j], big_glob[b0 + j], BIG_PIECE))
    s0 = step * SMALL_CAP
    _repeat(small_cnt[step], lambda j: fn(small_loc[s0 + j], small_glob[s0 + j], SEG_ROWS))


def _tile_piece_waits(lists, step, wait):
    _repeat(lists[2][step], lambda j: wait(BIG_PIECE))
    _repeat(lists[5][step], lambda j: wait(SEG_ROWS))


def _dispatch_kernel(*refs):
    lists, (zdst_ref, zbig_ref, zsmall_ref) = refs[:6], refs[6:9]
    h_ref, lpos_t_ref, xs_ref, lbuf_ref, zbuf_ref, sem, zsem = refs[9:]
    i = pl.program_id(0)
    last = pl.num_programs(0) - 1
    buf = i % 2

    def piece(b, local, glob, n):
        return pltpu.make_async_copy(lbuf_ref.at[b, _rows(local, n), :], xs_ref.at[_rows(glob, n), :], sem.at[b])

    def drain(step):
        _tile_piece_waits(lists, step, lambda n: piece(step % 2, 0, 0, n).wait())

    def zero_piece(glob, n):
        return pltpu.make_async_copy(zbuf_ref.at[_rows(0, n), :], xs_ref.at[_rows(glob, n), :], zsem)

    @pl.when(i == 0)
    def _():
        zbuf_ref[...] = jnp.zeros_like(zbuf_ref)
        for e in range(N_EXPERTS + 1):
            _repeat(zbig_ref[e], lambda j, e=e: zero_piece(zdst_ref[e] + j * BIG_PIECE, BIG_PIECE).start())
            _repeat(zsmall_ref[e], lambda j, e=e: zero_piece(
                zdst_ref[e] + zbig_ref[e] * BIG_PIECE + j * SEG_ROWS, SEG_ROWS).start())

    @pl.when(i >= 2)
    def _():
        drain(i - 2)

    slot = lax.broadcasted_iota(jnp.int32, (SORT_SLOTS, SORT_TILE), 0)
    onehot = jnp.zeros((SORT_SLOTS, SORT_TILE), F32)
    for k in range(TOP_K):
        onehot = onehot + jnp.where(slot == lpos_t_ref[k:k + 1, :], 1.0, 0.0)
    lbuf_ref[buf] = jnp.dot(onehot.astype(BF16), h_ref[...], preferred_element_type=F32)
    _tile_pieces(lists, i, lambda local, glob, n: piece(buf, local, glob, n).start())

    @pl.when(i == last)
    def _():
        @pl.when(i >= 1)
        def _():
            drain(i - 1)

        drain(i)
        for e in range(N_EXPERTS + 1):
            _repeat(zbig_ref[e], lambda j: zero_piece(0, BIG_PIECE).wait())
            _repeat(zsmall_ref[e], lambda j: zero_piece(0, SEG_ROWS).wait())


def _dispatch(tables, zero_tables, h2, lpos_t, n_rows):
    t = h2.shape[0]
    tile = lambda i, *_: (i, 0)
    return pl.pallas_call(
        _dispatch_kernel,
        grid_spec=pltpu.PrefetchScalarGridSpec(
            num_scalar_prefetch=9,
            grid=(t // SORT_TILE,),
            in_specs=[pl.BlockSpec((SORT_TILE, D_MODEL), tile),
                      pl.BlockSpec((TOP_K, SORT_TILE), lambda i, *_: (0, i))],
            out_specs=pl.BlockSpec(memory_space=pl.ANY),
            scratch_shapes=[pltpu.VMEM((2, SORT_SLOTS, D_MODEL), F32),
                            pltpu.VMEM((BIG_PIECE, D_MODEL), F32),
                            pltpu.SemaphoreType.DMA((2,)),
                            pltpu.SemaphoreType.DMA(())]),
        out_shape=jax.ShapeDtypeStruct((n_rows, D_MODEL), F32),
        compiler_params=_params("arbitrary"),
        name="dispatch",
    )(*tables, *zero_tables, h2, lpos_t)


CAST_ROWS = 128


def _expert_kernel(be_ref, nb_ref, eb_ref, filled_ref,
                   xs_ref, w1_hbm, b1_ref, w2_hbm, b2_ref, y_ref,
                   w1f_ref, w2f_ref, w1b_ref, w2i_ref, w2b_ref, started_ref, sem):
    blk = pl.program_id(0)
    half = LANES // 2

    def weight_copies(e, s):
        return (pltpu.make_async_copy(w1_hbm.at[e], w1f_ref.at[s], sem.at[0, s]),
                pltpu.make_async_copy(w2_hbm.at[e], w2f_ref.at[s], sem.at[1, s]))

    e = be_ref[blk]
    first_of_expert = (blk < nb_ref[0]) & ((blk == 0) | (e != be_ref[jnp.maximum(blk - 1, 0)]))

    @pl.when(first_of_expert)
    def _():
        done = jnp.where(blk == 0, 0, started_ref[0])
        started_ref[0] = done + 1
        s = done % 2
        next_blk = blk + eb_ref[e]

        @pl.when(blk == 0)
        def _():
            for c in weight_copies(e, s):
                c.start()

        @pl.when(next_blk < nb_ref[0])
        def _():
            for c in weight_copies(be_ref[next_blk], 1 - s):
                c.start()

        for c in weight_copies(e, s):
            c.wait()

        def cast_rows(r, carry):
            rows = pl.ds(pl.multiple_of(r * CAST_ROWS, CAST_ROWS), CAST_ROWS)
            w1b_ref[rows, :] = w1f_ref[s, rows, :].astype(BF16)
            return carry

        lax.fori_loop(0, D_MODEL // CAST_ROWS, cast_rows, 0)
        for c in range(D_MODEL // LANES):
            cols = slice(c * LANES, (c + 1) * LANES)
            for m in range(EXPERT_FF // LANES):
                lo = m * LANES
                w2i_ref[c, pl.ds(lo, half, stride=2), :] = w2f_ref[s, lo:lo + half, cols]
                w2i_ref[c, pl.ds(lo + 1, half, stride=2), :] = w2f_ref[s, lo + half:lo + LANES, cols]
            w2b_ref[:, cols] = w2i_ref[c].astype(BF16)

    def ffn(rows):
        x = xs_ref[:rows, :].astype(BF16)
        hid = jnp.dot(x, w1b_ref[...], preferred_element_type=F32) + b1_ref[...]
        even = (lax.broadcasted_iota(jnp.int32, (rows, LANES), 1) & 1) == 0
        glu, lin = [], []
        for m in range(EXPERT_FF // LANES):
            ha = hid[:, 2 * m * LANES:(2 * m + 1) * LANES]
            hb = hid[:, (2 * m + 1) * LANES:(2 * m + 2) * LANES]
            glu.append(jnp.where(even, ha, pltpu.roll(hb, 1, axis=1)))
            lin.append(jnp.where(even, pltpu.roll(ha, LANES - 1, axis=1), hb))
        glu = jnp.minimum(jnp.concatenate(glu, axis=1), SWIGLU_LIMIT)
        lin = jnp.clip(jnp.concatenate(lin, axis=1), -SWIGLU_LIMIT, SWIGLU_LIMIT)
        act = glu * jax.nn.sigmoid(SWIGLU_ALPHA * glu) * (lin + 1.0)
        y_ref[:rows, :] = jnp.dot(act.astype(BF16), w2b_ref[...], preferred_element_type=F32) + b2_ref[...]

    quarter = EXPERT_BLOCK // 4
    filled_quarters = (filled_ref[blk] + quarter - 1) // quarter
    for nq in range(1, 5):
        @pl.when(filled_quarters == nq)
        def _(rows=nq * quarter):
            ffn(rows)
            if rows < EXPERT_BLOCK:
                y_ref[rows:, :] = jnp.zeros((EXPERT_BLOCK - rows, D_MODEL), F32)

    @pl.when(filled_quarters == 0)
    def _():
        y_ref[...] = jnp.zeros_like(y_ref)


def _experts(block_tables, xs, w1, b1, w2, b2):
    n_rows = xs.shape[0]
    nblk = n_rows // EXPERT_BLOCK
    rows = lambda b, *_: (b, 0)
    used_rows = lambda b, be, nb, *_: (jnp.minimum(b, nb[0] - 1), 0)
    bias = lambda b, be, *_: (be[b], 0, 0)
    return pl.pallas_call(
        _expert_kernel,
        grid_spec=pltpu.PrefetchScalarGridSpec(
            num_scalar_prefetch=4,
            grid=(nblk,),
            in_specs=[pl.BlockSpec((EXPERT_BLOCK, D_MODEL), used_rows),
                      pl.BlockSpec(memory_space=pl.ANY),
                      pl.BlockSpec((None, 1, 2 * EXPERT_FF), bias),
                      pl.BlockSpec(memory_space=pl.ANY),
                      pl.BlockSpec((None, 1, D_MODEL), bias)],
            out_specs=pl.BlockSpec((EXPERT_BLOCK, D_MODEL), rows),
            scratch_shapes=[pltpu.VMEM((2, D_MODEL, 2 * EXPERT_FF), F32),
                            pltpu.VMEM((2, EXPERT_FF, D_MODEL), F32),
                            pltpu.VMEM((D_MODEL, 2 * EXPERT_FF), BF16),
                            pltpu.VMEM((D_MODEL // LANES, EXPERT_FF, LANES), F32),
                            pltpu.VMEM((EXPERT_FF, D_MODEL), BF16),
                            pltpu.SMEM((1,), jnp.int32),
                            pltpu.SemaphoreType.DMA((2, 2))]),
        out_shape=jax.ShapeDtypeStruct((n_rows, D_MODEL), F32),
        compiler_params=_params("arbitrary"),
        name="experts",
    )(*block_tables, xs, w1, b1, w2, b2)


def _combine_kernel(final_norm, *refs):
    lists = refs[:6]
    yb_ref, x1_ref, lpos_ref, gate_ref, g_ref, o_ref, gbuf_ref, sem = refs[6:]
    i = pl.program_id(0)
    last = pl.num_programs(0) - 1
    buf = i % 2

    def piece(b, local, glob, n):
        return pltpu.make_async_copy(yb_ref.at[_rows(glob, n), :], gbuf_ref.at[b, _rows(local, n), :], sem.at[b])

    def fetch(step):
        _tile_pieces(lists, step, lambda local, glob, n: piece(step % 2, local, glob, n).start())

    @pl.when(i == 0)
    def _():
        gbuf_ref[...] = jnp.zeros_like(gbuf_ref)
        fetch(0)

    @pl.when(i < last)
    def _():
        fetch(i + 1)

    _tile_piece_waits(lists, i, lambda n: piece(buf, 0, 0, n).wait())
    slot = lax.broadcasted_iota(jnp.int32, (SORT_TILE, SORT_SLOTS), 1)
    lpos = lpos_ref[...]
    gates = gate_ref[...]
    weights = jnp.zeros((SORT_TILE, SORT_SLOTS), F32)
    for k in range(TOP_K):
        weights = weights + jnp.where(slot == lpos[:, k:k + 1], gates[:, k:k + 1], 0.0)
    y = x1_ref[...] + jnp.dot(weights.astype(BF16), gbuf_ref[buf].astype(BF16), preferred_element_type=F32)
    o_ref[...] = _rms(y, g_ref[...]) if final_norm else y


def _combine(tables, yb, x1, lpos, gates, g, final_norm):
    t = x1.shape[0]
    tile = lambda i, *_: (i, 0)
    return pl.pallas_call(
        functools.partial(_combine_kernel, final_norm),
        grid_spec=pltpu.PrefetchScalarGridSpec(
            num_scalar_prefetch=6,
            grid=(t // SORT_TILE,),
            in_specs=[pl.BlockSpec(memory_space=pl.ANY),
                      pl.BlockSpec((SORT_TILE, D_MODEL), tile),
                      pl.BlockSpec((SORT_TILE, TOP_K), tile),
                      pl.BlockSpec((SORT_TILE, TOP_K), tile),
                      pl.BlockSpec((1, D_MODEL), lambda i, *_: (0, 0))],
            out_specs=pl.BlockSpec((SORT_TILE, D_MODEL), tile),
            scratch_shapes=[pltpu.VMEM((2, SORT_SLOTS, D_MODEL), F32),
                            pltpu.SemaphoreType.DMA((2,))]),
        out_shape=jax.ShapeDtypeStruct((t, D_MODEL), F32),
        compiler_params=_params("arbitrary"),
        name="combine",
    )(*tables, yb, x1, lpos, gates, g)


def _piece_lists(seg_loc, seg_glob, cnt, skip, rows, cap):
    first = jnp.cumsum(cnt, axis=1) - cnt
    p = jnp.arange(cap, dtype=jnp.int32)
    started = first[:, None, :] <= p[None, :, None]

    def per_piece(rows0):
        base = rows0 + skip - first * rows
        step = base - jnp.concatenate([jnp.zeros_like(base[:, :1]), base[:, :-1]], axis=1)
        return (jnp.sum(jnp.where(started, step[:, None, :], 0), axis=2) + p[None, :] * rows).reshape(-1)

    return per_piece(seg_loc), per_piece(seg_glob), jnp.sum(cnt, axis=1)


def kernel(x, norm1_g, w_in, b_in, attn_sinks, attn_out_g, hgrn_lb_logits, hgrn_out_g, w_out, b_out,
           norm2_g, router_w, router_b, w1, b1, w2, b2, final_g):
    batch, seq, d = x.shape
    t = batch * seq
    depth = w_in.shape[0]
    lower_bounds = jnp.cumsum(jax.nn.softmax(hgrn_lb_logits.astype(F32), axis=0), axis=0)
    assert t % SORT_TILE == 0 and SORT_SLOTS >= SORT_TILE * TOP_K + N_EXPERTS * (SEG_ROWS - 1)
    ntiles = t // SORT_TILE
    nblk = (t * TOP_K + ntiles * N_EXPERTS * (SEG_ROWS - 1)) // EXPERT_BLOCK + N_EXPERTS
    n_rows = nblk * EXPERT_BLOCK
    x2 = x.reshape(t, d)
    for l in range(depth):
        w_in_l = jnp.concatenate([_pair_heads(w_in[l][:, :ATTN_WIDTH], 1), w_in[l][:, ATTN_WIDTH:]], axis=1)
        b_in_l = jnp.concatenate([_pair_heads(b_in[l][:ATTN_WIDTH], 0), b_in[l][ATTN_WIDTH:]])
        aq, ak, av, hq, hf, hi, hg = _inproj(x2, norm1_g[l][None], w_in_l.astype(BF16), b_in_l[None])
        attn = _attention(aq, ak, av, attn_sinks[l], _pair_heads(attn_out_g[l], 0)[None], batch, seq)
        hgo = _hgrn(hq, hf, hi, hg, lower_bounds[l][None], hgrn_out_g[l][None], batch, seq)
        w_out_l = jnp.concatenate([_pair_heads(w_out[l][:ATTN_WIDTH], 0), w_out[l][ATTN_WIDTH:]], axis=0)
        rw_hi = router_w[l].astype(BF16)
        rw_lo = (router_w[l] - rw_hi.astype(F32)).astype(BF16)
        rw_cat = jnp.zeros((2 * LANES, d), BF16)
        rw_cat = rw_cat.at[:N_EXPERTS].set(rw_hi.T).at[LANES:LANES + N_EXPERTS].set(rw_lo.T)
        x1, h2, lpos_t, gates_t, seg = _outproj(
            attn, hgo, x2, w_out_l.astype(BF16), b_out[l][None], norm2_g[l][None],
            rw_cat, router_b[l][:, None])
        seg = seg.reshape(ntiles, N_EXPERTS).astype(jnp.int32)
        rows_e = jnp.sum(seg, axis=0)
        padded = (rows_e + EXPERT_BLOCK - 1) // EXPERT_BLOCK * EXPERT_BLOCK
        ends = jnp.cumsum(padded)
        pstart = ends - padded
        seg_glob = pstart[None, :] + jnp.cumsum(seg, axis=0) - seg
        seg_loc = jnp.cumsum(seg, axis=1) - seg
        tail = jnp.append(padded - rows_e, n_rows - ends[-1])
        zero_tables = (jnp.append(pstart + rows_e, ends[-1]), tail // BIG_PIECE, tail % BIG_PIECE // SEG_ROWS)
        big = seg // BIG_PIECE
        tables = (_piece_lists(seg_loc, seg_glob, big, jnp.zeros_like(seg), BIG_PIECE, BIG_CAP)
                  + _piece_lists(seg_loc, seg_glob, seg % BIG_PIECE // SEG_ROWS, big * BIG_PIECE, SEG_ROWS,
                                 SMALL_CAP))
        blk_ids = jnp.arange(nblk, dtype=jnp.int32)
        block_e = jnp.minimum(jnp.sum(blk_ids[:, None] * EXPERT_BLOCK >= ends[None, :], axis=-1), N_EXPERTS - 1)
        in_expert = rows_e[None, :] - (blk_ids[:, None] * EXPERT_BLOCK - pstart[None, :])
        filled = jnp.sum(jnp.where((block_e[:, None] == jnp.arange(N_EXPERTS)[None, :])
                                   & (blk_ids[:, None] * EXPERT_BLOCK < ends[-1]),
                                   jnp.clip(in_expert, 0, EXPERT_BLOCK), 0), axis=1)
        block_tables = (block_e.astype(jnp.int32), (ends[-1:] // EXPERT_BLOCK).astype(jnp.int32),
                        (padded // EXPERT_BLOCK).astype(jnp.int32), filled.astype(jnp.int32))
        xs = _dispatch(tables, zero_tables, h2, lpos_t, n_rows)
        yb = _experts(block_tables, xs, w1[l], b1[l][:, None, :], w2[l], b2[l][:, None, :])
        x2 = _combine(tables, yb, x1, lpos_t.T, gates_t.T, final_g[None], l == depth - 1)
    return x2.reshape(batch, seq, d)
```

```python
import functools

import numpy as np
import jax
import jax.numpy as jnp
from jax import lax
from jax.experimental import pallas as pl
from jax.experimental.pallas import tpu as pltpu

F32 = jnp.float32
BF16 = jnp.bfloat16

D_MODEL = 1024
ATTN_Q_HEADS = 8
ATTN_KV_HEADS = 2
ATTN_HEAD_DIM = 64
ATTN_GROUP = ATTN_Q_HEADS // ATTN_KV_HEADS
ATTN_WIDTH = ATTN_Q_HEADS * ATTN_HEAD_DIM
ATTN_KV_WIDTH = ATTN_KV_HEADS * ATTN_HEAD_DIM
WINDOW = 128
ATTN_STEP_BLOCKS = 4
HGRN_HEADS = 4
HGRN_DIM = 128
HGRN_WIDTH = HGRN_HEADS * HGRN_DIM
HGRN_CHUNK = 64
HGRN_SUB = 16
HGRN_STEP = 512
IN_WIDTH = ATTN_WIDTH + 2 * ATTN_KV_WIDTH + 4 * HGRN_WIDTH
N_EXPERTS = 32
TOP_K = 4
EXPERT_FF = D_MODEL
SWIGLU_LIMIT = 7.0
SWIGLU_ALPHA = 1.702
NORM_EPS = 1e-5

LANES = 128
ROW_TILE = 1024
EXPERT_BLOCK = 512
SORT_TILE = 256
OUTPROJ_TILES = 4
SEG_ROWS = 8
BIG_PIECE = 32
SORT_SLOTS = 1280
VMEM_LIMIT = 56 * 1024 * 1024

_ALIBI = [float(2.0 ** (-8.0 * (h + 1) / ATTN_Q_HEADS)) for h in range(ATTN_Q_HEADS)]


def _rms(x, g):
    return x * lax.rsqrt(jnp.mean(x * x, axis=-1, keepdims=True) + NORM_EPS) * g


def _params(*sem):
    return pltpu.CompilerParams(dimension_semantics=sem, vmem_limit_bytes=VMEM_LIMIT)


_IN_SPLITS = (ATTN_WIDTH, ATTN_KV_WIDTH, ATTN_KV_WIDTH, HGRN_WIDTH, HGRN_WIDTH, HGRN_WIDTH, HGRN_WIDTH)
_IN_DTYPES = (BF16, BF16, BF16, F32, F32, F32, F32)


def _inproj_kernel(x_ref, g_ref, w_ref, b_ref, *out_refs):
    h = _rms(x_ref[...], g_ref[...]).astype(BF16)
    lo = 0
    for ref, width in zip(out_refs, _IN_SPLITS):
        out = jnp.dot(h, w_ref[:, lo:lo + width], preferred_element_type=F32) + b_ref[:, lo:lo + width]
        ref[...] = out.astype(ref.dtype)
        lo += width


def _inproj(x2, g, w_bf, b):
    t = x2.shape[0]
    row = lambda i: (i, 0)
    fixed = lambda i: (0, 0)
    return pl.pallas_call(
        _inproj_kernel,
        grid=(t // ROW_TILE,),
        in_specs=[pl.BlockSpec((ROW_TILE, D_MODEL), row),
                  pl.BlockSpec((1, D_MODEL), fixed),
                  pl.BlockSpec((D_MODEL, IN_WIDTH), fixed),
                  pl.BlockSpec((1, IN_WIDTH), fixed)],
        out_specs=[pl.BlockSpec((ROW_TILE, w), row) for w in _IN_SPLITS],
        out_shape=[jax.ShapeDtypeStruct((t, w), dt) for w, dt in zip(_IN_SPLITS, _IN_DTYPES)],
        compiler_params=_params("parallel"),
        name="inproj",
    )(x2, g, w_bf, b)


def _attn_bias_tables():
    qi = np.arange(WINDOW)[:, None]
    ki = np.arange(2 * WINDOW)[None, :]
    dist = WINDOW + qi - ki
    in_window = (dist >= 0) & (dist < WINDOW)
    slopes = np.asarray(_ALIBI, np.float32)[:, None, None]
    bias = -(slopes * dist.astype(np.float32)[None])
    tables = [np.where(in_window & (ki >= WINDOW), bias, -np.inf), np.where(in_window, bias, -np.inf)]
    return np.stack(tables).astype(np.float32)


def _pair_heads(a, axis):
    shape = a.shape
    a = a.reshape(shape[:axis] + (ATTN_KV_HEADS, ATTN_GROUP, ATTN_HEAD_DIM) + shape[axis + 1:])
    return jnp.swapaxes(a, axis, axis + 1).reshape(shape)


def _attn_kernel(sink_ref, q_ref, kp_ref, kc_ref, vp_ref, vc_ref, bias_ref, g_ref, o_ref):
    n = pl.program_id(1)
    kc = kc_ref[...].astype(BF16)
    vc = vc_ref[...].astype(BF16)
    keys = [jnp.concatenate([kp_ref[...].astype(BF16), kc[:WINDOW]], axis=0)]
    vals = [jnp.concatenate([vp_ref[...].astype(BF16), vc[:WINDOW]], axis=0)]
    for i in range(1, ATTN_STEP_BLOCKS):
        keys.append(kc[(i - 1) * WINDOW:(i + 1) * WINDOW])
        vals.append(vc[(i - 1) * WINDOW:(i + 1) * WINDOW])
    tables = [jnp.minimum(n, 1)] + [1] * (ATTN_STEP_BLOCKS - 1)
    scale = ATTN_HEAD_DIM ** -0.5
    nt = (((1,), (1,)), ((), ()))
    low = lax.broadcasted_iota(jnp.int32, (WINDOW, LANES), 1) < ATTN_HEAD_DIM

    def softmax(s, table, h):
        s = s * scale + bias_ref[table, h]
        sink = sink_ref[h]
        m = jnp.maximum(jnp.max(s, axis=-1, keepdims=True), sink)
        p = jnp.exp(s - m)
        den = jnp.sum(p, axis=-1, keepdims=True) + jnp.exp(sink - m)
        return (p / den).astype(BF16)

    items = [(i, j) for i in range(ATTN_STEP_BLOCKS) for j in range(ATTN_GROUP)]
    scores = {}
    for i, j in items:
        q = q_ref[i * WINDOW:(i + 1) * WINDOW, j * LANES:(j + 1) * LANES]
        scores[i, j] = (
            lax.dot_general(jnp.where(low, q, 0.0).astype(BF16), keys[i], nt, preferred_element_type=F32),
            lax.dot_general(jnp.where(low, 0.0, q).astype(BF16), keys[i], nt, preferred_element_type=F32))
    probs = {(i, j): (softmax(scores[i, j][0], tables[i], j), softmax(scores[i, j][1], tables[i], ATTN_GROUP + j))
             for i, j in items}
    for i in range(ATTN_STEP_BLOCKS):
        outs = [jnp.where(low, jnp.dot(probs[i, j][0], vals[i], preferred_element_type=F32),
                          jnp.dot(probs[i, j][1], vals[i], preferred_element_type=F32))
                for j in range(ATTN_GROUP)]
        o_ref[i * WINDOW:(i + 1) * WINDOW, :] = _rms(jnp.concatenate(outs, axis=1), g_ref[...])


def _attention(aq, ak, av, sinks, g, batch, seq):
    step = ATTN_STEP_BLOCKS * WINDOW
    aq = aq.reshape(batch, seq, ATTN_WIDTH)
    ak = ak.reshape(batch, seq, ATTN_KV_WIDTH)
    av = av.reshape(batch, seq, ATTN_KV_WIDTH)
    cur = lambda b, n, s: (b, n, 0)
    prev = lambda b, n, s: (b, jnp.maximum(n * ATTN_STEP_BLOCKS - 1, 0), 0)
    out = pl.pallas_call(
        _attn_kernel,
        grid_spec=pltpu.PrefetchScalarGridSpec(
            num_scalar_prefetch=1,
            grid=(batch, seq // step),
            in_specs=[pl.BlockSpec((None, step, ATTN_WIDTH), cur),
                      pl.BlockSpec((None, WINDOW, ATTN_KV_WIDTH), prev),
                      pl.BlockSpec((None, step, ATTN_KV_WIDTH), cur),
                      pl.BlockSpec((None, WINDOW, ATTN_KV_WIDTH), prev),
                      pl.BlockSpec((None, step, ATTN_KV_WIDTH), cur),
                      pl.BlockSpec((2, ATTN_Q_HEADS, WINDOW, 2 * WINDOW), lambda b, n, s: (0, 0, 0, 0)),
                      pl.BlockSpec((1, ATTN_WIDTH), lambda b, n, s: (0, 0))],
            out_specs=pl.BlockSpec((None, step, ATTN_WIDTH), cur)),
        out_shape=jax.ShapeDtypeStruct((batch, seq, ATTN_WIDTH), F32),
        compiler_params=_params("parallel", "parallel"),
        name="attn",
    )(sinks, aq, ak, ak, av, av, jnp.asarray(_attn_bias_tables()), g)
    return out.reshape(batch * seq, ATTN_WIDTH)


HGRN_EXP_GUARD = 80.0


def _hgrn_kernel(guard, q_ref, f_ref, i_ref, gate_ref, lb_ref, og_ref, o_ref, st_ref, q_scr, k_scr, b_scr):
    step = pl.program_id(1)

    @pl.when(step == 0)
    def _():
        st_ref[...] = jnp.zeros_like(st_ref)

    C, S = HGRN_CHUNK, HGRN_SUB
    nsub = C // S
    nchunk = HGRN_STEP // C
    r_i = lax.broadcasted_iota(jnp.int32, (C, C), 0)
    c_i = lax.broadcasted_iota(jnp.int32, (C, C), 1)
    nt = (((1,), (1,)), ((), ()))

    tri = (r_i >= c_i).astype(F32)
    qx = q_ref[...]
    q_scr[...] = qx * jax.nn.sigmoid(qx)
    lb = lb_ref[...]
    f = lb + (1.0 - lb) * jax.nn.sigmoid(f_ref[...])
    k_scr[...] = 1.0 - f
    logf = jnp.log(f)
    for c in range(nchunk):
        b_scr[c * C:(c + 1) * C, :] = jnp.dot(tri, logf[c * C:(c + 1) * C, :], preferred_element_type=F32,
                                              precision=lax.Precision.HIGHEST)
    decay = jnp.concatenate([-jnp.sum(logf[i * S:(i + 1) * S, :], axis=0, keepdims=True)
                             for i in range(HGRN_STEP // S)], axis=0)
    risky = jnp.max(decay) > guard

    rowblk = lax.broadcasted_iota(jnp.int32, (C, HGRN_DIM), 0) // S
    pairs = [(c, h) for c in range(nchunk) for h in range(HGRN_HEADS)]

    def view(ref, c, h):
        return ref[c * C:(c + 1) * C, h * HGRN_DIM:(h + 1) * HGRN_DIM]

    def finish(intra):
        q_dec, gain, update = {}, {}, {}
        for c, h in pairs:
            q, k, b = view(q_scr, c, h), view(k_scr, c, h), view(b_scr, c, h)
            q_dec[c, h] = (q * jnp.exp(b)).astype(BF16)
            bl = b[C - 1:C, :]
            gain[c, h] = jnp.exp(bl)
            update[c, h] = lax.dot_general(view(i_ref, c, h).astype(BF16), (k * jnp.exp(bl - b)).astype(BF16),
                                           (((0,), (0,)), ((), ())), preferred_element_type=F32)
        state = {}
        for h in range(HGRN_HEADS):
            st = st_ref[h]
            for c in range(nchunk):
                state[c, h] = st
                st = st * gain[c, h] + update[c, h]
            st_ref[h] = st
        for c, h in pairs:
            o = intra[c, h] + lax.dot_general(q_dec[c, h], state[c, h].astype(BF16), nt,
                                              preferred_element_type=F32)
            gx = view(gate_ref, c, h)
            og = og_ref[:, h * HGRN_DIM:(h + 1) * HGRN_DIM]
            o = o * lax.rsqrt(jnp.mean(o * o, axis=-1, keepdims=True) + NORM_EPS) * og
            o_ref[c * C:(c + 1) * C, h * HGRN_DIM:(h + 1) * HGRN_DIM] = o * (gx * jax.nn.sigmoid(gx))

    def sub_block_queries(q, b):
        return [jnp.where(rowblk == i, q * jnp.exp(jnp.minimum(b - b[i * S:i * S + 1, :], 0.0)), 0.0)
                for i in range(nsub)]

    @pl.when(jnp.logical_not(risky))
    def _():
        att = {}
        for c, h in pairs:
            q, k, b = view(q_scr, c, h), view(k_scr, c, h), view(b_scr, c, h)
            k_sub = [jnp.where(rowblk <= i, k * jnp.exp(jnp.minimum(b[i * S:i * S + 1, :] - b, guard)), 0.0)
                     for i in range(nsub)]
            att[c, h] = lax.dot_general(jnp.concatenate(sub_block_queries(q, b), axis=1).astype(BF16),
                                        jnp.concatenate(k_sub, axis=1).astype(BF16), nt,
                                        preferred_element_type=F32)
        finish({(c, h): jnp.dot(jnp.where(r_i >= c_i, att[c, h], 0.0).astype(BF16),
                                view(i_ref, c, h).astype(BF16), preferred_element_type=F32) for c, h in pairs})

    @pl.when(risky)
    def _():
        sub_r = lax.broadcasted_iota(jnp.int32, (S, HGRN_DIM), 0)
        intra = {}
        for c, h in pairs:
            q, k, b, v = view(q_scr, c, h), view(k_scr, c, h), view(b_scr, c, h), view(i_ref, c, h)
            k_sub = [jnp.where(rowblk < i, k * jnp.exp(jnp.minimum(b[i * S:i * S + 1, :] - b, 0.0)), 0.0)
                     for i in range(1, nsub)]
            att = lax.dot_general(jnp.concatenate(sub_block_queries(q, b)[1:], axis=1).astype(BF16),
                                  jnp.concatenate(k_sub, axis=1).astype(BF16), nt, preferred_element_type=F32)
            diag = []
            for i in range(nsub):
                b_blk = b[i * S:(i + 1) * S, :]
                q_blk = q[i * S:(i + 1) * S, :]
                acc = jnp.zeros((S, HGRN_DIM), F32)
                for s_ in range(S):
                    r = i * S + s_
                    e = jnp.exp(jnp.minimum(b_blk - b[r:r + 1, :], 0.0))
                    a = jnp.where(sub_r >= s_, q_blk * e * k[r:r + 1, :], 0.0)
                    acc = acc + jnp.sum(a, axis=-1, keepdims=True) * v[r:r + 1, :]
                diag.append(acc)
            intra[c, h] = (jnp.dot(att.astype(BF16), v.astype(BF16), preferred_element_type=F32)
                           + jnp.concatenate(diag, axis=0))
        finish(intra)


def _hgrn(hq, hf, hi, hg, lb, og, batch, seq, guard=HGRN_EXP_GUARD):
    shp = (batch, seq, HGRN_WIDTH)
    blk = pl.BlockSpec((None, HGRN_STEP, HGRN_WIDTH), lambda b, c: (b, c, 0))
    vec = pl.BlockSpec((1, HGRN_WIDTH), lambda b, c: (0, 0))
    out = pl.pallas_call(
        functools.partial(_hgrn_kernel, guard),
        grid=(batch, seq // HGRN_STEP),
        in_specs=[blk, blk, blk, blk, vec, vec],
        out_specs=blk,
        out_shape=jax.ShapeDtypeStruct(shp, F32),
        scratch_shapes=[pltpu.VMEM((HGRN_HEADS, HGRN_DIM, HGRN_DIM), F32)]
        + [pltpu.VMEM((HGRN_STEP, HGRN_WIDTH), F32)] * 3,
        compiler_params=_params("parallel", "arbitrary"),
        name="hgrn",
    )(hq.reshape(shp), hf.reshape(shp), hi.reshape(shp), hg.reshape(shp), lb, og)
    return out.reshape(batch * seq, HGRN_WIDTH)


def _outproj_kernel(attn_ref, hg_ref, x_ref, wo_ref, bo_ref, g2_ref, rw_ref, rb_ref,
                    x1_ref, h2_ref, lpos_ref, gate_ref, seg_ref):
    tm = SORT_TILE
    tiles = range(OUTPROJ_TILES)
    y = jnp.dot(attn_ref[...].astype(BF16), wo_ref[:ATTN_WIDTH, :], preferred_element_type=F32)
    y = y + jnp.dot(hg_ref[...].astype(BF16), wo_ref[ATTN_WIDTH:, :], preferred_element_type=F32)
    x1 = x_ref[...] + y + bo_ref[...]
    x1_ref[...] = x1
    h2 = _rms(x1, g2_ref[...])
    h2_ref[...] = h2.astype(BF16)
    nt = (((1,), (1,)), ((), ()))
    h_hi = h2.astype(BF16)
    h_lo = (h2 - h_hi.astype(F32)).astype(BF16)
    work = []
    for u in tiles:
        p_hi = lax.dot_general(rw_ref[...], h_hi[u * tm:(u + 1) * tm], nt, preferred_element_type=F32)
        p_lo = lax.dot_general(rw_ref[...], h_lo[u * tm:(u + 1) * tm], nt, preferred_element_type=F32)
        work.append((p_hi[:N_EXPERTS] + p_hi[LANES:LANES + N_EXPERTS]) + p_lo[:N_EXPERTS] + rb_ref[...])
    eid = lax.broadcasted_iota(jnp.int32, (N_EXPERTS, tm), 0).astype(F32)
    vals = [[] for _ in tiles]
    hots = [[] for _ in tiles]
    for _ in range(TOP_K):
        for u in tiles:
            m = jnp.max(work[u], axis=0, keepdims=True)
            idx = jnp.min(jnp.where(work[u] == m, eid, float(N_EXPERTS)), axis=0, keepdims=True)
            hot = eid == idx
            vals[u].append(m)
            hots[u].append(hot)
            work[u] = jnp.where(hot, -jnp.inf, work[u])
    t_r = lax.broadcasted_iota(jnp.int32, (tm, tm), 0)
    t_c = lax.broadcasted_iota(jnp.int32, (tm, tm), 1)
    earlier = jnp.where(t_r < t_c, 1.0, 0.0).astype(BF16)
    e_r = lax.broadcasted_iota(jnp.int32, (N_EXPERTS, N_EXPERTS), 0)
    e_c = lax.broadcasted_iota(jnp.int32, (N_EXPERTS, N_EXPERTS), 1)
    before = jnp.where(e_c < e_r, 1.0, 0.0).astype(BF16)

    def round_up(n):
        return jnp.floor((n + (SEG_ROWS - 1.0)) * (1.0 / SEG_ROWS)) * SEG_ROWS

    sel = [sum(jnp.where(hot, 1.0, 0.0) for hot in hots[u]) for u in tiles]
    ranks = [jnp.dot(sel[u].astype(BF16), earlier, preferred_element_type=F32) for u in tiles]
    seg_col = [round_up(jnp.sum(sel[u], axis=1, keepdims=True)) for u in tiles]
    seg_off = [jnp.dot(before, jnp.broadcast_to(seg_col[u], (N_EXPERTS, LANES)).astype(BF16),
                       preferred_element_type=F32)[:, 0:1] for u in tiles]
    seg_row = [round_up(lax.dot_general(jnp.ones((SEG_ROWS, tm), BF16), sel[u].astype(BF16), nt,
                                        preferred_element_type=F32)[0:1, :]) for u in tiles]
    for u in tiles:
        slots = ranks[u] + seg_off[u]
        ex = [jnp.exp(v - vals[u][0]) for v in vals[u]]
        den = ex[0] + ex[1] + ex[2] + ex[3]
        lpos_ref[:, u * tm:(u + 1) * tm] = jnp.concatenate(
            [jnp.sum(jnp.where(hot, slots, 0.0), axis=0, keepdims=True) for hot in hots[u]],
            axis=0).astype(jnp.int32)
        gate_ref[:, u * tm:(u + 1) * tm] = jnp.concatenate([e / den for e in ex], axis=0)
        seg_ref[u] = seg_row[u]


def _outproj(attn, hgo, x2, wo_bf, bo, g2, rw, rb):
    t = x2.shape[0]
    step = OUTPROJ_TILES * SORT_TILE
    row = lambda i: (i, 0)
    col = lambda i: (0, i)
    fixed = lambda i: (0, 0)
    return pl.pallas_call(
        _outproj_kernel,
        grid=(t // step,),
        in_specs=[pl.BlockSpec((step, ATTN_WIDTH), row),
                  pl.BlockSpec((step, HGRN_WIDTH), row),
                  pl.BlockSpec((step, D_MODEL), row),
                  pl.BlockSpec((ATTN_WIDTH + HGRN_WIDTH, D_MODEL), fixed),
                  pl.BlockSpec((1, D_MODEL), fixed),
                  pl.BlockSpec((1, D_MODEL), fixed),
                  pl.BlockSpec((2 * LANES, D_MODEL), fixed),
                  pl.BlockSpec((N_EXPERTS, 1), fixed)],
        out_specs=[pl.BlockSpec((step, D_MODEL), row),
                   pl.BlockSpec((step, D_MODEL), row),
                   pl.BlockSpec((TOP_K, step), col),
                   pl.BlockSpec((TOP_K, step), col),
                   pl.BlockSpec((OUTPROJ_TILES, 1, N_EXPERTS), lambda i: (i, 0, 0))],
        out_shape=[jax.ShapeDtypeStruct((t, D_MODEL), F32),
                   jax.ShapeDtypeStruct((t, D_MODEL), BF16),
                   jax.ShapeDtypeStruct((TOP_K, t), jnp.int32),
                   jax.ShapeDtypeStruct((TOP_K, t), F32),
                   jax.ShapeDtypeStruct((t // SORT_TILE, 1, N_EXPERTS), F32)],
        compiler_params=_params("parallel"),
        name="outproj",
    )(attn, hgo, x2, wo_bf, bo, g2, rw, rb)


def _rows(start, n):
    return pl.ds(pl.multiple_of(start, SEG_ROWS), n)


REPEAT_UNROLL = 4


def _repeat(cnt, fn):
    groups = cnt // REPEAT_UNROLL

    def unrolled(g, carry):
        for u in range(REPEAT_UNROLL):
            fn(g * REPEAT_UNROLL + u)
        return carry

    def single(j, carry):
        fn(j)
        return carry

    lax.fori_loop(0, groups, unrolled, 0)
    lax.fori_loop(groups * REPEAT_UNROLL, cnt, single, 0)


BIG_CAP = SORT_SLOTS // BIG_PIECE
SMALL_CAP = N_EXPERTS * (BIG_PIECE // SEG_ROWS - 1)


def _tile_pieces(lists, step, fn):
    big_loc, big_glob, big_cnt, small_loc, small_glob, small_cnt = lists
    b0 = step * BIG_CAP
    _repeat(big_cnt[step], lambda j: fn(big_loc[b0 + j], big_glob[b0 + j], BIG_PIECE))
    s0 = step * SMALL_CAP
    _repeat(small_cnt[step], lambda j: fn(small_loc[s0 + j], small_glob[s0 + j], SEG_ROWS))


def _tile_piece_waits(lists, step, wait):
    _repeat(lists[2][step], lambda j: wait(BIG_PIECE))
    _repeat(lists[5][step], lambda j: wait(SEG_ROWS))


def _dispatch_kernel(*refs):
    lists, (zdst_ref, zbig_ref, zsmall_ref) = refs[:6], refs[6:9]
    h_ref, lpos_t_ref, xs_ref, lbuf_ref, zbuf_ref, sem, zsem = refs[9:]
    i = pl.program_id(0)
    last = pl.num_programs(0) - 1
    buf = i % 2

    def piece(b, local, glob, n):
        return pltpu.make_async_copy(lbuf_ref.at[b, _rows(local, n), :], xs_ref.at[_rows(glob, n), :], sem.at[b])

    def drain(step):
        _tile_piece_waits(lists, step, lambda n: piece(step % 2, 0, 0, n).wait())

    def zero_piece(glob, n):
        return pltpu.make_async_copy(zbuf_ref.at[_rows(0, n), :], xs_ref.at[_rows(glob, n), :], zsem)

    @pl.when(i == 0)
    def _():
        zbuf_ref[...] = jnp.zeros_like(zbuf_ref)
        for e in range(N_EXPERTS + 1):
            _repeat(zbig_ref[e], lambda j, e=e: zero_piece(zdst_ref[e] + j * BIG_PIECE, BIG_PIECE).start())
            _repeat(zsmall_ref[e], lambda j, e=e: zero_piece(
                zdst_ref[e] + zbig_ref[e] * BIG_PIECE + j * SEG_ROWS, SEG_ROWS).start())
        for e in range(N_EXPERTS + 1):
            _repeat(zbig_ref[e], lambda j: zero_piece(0, BIG_PIECE).wait())
            _repeat(zsmall_ref[e], lambda j: zero_piece(0, SEG_ROWS).wait())

    @pl.when(i >= 2)
    def _():
        drain(i - 2)

    slot = lax.broadcasted_iota(jnp.int32, (SORT_SLOTS, SORT_TILE), 0)
    onehot = jnp.zeros((SORT_SLOTS, SORT_TILE), F32)
    for k in range(TOP_K):
        onehot = onehot + jnp.where(slot == lpos_t_ref[k:k + 1, :], 1.0, 0.0)
    lbuf_ref[buf] = jnp.dot(onehot.astype(BF16), h_ref[...], preferred_element_type=F32)
    _tile_pieces(lists, i, lambda local, glob, n: piece(buf, local, glob, n).start())

    @pl.when(i == last)
    def _():
        @pl.when(i >= 1)
        def _():
            drain(i - 1)

        drain(i)


def _dispatch(tables, zero_tables, h2, lpos_t, n_rows):
    t = h2.shape[0]
    tile = lambda i, *_: (i, 0)
    return pl.pallas_call(
        _dispatch_kernel,
        grid_spec=pltpu.PrefetchScalarGridSpec(
            num_scalar_prefetch=9,
            grid=(t // SORT_TILE,),
            in_specs=[pl.BlockSpec((SORT_TILE, D_MODEL), tile),
                      pl.BlockSpec((TOP_K, SORT_TILE), lambda i, *_: (0, i))],
            out_specs=pl.BlockSpec(memory_space=pl.ANY),
            scratch_shapes=[pltpu.VMEM((2, SORT_SLOTS, D_MODEL), F32),
                            pltpu.VMEM((BIG_PIECE, D_MODEL), F32),
                            pltpu.SemaphoreType.DMA((2,)),
                            pltpu.SemaphoreType.DMA(())]),
        out_shape=jax.ShapeDtypeStruct((n_rows, D_MODEL), F32),
        compiler_params=_params("arbitrary"),
        name="dispatch",
    )(*tables, *zero_tables, h2, lpos_t)


CAST_ROWS = 128


def _expert_kernel(be_ref, nb_ref, eb_ref, filled_ref,
                   xs_ref, w1_hbm, b1_ref, w2_hbm, b2_ref, y_ref,
                   w1f_ref, w2f_ref, w1b_ref, w2i_ref, w2b_ref, started_ref, sem):
    blk = pl.program_id(0)
    half = LANES // 2

    def weight_copies(e, s):
        return (pltpu.make_async_copy(w1_hbm.at[e], w1f_ref.at[s], sem.at[0, s]),
                pltpu.make_async_copy(w2_hbm.at[e], w2f_ref.at[s], sem.at[1, s]))

    e = be_ref[blk]
    first_of_expert = (blk < nb_ref[0]) & ((blk == 0) | (e != be_ref[jnp.maximum(blk - 1, 0)]))

    @pl.when(first_of_expert)
    def _():
        done = jnp.where(blk == 0, 0, started_ref[0])
        started_ref[0] = done + 1
        s = done % 2
        next_blk = blk + eb_ref[e]

        @pl.when(blk == 0)
        def _():
            for c in weight_copies(e, s):
                c.start()

        @pl.when(next_blk < nb_ref[0])
        def _():
            for c in weight_copies(be_ref[next_blk], 1 - s):
                c.start()

        for c in weight_copies(e, s):
            c.wait()

        def cast_rows(r, carry):
            rows = pl.ds(pl.multiple_of(r * CAST_ROWS, CAST_ROWS), CAST_ROWS)
            w1b_ref[rows, :] = w1f_ref[s, rows, :].astype(BF16)
            return carry

        lax.fori_loop(0, D_MODEL // CAST_ROWS, cast_rows, 0)
        for c in range(D_MODEL // LANES):
            cols = slice(c * LANES, (c + 1) * LANES)
            for m in range(EXPERT_FF // LANES):
                lo = m * LANES
                w2i_ref[c, pl.ds(lo, half, stride=2), :] = w2f_ref[s, lo:lo + half, cols]
                w2i_ref[c, pl.ds(lo + 1, half, stride=2), :] = w2f_ref[s, lo + half:lo + LANES, cols]
            w2b_ref[:, cols] = w2i_ref[c].astype(BF16)

    def ffn(rows):
        x = xs_ref[:rows, :].astype(BF16)
        hid = jnp.dot(x, w1b_ref[...], preferred_element_type=F32) + b1_ref[...]
        even = (lax.broadcasted_iota(jnp.int32, (rows, LANES), 1) & 1) == 0
        glu, lin = [], []
        for m in range(EXPERT_FF // LANES):
            ha = hid[:, 2 * m * LANES:(2 * m + 1) * LANES]
            hb = hid[:, (2 * m + 1) * LANES:(2 * m + 2) * LANES]
            glu.append(jnp.where(even, ha, pltpu.roll(hb, 1, axis=1)))
            lin.append(jnp.where(even, pltpu.roll(ha, LANES - 1, axis=1), hb))
        glu = jnp.minimum(jnp.concatenate(glu, axis=1), SWIGLU_LIMIT)
        lin = jnp.clip(jnp.concatenate(lin, axis=1), -SWIGLU_LIMIT, SWIGLU_LIMIT)
        act = glu * jax.nn.sigmoid(SWIGLU_ALPHA * glu) * (lin + 1.0)
        y_ref[:rows, :] = jnp.dot(act.astype(BF16), w2b_ref[...], preferred_element_type=F32) + b2_ref[...]

    quarter = EXPERT_BLOCK // 4
    filled_quarters = (filled_ref[blk] + quarter - 1) // quarter
    for nq in range(1, 5):
        @pl.when(filled_quarters == nq)
        def _(rows=nq * quarter):
            ffn(rows)
            if rows < EXPERT_BLOCK:
                y_ref[rows:, :] = jnp.zeros((EXPERT_BLOCK - rows, D_MODEL), F32)

    @pl.when(filled_quarters == 0)
    def _():
        y_ref[...] = jnp.zeros_like(y_ref)


def _experts(block_tables, xs, w1, b1, w2, b2):
    n_rows = xs.shape[0]
    nblk = n_rows // EXPERT_BLOCK
    rows = lambda b, *_: (b, 0)
    used_rows = lambda b, be, nb, *_: (jnp.minimum(b, nb[0] - 1), 0)
    bias = lambda b, be, *_: (be[b], 0, 0)
    return pl.pallas_call(
        _expert_kernel,
        grid_spec=pltpu.PrefetchScalarGridSpec(
            num_scalar_prefetch=4,
            grid=(nblk,),
            in_specs=[pl.BlockSpec((EXPERT_BLOCK, D_MODEL), used_rows),
                      pl.BlockSpec(memory_space=pl.ANY),
                      pl.BlockSpec((None, 1, 2 * EXPERT_FF), bias),
                      pl.BlockSpec(memory_space=pl.ANY),
                      pl.BlockSpec((None, 1, D_MODEL), bias)],
            out_specs=pl.BlockSpec((EXPERT_BLOCK, D_MODEL), rows),
            scratch_shapes=[pltpu.VMEM((2, D_MODEL, 2 * EXPERT_FF), F32),
                            pltpu.VMEM((2, EXPERT_FF, D_MODEL), F32),
                            pltpu.VMEM((D_MODEL, 2 * EXPERT_FF), BF16),
                            pltpu.VMEM((D_MODEL // LANES, EXPERT_FF, LANES), F32),
                            pltpu.VMEM((EXPERT_FF, D_MODEL), BF16),
                            pltpu.SMEM((1,), jnp.int32),
                            pltpu.SemaphoreType.DMA((2, 2))]),
        out_shape=jax.ShapeDtypeStruct((n_rows, D_MODEL), F32),
        compiler_params=_params("arbitrary"),
        name="experts",
    )(*block_tables, xs, w1, b1, w2, b2)


def _combine_kernel(final_norm, *refs):
    lists = refs[:6]
    yb_ref, x1_ref, lpos_ref, gate_ref, g_ref, o_ref, gbuf_ref, sem = refs[6:]
    i = pl.program_id(0)
    last = pl.num_programs(0) - 1
    buf = i % 2

    def piece(b, local, glob, n):
        return pltpu.make_async_copy(yb_ref.at[_rows(glob, n), :], gbuf_ref.at[b, _rows(local, n), :], sem.at[b])

    def fetch(step):
        _tile_pieces(lists, step, lambda local, glob, n: piece(step % 2, local, glob, n).start())

    @pl.when(i == 0)
    def _():
        gbuf_ref[...] = jnp.zeros_like(gbuf_ref)
        fetch(0)

    @pl.when(i < last)
    def _():
        fetch(i + 1)

    _tile_piece_waits(lists, i, lambda n: piece(buf, 0, 0, n).wait())
    slot = lax.broadcasted_iota(jnp.int32, (SORT_TILE, SORT_SLOTS), 1)
    lpos = lpos_ref[...]
    gates = gate_ref[...]
    weights = jnp.zeros((SORT_TILE, SORT_SLOTS), F32)
    for k in range(TOP_K):
        weights = weights + jnp.where(slot == lpos[:, k:k + 1], gates[:, k:k + 1], 0.0)
    y = x1_ref[...] + jnp.dot(weights.astype(BF16), gbuf_ref[buf].astype(BF16), preferred_element_type=F32)
    o_ref[...] = _rms(y, g_ref[...]) if final_norm else y


def _combine(tables, yb, x1, lpos, gates, g, final_norm):
    t = x1.shape[0]
    tile = lambda i, *_: (i, 0)
    return pl.pallas_call(
        functools.partial(_combine_kernel, final_norm),
        grid_spec=pltpu.PrefetchScalarGridSpec(
            num_scalar_prefetch=6,
            grid=(t // SORT_TILE,),
            in_specs=[pl.BlockSpec(memory_space=pl.ANY),
                      pl.BlockSpec((SORT_TILE, D_MODEL), tile),
                      pl.BlockSpec((SORT_TILE, TOP_K), tile),
                      pl.BlockSpec((SORT_TILE, TOP_K), tile),
                      pl.BlockSpec((1, D_MODEL), lambda i, *_: (0, 0))],
            out_specs=pl.BlockSpec((SORT_TILE, D_MODEL), tile),
            scratch_shapes=[pltpu.VMEM((2, SORT_SLOTS, D_MODEL), F32),
                            pltpu.SemaphoreType.DMA((2,))]),
        out_shape=jax.ShapeDtypeStruct((t, D_MODEL), F32),
        compiler_params=_params("arbitrary"),
        name="combine",
    )(*tables, yb, x1, lpos, gates, g)


def _piece_lists(seg_loc, seg_glob, cnt, skip, rows, cap):
    first = jnp.cumsum(cnt, axis=1) - cnt
    p = jnp.arange(cap, dtype=jnp.int32)
    started = first[:, None, :] <= p[None, :, None]

    def per_piece(rows0):
        base = rows0 + skip - first * rows
        step = base - jnp.concatenate([jnp.zeros_like(base[:, :1]), base[:, :-1]], axis=1)
        return (jnp.sum(jnp.where(started, step[:, None, :], 0), axis=2) + p[None, :] * rows).reshape(-1)

    return per_piece(seg_loc), per_piece(seg_glob), jnp.sum(cnt, axis=1)


def kernel(x, norm1_g, w_in, b_in, attn_sinks, attn_out_g, hgrn_lb_logits, hgrn_out_g, w_out, b_out,
           norm2_g, router_w, router_b, w1, b1, w2, b2, final_g):
    batch, seq, d = x.shape
    t = batch * seq
    depth = w_in.shape[0]
    lower_bounds = jnp.cumsum(jax.nn.softmax(hgrn_lb_logits.astype(F32), axis=0), axis=0)
    assert t % SORT_TILE == 0 and SORT_SLOTS >= SORT_TILE * TOP_K + N_EXPERTS * (SEG_ROWS - 1)
    ntiles = t // SORT_TILE
    nblk = (t * TOP_K + ntiles * N_EXPERTS * (SEG_ROWS - 1)) // EXPERT_BLOCK + N_EXPERTS
    n_rows = nblk * EXPERT_BLOCK
    x2 = x.reshape(t, d)
    for l in range(depth):
        w_in_l = jnp.concatenate([_pair_heads(w_in[l][:, :ATTN_WIDTH], 1), w_in[l][:, ATTN_WIDTH:]], axis=1)
        b_in_l = jnp.concatenate([_pair_heads(b_in[l][:ATTN_WIDTH], 0), b_in[l][ATTN_WIDTH:]])
        aq, ak, av, hq, hf, hi, hg = _inproj(x2, norm1_g[l][None], w_in_l.astype(BF16), b_in_l[None])
        attn = _attention(aq, ak, av, attn_sinks[l], _pair_heads(attn_out_g[l], 0)[None], batch, seq)
        hgo = _hgrn(hq, hf, hi, hg, lower_bounds[l][None], hgrn_out_g[l][None], batch, seq)
        w_out_l = jnp.concatenate([_pair_heads(w_out[l][:ATTN_WIDTH], 0), w_out[l][ATTN_WIDTH:]], axis=0)
        rw_hi = router_w[l].astype(BF16)
        rw_lo = (router_w[l] - rw_hi.astype(F32)).astype(BF16)
        rw_cat = jnp.zeros((2 * LANES, d), BF16)
        rw_cat = rw_cat.at[:N_EXPERTS].set(rw_hi.T).at[LANES:LANES + N_EXPERTS].set(rw_lo.T)
        x1, h2, lpos_t, gates_t, seg = _outproj(
            attn, hgo, x2, w_out_l.astype(BF16), b_out[l][None], norm2_g[l][None],
            rw_cat, router_b[l][:, None])
        seg = seg.reshape(ntiles, N_EXPERTS).astype(jnp.int32)
        rows_e = jnp.sum(seg, axis=0)
        padded = (rows_e + EXPERT_BLOCK - 1) // EXPERT_BLOCK * EXPERT_BLOCK
        ends = jnp.cumsum(padded)
        pstart = ends - padded
        seg_glob = pstart[None, :] + jnp.cumsum(seg, axis=0) - seg
        seg_loc = jnp.cumsum(seg, axis=1) - seg
        tail = jnp.append(padded - rows_e, n_rows - ends[-1])
        zero_tables = (jnp.append(pstart + rows_e, ends[-1]), tail // BIG_PIECE, tail % BIG_PIECE // SEG_ROWS)
        big = seg // BIG_PIECE
        tables = (_piece_lists(seg_loc, seg_glob, big, jnp.zeros_like(seg), BIG_PIECE, BIG_CAP)
                  + _piece_lists(seg_loc, seg_glob, seg % BIG_PIECE // SEG_ROWS, big * BIG_PIECE, SEG_ROWS,
                                 SMALL_CAP))
        blk_ids = jnp.arange(nblk, dtype=jnp.int32)
        block_e = jnp.minimum(jnp.sum(blk_ids[:, None] * EXPERT_BLOCK >= ends[None, :], axis=-1), N_EXPERTS - 1)
        in_expert = rows_e[None, :] - (blk_ids[:, None] * EXPERT_BLOCK - pstart[None, :])
        filled = jnp.sum(jnp.where((block_e[:, None] == jnp.arange(N_EXPERTS)[None, :])
                                   & (blk_ids[:, None] * EXPERT_BLOCK < ends[-1]),
                                   jnp.clip(in_expert, 0, EXPERT_BLOCK), 0), axis=1)
        block_tables = (block_e.astype(jnp.int32), (ends[-1:] // EXPERT_BLOCK).astype(jnp.int32),
                        (padded // EXPERT_BLOCK).astype(jnp.int32), filled.astype(jnp.int32))
        xs = _dispatch(tables, zero_tables, h2, lpos_t, n_rows)
        yb = _experts(block_tables, xs, w1[l], b1[l][:, None, :], w2[l], b2[l][:, None, :])
        x2 = _combine(tables, yb, x1, lpos_t.T, gates_t.T, final_g[None], l == depth - 1)
    return x2.reshape(batch, seq, d)
```

```python
import functools

import numpy as np
import jax
import jax.numpy as jnp
from jax import lax
from jax.experimental import pallas as pl
from jax.experimental.pallas import tpu as pltpu

F32 = jnp.float32
BF16 = jnp.bfloat16

D_MODEL = 1024
ATTN_Q_HEADS = 8
ATTN_KV_HEADS = 2
ATTN_HEAD_DIM = 64
ATTN_GROUP = ATTN_Q_HEADS // ATTN_KV_HEADS
ATTN_WIDTH = ATTN_Q_HEADS * ATTN_HEAD_DIM
ATTN_KV_WIDTH = ATTN_KV_HEADS * ATTN_HEAD_DIM
WINDOW = 128
ATTN_STEP_BLOCKS = 4
HGRN_HEADS = 4
HGRN_DIM = 128
HGRN_WIDTH = HGRN_HEADS * HGRN_DIM
HGRN_CHUNK = 64
HGRN_SUB = 16
HGRN_STEP = 512
IN_WIDTH = ATTN_WIDTH + 2 * ATTN_KV_WIDTH + 4 * HGRN_WIDTH
N_EXPERTS = 32
TOP_K = 4
EXPERT_FF = D_MODEL
SWIGLU_LIMIT = 7.0
SWIGLU_ALPHA = 1.702
NORM_EPS = 1e-5

LANES = 128
ROW_TILE = 1024
EXPERT_BLOCK = 512
SORT_TILE = 256
OUTPROJ_TILES = 4
SEG_ROWS = 8
BIG_PIECE = 32
SORT_SLOTS = 1280
VMEM_LIMIT = 56 * 1024 * 1024

_ALIBI = [float(2.0 ** (-8.0 * (h + 1) / ATTN_Q_HEADS)) for h in range(ATTN_Q_HEADS)]


def _rms(x, g):
    return x * lax.rsqrt(jnp.mean(x * x, axis=-1, keepdims=True) + NORM_EPS) * g


def _params(*sem):
    return pltpu.CompilerParams(dimension_semantics=sem, vmem_limit_bytes=VMEM_LIMIT)


_IN_SPLITS = (ATTN_WIDTH, ATTN_KV_WIDTH, ATTN_KV_WIDTH, HGRN_WIDTH, HGRN_WIDTH, HGRN_WIDTH, HGRN_WIDTH)
_IN_DTYPES = (BF16, BF16, BF16, F32, F32, F32, F32)


def _inproj_kernel(x_ref, g_ref, w_ref, b_ref, *out_refs):
    h = _rms(x_ref[...], g_ref[...]).astype(BF16)
    lo = 0
    for ref, width in zip(out_refs, _IN_SPLITS):
        out = jnp.dot(h, w_ref[:, lo:lo + width], preferred_element_type=F32) + b_ref[:, lo:lo + width]
        ref[...] = out.astype(ref.dtype)
        lo += width


def _inproj(x2, g, w_bf, b):
    t = x2.shape[0]
    row = lambda i: (i, 0)
    fixed = lambda i: (0, 0)
    return pl.pallas_call(
        _inproj_kernel,
        grid=(t // ROW_TILE,),
        in_specs=[pl.BlockSpec((ROW_TILE, D_MODEL), row),
                  pl.BlockSpec((1, D_MODEL), fixed),
                  pl.BlockSpec((D_MODEL, IN_WIDTH), fixed),
                  pl.BlockSpec((1, IN_WIDTH), fixed)],
        out_specs=[pl.BlockSpec((ROW_TILE, w), row) for w in _IN_SPLITS],
        out_shape=[jax.ShapeDtypeStruct((t, w), dt) for w, dt in zip(_IN_SPLITS, _IN_DTYPES)],
        compiler_params=_params("parallel"),
        name="inproj",
    )(x2, g, w_bf, b)


def _attn_bias_tables():
    qi = np.arange(WINDOW)[:, None]
    ki = np.arange(2 * WINDOW)[None, :]
    dist = WINDOW + qi - ki
    in_window = (dist >= 0) & (dist < WINDOW)
    slopes = np.asarray(_ALIBI, np.float32)[:, None, None]
    bias = -(slopes * dist.astype(np.float32)[None])
    tables = [np.where(in_window & (ki >= WINDOW), bias, -np.inf), np.where(in_window, bias, -np.inf)]
    return np.stack(tables).astype(np.float32)


def _pair_heads(a, axis):
    shape = a.shape
    a = a.reshape(shape[:axis] + (ATTN_KV_HEADS, ATTN_GROUP, ATTN_HEAD_DIM) + shape[axis + 1:])
    return jnp.swapaxes(a, axis, axis + 1).reshape(shape)


def _attn_kernel(sink_ref, q_ref, kp_ref, kc_ref, vp_ref, vc_ref, bias_ref, g_ref, o_ref):
    n = pl.program_id(1)
    kc = kc_ref[...].astype(BF16)
    vc = vc_ref[...].astype(BF16)
    keys = [jnp.concatenate([kp_ref[...].astype(BF16), kc[:WINDOW]], axis=0)]
    vals = [jnp.concatenate([vp_ref[...].astype(BF16), vc[:WINDOW]], axis=0)]
    for i in range(1, ATTN_STEP_BLOCKS):
        keys.append(kc[(i - 1) * WINDOW:(i + 1) * WINDOW])
        vals.append(vc[(i - 1) * WINDOW:(i + 1) * WINDOW])
    tables = [jnp.minimum(n, 1)] + [1] * (ATTN_STEP_BLOCKS - 1)
    scale = ATTN_HEAD_DIM ** -0.5
    nt = (((1,), (1,)), ((), ()))
    low = lax.broadcasted_iota(jnp.int32, (WINDOW, LANES), 1) < ATTN_HEAD_DIM

    def softmax(s, table, h):
        s = s * scale + bias_ref[table, h]
        sink = sink_ref[h]
        m = jnp.maximum(jnp.max(s, axis=-1, keepdims=True), sink)
        p = jnp.exp(s - m)
        den = jnp.sum(p, axis=-1, keepdims=True) + jnp.exp(sink - m)
        return (p / den).astype(BF16)

    items = [(i, j) for i in range(ATTN_STEP_BLOCKS) for j in range(ATTN_GROUP)]
    scores = {}
    for i, j in items:
        q = q_ref[i * WINDOW:(i + 1) * WINDOW, j * LANES:(j + 1) * LANES]
        scores[i, j] = (
            lax.dot_general(jnp.where(low, q, 0.0).astype(BF16), keys[i], nt, preferred_element_type=F32),
            lax.dot_general(jnp.where(low, 0.0, q).astype(BF16), keys[i], nt, preferred_element_type=F32))
    probs = {(i, j): (softmax(scores[i, j][0], tables[i], j), softmax(scores[i, j][1], tables[i], ATTN_GROUP + j))
             for i, j in items}
    for i in range(ATTN_STEP_BLOCKS):
        outs = [jnp.where(low, jnp.dot(probs[i, j][0], vals[i], preferred_element_type=F32),
                          jnp.dot(probs[i, j][1], vals[i], preferred_element_type=F32))
                for j in range(ATTN_GROUP)]
        o_ref[i * WINDOW:(i + 1) * WINDOW, :] = _rms(jnp.concatenate(outs, axis=1), g_ref[...])


def _attention(aq, ak, av, sinks, g, batch, seq):
    step = ATTN_STEP_BLOCKS * WINDOW
    aq = aq.reshape(batch, seq, ATTN_WIDTH)
    ak = ak.reshape(batch, seq, ATTN_KV_WIDTH)
    av = av.reshape(batch, seq, ATTN_KV_WIDTH)
    cur = lambda b, n, s: (b, n, 0)
    prev = lambda b, n, s: (b, jnp.maximum(n * ATTN_STEP_BLOCKS - 1, 0), 0)
    out = pl.pallas_call(
        _attn_kernel,
        grid_spec=pltpu.PrefetchScalarGridSpec(
            num_scalar_prefetch=1,
            grid=(batch, seq // step),
            in_specs=[pl.BlockSpec((None, step, ATTN_WIDTH), cur),
                      pl.BlockSpec((None, WINDOW, ATTN_KV_WIDTH), prev),
                      pl.BlockSpec((None, step, ATTN_KV_WIDTH), cur),
                      pl.BlockSpec((None, WINDOW, ATTN_KV_WIDTH), prev),
                      pl.BlockSpec((None, step, ATTN_KV_WIDTH), cur),
                      pl.BlockSpec((2, ATTN_Q_HEADS, WINDOW, 2 * WINDOW), lambda b, n, s: (0, 0, 0, 0)),
                      pl.BlockSpec((1, ATTN_WIDTH), lambda b, n, s: (0, 0))],
            out_specs=pl.BlockSpec((None, step, ATTN_WIDTH), cur)),
        out_shape=jax.ShapeDtypeStruct((batch, seq, ATTN_WIDTH), F32),
        compiler_params=_params("parallel", "parallel"),
        name="attn",
    )(sinks, aq, ak, ak, av, av, jnp.asarray(_attn_bias_tables()), g)
    return out.reshape(batch * seq, ATTN_WIDTH)


HGRN_EXP_GUARD = 80.0


def _hgrn_kernel(guard, q_ref, f_ref, i_ref, gate_ref, lb_ref, og_ref, o_ref, st_ref, q_scr, k_scr, b_scr):
    step = pl.program_id(1)

    @pl.when(step == 0)
    def _():
        st_ref[...] = jnp.zeros_like(st_ref)

    C, S = HGRN_CHUNK, HGRN_SUB
    nsub = C // S
    nchunk = HGRN_STEP // C
    r_i = lax.broadcasted_iota(jnp.int32, (C, C), 0)
    c_i = lax.broadcasted_iota(jnp.int32, (C, C), 1)
    nt = (((1,), (1,)), ((), ()))

    tri = (r_i >= c_i).astype(F32)
    qx = q_ref[...]
    q_scr[...] = qx * jax.nn.sigmoid(qx)
    lb = lb_ref[...]
    f = lb + (1.0 - lb) * jax.nn.sigmoid(f_ref[...])
    k_scr[...] = 1.0 - f
    logf = jnp.log(f)
    for c in range(nchunk):
        b_scr[c * C:(c + 1) * C, :] = jnp.dot(tri, logf[c * C:(c + 1) * C, :], preferred_element_type=F32,
                                              precision=lax.Precision.HIGHEST)
    decay = jnp.concatenate([-jnp.sum(logf[i * S:(i + 1) * S, :], axis=0, keepdims=True)
                             for i in range(HGRN_STEP // S)], axis=0)
    risky = jnp.max(decay) > guard

    rowblk = lax.broadcasted_iota(jnp.int32, (C, HGRN_DIM), 0) // S
    pairs = [(c, h) for c in range(nchunk) for h in range(HGRN_HEADS)]

    def view(ref, c, h):
        return ref[c * C:(c + 1) * C, h * HGRN_DIM:(h + 1) * HGRN_DIM]

    def finish(intra):
        q_dec, gain, update = {}, {}, {}
        for c, h in pairs:
            q, k, b = view(q_scr, c, h), view(k_scr, c, h), view(b_scr, c, h)
            q_dec[c, h] = (q * jnp.exp(b)).astype(BF16)
            bl = b[C - 1:C, :]
            gain[c, h] = jnp.exp(bl)
            update[c, h] = lax.dot_general(view(i_ref, c, h).astype(BF16), (k * jnp.exp(bl - b)).astype(BF16),
                                           (((0,), (0,)), ((), ())), preferred_element_type=F32)
        state = {}
        for h in range(HGRN_HEADS):
            st = st_ref[h]
            for c in range(nchunk):
                state[c, h] = st
                st = st * gain[c, h] + update[c, h]
            st_ref[h] = st
        for c, h in pairs:
            o = intra[c, h] + lax.dot_general(q_dec[c, h], state[c, h].astype(BF16), nt,
                                              preferred_element_type=F32)
            gx = view(gate_ref, c, h)
            og = og_ref[:, h * HGRN_DIM:(h + 1) * HGRN_DIM]
            o = o * lax.rsqrt(jnp.mean(o * o, axis=-1, keepdims=True) + NORM_EPS) * og
            o_ref[c * C:(c + 1) * C, h * HGRN_DIM:(h + 1) * HGRN_DIM] = o * (gx * jax.nn.sigmoid(gx))

    def sub_block_queries(q, b):
        return [jnp.where(rowblk == i, q * jnp.exp(jnp.minimum(b - b[i * S:i * S + 1, :], 0.0)), 0.0)
                for i in range(nsub)]

    @pl.when(jnp.logical_not(risky))
    def _():
        att = {}
        for c, h in pairs:
            q, k, b = view(q_scr, c, h), view(k_scr, c, h), view(b_scr, c, h)
            k_sub = [jnp.where(rowblk <= i, k * jnp.exp(jnp.minimum(b[i * S:i * S + 1, :] - b, guard)), 0.0)
                     for i in range(nsub)]
            att[c, h] = lax.dot_general(jnp.concatenate(sub_block_queries(q, b), axis=1).astype(BF16),
                                        jnp.concatenate(k_sub, axis=1).astype(BF16), nt,
                                        preferred_element_type=F32)
        finish({(c, h): jnp.dot(jnp.where(r_i >= c_i, att[c, h], 0.0).astype(BF16),
                                view(i_ref, c, h).astype(BF16), preferred_element_type=F32) for c, h in pairs})

    @pl.when(risky)
    def _():
        sub_r = lax.broadcasted_iota(jnp.int32, (S, HGRN_DIM), 0)
        intra = {}
        for c, h in pairs:
            q, k, b, v = view(q_scr, c, h), view(k_scr, c, h), view(b_scr, c, h), view(i_ref, c, h)
            k_sub = [jnp.where(rowblk < i, k * jnp.exp(jnp.minimum(b[i * S:i * S + 1, :] - b, 0.0)), 0.0)
                     for i in range(1, nsub)]
            att = lax.dot_general(jnp.concatenate(sub_block_queries(q, b)[1:], axis=1).astype(BF16),
                                  jnp.concatenate(k_sub, axis=1).astype(BF16), nt, preferred_element_type=F32)
            diag = []
            for i in range(nsub):
                b_blk = b[i * S:(i + 1) * S, :]
                q_blk = q[i * S:(i + 1) * S, :]
                acc = jnp.zeros((S, HGRN_DIM), F32)
                for s_ in range(S):
                    r = i * S + s_
                    e = jnp.exp(jnp.minimum(b_blk - b[r:r + 1, :], 0.0))
                    a = jnp.where(sub_r >= s_, q_blk * e * k[r:r + 1, :], 0.0)
                    acc = acc + jnp.sum(a, axis=-1, keepdims=True) * v[r:r + 1, :]
                diag.append(acc)
            intra[c, h] = (jnp.dot(att.astype(BF16), v.astype(BF16), preferred_element_type=F32)
                           + jnp.concatenate(diag, axis=0))
        finish(intra)


def _hgrn(hq, hf, hi, hg, lb, og, batch, seq, guard=HGRN_EXP_GUARD):
    shp = (batch, seq, HGRN_WIDTH)
    blk = pl.BlockSpec((None, HGRN_STEP, HGRN_WIDTH), lambda b, c: (b, c, 0))
    vec = pl.BlockSpec((1, HGRN_WIDTH), lambda b, c: (0, 0))
    out = pl.pallas_call(
        functools.partial(_hgrn_kernel, guard),
        grid=(batch, seq // HGRN_STEP),
        in_specs=[blk, blk, blk, blk, vec, vec],
        out_specs=blk,
        out_shape=jax.ShapeDtypeStruct(shp, F32),
        scratch_shapes=[pltpu.VMEM((HGRN_HEADS, HGRN_DIM, HGRN_DIM), F32)]
        + [pltpu.VMEM((HGRN_STEP, HGRN_WIDTH), F32)] * 3,
        compiler_params=_params("parallel", "arbitrary"),
        name="hgrn",
    )(hq.reshape(shp), hf.reshape(shp), hi.reshape(shp), hg.reshape(shp), lb, og)
    return out.reshape(batch * seq, HGRN_WIDTH)


def _outproj_kernel(attn_ref, hg_ref, x_ref, wo_ref, bo_ref, g2_ref, rw_ref, rb_ref,
                    x1_ref, h2_ref, lpos_ref, gate_ref, seg_ref):
    tm = SORT_TILE
    tiles = range(OUTPROJ_TILES)
    y = jnp.dot(attn_ref[...].astype(BF16), wo_ref[:ATTN_WIDTH, :], preferred_element_type=F32)
    y = y + jnp.dot(hg_ref[...].astype(BF16), wo_ref[ATTN_WIDTH:, :], preferred_element_type=F32)
    x1 = x_ref[...] + y + bo_ref[...]
    x1_ref[...] = x1
    h2 = _rms(x1, g2_ref[...])
    h2_ref[...] = h2.astype(BF16)
    nt = (((1,), (1,)), ((), ()))
    h_hi = h2.astype(BF16)
    h_lo = (h2 - h_hi.astype(F32)).astype(BF16)
    work = []
    for u in tiles:
        p_hi = lax.dot_general(rw_ref[...], h_hi[u * tm:(u + 1) * tm], nt, preferred_element_type=F32)
        p_lo = lax.dot_general(rw_ref[...], h_lo[u * tm:(u + 1) * tm], nt, preferred_element_type=F32)
        work.append((p_hi[:N_EXPERTS] + p_hi[LANES:LANES + N_EXPERTS]) + p_lo[:N_EXPERTS] + rb_ref[...])
    eid = lax.broadcasted_iota(jnp.int32, (N_EXPERTS, tm), 0).astype(F32)
    vals = [[] for _ in tiles]
    hots = [[] for _ in tiles]
    for _ in range(TOP_K):
        for u in tiles:
            m = jnp.max(work[u], axis=0, keepdims=True)
            idx = jnp.min(jnp.where(work[u] == m, eid, float(N_EXPERTS)), axis=0, keepdims=True)
            hot = eid == idx
            vals[u].append(m)
            hots[u].append(hot)
            work[u] = jnp.where(hot, -jnp.inf, work[u])
    t_r = lax.broadcasted_iota(jnp.int32, (tm, tm), 0)
    t_c = lax.broadcasted_iota(jnp.int32, (tm, tm), 1)
    earlier = jnp.where(t_r < t_c, 1.0, 0.0).astype(BF16)
    e_r = lax.broadcasted_iota(jnp.int32, (N_EXPERTS, N_EXPERTS), 0)
    e_c = lax.broadcasted_iota(jnp.int32, (N_EXPERTS, N_EXPERTS), 1)
    before = jnp.where(e_c < e_r, 1.0, 0.0).astype(BF16)

    def round_up(n):
        return jnp.floor((n + (SEG_ROWS - 1.0)) * (1.0 / SEG_ROWS)) * SEG_ROWS

    sel = [sum(jnp.where(hot, 1.0, 0.0) for hot in hots[u]) for u in tiles]
    ranks = [jnp.dot(sel[u].astype(BF16), earlier, preferred_element_type=F32) for u in tiles]
    seg_col = [round_up(jnp.sum(sel[u], axis=1, keepdims=True)) for u in tiles]
    seg_off = [jnp.dot(before, jnp.broadcast_to(seg_col[u], (N_EXPERTS, LANES)).astype(BF16),
                       preferred_element_type=F32)[:, 0:1] for u in tiles]
    seg_row = [round_up(lax.dot_general(jnp.ones((SEG_ROWS, tm), BF16), sel[u].astype(BF16), nt,
                                        preferred_element_type=F32)[0:1, :]) for u in tiles]
    for u in tiles:
        slots = ranks[u] + seg_off[u]
        ex = [jnp.exp(v - vals[u][0]) for v in vals[u]]
        den = ex[0] + ex[1] + ex[2] + ex[3]
        lpos_ref[:, u * tm:(u + 1) * tm] = jnp.concatenate(
            [jnp.sum(jnp.where(hot, slots, 0.0), axis=0, keepdims=True) for hot in hots[u]],
            axis=0).astype(jnp.int32)
        gate_ref[:, u * tm:(u + 1) * tm] = jnp.concatenate([e / den for e in ex], axis=0)
        seg_ref[u] = seg_row[u]


def _outproj(attn, hgo, x2, wo_bf, bo, g2, rw, rb):
    t = x2.shape[0]
    step = OUTPROJ_TILES * SORT_TILE
    row = lambda i: (i, 0)
    col = lambda i: (0, i)
    fixed = lambda i: (0, 0)
    return pl.pallas_call(
        _outproj_kernel,
        grid=(t // step,),
        in_specs=[pl.BlockSpec((step, ATTN_WIDTH), row),
                  pl.BlockSpec((step, HGRN_WIDTH), row),
                  pl.BlockSpec((step, D_MODEL), row),
                  pl.BlockSpec((ATTN_WIDTH + HGRN_WIDTH, D_MODEL), fixed),
                  pl.BlockSpec((1, D_MODEL), fixed),
                  pl.BlockSpec((1, D_MODEL), fixed),
                  pl.BlockSpec((2 * LANES, D_MODEL), fixed),
                  pl.BlockSpec((N_EXPERTS, 1), fixed)],
        out_specs=[pl.BlockSpec((step, D_MODEL), row),
                   pl.BlockSpec((step, D_MODEL), row),
                   pl.BlockSpec((TOP_K, step), col),
                   pl.BlockSpec((TOP_K, step), col),
                   pl.BlockSpec((OUTPROJ_TILES, 1, N_EXPERTS), lambda i: (i, 0, 0))],
        out_shape=[jax.ShapeDtypeStruct((t, D_MODEL), F32),
                   jax.ShapeDtypeStruct((t, D_MODEL), BF16),
                   jax.ShapeDtypeStruct((TOP_K, t), jnp.int32),
                   jax.ShapeDtypeStruct((TOP_K, t), F32),
                   jax.ShapeDtypeStruct((t // SORT_TILE, 1, N_EXPERTS), F32)],
        compiler_params=_params("parallel"),
        name="outproj",
    )(attn, hgo, x2, wo_bf, bo, g2, rw, rb)


def _rows(start, n):
    return pl.ds(pl.multiple_of(start, SEG_ROWS), n)


REPEAT_UNROLL = 8


def _repeat(cnt, fn):
    groups = cnt // REPEAT_UNROLL

    def unrolled(g, carry):
        for u in range(REPEAT_UNROLL):
            fn(g * REPEAT_UNROLL + u)
        return carry

    def single(j, carry):
        fn(j)
        return carry

    lax.fori_loop(0, groups, unrolled, 0)
    lax.fori_loop(groups * REPEAT_UNROLL, cnt, single, 0)


BIG_CAP = SORT_SLOTS // BIG_PIECE
SMALL_CAP = N_EXPERTS * (BIG_PIECE // SEG_ROWS - 1)


def _tile_pieces(lists, step, fn):
    big_loc, big_glob, big_cnt, small_loc, small_glob, small_cnt = lists
    b0 = step * BIG_CAP
    _repeat(big_cnt[step], lambda j: fn(big_loc[b0 + j], big_glob[b0 + j], BIG_PIECE))
    s0 = step * SMALL_CAP
    _repeat(small_cnt[step], lambda j: fn(small_loc[s0 + j], small_glob[s0 + j], SEG_ROWS))


def _tile_piece_waits(lists, step, wait):
    _repeat(lists[2][step], lambda j: wait(BIG_PIECE))
    _repeat(lists[5][step], lambda j: wait(SEG_ROWS))


def _dispatch_kernel(*refs):
    lists, (zdst_ref, zbig_ref, zsmall_ref) = refs[:6], refs[6:9]
    h_ref, lpos_t_ref, xs_ref, lbuf_ref, zbuf_ref, sem, zsem = refs[9:]
    i = pl.program_id(0)
    last = pl.num_programs(0) - 1
    buf = i % 2

    def piece(b, local, glob, n):
        return pltpu.make_async_copy(lbuf_ref.at[b, _rows(local, n), :], xs_ref.at[_rows(glob, n), :], sem.at[b])

    def drain(step):
        _tile_piece_waits(lists, step, lambda n: piece(step % 2, 0, 0, n).wait())

    def zero_piece(glob, n):
        return pltpu.make_async_copy(zbuf_ref.at[_rows(0, n), :], xs_ref.at[_rows(glob, n), :], zsem)

    @pl.when(i == 0)
    def _():
        zbuf_ref[...] = jnp.zeros_like(zbuf_ref)
        for e in range(N_EXPERTS + 1):
            _repeat(zbig_ref[e], lambda j, e=e: zero_piece(zdst_ref[e] + j * BIG_PIECE, BIG_PIECE).start())
            _repeat(zsmall_ref[e], lambda j, e=e: zero_piece(
                zdst_ref[e] + zbig_ref[e] * BIG_PIECE + j * SEG_ROWS, SEG_ROWS).start())
        for e in range(N_EXPERTS + 1):
            _repeat(zbig_ref[e], lambda j: zero_piece(0, BIG_PIECE).wait())
            _repeat(zsmall_ref[e], lambda j: zero_piece(0, SEG_ROWS).wait())

    @pl.when(i >= 2)
    def _():
        drain(i - 2)

    slot = lax.broadcasted_iota(jnp.int32, (SORT_SLOTS, SORT_TILE), 0)
    onehot = jnp.zeros((SORT_SLOTS, SORT_TILE), F32)
    for k in range(TOP_K):
        onehot = onehot + jnp.where(slot == lpos_t_ref[k:k + 1, :], 1.0, 0.0)
    lbuf_ref[buf] = jnp.dot(onehot.astype(BF16), h_ref[...], preferred_element_type=F32)
    _tile_pieces(lists, i, lambda local, glob, n: piece(buf, local, glob, n).start())

    @pl.when(i == last)
    def _():
        @pl.when(i >= 1)
        def _():
            drain(i - 1)

        drain(i)


def _dispatch(tables, zero_tables, h2, lpos_t, n_rows):
    t = h2.shape[0]
    tile = lambda i, *_: (i, 0)
    return pl.pallas_call(
        _dispatch_kernel,
        grid_spec=pltpu.PrefetchScalarGridSpec(
            num_scalar_prefetch=9,
            grid=(t // SORT_TILE,),
            in_specs=[pl.BlockSpec((SORT_TILE, D_MODEL), tile),
                      pl.BlockSpec((TOP_K, SORT_TILE), lambda i, *_: (0, i))],
            out_specs=pl.BlockSpec(memory_space=pl.ANY),
            scratch_shapes=[pltpu.VMEM((2, SORT_SLOTS, D_MODEL), F32),
                            pltpu.VMEM((BIG_PIECE, D_MODEL), F32),
                            pltpu.SemaphoreType.DMA((2,)),
                            pltpu.SemaphoreType.DMA(())]),
        out_shape=jax.ShapeDtypeStruct((n_rows, D_MODEL), F32),
        compiler_params=_params("arbitrary"),
        name="dispatch",
    )(*tables, *zero_tables, h2, lpos_t)


CAST_ROWS = 128


def _expert_kernel(be_ref, nb_ref, eb_ref, filled_ref,
                   xs_ref, w1_hbm, b1_ref, w2_hbm, b2_ref, y_ref,
                   w1f_ref, w2f_ref, w1b_ref, w2i_ref, w2b_ref, started_ref, sem):
    blk = pl.program_id(0)
    half = LANES // 2

    def weight_copies(e, s):
        return (pltpu.make_async_copy(w1_hbm.at[e], w1f_ref.at[s], sem.at[0, s]),
                pltpu.make_async_copy(w2_hbm.at[e], w2f_ref.at[s], sem.at[1, s]))

    e = be_ref[blk]
    first_of_expert = (blk < nb_ref[0]) & ((blk == 0) | (e != be_ref[jnp.maximum(blk - 1, 0)]))

    @pl.when(first_of_expert)
    def _():
        done = jnp.where(blk == 0, 0, started_ref[0])
        started_ref[0] = done + 1
        s = done % 2
        next_blk = blk + eb_ref[e]

        @pl.when(blk == 0)
        def _():
            for c in weight_copies(e, s):
                c.start()

        @pl.when(next_blk < nb_ref[0])
        def _():
            for c in weight_copies(be_ref[next_blk], 1 - s):
                c.start()

        for c in weight_copies(e, s):
            c.wait()

        def cast_rows(r, carry):
            rows = pl.ds(pl.multiple_of(r * CAST_ROWS, CAST_ROWS), CAST_ROWS)
            w1b_ref[rows, :] = w1f_ref[s, rows, :].astype(BF16)
            return carry

        lax.fori_loop(0, D_MODEL // CAST_ROWS, cast_rows, 0)
        for c in range(D_MODEL // LANES):
            cols = slice(c * LANES, (c + 1) * LANES)
            for m in range(EXPERT_FF // LANES):
                lo = m * LANES
                w2i_ref[c, pl.ds(lo, half, stride=2), :] = w2f_ref[s, lo:lo + half, cols]
                w2i_ref[c, pl.ds(lo + 1, half, stride=2), :] = w2f_ref[s, lo + half:lo + LANES, cols]
            w2b_ref[:, cols] = w2i_ref[c].astype(BF16)

    def ffn(rows):
        x = xs_ref[:rows, :].astype(BF16)
        hid = jnp.dot(x, w1b_ref[...], preferred_element_type=F32) + b1_ref[...]
        even = (lax.broadcasted_iota(jnp.int32, (rows, LANES), 1) & 1) == 0
        glu, lin = [], []
        for m in range(EXPERT_FF // LANES):
            ha = hid[:, 2 * m * LANES:(2 * m + 1) * LANES]
            hb = hid[:, (2 * m + 1) * LANES:(2 * m + 2) * LANES]
            glu.append(jnp.where(even, ha, pltpu.roll(hb, 1, axis=1)))
            lin.append(jnp.where(even, pltpu.roll(ha, LANES - 1, axis=1), hb))
        glu = jnp.minimum(jnp.concatenate(glu, axis=1), SWIGLU_LIMIT)
        lin = jnp.clip(jnp.concatenate(lin, axis=1), -SWIGLU_LIMIT, SWIGLU_LIMIT)
        act = glu * jax.nn.sigmoid(SWIGLU_ALPHA * glu) * (lin + 1.0)
        y_ref[:rows, :] = jnp.dot(act.astype(BF16), w2b_ref[...], preferred_element_type=F32) + b2_ref[...]

    quarter = EXPERT_BLOCK // 4
    filled_quarters = (filled_ref[blk] + quarter - 1) // quarter
    for nq in range(1, 5):
        @pl.when(filled_quarters == nq)
        def _(rows=nq * quarter):
            ffn(rows)
            if rows < EXPERT_BLOCK:
                y_ref[rows:, :] = jnp.zeros((EXPERT_BLOCK - rows, D_MODEL), F32)

    @pl.when(filled_quarters == 0)
    def _():
        y_ref[...] = jnp.zeros_like(y_ref)


def _experts(block_tables, xs, w1, b1, w2, b2):
    n_rows = xs.shape[0]
    nblk = n_rows // EXPERT_BLOCK
    rows = lambda b, *_: (b, 0)
    used_rows = lambda b, be, nb, *_: (jnp.minimum(b, nb[0] - 1), 0)
    bias = lambda b, be, *_: (be[b], 0, 0)
    return pl.pallas_call(
        _expert_kernel,
        grid_spec=pltpu.PrefetchScalarGridSpec(
            num_scalar_prefetch=4,
            grid=(nblk,),
            in_specs=[pl.BlockSpec((EXPERT_BLOCK, D_MODEL), used_rows),
                      pl.BlockSpec(memory_space=pl.ANY),
                      pl.BlockSpec((None, 1, 2 * EXPERT_FF), bias),
                      pl.BlockSpec(memory_space=pl.ANY),
                      pl.BlockSpec((None, 1, D_MODEL), bias)],
            out_specs=pl.BlockSpec((EXPERT_BLOCK, D_MODEL), rows),
            scratch_shapes=[pltpu.VMEM((2, D_MODEL, 2 * EXPERT_FF), F32),
                            pltpu.VMEM((2, EXPERT_FF, D_MODEL), F32),
                            pltpu.VMEM((D_MODEL, 2 * EXPERT_FF), BF16),
                            pltpu.VMEM((D_MODEL // LANES, EXPERT_FF, LANES), F32),
                            pltpu.VMEM((EXPERT_FF, D_MODEL), BF16),
                            pltpu.SMEM((1,), jnp.int32),
                            pltpu.SemaphoreType.DMA((2, 2))]),
        out_shape=jax.ShapeDtypeStruct((n_rows, D_MODEL), F32),
        compiler_params=_params("arbitrary"),
        name="experts",
    )(*block_tables, xs, w1, b1, w2, b2)


def _combine_kernel(final_norm, *refs):
    lists = refs[:6]
    yb_ref, x1_ref, lpos_ref, gate_ref, g_ref, o_ref, gbuf_ref, sem = refs[6:]
    i = pl.program_id(0)
    last = pl.num_programs(0) - 1
    buf = i % 2

    def piece(b, local, glob, n):
        return pltpu.make_async_copy(yb_ref.at[_rows(glob, n), :], gbuf_ref.at[b, _rows(local, n), :], sem.at[b])

    def fetch(step):
        _tile_pieces(lists, step, lambda local, glob, n: piece(step % 2, local, glob, n).start())

    @pl.when(i == 0)
    def _():
        gbuf_ref[...] = jnp.zeros_like(gbuf_ref)
        fetch(0)

    @pl.when(i < last)
    def _():
        fetch(i + 1)

    _tile_piece_waits(lists, i, lambda n: piece(buf, 0, 0, n).wait())
    slot = lax.broadcasted_iota(jnp.int32, (SORT_TILE, SORT_SLOTS), 1)
    lpos = lpos_ref[...]
    gates = gate_ref[...]
    weights = jnp.zeros((SORT_TILE, SORT_SLOTS), F32)
    for k in range(TOP_K):
        weights = weights + jnp.where(slot == lpos[:, k:k + 1], gates[:, k:k + 1], 0.0)
    y = x1_ref[...] + jnp.dot(weights.astype(BF16), gbuf_ref[buf].astype(BF16), preferred_element_type=F32)
    o_ref[...] = _rms(y, g_ref[...]) if final_norm else y


def _combine(tables, yb, x1, lpos, gates, g, final_norm):
    t = x1.shape[0]
    tile = lambda i, *_: (i, 0)
    return pl.pallas_call(
        functools.partial(_combine_kernel, final_norm),
        grid_spec=pltpu.PrefetchScalarGridSpec(
            num_scalar_prefetch=6,
            grid=(t // SORT_TILE,),
            in_specs=[pl.BlockSpec(memory_space=pl.ANY),
                      pl.BlockSpec((SORT_TILE, D_MODEL), tile),
                      pl.BlockSpec((SORT_TILE, TOP_K), tile),
                      pl.BlockSpec((SORT_TILE, TOP_K), tile),
                      pl.BlockSpec((1, D_MODEL), lambda i, *_: (0, 0))],
            out_specs=pl.BlockSpec((SORT_TILE, D_MODEL), tile),
            scratch_shapes=[pltpu.VMEM((2, SORT_SLOTS, D_MODEL), F32),
                            pltpu.SemaphoreType.DMA((2,))]),
        out_shape=jax.ShapeDtypeStruct((t, D_MODEL), F32),
        compiler_params=_params("arbitrary"),
        name="combine",
    )(*tables, yb, x1, lpos, gates, g)


def _piece_lists(seg_loc, seg_glob, cnt, skip, rows, cap):
    first = jnp.cumsum(cnt, axis=1) - cnt
    p = jnp.arange(cap, dtype=jnp.int32)
    started = first[:, None, :] <= p[None, :, None]

    def per_piece(rows0):
        base = rows0 + skip - first * rows
        step = base - jnp.concatenate([jnp.zeros_like(base[:, :1]), base[:, :-1]], axis=1)
        return (jnp.sum(jnp.where(started, step[:, None, :], 0), axis=2) + p[None, :] * rows).reshape(-1)

    return per_piece(seg_loc), per_piece(seg_glob), jnp.sum(cnt, axis=1)


def kernel(x, norm1_g, w_in, b_in, attn_sinks, attn_out_g, hgrn_lb_logits, hgrn_out_g, w_out, b_out,
           norm2_g, router_w, router_b, w1, b1, w2, b2, final_g):
    batch, seq, d = x.shape
    t = batch * seq
    depth = w_in.shape[0]
    lower_bounds = jnp.cumsum(jax.nn.softmax(hgrn_lb_logits.astype(F32), axis=0), axis=0)
    assert t % SORT_TILE == 0 and SORT_SLOTS >= SORT_TILE * TOP_K + N_EXPERTS * (SEG_ROWS - 1)
    ntiles = t // SORT_TILE
    nblk = (t * TOP_K + ntiles * N_EXPERTS * (SEG_ROWS - 1)) // EXPERT_BLOCK + N_EXPERTS
    n_rows = nblk * EXPERT_BLOCK
    x2 = x.reshape(t, d)
    for l in range(depth):
        w_in_l = jnp.concatenate([_pair_heads(w_in[l][:, :ATTN_WIDTH], 1), w_in[l][:, ATTN_WIDTH:]], axis=1)
        b_in_l = jnp.concatenate([_pair_heads(b_in[l][:ATTN_WIDTH], 0), b_in[l][ATTN_WIDTH:]])
        aq, ak, av, hq, hf, hi, hg = _inproj(x2, norm1_g[l][None], w_in_l.astype(BF16), b_in_l[None])
        attn = _attention(aq, ak, av, attn_sinks[l], _pair_heads(attn_out_g[l], 0)[None], batch, seq)
        hgo = _hgrn(hq, hf, hi, hg, lower_bounds[l][None], hgrn_out_g[l][None], batch, seq)
        w_out_l = jnp.concatenate([_pair_heads(w_out[l][:ATTN_WIDTH], 0), w_out[l][ATTN_WIDTH:]], axis=0)
        rw_hi = router_w[l].astype(BF16)
        rw_lo = (router_w[l] - rw_hi.astype(F32)).astype(BF16)
        rw_cat = jnp.zeros((2 * LANES, d), BF16)
        rw_cat = rw_cat.at[:N_EXPERTS].set(rw_hi.T).at[LANES:LANES + N_EXPERTS].set(rw_lo.T)
        x1, h2, lpos_t, gates_t, seg = _outproj(
            attn, hgo, x2, w_out_l.astype(BF16), b_out[l][None], norm2_g[l][None],
            rw_cat, router_b[l][:, None])
        seg = seg.reshape(ntiles, N_EXPERTS).astype(jnp.int32)
        rows_e = jnp.sum(seg, axis=0)
        padded = (rows_e + EXPERT_BLOCK - 1) // EXPERT_BLOCK * EXPERT_BLOCK
        ends = jnp.cumsum(padded)
        pstart = ends - padded
        seg_glob = pstart[None, :] + jnp.cumsum(seg, axis=0) - seg
        seg_loc = jnp.cumsum(seg, axis=1) - seg
        tail = jnp.append(padded - rows_e, n_rows - ends[-1])
        zero_tables = (jnp.append(pstart + rows_e, ends[-1]), tail // BIG_PIECE, tail % BIG_PIECE // SEG_ROWS)
        big = seg // BIG_PIECE
        tables = (_piece_lists(seg_loc, seg_glob, big, jnp.zeros_like(seg), BIG_PIECE, BIG_CAP)
                  + _piece_lists(seg_loc, seg_glob, seg % BIG_PIECE // SEG_ROWS, big * BIG_PIECE, SEG_ROWS,
                                 SMALL_CAP))
        blk_ids = jnp.arange(nblk, dtype=jnp.int32)
        block_e = jnp.minimum(jnp.sum(blk_ids[:, None] * EXPERT_BLOCK >= ends[None, :], axis=-1), N_EXPERTS - 1)
        in_expert = rows_e[None, :] - (blk_ids[:, None] * EXPERT_BLOCK - pstart[None, :])
        filled = jnp.sum(jnp.where((block_e[:, None] == jnp.arange(N_EXPERTS)[None, :])
                                   & (blk_ids[:, None] * EXPERT_BLOCK < ends[-1]),
                                   jnp.clip(in_expert, 0, EXPERT_BLOCK), 0), axis=1)
        block_tables = (block_e.astype(jnp.int32), (ends[-1:] // EXPERT_BLOCK).astype(jnp.int32),
                        (padded // EXPERT_BLOCK).astype(jnp.int32), filled.astype(jnp.int32))
        xs = _dispatch(tables, zero_tables, h2, lpos_t, n_rows)
        yb = _experts(block_tables, xs, w1[l], b1[l][:, None, :], w2[l], b2[l][:, None, :])
        x2 = _combine(tables, yb, x1, lpos_t.T, gates_t.T, final_g[None], l == depth - 1)
    return x2.reshape(batch, seq, d)
```

```python
import functools

import numpy as np
import jax
import jax.numpy as jnp
from jax import lax
from jax.experimental import pallas as pl
from jax.experimental.pallas import tpu as pltpu

F32 = jnp.float32
BF16 = jnp.bfloat16

D_MODEL = 1024
ATTN_Q_HEADS = 8
ATTN_KV_HEADS = 2
ATTN_HEAD_DIM = 64
ATTN_GROUP = ATTN_Q_HEADS // ATTN_KV_HEADS
ATTN_WIDTH = ATTN_Q_HEADS * ATTN_HEAD_DIM
ATTN_KV_WIDTH = ATTN_KV_HEADS * ATTN_HEAD_DIM
WINDOW = 128
ATTN_STEP_BLOCKS = 4
HGRN_HEADS = 4
HGRN_DIM = 128
HGRN_WIDTH = HGRN_HEADS * HGRN_DIM
HGRN_CHUNK = 64
HGRN_SUB = 16
HGRN_STEP = 512
IN_WIDTH = ATTN_WIDTH + 2 * ATTN_KV_WIDTH + 4 * HGRN_WIDTH
N_EXPERTS = 32
TOP_K = 4
EXPERT_FF = D_MODEL
SWIGLU_LIMIT = 7.0
SWIGLU_ALPHA = 1.702
NORM_EPS = 1e-5

LANES = 128
ROW_TILE = 1024
EXPERT_BLOCK = 512
SORT_TILE = 256
OUTPROJ_TILES = 4
SEG_ROWS = 8
BIG_PIECE = 32
SORT_SLOTS = 1280
VMEM_LIMIT = 56 * 1024 * 1024

_ALIBI = [float(2.0 ** (-8.0 * (h + 1) / ATTN_Q_HEADS)) for h in range(ATTN_Q_HEADS)]


def _rms(x, g):
    return x * lax.rsqrt(jnp.mean(x * x, axis=-1, keepdims=True) + NORM_EPS) * g


def _params(*sem):
    return pltpu.CompilerParams(dimension_semantics=sem, vmem_limit_bytes=VMEM_LIMIT)


_IN_SPLITS = (ATTN_WIDTH, ATTN_KV_WIDTH, ATTN_KV_WIDTH, HGRN_WIDTH, HGRN_WIDTH, HGRN_WIDTH, HGRN_WIDTH)
_IN_DTYPES = (BF16, BF16, BF16, F32, F32, F32, F32)


def _inproj_kernel(x_ref, g_ref, w_ref, b_ref, *out_refs):
    h = _rms(x_ref[...], g_ref[...]).astype(BF16)
    lo = 0
    for ref, width in zip(out_refs, _IN_SPLITS):
        out = jnp.dot(h, w_ref[:, lo:lo + width], preferred_element_type=F32) + b_ref[:, lo:lo + width]
        ref[...] = out.astype(ref.dtype)
        lo += width


def _inproj(x2, g, w_bf, b):
    t = x2.shape[0]
    row = lambda i: (i, 0)
    fixed = lambda i: (0, 0)
    return pl.pallas_call(
        _inproj_kernel,
        grid=(t // ROW_TILE,),
        in_specs=[pl.BlockSpec((ROW_TILE, D_MODEL), row),
                  pl.BlockSpec((1, D_MODEL), fixed),
                  pl.BlockSpec((D_MODEL, IN_WIDTH), fixed),
                  pl.BlockSpec((1, IN_WIDTH), fixed)],
        out_specs=[pl.BlockSpec((ROW_TILE, w), row) for w in _IN_SPLITS],
        out_shape=[jax.ShapeDtypeStruct((t, w), dt) for w, dt in zip(_IN_SPLITS, _IN_DTYPES)],
        compiler_params=_params("parallel"),
        name="inproj",
    )(x2, g, w_bf, b)


def _attn_bias_tables():
    qi = np.arange(WINDOW)[:, None]
    ki = np.arange(2 * WINDOW)[None, :]
    dist = WINDOW + qi - ki
    in_window = (dist >= 0) & (dist < WINDOW)
    slopes = np.asarray(_ALIBI, np.float32)[:, None, None]
    bias = -(slopes * dist.astype(np.float32)[None])
    tables = [np.where(in_window & (ki >= WINDOW), bias, -np.inf), np.where(in_window, bias, -np.inf)]
    return np.stack(tables).astype(np.float32)


def _pair_heads(a, axis):
    shape = a.shape
    a = a.reshape(shape[:axis] + (ATTN_KV_HEADS, ATTN_GROUP, ATTN_HEAD_DIM) + shape[axis + 1:])
    return jnp.swapaxes(a, axis, axis + 1).reshape(shape)


def _attn_kernel(sink_ref, q_ref, kp_ref, kc_ref, vp_ref, vc_ref, bias_ref, g_ref, o_ref):
    n = pl.program_id(1)
    kc = kc_ref[...].astype(BF16)
    vc = vc_ref[...].astype(BF16)
    keys = [jnp.concatenate([kp_ref[...].astype(BF16), kc[:WINDOW]], axis=0)]
    vals = [jnp.concatenate([vp_ref[...].astype(BF16), vc[:WINDOW]], axis=0)]
    for i in range(1, ATTN_STEP_BLOCKS):
        keys.append(kc[(i - 1) * WINDOW:(i + 1) * WINDOW])
        vals.append(vc[(i - 1) * WINDOW:(i + 1) * WINDOW])
    tables = [jnp.minimum(n, 1)] + [1] * (ATTN_STEP_BLOCKS - 1)
    scale = ATTN_HEAD_DIM ** -0.5
    nt = (((1,), (1,)), ((), ()))
    low = lax.broadcasted_iota(jnp.int32, (WINDOW, LANES), 1) < ATTN_HEAD_DIM

    def softmax(s, table, h):
        s = s * scale + bias_ref[table, h]
        sink = sink_ref[h]
        m = jnp.maximum(jnp.max(s, axis=-1, keepdims=True), sink)
        p = jnp.exp(s - m)
        den = jnp.sum(p, axis=-1, keepdims=True) + jnp.exp(sink - m)
        return (p / den).astype(BF16)

    items = [(i, j) for i in range(ATTN_STEP_BLOCKS) for j in range(ATTN_GROUP)]
    scores = {}
    for i, j in items:
        q = q_ref[i * WINDOW:(i + 1) * WINDOW, j * LANES:(j + 1) * LANES]
        scores[i, j] = (
            lax.dot_general(jnp.where(low, q, 0.0).astype(BF16), keys[i], nt, preferred_element_type=F32),
            lax.dot_general(jnp.where(low, 0.0, q).astype(BF16), keys[i], nt, preferred_element_type=F32))
    probs = {(i, j): (softmax(scores[i, j][0], tables[i], j), softmax(scores[i, j][1], tables[i], ATTN_GROUP + j))
             for i, j in items}
    for i in range(ATTN_STEP_BLOCKS):
        outs = [jnp.where(low, jnp.dot(probs[i, j][0], vals[i], preferred_element_type=F32),
                          jnp.dot(probs[i, j][1], vals[i], preferred_element_type=F32))
                for j in range(ATTN_GROUP)]
        o_ref[i * WINDOW:(i + 1) * WINDOW, :] = _rms(jnp.concatenate(outs, axis=1), g_ref[...])


def _attention(aq, ak, av, sinks, g, batch, seq):
    step = ATTN_STEP_BLOCKS * WINDOW
    aq = aq.reshape(batch, seq, ATTN_WIDTH)
    ak = ak.reshape(batch, seq, ATTN_KV_WIDTH)
    av = av.reshape(batch, seq, ATTN_KV_WIDTH)
    cur = lambda b, n, s: (b, n, 0)
    prev = lambda b, n, s: (b, jnp.maximum(n * ATTN_STEP_BLOCKS - 1, 0), 0)
    out = pl.pallas_call(
        _attn_kernel,
        grid_spec=pltpu.PrefetchScalarGridSpec(
            num_scalar_prefetch=1,
            grid=(batch, seq // step),
            in_specs=[pl.BlockSpec((None, step, ATTN_WIDTH), cur),
                      pl.BlockSpec((None, WINDOW, ATTN_KV_WIDTH), prev),
                      pl.BlockSpec((None, step, ATTN_KV_WIDTH), cur),
                      pl.BlockSpec((None, WINDOW, ATTN_KV_WIDTH), prev),
                      pl.BlockSpec((None, step, ATTN_KV_WIDTH), cur),
                      pl.BlockSpec((2, ATTN_Q_HEADS, WINDOW, 2 * WINDOW), lambda b, n, s: (0, 0, 0, 0)),
                      pl.BlockSpec((1, ATTN_WIDTH), lambda b, n, s: (0, 0))],
            out_specs=pl.BlockSpec((None, step, ATTN_WIDTH), cur)),
        out_shape=jax.ShapeDtypeStruct((batch, seq, ATTN_WIDTH), F32),
        compiler_params=_params("parallel", "parallel"),
        name="attn",
    )(sinks, aq, ak, ak, av, av, jnp.asarray(_attn_bias_tables()), g)
    return out.reshape(batch * seq, ATTN_WIDTH)


HGRN_EXP_GUARD = 80.0


def _hgrn_kernel(guard, q_ref, f_ref, i_ref, gate_ref, lb_ref, og_ref, o_ref, st_ref, q_scr, k_scr, b_scr):
    step = pl.program_id(1)

    @pl.when(step == 0)
    def _():
        st_ref[...] = jnp.zeros_like(st_ref)

    C, S = HGRN_CHUNK, HGRN_SUB
    nsub = C // S
    nchunk = HGRN_STEP // C
    r_i = lax.broadcasted_iota(jnp.int32, (C, C), 0)
    c_i = lax.broadcasted_iota(jnp.int32, (C, C), 1)
    nt = (((1,), (1,)), ((), ()))

    tri = (r_i >= c_i).astype(F32)
    qx = q_ref[...]
    q_scr[...] = qx * jax.nn.sigmoid(qx)
    lb = lb_ref[...]
    f = lb + (1.0 - lb) * jax.nn.sigmoid(f_ref[...])
    k_scr[...] = 1.0 - f
    logf = jnp.log(f)
    for c in range(nchunk):
        b_scr[c * C:(c + 1) * C, :] = jnp.dot(tri, logf[c * C:(c + 1) * C, :], preferred_element_type=F32,
                                              precision=lax.Precision.HIGHEST)
    decay = jnp.concatenate([-jnp.sum(logf[i * S:(i + 1) * S, :], axis=0, keepdims=True)
                             for i in range(HGRN_STEP // S)], axis=0)
    risky = jnp.max(decay) > guard

    rowblk = lax.broadcasted_iota(jnp.int32, (C, HGRN_DIM), 0) // S
    pairs = [(c, h) for c in range(nchunk) for h in range(HGRN_HEADS)]

    def view(ref, c, h):
        return ref[c * C:(c + 1) * C, h * HGRN_DIM:(h + 1) * HGRN_DIM]

    def finish(intra):
        q_dec, gain, update = {}, {}, {}
        for c, h in pairs:
            q, k, b = view(q_scr, c, h), view(k_scr, c, h), view(b_scr, c, h)
            q_dec[c, h] = (q * jnp.exp(b)).astype(BF16)
            bl = b[C - 1:C, :]
            gain[c, h] = jnp.exp(bl)
            update[c, h] = lax.dot_general(view(i_ref, c, h).astype(BF16), (k * jnp.exp(bl - b)).astype(BF16),
                                           (((0,), (0,)), ((), ())), preferred_element_type=F32)
        state = {}
        for h in range(HGRN_HEADS):
            st = st_ref[h]
            for c in range(nchunk):
                state[c, h] = st
                st = st * gain[c, h] + update[c, h]
            st_ref[h] = st
        for c, h in pairs:
            o = intra[c, h] + lax.dot_general(q_dec[c, h], state[c, h].astype(BF16), nt,
                                              preferred_element_type=F32)
            gx = view(gate_ref, c, h)
            og = og_ref[:, h * HGRN_DIM:(h + 1) * HGRN_DIM]
            o = o * lax.rsqrt(jnp.mean(o * o, axis=-1, keepdims=True) + NORM_EPS) * og
            o_ref[c * C:(c + 1) * C, h * HGRN_DIM:(h + 1) * HGRN_DIM] = o * (gx * jax.nn.sigmoid(gx))

    def sub_block_queries(q, b):
        return [jnp.where(rowblk == i, q * jnp.exp(jnp.minimum(b - b[i * S:i * S + 1, :], 0.0)), 0.0)
                for i in range(nsub)]

    @pl.when(jnp.logical_not(risky))
    def _():
        att = {}
        for c, h in pairs:
            q, k, b = view(q_scr, c, h), view(k_scr, c, h), view(b_scr, c, h)
            k_sub = [jnp.where(rowblk <= i, k * jnp.exp(jnp.minimum(b[i * S:i * S + 1, :] - b, guard)), 0.0)
                     for i in range(nsub)]
            att[c, h] = lax.dot_general(jnp.concatenate(sub_block_queries(q, b), axis=1).astype(BF16),
                                        jnp.concatenate(k_sub, axis=1).astype(BF16), nt,
                                        preferred_element_type=F32)
        finish({(c, h): jnp.dot(jnp.where(r_i >= c_i, att[c, h], 0.0).astype(BF16),
                                view(i_ref, c, h).astype(BF16), preferred_element_type=F32) for c, h in pairs})

    @pl.when(risky)
    def _():
        sub_r = lax.broadcasted_iota(jnp.int32, (S, HGRN_DIM), 0)
        intra = {}
        for c, h in pairs:
            q, k, b, v = view(q_scr, c, h), view(k_scr, c, h), view(b_scr, c, h), view(i_ref, c, h)
            k_sub = [jnp.where(rowblk < i, k * jnp.exp(jnp.minimum(b[i * S:i * S + 1, :] - b, 0.0)), 0.0)
                     for i in range(1, nsub)]
            att = lax.dot_general(jnp.concatenate(sub_block_queries(q, b)[1:], axis=1).astype(BF16),
                                  jnp.concatenate(k_sub, axis=1).astype(BF16), nt, preferred_element_type=F32)
            diag = []
            for i in range(nsub):
                b_blk = b[i * S:(i + 1) * S, :]
                q_blk = q[i * S:(i + 1) * S, :]
                acc = jnp.zeros((S, HGRN_DIM), F32)
                for s_ in range(S):
                    r = i * S + s_
                    e = jnp.exp(jnp.minimum(b_blk - b[r:r + 1, :], 0.0))
                    a = jnp.where(sub_r >= s_, q_blk * e * k[r:r + 1, :], 0.0)
                    acc = acc + jnp.sum(a, axis=-1, keepdims=True) * v[r:r + 1, :]
                diag.append(acc)
            intra[c, h] = (jnp.dot(att.astype(BF16), v.astype(BF16), preferred_element_type=F32)
                           + jnp.concatenate(diag, axis=0))
        finish(intra)


def _hgrn(hq, hf, hi, hg, lb, og, batch, seq, guard=HGRN_EXP_GUARD):
    shp = (batch, seq, HGRN_WIDTH)
    blk = pl.BlockSpec((None, HGRN_STEP, HGRN_WIDTH), lambda b, c: (b, c, 0))
    vec = pl.BlockSpec((1, HGRN_WIDTH), lambda b, c: (0, 0))
    out = pl.pallas_call(
        functools.partial(_hgrn_kernel, guard),
        grid=(batch, seq // HGRN_STEP),
        in_specs=[blk, blk, blk, blk, vec, vec],
        out_specs=blk,
        out_shape=jax.ShapeDtypeStruct(shp, F32),
        scratch_shapes=[pltpu.VMEM((HGRN_HEADS, HGRN_DIM, HGRN_DIM), F32)]
        + [pltpu.VMEM((HGRN_STEP, HGRN_WIDTH), F32)] * 3,
        compiler_params=_params("parallel", "arbitrary"),
        name="hgrn",
    )(hq.reshape(shp), hf.reshape(shp), hi.reshape(shp), hg.reshape(shp), lb, og)
    return out.reshape(batch * seq, HGRN_WIDTH)


def _outproj_kernel(attn_ref, hg_ref, x_ref, wo_ref, bo_ref, g2_ref, rw_ref, rb_ref,
                    x1_ref, h2_ref, lpos_ref, gate_ref, seg_ref):
    tm = SORT_TILE
    tiles = range(OUTPROJ_TILES)
    y = jnp.dot(attn_ref[...].astype(BF16), wo_ref[:ATTN_WIDTH, :], preferred_element_type=F32)
    y = y + jnp.dot(hg_ref[...].astype(BF16), wo_ref[ATTN_WIDTH:, :], preferred_element_type=F32)
    x1 = x_ref[...] + y + bo_ref[...]
    x1_ref[...] = x1
    h2 = _rms(x1, g2_ref[...])
    h2_ref[...] = h2.astype(BF16)
    nt = (((1,), (1,)), ((), ()))
    h_hi = h2.astype(BF16)
    h_lo = (h2 - h_hi.astype(F32)).astype(BF16)
    work = []
    for u in tiles:
        p_hi = lax.dot_general(rw_ref[...], h_hi[u * tm:(u + 1) * tm], nt, preferred_element_type=F32)
        p_lo = lax.dot_general(rw_ref[...], h_lo[u * tm:(u + 1) * tm], nt, preferred_element_type=F32)
        work.append((p_hi[:N_EXPERTS] + p_hi[LANES:LANES + N_EXPERTS]) + p_lo[:N_EXPERTS] + rb_ref[...])
    eid = lax.broadcasted_iota(jnp.int32, (N_EXPERTS, tm), 0).astype(F32)
    vals = [[] for _ in tiles]
    hots = [[] for _ in tiles]
    for _ in range(TOP_K):
        for u in tiles:
            m = jnp.max(work[u], axis=0, keepdims=True)
            idx = jnp.min(jnp.where(work[u] == m, eid, float(N_EXPERTS)), axis=0, keepdims=True)
            hot = eid == idx
            vals[u].append(m)
            hots[u].append(hot)
            work[u] = jnp.where(hot, -jnp.inf, work[u])
    t_r = lax.broadcasted_iota(jnp.int32, (tm, tm), 0)
    t_c = lax.broadcasted_iota(jnp.int32, (tm, tm), 1)
    earlier = jnp.where(t_r < t_c, 1.0, 0.0).astype(BF16)
    e_r = lax.broadcasted_iota(jnp.int32, (N_EXPERTS, N_EXPERTS), 0)
    e_c = lax.broadcasted_iota(jnp.int32, (N_EXPERTS, N_EXPERTS), 1)
    before = jnp.where(e_c < e_r, 1.0, 0.0).astype(BF16)

    def round_up(n):
        return jnp.floor((n + (SEG_ROWS - 1.0)) * (1.0 / SEG_ROWS)) * SEG_ROWS

    sel = [sum(jnp.where(hot, 1.0, 0.0) for hot in hots[u]) for u in tiles]
    ranks = [jnp.dot(sel[u].astype(BF16), earlier, preferred_element_type=F32) for u in tiles]
    seg_col = [round_up(jnp.sum(sel[u], axis=1, keepdims=True)) for u in tiles]
    seg_off = [jnp.dot(before, jnp.broadcast_to(seg_col[u], (N_EXPERTS, LANES)).astype(BF16),
                       preferred_element_type=F32)[:, 0:1] for u in tiles]
    seg_row = [round_up(lax.dot_general(jnp.ones((SEG_ROWS, tm), BF16), sel[u].astype(BF16), nt,
                                        preferred_element_type=F32)[0:1, :]) for u in tiles]
    for u in tiles:
        slots = ranks[u] + seg_off[u]
        ex = [jnp.exp(v - vals[u][0]) for v in vals[u]]
        den = ex[0] + ex[1] + ex[2] + ex[3]
        lpos_ref[:, u * tm:(u + 1) * tm] = jnp.concatenate(
            [jnp.sum(jnp.where(hot, slots, 0.0), axis=0, keepdims=True) for hot in hots[u]],
            axis=0).astype(jnp.int32)
        gate_ref[:, u * tm:(u + 1) * tm] = jnp.concatenate([e / den for e in ex], axis=0)
        seg_ref[u] = seg_row[u]


def _outproj(attn, hgo, x2, wo_bf, bo, g2, rw, rb):
    t = x2.shape[0]
    step = OUTPROJ_TILES * SORT_TILE
    row = lambda i: (i, 0)
    col = lambda i: (0, i)
    fixed = lambda i: (0, 0)
    return pl.pallas_call(
        _outproj_kernel,
        grid=(t // step,),
        in_specs=[pl.BlockSpec((step, ATTN_WIDTH), row),
                  pl.BlockSpec((step, HGRN_WIDTH), row),
                  pl.BlockSpec((step, D_MODEL), row),
                  pl.BlockSpec((ATTN_WIDTH + HGRN_WIDTH, D_MODEL), fixed),
                  pl.BlockSpec((1, D_MODEL), fixed),
                  pl.BlockSpec((1, D_MODEL), fixed),
                  pl.BlockSpec((2 * LANES, D_MODEL), fixed),
                  pl.BlockSpec((N_EXPERTS, 1), fixed)],
        out_specs=[pl.BlockSpec((step, D_MODEL), row),
                   pl.BlockSpec((step, D_MODEL), row),
                   pl.BlockSpec((TOP_K, step), col),
                   pl.BlockSpec((TOP_K, step), col),
                   pl.BlockSpec((OUTPROJ_TILES, 1, N_EXPERTS), lambda i: (i, 0, 0))],
        out_shape=[jax.ShapeDtypeStruct((t, D_MODEL), F32),
                   jax.ShapeDtypeStruct((t, D_MODEL), BF16),
                   jax.ShapeDtypeStruct((TOP_K, t), jnp.int32),
                   jax.ShapeDtypeStruct((TOP_K, t), F32),
                   jax.ShapeDtypeStruct((t // SORT_TILE, 1, N_EXPERTS), F32)],
        compiler_params=_params("parallel"),
        name="outproj",
    )(attn, hgo, x2, wo_bf, bo, g2, rw, rb)


def _rows(start, n):
    return pl.ds(pl.multiple_of(start, SEG_ROWS), n)


REPEAT_UNROLL = 4


def _repeat(cnt, fn):
    groups = cnt // REPEAT_UNROLL

    def unrolled(g, carry):
        for u in range(REPEAT_UNROLL):
            fn(g * REPEAT_UNROLL + u)
        return carry

    def single(j, carry):
        fn(j)
        return carry

    lax.fori_loop(0, groups, unrolled, 0)
    lax.fori_loop(groups * REPEAT_UNROLL, cnt, single, 0)


BIG_CAP = SORT_SLOTS // BIG_PIECE
SMALL_CAP = N_EXPERTS * (BIG_PIECE // SEG_ROWS - 1)


def _tile_pieces(lists, step, fn):
    big_loc, big_glob, big_cnt, small_loc, small_glob, small_cnt = lists
    b0 = step * BIG_CAP
    _repeat(big_cnt[step], lambda j: fn(big_loc[b0 + j], big_glob[b0 + j], BIG_PIECE))
    s0 = step * SMALL_CAP
    _repeat(small_cnt[step], lambda j: fn(small_loc[s0 + j], small_glob[s0 + j], SEG_ROWS))


def _tile_piece_waits(lists, step, wait):
    _repeat(lists[2][step], lambda j: wait(BIG_PIECE))
    _repeat(lists[5][step], lambda j: wait(SEG_ROWS))


def _dispatch_kernel(*refs):
    lists, (zdst_ref, zbig_ref, zsmall_ref) = refs[:6], refs[6:9]
    h_ref, lpos_t_ref, xs_ref, lbuf_ref, zbuf_ref, sem, zsem = refs[9:]
    i = pl.program_id(0)
    last = pl.num_programs(0) - 1
    buf = i % 2

    def piece(b, local, glob, n):
        return pltpu.make_async_copy(lbuf_ref.at[b, _rows(local, n), :], xs_ref.at[_rows(glob, n), :], sem.at[b])

    def drain(step):
        _tile_piece_waits(lists, step, lambda n: piece(step % 2, 0, 0, n).wait())

    def zero_piece(glob, n):
        return pltpu.make_async_copy(zbuf_ref.at[_rows(0, n), :], xs_ref.at[_rows(glob, n), :], zsem)

    @pl.when(i == 0)
    def _():
        zbuf_ref[...] = jnp.zeros_like(zbuf_ref)
        for e in range(N_EXPERTS + 1):
            _repeat(zbig_ref[e], lambda j, e=e: zero_piece(zdst_ref[e] + j * BIG_PIECE, BIG_PIECE).start())
            _repeat(zsmall_ref[e], lambda j, e=e: zero_piece(
                zdst_ref[e] + zbig_ref[e] * BIG_PIECE + j * SEG_ROWS, SEG_ROWS).start())
        for e in range(N_EXPERTS + 1):
            _repeat(zbig_ref[e], lambda j: zero_piece(0, BIG_PIECE).wait())
            _repeat(zsmall_ref[e], lambda j: zero_piece(0, SEG_ROWS).wait())

    @pl.when(i >= 2)
    def _():
        drain(i - 2)

    slot = lax.broadcasted_iota(jnp.int32, (SORT_SLOTS, SORT_TILE), 0)
    onehot = jnp.zeros((SORT_SLOTS, SORT_TILE), F32)
    for k in range(TOP_K):
        onehot = onehot + jnp.where(slot == lpos_t_ref[k:k + 1, :], 1.0, 0.0)
    lbuf_ref[buf] = jnp.dot(onehot.astype(BF16), h_ref[...], preferred_element_type=F32)
    _tile_pieces(lists, i, lambda local, glob, n: piece(buf, local, glob, n).start())

    @pl.when(i == last)
    def _():
        @pl.when(i >= 1)
        def _():
            drain(i - 1)

        drain(i)


def _dispatch(tables, zero_tables, h2, lpos_t, n_rows):
    t = h2.shape[0]
    tile = lambda i, *_: (i, 0)
    return pl.pallas_call(
        _dispatch_kernel,
        grid_spec=pltpu.PrefetchScalarGridSpec(
            num_scalar_prefetch=9,
            grid=(t // SORT_TILE,),
            in_specs=[pl.BlockSpec((SORT_TILE, D_MODEL), tile),
                      pl.BlockSpec((TOP_K, SORT_TILE), lambda i, *_: (0, i))],
            out_specs=pl.BlockSpec(memory_space=pl.ANY),
            scratch_shapes=[pltpu.VMEM((2, SORT_SLOTS, D_MODEL), F32),
                            pltpu.VMEM((BIG_PIECE, D_MODEL), F32),
                            pltpu.SemaphoreType.DMA((2,)),
                            pltpu.SemaphoreType.DMA(())]),
        out_shape=jax.ShapeDtypeStruct((n_rows, D_MODEL), F32),
        compiler_params=_params("arbitrary"),
        name="dispatch",
    )(*tables, *zero_tables, h2, lpos_t)


CAST_ROWS = 128


def _expert_kernel(be_ref, nb_ref, eb_ref, filled_ref,
                   xs_ref, w1_hbm, b1_ref, w2_hbm, b2_ref, y_ref,
                   w1f_ref, w2f_ref, w1b_ref, w2i_ref, w2b_ref, started_ref, sem):
    blk = pl.program_id(0)
    half = LANES // 2

    def weight_copies(e, s):
        return (pltpu.make_async_copy(w1_hbm.at[e], w1f_ref.at[s], sem.at[0, s]),
                pltpu.make_async_copy(w2_hbm.at[e], w2f_ref.at[s], sem.at[1, s]))

    e = be_ref[blk]
    first_of_expert = (blk < nb_ref[0]) & ((blk == 0) | (e != be_ref[jnp.maximum(blk - 1, 0)]))

    @pl.when(first_of_expert)
    def _():
        done = jnp.where(blk == 0, 0, started_ref[0])
        started_ref[0] = done + 1
        s = done % 2
        next_blk = blk + eb_ref[e]

        @pl.when(blk == 0)
        def _():
            for c in weight_copies(e, s):
                c.start()

        @pl.when(next_blk < nb_ref[0])
        def _():
            for c in weight_copies(be_ref[next_blk], 1 - s):
                c.start()

        for c in weight_copies(e, s):
            c.wait()

        def cast_rows(r, carry):
            rows = pl.ds(pl.multiple_of(r * CAST_ROWS, CAST_ROWS), CAST_ROWS)
            w1b_ref[rows, :] = w1f_ref[s, rows, :].astype(BF16)
            return carry

        lax.fori_loop(0, D_MODEL // CAST_ROWS, cast_rows, 0)
        for c in range(D_MODEL // LANES):
            cols = slice(c * LANES, (c + 1) * LANES)
            for m in range(EXPERT_FF // LANES):
                lo = m * LANES
                w2i_ref[c, pl.ds(lo, half, stride=2), :] = w2f_ref[s, lo:lo + half, cols]
                w2i_ref[c, pl.ds(lo + 1, half, stride=2), :] = w2f_ref[s, lo + half:lo + LANES, cols]
            w2b_ref[:, cols] = w2i_ref[c].astype(BF16)

    def ffn(rows):
        x = xs_ref[:rows, :].astype(BF16)
        hid = jnp.dot(x, w1b_ref[...], preferred_element_type=F32) + b1_ref[...]
        even = (lax.broadcasted_iota(jnp.int32, (rows, LANES), 1) & 1) == 0
        glu, lin = [], []
        for m in range(EXPERT_FF // LANES):
            ha = hid[:, 2 * m * LANES:(2 * m + 1) * LANES]
            hb = hid[:, (2 * m + 1) * LANES:(2 * m + 2) * LANES]
            glu.append(jnp.where(even, ha, pltpu.roll(hb, 1, axis=1)))
            lin.append(jnp.where(even, pltpu.roll(ha, LANES - 1, axis=1), hb))
        glu = jnp.minimum(jnp.concatenate(glu, axis=1), SWIGLU_LIMIT)
        lin = jnp.clip(jnp.concatenate(lin, axis=1), -SWIGLU_LIMIT, SWIGLU_LIMIT)
        act = glu * jax.nn.sigmoid(SWIGLU_ALPHA * glu) * (lin + 1.0)
        y_ref[:rows, :] = jnp.dot(act.astype(BF16), w2b_ref[...], preferred_element_type=F32) + b2_ref[...]

    quarter = EXPERT_BLOCK // 4
    filled_quarters = (filled_ref[blk] + quarter - 1) // quarter
    for nq in range(1, 5):
        @pl.when(filled_quarters == nq)
        def _(rows=nq * quarter):
            ffn(rows)
            if rows < EXPERT_BLOCK:
                y_ref[rows:, :] = jnp.zeros((EXPERT_BLOCK - rows, D_MODEL), F32)

    @pl.when(filled_quarters == 0)
    def _():
        y_ref[...] = jnp.zeros_like(y_ref)


def _experts(block_tables, xs, w1, b1, w2, b2):
    n_rows = xs.shape[0]
    nblk = n_rows // EXPERT_BLOCK
    rows = lambda b, *_: (b, 0)
    used_rows = lambda b, be, nb, *_: (jnp.minimum(b, nb[0] - 1), 0)
    bias = lambda b, be, *_: (be[b], 0, 0)
    return pl.pallas_call(
        _expert_kernel,
        grid_spec=pltpu.PrefetchScalarGridSpec(
            num_scalar_prefetch=4,
            grid=(nblk,),
            in_specs=[pl.BlockSpec((EXPERT_BLOCK, D_MODEL), used_rows),
                      pl.BlockSpec(memory_space=pl.ANY),
                      pl.BlockSpec((None, 1, 2 * EXPERT_FF), bias),
                      pl.BlockSpec(memory_space=pl.ANY),
                      pl.BlockSpec((None, 1, D_MODEL), bias)],
            out_specs=pl.BlockSpec((EXPERT_BLOCK, D_MODEL), rows),
            scratch_shapes=[pltpu.VMEM((2, D_MODEL, 2 * EXPERT_FF), F32),
                            pltpu.VMEM((2, EXPERT_FF, D_MODEL), F32),
                            pltpu.VMEM((D_MODEL, 2 * EXPERT_FF), BF16),
                            pltpu.VMEM((D_MODEL // LANES, EXPERT_FF, LANES), F32),
                            pltpu.VMEM((EXPERT_FF, D_MODEL), BF16),
                            pltpu.SMEM((1,), jnp.int32),
                            pltpu.SemaphoreType.DMA((2, 2))]),
        out_shape=jax.ShapeDtypeStruct((n_rows, D_MODEL), F32),
        compiler_params=_params("arbitrary"),
        name="experts",
    )(*block_tables, xs, w1, b1, w2, b2)


def _combine_kernel(final_norm, *refs):
    lists = refs[:6]
    yb_ref, x1_ref, lpos_ref, gate_ref, g_ref, o_ref, gbuf_ref, sem = refs[6:]
    i = pl.program_id(0)
    last = pl.num_programs(0) - 1
    buf = i % 2

    def piece(b, local, glob, n):
        return pltpu.make_async_copy(yb_ref.at[_rows(glob, n), :], gbuf_ref.at[b, _rows(local, n), :], sem.at[b])

    def fetch(step):
        _tile_pieces(lists, step, lambda local, glob, n: piece(step % 2, local, glob, n).start())

    @pl.when(i == 0)
    def _():
        gbuf_ref[...] = jnp.zeros_like(gbuf_ref)
        fetch(0)

    @pl.when(i < last)
    def _():
        fetch(i + 1)

    _tile_piece_waits(lists, i, lambda n: piece(buf, 0, 0, n).wait())
    slot = lax.broadcasted_iota(jnp.int32, (SORT_SLOTS, SORT_TILE), 0)
    weights = jnp.zeros((SORT_SLOTS, SORT_TILE), F32)
    for k in range(TOP_K):
        weights = weights + jnp.where(slot == lpos_ref[k:k + 1, :], gate_ref[k:k + 1, :], 0.0)
    y = x1_ref[...] + lax.dot_general(weights.astype(BF16), gbuf_ref[buf].astype(BF16),
                                      (((0,), (0,)), ((), ())), preferred_element_type=F32)
    o_ref[...] = _rms(y, g_ref[...]) if final_norm else y


def _combine(tables, yb, x1, lpos, gates, g, final_norm):
    t = x1.shape[0]
    tile = lambda i, *_: (i, 0)
    return pl.pallas_call(
        functools.partial(_combine_kernel, final_norm),
        grid_spec=pltpu.PrefetchScalarGridSpec(
            num_scalar_prefetch=6,
            grid=(t // SORT_TILE,),
            in_specs=[pl.BlockSpec(memory_space=pl.ANY),
                      pl.BlockSpec((SORT_TILE, D_MODEL), tile),
                      pl.BlockSpec((TOP_K, SORT_TILE), lambda i, *_: (0, i)),
                      pl.BlockSpec((TOP_K, SORT_TILE), lambda i, *_: (0, i)),
                      pl.BlockSpec((1, D_MODEL), lambda i, *_: (0, 0))],
            out_specs=pl.BlockSpec((SORT_TILE, D_MODEL), tile),
            scratch_shapes=[pltpu.VMEM((2, SORT_SLOTS, D_MODEL), F32),
                            pltpu.SemaphoreType.DMA((2,))]),
        out_shape=jax.ShapeDtypeStruct((t, D_MODEL), F32),
        compiler_params=_params("arbitrary"),
        name="combine",
    )(*tables, yb, x1, lpos, gates, g)


def _piece_lists(seg_loc, seg_glob, cnt, skip, rows, cap):
    first = jnp.cumsum(cnt, axis=1) - cnt
    p = jnp.arange(cap, dtype=jnp.int32)
    started = first[:, None, :] <= p[None, :, None]

    def per_piece(rows0):
        base = rows0 + skip - first * rows
        step = base - jnp.concatenate([jnp.zeros_like(base[:, :1]), base[:, :-1]], axis=1)
        return (jnp.sum(jnp.where(started, step[:, None, :], 0), axis=2) + p[None, :] * rows).reshape(-1)

    return per_piece(seg_loc), per_piece(seg_glob), jnp.sum(cnt, axis=1)


def kernel(x, norm1_g, w_in, b_in, attn_sinks, attn_out_g, hgrn_lb_logits, hgrn_out_g, w_out, b_out,
           norm2_g, router_w, router_b, w1, b1, w2, b2, final_g):
    batch, seq, d = x.shape
    t = batch * seq
    depth = w_in.shape[0]
    lower_bounds = jnp.cumsum(jax.nn.softmax(hgrn_lb_logits.astype(F32), axis=0), axis=0)
    assert t % SORT_TILE == 0 and SORT_SLOTS >= SORT_TILE * TOP_K + N_EXPERTS * (SEG_ROWS - 1)
    ntiles = t // SORT_TILE
    nblk = (t * TOP_K + ntiles * N_EXPERTS * (SEG_ROWS - 1)) // EXPERT_BLOCK + N_EXPERTS
    n_rows = nblk * EXPERT_BLOCK
    x2 = x.reshape(t, d)
    for l in range(depth):
        w_in_l = jnp.concatenate([_pair_heads(w_in[l][:, :ATTN_WIDTH], 1), w_in[l][:, ATTN_WIDTH:]], axis=1)
        b_in_l = jnp.concatenate([_pair_heads(b_in[l][:ATTN_WIDTH], 0), b_in[l][ATTN_WIDTH:]])
        aq, ak, av, hq, hf, hi, hg = _inproj(x2, norm1_g[l][None], w_in_l.astype(BF16), b_in_l[None])
        attn = _attention(aq, ak, av, attn_sinks[l], _pair_heads(attn_out_g[l], 0)[None], batch, seq)
        hgo = _hgrn(hq, hf, hi, hg, lower_bounds[l][None], hgrn_out_g[l][None], batch, seq)
        w_out_l = jnp.concatenate([_pair_heads(w_out[l][:ATTN_WIDTH], 0), w_out[l][ATTN_WIDTH:]], axis=0)
        rw_hi = router_w[l].astype(BF16)
        rw_lo = (router_w[l] - rw_hi.astype(F32)).astype(BF16)
        rw_cat = jnp.zeros((2 * LANES, d), BF16)
        rw_cat = rw_cat.at[:N_EXPERTS].set(rw_hi.T).at[LANES:LANES + N_EXPERTS].set(rw_lo.T)
        x1, h2, lpos_t, gates_t, seg = _outproj(
            attn, hgo, x2, w_out_l.astype(BF16), b_out[l][None], norm2_g[l][None],
            rw_cat, router_b[l][:, None])
        seg = seg.reshape(ntiles, N_EXPERTS).astype(jnp.int32)
        rows_e = jnp.sum(seg, axis=0)
        padded = (rows_e + EXPERT_BLOCK - 1) // EXPERT_BLOCK * EXPERT_BLOCK
        ends = jnp.cumsum(padded)
        pstart = ends - padded
        seg_glob = pstart[None, :] + jnp.cumsum(seg, axis=0) - seg
        seg_loc = jnp.cumsum(seg, axis=1) - seg
        tail = jnp.append(padded - rows_e, n_rows - ends[-1])
        zero_tables = (jnp.append(pstart + rows_e, ends[-1]), tail // BIG_PIECE, tail % BIG_PIECE // SEG_ROWS)
        big = seg // BIG_PIECE
        tables = (_piece_lists(seg_loc, seg_glob, big, jnp.zeros_like(seg), BIG_PIECE, BIG_CAP)
                  + _piece_lists(seg_loc, seg_glob, seg % BIG_PIECE // SEG_ROWS, big * BIG_PIECE, SEG_ROWS,
                                 SMALL_CAP))
        blk_ids = jnp.arange(nblk, dtype=jnp.int32)
        block_e = jnp.minimum(jnp.sum(blk_ids[:, None] * EXPERT_BLOCK >= ends[None, :], axis=-1), N_EXPERTS - 1)
        in_expert = rows_e[None, :] - (blk_ids[:, None] * EXPERT_BLOCK - pstart[None, :])
        filled = jnp.sum(jnp.where((block_e[:, None] == jnp.arange(N_EXPERTS)[None, :])
                                   & (blk_ids[:, None] * EXPERT_BLOCK < ends[-1]),
                                   jnp.clip(in_expert, 0, EXPERT_BLOCK), 0), axis=1)
        block_tables = (block_e.astype(jnp.int32), (ends[-1:] // EXPERT_BLOCK).astype(jnp.int32),
                        (padded // EXPERT_BLOCK).astype(jnp.int32), filled.astype(jnp.int32))
        xs = _dispatch(tables, zero_tables, h2, lpos_t, n_rows)
        yb = _experts(block_tables, xs, w1[l], b1[l][:, None, :], w2[l], b2[l][:, None, :])
        x2 = _combine(tables, yb, x1, lpos_t, gates_t, final_g[None], l == depth - 1)
    return x2.reshape(batch, seq, d)
```
